```python
import math
import jax, jax.numpy as jnp
from jax import lax
import numpy as np

D_MODEL = 1024
BATCH = 16
SEQ = 256
DEPTH = 2
DEC_BATCH = 8
DEC_SEQ = 2048
PAST_LEN = 512

GRID_W = 64
EPS = 1e-6
CHUNK = 128
Q_BLOCK = 128
ROPE_BASE = 10000.0
N_DIR = 2
BRANCH_W = 512

DA_HEADS = 4
DA_HD = 64
DA_WIDTH = DA_HEADS * 2 * DA_HD
MB_INNER = 512
MB_HD = 64
MB_HEADS = MB_INNER // MB_HD
MB_GROUPS = 2
MB_STATE = 64
MB_CONV = 3
MB_CONV_DIM = MB_INNER + 2 * MB_GROUPS * MB_STATE
ML_HEADS = 4
ML_DK = 64
ML_DV = 128
ML_WIDTH = ML_HEADS * ML_DV
N_EXPERTS = 16
EC_FACTOR = 2
EXPERT_FF = 1024

IN_SIZES = (DA_WIDTH, DA_WIDTH, DA_WIDTH,
            MB_INNER, MB_CONV_DIM, N_DIR * MB_HEADS,
            ML_HEADS * ML_DK, ML_HEADS * ML_DK, ML_WIDTH, ML_WIDTH, N_DIR * 2 * ML_HEADS)
IN_WIDTH = sum(IN_SIZES)

kernel_name = 'hybrid_diff_mamba2_mlstm_ecmoe_dit_step'

F32 = jnp.float32


def rmsnorm(x, g):
    xf = x.astype(F32)
    y = xf * lax.rsqrt(jnp.mean(xf * xf, axis=-1, keepdims=True) + EPS)
    return (y * g.astype(F32)).astype(x.dtype)


def flip(a):
    return jnp.flip(a, axis=1)


def split_columns(p):
    offs, o = [], 0
    for s in IN_SIZES[:-1]:
        o += s
        offs.append(o)
    return jnp.split(p, offs, axis=-1)


def apply_axial_rope(x):
    L = x.shape[1]
    t = jnp.arange(L)
    pos = jnp.stack([t // GRID_W, t % GRID_W], axis=-1).astype(F32)
    nf = DA_HD // 4
    inv = ROPE_BASE ** (-jnp.arange(nf, dtype=F32) / nf)
    ang = pos[:, :, None] * inv
    cos = jnp.cos(ang)[None, :, None, None]
    sin = jnp.sin(ang)[None, :, None, None]
    xr = x.astype(F32).reshape(x.shape[:-1] + (2, 2, nf))
    x1, x2 = xr[..., 0, :], xr[..., 1, :]
    out = jnp.stack([x1 * cos - x2 * sin, x1 * sin + x2 * cos], axis=-2)
    return out.reshape(x.shape).astype(x.dtype)


def diff_attend(q, k, v, lam):
    b, Lq, H, _, d = q.shape
    nb = Lq // Q_BLOCK
    qb = jnp.moveaxis(q.reshape(b, nb, Q_BLOCK, H, 2, d), 1, 0)
    scale = d ** -0.5

    def one_block(qblk):
        s = jnp.einsum('bqhmd,bkhmd->bhmqk', qblk, k).astype(F32) * scale
        p = jax.nn.softmax(s, axis=-1)
        p = (p[:, :, 0] - lam * p[:, :, 1]).astype(v.dtype)
        return jnp.einsum('bhqk,bkhe->bqhe', p, v)

    o = lax.map(one_block, qb)
    return jnp.moveaxis(o, 0, 1).reshape(b, Lq, H, 2 * d)


def depthwise_conv(x, w, bias):
    K = w.shape[0]
    y = lax.conv_general_dilated(x, w[:, None, :].astype(x.dtype), window_strides=(1,),
                                 padding=[((K - 1) // 2, K // 2)],
                                 dimension_numbers=('NWC', 'WIO', 'NWC'),
                                 feature_group_count=x.shape[-1])
    return y + bias.astype(x.dtype)


def ssd_scan(x, dt, A, Bm, Cm, h0):
    b, L, H, P = x.shape
    G, N = Bm.shape[2], Bm.shape[3]
    nc = L // CHUNK
    rep = H // G
    xc = x.astype(F32).reshape(b, nc, CHUNK, H, P)
    Bc = jnp.repeat(Bm.astype(F32), rep, axis=2).reshape(b, nc, CHUNK, H, N)
    Cc = jnp.repeat(Cm.astype(F32), rep, axis=2).reshape(b, nc, CHUNK, H, N)
    dtc = dt.reshape(b, nc, CHUNK, H)
    acum = jnp.cumsum(dtc * A, axis=2)
    causal = jnp.tril(jnp.ones((CHUNK, CHUNK), bool))[None, None, :, :, None]
    seg = acum[:, :, :, None, :] - acum[:, :, None, :, :]
    decay = jnp.exp(jnp.where(causal, seg, -jnp.inf))
    cb = jnp.einsum('bcihn,bcjhn->bcijh', Cc, Bc)
    y_diag = jnp.einsum('bcijh,bcjhp->bcihp', cb * decay * dtc[:, :, None, :, :], xc)
    to_end = jnp.exp(acum[:, :, -1:, :] - acum) * dtc
    states = jnp.einsum('bcjhn,bcjh,bcjhp->bchpn', Bc, to_end, xc)
    chunk_decay = jnp.exp(acum[:, :, -1, :])

    def step(h, inp):
        s, dcy = inp
        return h * dcy[:, :, None, None] + s, h

    h_final, h_start = lax.scan(step, h0.astype(F32),
                                (jnp.moveaxis(states, 1, 0), jnp.moveaxis(chunk_decay, 1, 0)))
    h_start = jnp.moveaxis(h_start, 0, 1)
    y_off = jnp.einsum('bcihn,bchpn->bcihp', Cc, h_start) * jnp.exp(acum)[..., None]
    return (y_diag + y_off).reshape(b, L, H, P), h_final


def mlstm_chunked(q, k, v, log_i, log_f, C0, n0, m0):
    b, L, H, DK = q.shape
    DV = v.shape[-1]
    nc = L // CHUNK
    qc = q.reshape(b, nc, CHUNK, H, DK)
    kc = k.reshape(b, nc, CHUNK, H, DK)
    vc = v.reshape(b, nc, CHUNK, H, DV)
    li = log_i.reshape(b, nc, CHUNK, H)
    bcum = jnp.cumsum(log_f.reshape(b, nc, CHUNK, H), axis=2)
    b_end = bcum[:, :, -1]
    w_end = b_end[:, :, None] - bcum + li

    def step(carry, inp):
        C, n, m = carry
        be, we, kj, vj = inp
        m_new = jnp.maximum(be + m, jnp.max(we, axis=1))
        s_old = jnp.exp(be + m - m_new)
        s_tok = jnp.exp(we - m_new[:, None])
        C_new = s_old[..., None, None] * C + jnp.einsum('bjh,bjhk,bjhv->bhkv', s_tok, kj, vj)
        n_new = s_old[..., None] * n + jnp.einsum('bjh,bjhk->bhk', s_tok, kj)
        return (C_new, n_new, m_new), (C, n, m)

    mv = lambda a: jnp.moveaxis(a, 1, 0)
    (Cf, nf, mf), (Cs, ns, ms) = lax.scan(
        step, (C0.astype(F32), n0.astype(F32), m0.astype(F32)),
        (mv(b_end), mv(w_end), mv(kc), mv(vc)))
    Cs, ns, ms = mv(Cs), mv(ns), mv(ms)
    causal = jnp.tril(jnp.ones((CHUNK, CHUNK), bool))[None, None, :, :, None]
    dmat = jnp.where(causal, bcum[:, :, :, None, :] - bcum[:, :, None, :, :] + li[:, :, None, :, :], -jnp.inf)
    inter = bcum + ms[:, :, None, :]
    m_t = jnp.maximum(inter, jnp.max(dmat, axis=3))
    s_intra = jnp.exp(dmat - m_t[:, :, :, None, :]) * jnp.einsum('bcihd,bcjhd->bcijh', qc, kc)
    s_inter = jnp.exp(inter - m_t)
    num = (s_inter[..., None] * jnp.einsum('bcihd,bchdv->bcihv', qc, Cs)
           + jnp.einsum('bcijh,bcjhv->bcihv', s_intra, vc))
    den = s_inter * jnp.einsum('bcihd,bchd->bcih', qc, ns) + jnp.sum(s_intra, axis=3)
    h = num / jnp.maximum(jnp.abs(den), jnp.exp(-m_t))[..., None]
    return h.reshape(b, L, H, DV), (Cf, nf, mf)


def mamba_branch(z, xbc, dt_raw, P, h0):
    b, L, _ = xbc.shape
    xbc = jax.nn.silu(depthwise_conv(xbc, P['mb_conv_w'], P['mb_conv_b']))
    xs, Bm, Cm = jnp.split(xbc, [MB_INNER, MB_INNER + MB_GROUPS * MB_STATE], axis=-1)
    xs = xs.reshape(b, L, MB_HEADS, MB_HD)
    Bm = Bm.reshape(b, L, MB_GROUPS, MB_STATE)
    Cm = Cm.reshape(b, L, MB_GROUPS, MB_STATE)
    dt = jax.nn.softplus(dt_raw.reshape(b, L, N_DIR, MB_HEADS).astype(F32) + P['mb_dt_bias'].astype(F32))
    A = -jnp.exp(P['mb_a_log'].astype(F32))
    y_f, h_f = ssd_scan(xs, dt[:, :, 0], A[0], Bm, Cm, h0[:, 0])
    y_b, h_b = ssd_scan(flip(xs), flip(dt[:, :, 1]), A[1], flip(Bm), flip(Cm), h0[:, 1])
    y = y_f + flip(y_b) + P['mb_d'].astype(F32)[:, None] * xs.astype(F32)
    y = y.reshape(b, L, MB_INNER).astype(z.dtype)
    y = rmsnorm(y * jax.nn.silu(z), P['mb_norm_g'])
    return y, jnp.stack([h_f, h_b], axis=1)


def mlstm_branch(q, k, v, o, gates, P, C0, n0, m0):
    b, L, _ = q.shape
    q = q.reshape(b, L, ML_HEADS, ML_DK).astype(F32)
    k = k.reshape(b, L, ML_HEADS, ML_DK).astype(F32) * (ML_DK ** -0.5)
    v = v.reshape(b, L, ML_HEADS, ML_DV).astype(F32)
    pre = gates.reshape(b, L, N_DIR, 2, ML_HEADS).astype(F32) + P['ml_gate_b'].astype(F32)
    log_i = pre[:, :, :, 0]
    log_f = jax.nn.log_sigmoid(pre[:, :, :, 1])
    h_f, (Cf, nf, mf) = mlstm_chunked(q, k, v, log_i[:, :, 0], log_f[:, :, 0], C0[:, 0], n0[:, 0], m0[:, 0])
    h_b, (Cb, nb, mb) = mlstm_chunked(flip(q), flip(k), flip(v), flip(log_i[:, :, 1]), flip(log_f[:, :, 1]),
                                      C0[:, 1], n0[:, 1], m0[:, 1])
    h = rmsnorm(h_f + flip(h_b), P['ml_norm_g']) * jax.nn.sigmoid(o.reshape(b, L, ML_HEADS, ML_DV).astype(F32))
    return (h.reshape(b, L, ML_WIDTH).astype(o.dtype),
            jnp.stack([Cf, Cb], axis=1), jnp.stack([nf, nb], axis=1), jnp.stack([mf, mb], axis=1))


def mixer_block(h, l, P, ctx):
    b, L, _ = h.shape
    (da_q, da_k, da_v, mb_z, mb_xbc, mb_dt,
     ml_q, ml_k, ml_v, ml_o, ml_if) = split_columns(h @ P['w_in'])
    q = da_q.reshape(b, L, DA_HEADS, 2, DA_HD)
    k = da_k.reshape(b, L, DA_HEADS, 2, DA_HD)
    v = da_v.reshape(b, L, DA_HEADS, 2 * DA_HD)
    lam_init = 0.8 - 0.6 * math.exp(-0.3 * l)
    lp = P['da_lambda'].astype(F32)
    lam = jnp.exp(jnp.sum(lp[0] * lp[1])) - jnp.exp(jnp.sum(lp[2] * lp[3])) + lam_init
    if ctx is None:
        att = diff_attend(q, k, v, lam)
        ssm0 = jnp.zeros((b, N_DIR, MB_HEADS, MB_HD, MB_STATE), F32)
        C0 = jnp.zeros((b, N_DIR, ML_HEADS, ML_DK, ML_DV), F32)
        n0 = jnp.zeros((b, N_DIR, ML_HEADS, ML_DK), F32)
        m0 = jnp.zeros((b, N_DIR, ML_HEADS), F32)
    else:
        k_ctx, v_ctx, ssm0, C0, n0, m0 = ctx
        k_all = jnp.concatenate([k_ctx.astype(k.dtype), apply_axial_rope(k)], axis=1)
        v_all = jnp.concatenate([v_ctx.astype(v.dtype), v], axis=1)
        att = diff_attend(apply_axial_rope(q), k_all, v_all, lam)
    y_da = (rmsnorm(att, P['da_subln_g']) * (1.0 - lam_init)).reshape(b, L, DA_WIDTH)
    y_mb, ssm_fin = mamba_branch(mb_z, mb_xbc, mb_dt, P, ssm0)
    y_ml, Cfin, nfin, mfin = mlstm_branch(ml_q, ml_k, ml_v, ml_o, ml_if, P, C0, n0, m0)
    gates = jax.nn.sigmoid((h @ P['w_mgate'] + P['b_mgate']).astype(F32)).reshape(b, L, 3, D_MODEL)
    branches = jnp.stack([y_da, y_mb, y_ml], axis=2)
    proj_br = jnp.einsum('blnw,nwd->blnd', branches, P['w_branch'])
    merged = jnp.sum(gates * proj_br, axis=2).astype(h.dtype)
    out = merged @ P['w_out']
    new_ctx = (k, v, ssm_fin, Cfin, nfin, mfin) if ctx is None else None
    return out, new_ctx


def expert_choice_ffn(h, router_w, w_gate, w_up, w_down):
    B_, L, D = h.shape
    T = B_ * L
    xt = h.reshape(T, D)
    aff = jax.nn.softmax((xt @ router_w).astype(F32), axis=-1)
    cap = EC_FACTOR * T // N_EXPERTS
    g, idx = lax.top_k(aff.T, cap)
    xe = xt[idx]
    hid = jax.nn.silu(jnp.einsum('ecd,edf->ecf', xe, w_gate)) * jnp.einsum('ecd,edf->ecf', xe, w_up)
    ye = jnp.einsum('ecf,efd->ecd', hid, w_down).astype(F32) * g[..., None]
    out = jnp.zeros((T, D), F32).at[idx.reshape(-1)].add(ye.reshape(-1, D))
    return out.reshape(B_, L, D).astype(h.dtype)


def trunk_layer(x, mod, l, P, ctx):
    sh1, sc1, g1, sh2, sc2, g2 = jnp.split(mod.astype(x.dtype), 6, axis=-1)
    h = rmsnorm(x, P['norm1_g']) * (1 + sc1) + sh1
    mix, new_ctx = mixer_block(h, l, P, ctx)
    x = x + g1 * mix
    h = rmsnorm(x, P['norm2_g']) * (1 + sc2) + sh2
    x = x + g2 * expert_choice_ffn(h, P['router_w'], P['ex_w_gate'], P['ex_w_up'], P['ex_w_down'])
    return x, new_ctx


def setup_inputs(seed: int = 0) -> dict:
    key = jax.random.key(seed)
    ks = iter(jax.random.split(key, 48))

    def nrm(shape, s):
        return jax.random.normal(next(ks), shape, F32) * s

    x_prompt = nrm((BATCH, SEQ, D_MODEL), 1.0)
    x_sample = nrm((DEC_BATCH, DEC_SEQ, D_MODEL), 1.0)
    cache_k = nrm((DEC_BATCH, DEPTH, PAST_LEN, DA_HEADS, 2, DA_HD), 1.0)
    cache_v = nrm((DEC_BATCH, DEPTH, PAST_LEN, DA_HEADS, 2 * DA_HD), 1.0)
    state_ssm = nrm((DEC_BATCH, DEPTH, N_DIR, MB_HEADS, MB_HD, MB_STATE), 0.5)
    state_mlstm_c = nrm((DEC_BATCH, DEPTH, N_DIR, ML_HEADS, ML_DK, ML_DV), 0.3)
    state_mlstm_n = nrm((DEC_BATCH, DEPTH, N_DIR, ML_HEADS, ML_DK), 0.3)
    state_mlstm_m = nrm((DEC_BATCH, DEPTH, N_DIR, ML_HEADS), 0.5)
    c = nrm((DEC_BATCH, D_MODEL), 1.0)
    c_ctx = nrm((D_MODEL,), 1.0)
    ada_w = nrm((DEPTH, D_MODEL, 6 * D_MODEL), 0.5 * D_MODEL ** -0.5)
    ada_b = nrm((DEPTH, 6 * D_MODEL), 0.02)
    norm1_g = 1.0 + nrm((DEPTH, D_MODEL), 0.05)
    norm2_g = 1.0 + nrm((DEPTH, D_MODEL), 0.05)
    w_in = nrm((DEPTH, D_MODEL, IN_WIDTH), D_MODEL ** -0.5)
    da_lambda = nrm((DEPTH, 4, DA_HD), 0.1)
    da_subln_g = 1.0 + nrm((DEPTH, 2 * DA_HD), 0.05)
    mb_conv_w = nrm((DEPTH, MB_CONV, MB_CONV_DIM), MB_CONV ** -0.5)
    mb_conv_b = nrm((DEPTH, MB_CONV_DIM), 0.02)
    dt0 = jnp.exp(jax.random.uniform(next(ks), (DEPTH, N_DIR, MB_HEADS), F32, math.log(1e-3), math.log(1e-1)))
    mb_dt_bias = dt0 + jnp.log(-jnp.expm1(-dt0))
    mb_a_log = jnp.log(jax.random.uniform(next(ks), (DEPTH, N_DIR, MB_HEADS), F32, 1.0, 16.0))
    mb_d = 1.0 + nrm((DEPTH, MB_HEADS), 0.05)
    mb_norm_g = 1.0 + nrm((DEPTH, MB_INNER), 0.05)
    ml_gate_b = jnp.stack([nrm((DEPTH, N_DIR, ML_HEADS), 0.1),
                           3.0 + nrm((DEPTH, N_DIR, ML_HEADS), 0.1)], axis=2)
    ml_norm_g = 1.0 + nrm((DEPTH, ML_HEADS, ML_DV), 0.05)
    w_branch = nrm((DEPTH, 3, BRANCH_W, D_MODEL), BRANCH_W ** -0.5)
    w_mgate = nrm((DEPTH, D_MODEL, 3 * D_MODEL), D_MODEL ** -0.5)
    b_mgate = nrm((DEPTH, 3 * D_MODEL), 0.02)
    w_out = nrm((DEPTH, D_MODEL, D_MODEL), D_MODEL ** -0.5)
    router_w = nrm((DEPTH, D_MODEL, N_EXPERTS), D_MODEL ** -0.5)
    ex_w_gate = nrm((DEPTH, N_EXPERTS, D_MODEL, EXPERT_FF), D_MODEL ** -0.5)
    ex_w_up = nrm((DEPTH, N_EXPERTS, D_MODEL, EXPERT_FF), D_MODEL ** -0.5)
    ex_w_down = nrm((DEPTH, N_EXPERTS, EXPERT_FF, D_MODEL), EXPERT_FF ** -0.5)
    final_g = 1.0 + nrm((D_MODEL,), 0.05)
    return {'x_prompt': x_prompt, 'x_sample': x_sample,
            'cache_k': cache_k, 'cache_v': cache_v, 'state_ssm': state_ssm,
            'state_mlstm_c': state_mlstm_c, 'state_mlstm_n': state_mlstm_n, 'state_mlstm_m': state_mlstm_m,
            'c': c, 'c_ctx': c_ctx, 'ada_w': ada_w, 'ada_b': ada_b,
            'norm1_g': norm1_g, 'norm2_g': norm2_g, 'w_in': w_in,
            'da_lambda': da_lambda, 'da_subln_g': da_subln_g,
            'mb_conv_w': mb_conv_w, 'mb_conv_b': mb_conv_b, 'mb_dt_bias': mb_dt_bias,
            'mb_a_log': mb_a_log, 'mb_d': mb_d, 'mb_norm_g': mb_norm_g,
            'ml_gate_b': ml_gate_b, 'ml_norm_g': ml_norm_g,
            'w_branch': w_branch, 'w_mgate': w_mgate, 'b_mgate': b_mgate, 'w_out': w_out,
            'router_w': router_w, 'ex_w_gate': ex_w_gate, 'ex_w_up': ex_w_up, 'ex_w_down': ex_w_down,
            'final_g': final_g}


def reference(x_prompt, x_sample, cache_k, cache_v, state_ssm, state_mlstm_c, state_mlstm_n,
              state_mlstm_m, c, c_ctx, ada_w, ada_b, norm1_g, norm2_g, w_in, da_lambda, da_subln_g,
              mb_conv_w, mb_conv_b, mb_dt_bias, mb_a_log, mb_d, mb_norm_g, ml_gate_b, ml_norm_g,
              w_branch, w_mgate, b_mgate, w_out, router_w, ex_w_gate, ex_w_up, ex_w_down, final_g):
    layers = [dict(norm1_g=norm1_g[l], norm2_g=norm2_g[l], w_in=w_in[l], da_lambda=da_lambda[l],
                   da_subln_g=da_subln_g[l], mb_conv_w=mb_conv_w[l], mb_conv_b=mb_conv_b[l],
                   mb_dt_bias=mb_dt_bias[l], mb_a_log=mb_a_log[l], mb_d=mb_d[l], mb_norm_g=mb_norm_g[l],
                   ml_gate_b=ml_gate_b[l], ml_norm_g=ml_norm_g[l], w_branch=w_branch[l],
                   w_mgate=w_mgate[l], b_mgate=b_mgate[l], w_out=w_out[l], router_w=router_w[l],
                   ex_w_gate=ex_w_gate[l], ex_w_up=ex_w_up[l], ex_w_down=ex_w_down[l])
              for l in range(DEPTH)]
    y_p = x_prompt
    ks, vs, ssms, Cs, ns, ms = [], [], [], [], [], []
    for l in range(DEPTH):
        mod = (jax.nn.silu(c_ctx) @ ada_w[l] + ada_b[l])[None, None, :]
        y_p, st = trunk_layer(y_p, mod, l, layers[l], None)
        ks.append(st[0]); vs.append(st[1]); ssms.append(st[2])
        Cs.append(st[3]); ns.append(st[4]); ms.append(st[5])
    y_s = x_sample
    for l in range(DEPTH):
        mod = (jax.nn.silu(c) @ ada_w[l] + ada_b[l])[:, None, :]
        ctx = (cache_k[:, l], cache_v[:, l], state_ssm[:, l],
               state_mlstm_c[:, l], state_mlstm_n[:, l], state_mlstm_m[:, l])
        y_s, _ = trunk_layer(y_s, mod, l, layers[l], ctx)
    y_prompt = rmsnorm(y_p, final_g)
    y_sample = rmsnorm(y_s, final_g)
    new_cache_k = jnp.stack(ks, axis=1)
    new_cache_v = jnp.stack(vs, axis=1)
    new_state_ssm = jnp.stack(ssms, axis=1)
    new_state_mlstm_c = jnp.stack(Cs, axis=1)
    new_state_mlstm_n = jnp.stack(ns, axis=1)
    new_state_mlstm_m = jnp.stack(ms, axis=1)
    return (y_prompt, y_sample, new_cache_k, new_cache_v, new_state_ssm,
            new_state_mlstm_c, new_state_mlstm_n, new_state_mlstm_m)
```

```python
import functools
import math

import jax
import jax.numpy as jnp
from jax import lax
from jax.experimental import pallas as pl
from jax.experimental.pallas import tpu as pltpu

F32 = jnp.float32
BF16 = jnp.bfloat16

D_MODEL = 1024
BATCH = 16
SEQ = 256
DEPTH = 2
DEC_BATCH = 8
DEC_SEQ = 2048
PAST_LEN = 512
GRID_W = 64
EPS = 1e-6
CHUNK = 128
ROPE_BASE = 10000.0
DA_HEADS = 4
DA_HD = 64
DA_WIDTH = 512
MB_INNER = 512
MB_HD = 64
MB_HEADS = 8
MB_GROUPS = 2
MB_STATE = 64
MB_CONV_DIM = 768
ML_HEADS = 4
ML_DK = 64
ML_DV = 128
ML_WIDTH = 512
N_EXPERTS = 16
EC_FACTOR = 2
EXPERT_FF = 1024

T_CTX = BATCH * SEQ
T_SMP = DEC_BATCH * DEC_SEQ
T_ALL = T_CTX + T_SMP
TM = 256
NT_CTX = T_CTX // TM
NT_ALL = T_ALL // TM
TILES_PER_REQ = DEC_SEQ // TM
LANES = 128
VMEM_LIMIT = 56 * 1024 * 1024
HI = lax.Precision.HIGHEST

C_Q, C_K, C_V, C_Z, C_XBC, C_MQ, C_MK, C_MV, C_MO, C_END = (
    0, 512, 1024, 1536, 2048, 2816, 3072, 3328, 3840, 4352)


def _mod_row(i):
    return jnp.where(i < NT_CTX, 0, 1 + (i - NT_CTX) // TILES_PER_REQ)


def _rope_blk(i):
    return jnp.where(i < NT_CTX, 0, 1 + (i - NT_CTX) % TILES_PER_REQ)


def _cparams(n_grid):
    return pltpu.CompilerParams(dimension_semantics=("arbitrary",) * n_grid,
                                vmem_limit_bytes=VMEM_LIMIT)


def _silu(x):
    return x * jax.nn.sigmoid(x)


def _softplus(x):
    u = jnp.exp(-jnp.abs(x))
    w = 1.0 + u
    l1p = jnp.where(w == 1.0, u, jnp.log(w) * (u / (w - 1.0)))
    return jnp.maximum(x, 0.0) + l1p


def _dot(a, b):
    return jnp.dot(a, b, preferred_element_type=F32)


def _dot_nt(a, b):
    return lax.dot_general(a, b, (((1,), (1,)), ((), ())), preferred_element_type=F32)


def _dot_hi(a, b):
    return jnp.dot(a, b, precision=HI, preferred_element_type=F32)


def _rms(x, g):
    return x * lax.rsqrt(jnp.mean(x * x, axis=-1, keepdims=True) + EPS) * g


def _mod_kernel(c_ref, w_ref, b_ref, o_ref):
    s = _silu(c_ref[...])
    o_ref[...] = _dot(s.astype(BF16), w_ref[...].astype(BF16)) + b_ref[...]


def _modulation(cc, ada_w, ada_b):
    tn = 1536
    return pl.pallas_call(
        _mod_kernel,
        grid=(DEPTH, 6 * D_MODEL // tn),
        in_specs=[pl.BlockSpec((16, D_MODEL), lambda l, j: (0, 0)),
                  pl.BlockSpec((None, D_MODEL, tn), lambda l, j: (l, 0, j)),
                  pl.BlockSpec((None, 1, tn), lambda l, j: (l, 0, j))],
        out_specs=pl.BlockSpec((None, 16, tn), lambda l, j: (l, 0, j)),
        out_shape=jax.ShapeDtypeStruct((DEPTH, 16, 6 * D_MODEL), F32),
        compiler_params=_cparams(2),
        name="adaln_mod",
    )(cc, ada_w, ada_b.reshape(DEPTH, 1, 6 * D_MODEL))


def _rope(t, cos, sin, first_half):
    outs = []
    for c in range(DA_WIDTH // LANES):
        xc = t[:, LANES * c:LANES * (c + 1)]
        partner = jnp.where(first_half, pltpu.roll(xc, LANES - 16, 1), pltpu.roll(xc, 16, 1))
        outs.append(xc * cos + partner * sin)
    return jnp.concatenate(outs, axis=1)


def _proj_body(x, mod_ref, g_ref, wm_ref, ws_ref, wst_ref, cos_ref, sin_ref,
               q_ref, k_ref, v_ref, kf_ref, vf_ref, z_ref, xbc_ref,
               mq_ref, mk_ref, mv_ref, mo_ref, sm_ref, smt_ref):
    mod = mod_ref[...]
    sh1 = mod[:, 0:D_MODEL]
    sc1 = mod[:, D_MODEL:2 * D_MODEL]
    h = _rms(x, g_ref[...]) * (1.0 + sc1) + sh1
    hb = h.astype(BF16)

    def proj(a, b):
        return _dot(hb, wm_ref[:, a:b])

    lane = lax.broadcasted_iota(jnp.int32, (TM, LANES), 1)
    first_half = (lane % 32) < 16
    cos = cos_ref[...]
    sin = sin_ref[...]
    q = proj(C_Q, C_K)
    k = proj(C_K, C_V)
    v = proj(C_V, C_Z)
    kf_ref[...] = k
    vf_ref[...] = v
    q_ref[...] = (_rope(q, cos, sin, first_half) * (DA_HD ** -0.5)).astype(BF16)
    k_ref[...] = _rope(k, cos, sin, first_half).astype(BF16)
    v_ref[...] = v.astype(BF16)
    z_ref[...] = proj(C_Z, C_XBC)
    xbc_ref[...] = proj(C_XBC, C_MQ)
    mq_ref[...] = proj(C_MQ, C_MK).astype(BF16)
    mk_ref[...] = (proj(C_MK, C_MV) * (ML_DK ** -0.5)).astype(BF16)
    mv_ref[...] = proj(C_MV, C_MO).astype(BF16)
    mo_ref[...] = proj(C_MO, C_END)
    sm_ref[...] = _dot(hb, ws_ref[...])
    st = _dot_nt(wst_ref[...], hb)
    for j in range(TM // CHUNK):
        smt_ref[j] = st[:, CHUNK * j:CHUNK * (j + 1)]


def _proj_kernel_first(x_ref, *refs):
    _proj_body(x_ref[...], *refs)


def _proj_kernel_next(x_ref, moe_ref, modp_ref, *refs):
    g2 = modp_ref[...][:, 5 * D_MODEL:6 * D_MODEL]
    x = x_ref[...] + g2 * moe_ref[...]
    xo_ref = refs[-1]
    xo_ref[...] = x
    _proj_body(x, *refs[:-1])


def _projection(x, moe_prev, mod_prev, mod_l, g1, wm, ws, wst, cos_t, sin_t):
    tile = lambda w: pl.BlockSpec((TM, w), lambda i: (i, 0))
    full = lambda a: pl.BlockSpec(a.shape, lambda i: (0,) * a.ndim)
    modspec = pl.BlockSpec((None, 1, 6 * D_MODEL), lambda i: (_mod_row(i), 0, 0))
    in_specs = [tile(D_MODEL)]
    args = [x]
    if moe_prev is not None:
        in_specs += [tile(D_MODEL), modspec]
        args += [moe_prev, mod_prev]
    in_specs += [modspec, full(g1), full(wm), full(ws), full(wst),
                 pl.BlockSpec((TM, LANES), lambda i: (_rope_blk(i), 0)),
                 pl.BlockSpec((TM, LANES), lambda i: (_rope_blk(i), 0))]
    args += [mod_l, g1, wm, ws, wst, cos_t, sin_t]
    ctx_only = pl.BlockSpec((TM, DA_WIDTH), lambda i: (jnp.minimum(i, NT_CTX), 0))
    out_specs = [tile(512), tile(512), tile(512), ctx_only, ctx_only, tile(512), tile(768),
                 tile(256), tile(256), tile(512), tile(512), tile(32),
                 pl.BlockSpec((TM // CHUNK, 32, CHUNK), lambda i: (i, 0, 0))]
    sds = jax.ShapeDtypeStruct
    out_shape = [sds((T_ALL, 512), BF16), sds((T_ALL, 512), BF16), sds((T_ALL, 512), BF16),
                 sds((T_CTX + TM, 512), F32), sds((T_CTX + TM, 512), F32),
                 sds((T_ALL, 512), F32), sds((T_ALL, 768), F32),
                 sds((T_ALL, 256), BF16), sds((T_ALL, 256), BF16), sds((T_ALL, 512), BF16),
                 sds((T_ALL, 512), F32), sds((T_ALL, 32), F32),
                 sds((T_ALL // CHUNK, 32, CHUNK), F32)]
    kern = _proj_kernel_first
    if moe_prev is not None:
        out_specs.append(tile(D_MODEL))
        out_shape.append(sds((T_ALL, D_MODEL), F32))
        kern = _proj_kernel_next
    return pl.pallas_call(
        kern, grid=(NT_ALL,), in_specs=in_specs, out_specs=out_specs, out_shape=out_shape,
        compiler_params=_cparams(1), name="in_proj",
    )(*args)


def _attn_body(lam_init, q_ref, k_ref, v_ref, kc_ref, vc_ref, lp_ref, g_ref, o_ref):
    lp = lp_ref[...]
    s01 = jnp.sum(lp[0:1] * lp[1:2], axis=-1, keepdims=True)
    s23 = jnp.sum(lp[2:3] * lp[3:4], axis=-1, keepdims=True)
    lam = jnp.exp(s01) - jnp.exp(s23) + lam_init
    tq = q_ref.shape[0]
    lane = lax.broadcasted_iota(jnp.int32, (tq, LANES), 1)
    g = g_ref[...]
    for h in range(DA_HEADS):
        cols = slice(LANES * h, LANES * (h + 1))
        qh = q_ref[:, cols]
        kh = k_ref[:, cols]
        vh = v_ref[:, cols]
        if kc_ref is not None:
            kch = kc_ref[:, cols].astype(BF16)
            vch = vc_ref[:, cols].astype(BF16)
        parts = []
        for m in range(2):
            qm = jnp.where((lane < DA_HD) == (m == 0), qh, jnp.zeros_like(qh))
            s = _dot_nt(qm, kh)
            mx = jnp.max(s, axis=-1, keepdims=True)
            if kc_ref is not None:
                sc = _dot_nt(qm, kch)
                mx = jnp.maximum(mx, jnp.max(sc, axis=-1, keepdims=True))
                ec = jnp.exp(sc - mx)
            e = jnp.exp(s - mx)
            den = jnp.sum(e, axis=-1, keepdims=True)
            acc = _dot(e.astype(BF16), vh)
            if kc_ref is not None:
                den = den + jnp.sum(ec, axis=-1, keepdims=True)
                acc = acc + _dot(ec.astype(BF16), vch)
            parts.append(acc / den)
        att = parts[0] - lam * parts[1]
        o_ref[:, cols] = (_rms(att, g) * (1.0 - lam_init)).astype(BF16)


def _attn_kernel_ctx(lam_init, q_ref, k_ref, v_ref, lp_ref, g_ref, o_ref):
    _attn_body(lam_init, q_ref, k_ref, v_ref, None, None, lp_ref, g_ref, o_ref)


def _attn_kernel_smp(lam_init, q_ref, k_ref, v_ref, kc_ref, vc_ref, lp_ref, g_ref, o_ref):
    _attn_body(lam_init, q_ref, k_ref, v_ref, kc_ref, vc_ref, lp_ref, g_ref, o_ref)


def _attention(l, q, k, v, cache_k, cache_v, lp, g, sample):
    lam_init = 0.8 - 0.6 * math.exp(-0.3 * l)
    full = lambda a: pl.BlockSpec(a.shape, lambda *_: (0,) * a.ndim)
    sds = jax.ShapeDtypeStruct
    if not sample:
        blk = pl.BlockSpec((SEQ, DA_WIDTH), lambda b: (b, 0))
        return pl.pallas_call(
            functools.partial(_attn_kernel_ctx, lam_init), grid=(BATCH,),
            in_specs=[blk, blk, blk, full(lp), full(g)],
            out_specs=blk, out_shape=sds((T_CTX, DA_WIDTH), BF16),
            compiler_params=_cparams(1), name="diff_attn_ctx",
        )(q, k, v, lp, g)
    tq = 256
    nq = DEC_SEQ // tq
    off_q = T_CTX // tq
    off_k = T_CTX // DEC_SEQ
    qblk = pl.BlockSpec((tq, DA_WIDTH), lambda b, i: (off_q + b * nq + i, 0))
    kblk = pl.BlockSpec((DEC_SEQ, DA_WIDTH), lambda b, i: (off_k + b, 0))
    cblk = pl.BlockSpec((None, None, PAST_LEN, DA_WIDTH), lambda b, i: (b, l, 0, 0))
    return pl.pallas_call(
        functools.partial(_attn_kernel_smp, lam_init), grid=(DEC_BATCH, nq),
        in_specs=[qblk, kblk, kblk, cblk, cblk, full(lp), full(g)],
        out_specs=pl.BlockSpec((tq, DA_WIDTH), lambda b, i: (b * nq + i, 0)),
        out_shape=sds((T_SMP, DA_WIDTH), BF16),
        compiler_params=_cparams(2), name="diff_attn_smp",
    )(q, k, v, cache_k, cache_v, lp, g)


def _tri_masks():
    r = lax.broadcasted_iota(jnp.int32, (CHUNK, CHUNK), 0)
    c = lax.broadcasted_iota(jnp.int32, (CHUNK, CHUNK), 1)
    return c <= r, c >= r


def _ssd_kernel(L, z_ref, xbc_ref, sm_ref, smt_ref, cw_ref, cb_ref, dtbr_ref, dtbc_ref,
                ar_ref, ac_ref, dexp_ref, ng_ref, h0_ref,
                y_ref, hfin_ref, xc_s, yacc_s, ht_s):
    nc = L // CHUNK
    low, upp = _tri_masks()
    low_f = low.astype(F32)
    upp_f = upp.astype(F32)
    lane512 = lax.broadcasted_iota(jnp.int32, (1, MB_INNER), 1)
    row16 = lax.broadcasted_iota(jnp.int32, (16, MB_INNER), 0)
    lane16 = lax.broadcasted_iota(jnp.int32, (16, MB_INNER), 1)
    lane128 = lax.broadcasted_iota(jnp.int32, (CHUNK, LANES), 1)
    rowblk = lax.broadcasted_iota(jnp.int32, (LANES, MB_INNER), 0) // MB_STATE
    colblk = lax.broadcasted_iota(jnp.int32, (LANES, MB_INNER), 1) // (MB_INNER // MB_GROUPS)
    same_group = rowblk == colblk
    cw = cw_ref[...]
    cbias = cb_ref[...]

    def conv_chunk(c, _):
        base = pl.multiple_of(c * CHUNK, CHUNK)
        x = xbc_ref[pl.ds(base, CHUNK), :]
        prev = xbc_ref[pl.ds(jnp.maximum(base - 1, 0), 1), :]
        nxt = xbc_ref[pl.ds(jnp.minimum(base + CHUNK, L - 1), 1), :]
        prev = jnp.where(c == 0, 0.0, prev)
        nxt = jnp.where(c == nc - 1, 0.0, nxt)
        row = lax.broadcasted_iota(jnp.int32, (CHUNK, 1), 0)
        xp = jnp.where(row == 0, prev, pltpu.roll(x, 1, 0))
        xn = jnp.where(row == CHUNK - 1, nxt, pltpu.roll(x, CHUNK - 1, 0))
        conv = xp * cw[0:1] + x * cw[1:2] + xn * cw[2:3] + cbias
        xc_s[pl.ds(base, CHUNK), :] = _silu(conv)
        return 0

    lax.fori_loop(0, nc, conv_chunk, 0)

    h0 = h0_ref[...]
    for d in range(2):
        h0d = h0[d]
        ht_s[d] = jnp.concatenate(
            [jnp.where(lane512 < MB_INNER // 2, h0d, 0.0),
             jnp.where(lane512 >= MB_INNER // 2, h0d, 0.0)], axis=0)

    def chunk_dir(c, d):
        base = pl.multiple_of(c * CHUNK, CHUNK)
        rows = pl.ds(base, CHUNK)
        xs = xc_s[rows, 0:MB_INNER]
        bm = xc_s[rows, MB_INNER:MB_INNER + LANES]
        cm = xc_s[rows, MB_INNER + LANES:MB_CONV_DIM]
        dt_col = _softplus(sm_ref[rows, 0:16] + dtbr_ref[...])
        a_col = dt_col * ar_ref[...]
        dt_row = _softplus(smt_ref[c][0:16, :] + dtbc_ref[...])
        a_row = dt_row * ac_ref[...]
        if d == 0:
            cum_col = _dot_hi(low_f, a_col)
            cum_row = _dot_hi(a_row, upp_f)
            mask = low
            last = CHUNK - 1
        else:
            cum_col = _dot_hi(upp_f, a_col)
            cum_row = _dot_hi(a_row, low_f)
            mask = upp
            last = 0
        cum_last = cum_col[last:last + 1, :]
        expand = (row16 == 8 * d + lane16 // MB_HD).astype(F32)
        w_exp = _dot_hi(jnp.exp(cum_last - cum_col) * dt_col, expand)
        g_exp = _dot_hi(jnp.exp(cum_col), expand)
        cd_exp = _dot_hi(jnp.broadcast_to(jnp.exp(cum_last), (8, 16)), expand)[0:1]
        xw = (xs * w_exp).astype(BF16)
        xsb = xs.astype(BF16)
        bb = bm.astype(BF16)
        cb = cm.astype(BF16)
        ht = ht_s[d]
        y_off = _dot(cb, ht.astype(BF16)) * g_exp
        s_new = _dot(bm.T.astype(BF16), xw)
        ht_s[d] = ht * cd_exp + jnp.where(same_group, s_new, 0.0)
        pairs = []
        for g in range(MB_GROUPS):
            cg = jnp.where((lane128 < MB_STATE) == (g == 0), cb, jnp.zeros_like(cb))
            cbg = _dot_nt(cg, bb)
            for pr in range(2):
                ys = []
                for hh in range(2):
                    ci = 8 * d + 4 * g + 2 * pr + hh
                    seg = cum_col[:, ci:ci + 1] - cum_row[ci:ci + 1, :]
                    m = jnp.where(mask, jnp.exp(seg), 0.0) * cbg * dt_row[ci:ci + 1, :]
                    k = 2 * g + pr
                    ys.append(_dot(m.astype(BF16), xsb[:, LANES * k:LANES * (k + 1)]))
                pairs.append(jnp.where(lane128 < MB_HD, ys[0], ys[1]))
        y = jnp.concatenate(pairs, axis=1) + y_off
        if d == 0:
            yacc_s[rows, :] = y
        else:
            yacc_s[rows, :] = yacc_s[rows, :] + y

    def fwd(c, _):
        chunk_dir(c, 0)
        return 0

    def bwd(i, _):
        chunk_dir(nc - 1 - i, 1)
        return 0

    lax.fori_loop(0, nc, fwd, 0)
    lax.fori_loop(0, nc, bwd, 0)

    dexp = dexp_ref[...]
    ng = ng_ref[...]

    def fin(c, _):
        rows = pl.ds(pl.multiple_of(c * CHUNK, CHUNK), CHUNK)
        y = yacc_s[rows, :] + dexp * xc_s[rows, 0:MB_INNER]
        y = y * _silu(z_ref[rows, :])
        y_ref[rows, :] = _rms(y, ng).astype(BF16)
        return 0

    lax.fori_loop(0, nc, fin, 0)
    for d in range(2):
        ht = ht_s[d]
        hfin_ref[d] = ht[0:MB_STATE, :] + ht[MB_STATE:2 * MB_STATE, :]


def _ssd(z, xbc, sm, smt, cw, cb, dtb, a_neg, dexp, ng, h0t, sample):
    nb, L, off = (DEC_BATCH, DEC_SEQ, T_CTX // DEC_SEQ) if sample else (BATCH, SEQ, 0)
    full = lambda a: pl.BlockSpec(a.shape, lambda b: (0,) * a.ndim)
    seq = lambda w: pl.BlockSpec((L, w), lambda b: (off + b, 0))
    dtb_r, dtb_c = dtb.reshape(1, 16), dtb.reshape(16, 1)
    a_r, a_c = a_neg.reshape(1, 16), a_neg.reshape(16, 1)
    sds = jax.ShapeDtypeStruct
    return pl.pallas_call(
        functools.partial(_ssd_kernel, L), grid=(nb,),
        in_specs=[seq(MB_INNER), seq(MB_CONV_DIM), seq(32),
                  pl.BlockSpec((L // CHUNK, 32, CHUNK), lambda b: (off + b, 0, 0)),
                  full(cw), full(cb), full(dtb_r), full(dtb_c), full(a_r), full(a_c),
                  full(dexp), full(ng),
                  pl.BlockSpec((None, 2, MB_STATE, MB_INNER), lambda b: (b, 0, 0, 0))],
        out_specs=[pl.BlockSpec((L, MB_INNER), lambda b: (b, 0)),
                   pl.BlockSpec((None, 2, MB_STATE, MB_INNER), lambda b: (b, 0, 0, 0))],
        out_shape=[sds((nb * L, MB_INNER), BF16), sds((nb, 2, MB_STATE, MB_INNER), F32)],
        scratch_shapes=[pltpu.VMEM((L, MB_CONV_DIM), F32), pltpu.VMEM((L, MB_INNER), F32),
                        pltpu.VMEM((2, LANES, MB_INNER), F32)],
        compiler_params=_cparams(1), name="ssd_smp" if sample else "ssd_ctx",
    )(z, xbc, sm, smt, cw, cb, dtb_r, dtb_c, a_r, a_c, dexp, ng, h0t)


def _mlstm_kernel(L, q_ref, k_ref, v_ref, o_ref, sm_ref, smt_ref, gbr_ref, gbc_ref, ng_ref,
                  c0_ref, n0_ref, m0_ref,
                  y_ref, cf_ref, nf_ref, mf_ref, hacc_s, c_s, n_s):
    nc = L // CHUNK
    low, upp = _tri_masks()
    low_f = low.astype(F32)
    upp_f = upp.astype(F32)
    lane128 = lax.broadcasted_iota(jnp.int32, (CHUNK, LANES), 1)
    lane8 = lax.broadcasted_iota(jnp.int32, (1, 2 * ML_HEADS), 1)
    neg_inf = -jnp.inf

    c_s[...] = c0_ref[...]
    n_s[...] = n0_ref[...]
    m0 = m0_ref[...]

    def chunk_dir(c, d, m_in):
        base = pl.multiple_of(c * CHUNK, CHUNK)
        rows = pl.ds(base, CHUNK)
        pre_col = sm_ref[rows, 16:32] + gbr_ref[...]
        pre_row = smt_ref[c][16:32, :] + gbc_ref[...]
        lf_col = -_softplus(-pre_col)
        lf_row = -_softplus(-pre_row)
        if d == 0:
            b_col = _dot_hi(low_f, lf_col)
            b_row = _dot_hi(lf_row, upp_f)
            mask = low
            last = CHUNK - 1
        else:
            b_col = _dot_hi(upp_f, lf_col)
            b_row = _dot_hi(lf_row, low_f)
            mask = upp
            last = 0
        m_out = m_in
        for pr in range(2):
            qp = q_ref[rows, LANES * pr:LANES * (pr + 1)]
            kp = k_ref[rows, LANES * pr:LANES * (pr + 1)]
            kt = kp.astype(F32).T
            cst = c_s[d, pr]
            nst = n_s[d, pr]
            cb16 = cst.astype(BF16)
            nb16 = jnp.broadcast_to(nst, (LANES, LANES)).astype(BF16)
            c_new = []
            n_new = []
            for hh in range(2):
                h = 2 * pr + hh
                ci = 8 * d + h
                cf = 8 * d + 4 + h
                m_st = m_in[:, 4 * d + h:4 * d + h + 1]
                qm = jnp.where((lane128 < ML_DK) == (hh == 0), qp, jnp.zeros_like(qp))
                vh = v_ref[rows, ML_DV * h:ML_DV * (h + 1)]
                bcol = b_col[:, cf:cf + 1]
                brow = b_row[cf:cf + 1, :]
                li_row = pre_row[ci:ci + 1, :]
                dm = jnp.where(mask, bcol - brow + li_row, neg_inf)
                inter = bcol + m_st
                m_t = jnp.maximum(inter, jnp.max(dm, axis=-1, keepdims=True))
                s_intra = jnp.exp(dm - m_t) * _dot_nt(qm, kp)
                s_inter = jnp.exp(inter - m_t)
                num = s_inter * _dot(qm, cb16) + _dot(s_intra.astype(BF16), vh)
                den = s_inter * _dot(qm, nb16) + jnp.sum(s_intra, axis=-1, keepdims=True)
                hout = num / jnp.maximum(jnp.abs(den), jnp.exp(-m_t))
                cols = slice(ML_DV * h, ML_DV * (h + 1))
                if d == 0:
                    hacc_s[rows, cols] = hout
                else:
                    hacc_s[rows, cols] = hacc_s[rows, cols] + hout
                b_end = b_row[cf:cf + 1, last:last + 1]
                w_end = b_end - brow + li_row
                m_new = jnp.maximum(b_end + m_st, jnp.max(w_end, axis=-1, keepdims=True))
                s_old = jnp.exp(b_end + m_st - m_new)
                s_tok = jnp.exp(w_end - m_new)
                half = slice(ML_DK * hh, ML_DK * (hh + 1))
                kts = kt[half, :] * s_tok
                c_new.append(s_old * cst[half, :] + _dot(kts.astype(BF16), vh))
                n_new.append(s_old * nst[half, :] + jnp.sum(kts, axis=-1, keepdims=True))
                m_out = jnp.where(lane8 == 4 * d + h, m_new, m_out)
            c_s[d, pr] = jnp.concatenate(c_new, axis=0)
            n_s[d, pr] = jnp.concatenate(n_new, axis=0)
        return m_out

    def fwd(c, m):
        return chunk_dir(c, 0, m)

    def bwd(i, m):
        return chunk_dir(nc - 1 - i, 1, m)

    m_fin = lax.fori_loop(0, nc, bwd, lax.fori_loop(0, nc, fwd, m0))

    ng = ng_ref[...]

    def fin(c, _):
        rows = pl.ds(pl.multiple_of(c * CHUNK, CHUNK), CHUNK)
        for h in range(ML_HEADS):
            cols = slice(ML_DV * h, ML_DV * (h + 1))
            y = _rms(hacc_s[rows, cols], ng[:, cols]) * jax.nn.sigmoid(o_ref[rows, cols])
            y_ref[rows, cols] = y.astype(BF16)
        return 0

    lax.fori_loop(0, nc, fin, 0)
    cf_ref[...] = c_s[...]
    nf_ref[...] = n_s[...]
    mf_ref[...] = m_fin


def _mlstm(q, k, v, o, sm, smt, gb, ng, c0, n0, m0, sample):
    nb, L, off = (DEC_BATCH, DEC_SEQ, T_CTX // DEC_SEQ) if sample else (BATCH, SEQ, 0)
    full = lambda a: pl.BlockSpec(a.shape, lambda b: (0,) * a.ndim)
    seq = lambda w: pl.BlockSpec((L, w), lambda b: (off + b, 0))
    gb_r, gb_c = gb.reshape(1, 16), gb.reshape(16, 1)
    st_c = pl.BlockSpec((None, 2, 2, LANES, ML_DV), lambda b: (b, 0, 0, 0, 0))
    st_n = pl.BlockSpec((None, 2, 2, LANES, 1), lambda b: (b, 0, 0, 0, 0))
    st_m = pl.BlockSpec((None, 1, 8), lambda b: (b, 0, 0))
    sds = jax.ShapeDtypeStruct
    return pl.pallas_call(
        functools.partial(_mlstm_kernel, L), grid=(nb,),
        in_specs=[seq(256), seq(256), seq(512), seq(512), seq(32),
                  pl.BlockSpec((L // CHUNK, 32, CHUNK), lambda b: (off + b, 0, 0)),
                  full(gb_r), full(gb_c), full(ng), st_c, st_n, st_m],
        out_specs=[pl.BlockSpec((L, ML_WIDTH), lambda b: (b, 0)), st_c, st_n, st_m],
        out_shape=[sds((nb * L, ML_WIDTH), BF16), sds((nb, 2, 2, LANES, ML_DV), F32),
                   sds((nb, 2, 2, LANES, 1), F32), sds((nb, 1, 8), F32)],
        scratch_shapes=[pltpu.VMEM((L, ML_WIDTH), F32), pltpu.VMEM((2, 2, LANES, ML_DV), F32),
                        pltpu.VMEM((2, 2, LANES, 1), F32)],
        compiler_params=_cparams(1), name="mlstm_smp" if sample else "mlstm_ctx",
    )(q, k, v, o, sm, smt, gb_r, gb_c, ng, c0, n0, m0)


def _merge_kernel(x_ref, mod_ref, g1_ref, yda_ref, ymb_ref, yml_ref, wg_ref, bg_ref, wb_ref,
                  wo_ref, g2_ref, rwt_ref, xo_ref, h2_ref, aff_ref):
    x = x_ref[...]
    mod = mod_ref[...]
    sh1, sc1, gt1, sh2, sc2 = (mod[:, j * D_MODEL:(j + 1) * D_MODEL] for j in range(5))
    hb = (_rms(x, g1_ref[...]) * (1.0 + sc1) + sh1).astype(BF16)
    merged = None
    for n, y_ref in enumerate((yda_ref, ymb_ref, yml_ref)):
        cols = slice(n * D_MODEL, (n + 1) * D_MODEL)
        gate = jax.nn.sigmoid(_dot(hb, wg_ref[:, cols]) + bg_ref[:, cols])
        term = gate * _dot(y_ref[...], wb_ref[n])
        merged = term if merged is None else merged + term
    out = _dot(merged.astype(BF16), wo_ref[...])
    xn = x + gt1 * out
    xo_ref[...] = xn
    h2 = _rms(xn, g2_ref[...]) * (1.0 + sc2) + sh2
    h2_ref[...] = h2.astype(BF16)
    logits = lax.dot_general(rwt_ref[...], h2, (((1,), (1,)), ((), ())),
                             precision=HI, preferred_element_type=F32)
    e = jnp.exp(logits - jnp.max(logits, axis=0, keepdims=True))
    aff_ref[...] = e / jnp.sum(e, axis=0, keepdims=True)


def _merge(x, mod_l, g1, yda, ymb, yml, wg, bg, wb, wo, g2, rwt):
    tile = lambda w: pl.BlockSpec((TM, w), lambda i: (i, 0))
    full = lambda a: pl.BlockSpec(a.shape, lambda i: (0,) * a.ndim)
    sds = jax.ShapeDtypeStruct
    return pl.pallas_call(
        _merge_kernel, grid=(NT_ALL,),
        in_specs=[tile(D_MODEL), pl.BlockSpec((None, 1, 6 * D_MODEL), lambda i: (_mod_row(i), 0, 0)),
                  full(g1), tile(512), tile(512), tile(512), full(wg), full(bg), full(wb),
                  full(wo), full(g2), full(rwt)],
        out_specs=[tile(D_MODEL), tile(D_MODEL), pl.BlockSpec((N_EXPERTS, TM), lambda i: (0, i))],
        out_shape=[sds((T_ALL, D_MODEL), F32), sds((T_ALL, D_MODEL), BF16),
                   sds((N_EXPERTS, T_ALL), F32)],
        compiler_params=_cparams(1), name="merge_out",
    )(x, mod_l, g1, yda, ymb, yml, wg, bg, wb, wo, g2, rwt)


def _expert_kernel(xe_ref, g_ref, wg_ref, wu_ref, wd_ref, o_ref, wg_s, wu_s, wd_s):
    @pl.when(pl.program_id(1) == 0)
    def _():
        wg_s[...] = wg_ref[...].astype(BF16)
        wu_s[...] = wu_ref[...].astype(BF16)
        wd_s[...] = wd_ref[...].astype(BF16)

    xe = xe_ref[...]
    hid = _silu(_dot(xe, wg_s[...])) * _dot(xe, wu_s[...])
    o_ref[...] = _dot(hid.astype(BF16), wd_s[...]) * g_ref[...]


def _experts(l, xe, g, w_gate, w_up, w_down):
    rb = 256
    cap = xe.shape[1]
    row = lambda w, dt: pl.BlockSpec((None, rb, w), lambda e, j: (e, j, 0))
    wsp = pl.BlockSpec((None, None, D_MODEL, EXPERT_FF), lambda e, j: (l, e, 0, 0))
    wsd = pl.BlockSpec((None, None, EXPERT_FF, D_MODEL), lambda e, j: (l, e, 0, 0))
    return pl.pallas_call(
        _expert_kernel, grid=(N_EXPERTS, cap // rb),
        in_specs=[row(D_MODEL, BF16), row(1, F32), wsp, wsp, wsd],
        out_specs=row(D_MODEL, F32),
        out_shape=jax.ShapeDtypeStruct((N_EXPERTS, cap, D_MODEL), F32),
        scratch_shapes=[pltpu.VMEM((D_MODEL, EXPERT_FF), BF16), pltpu.VMEM((D_MODEL, EXPERT_FF), BF16),
                        pltpu.VMEM((EXPERT_FF, D_MODEL), BF16)],
        compiler_params=_cparams(2), name="expert_ffn",
    )(xe, g, w_gate, w_up, w_down)


def _final_kernel(x_ref, moe_ref, mod_ref, g_ref, o_ref):
    g2 = mod_ref[...][:, 5 * D_MODEL:6 * D_MODEL]
    o_ref[...] = _rms(x_ref[...] + g2 * moe_ref[...], g_ref[...])


def _final(x, moe, mod_l, fg, sample):
    n, off = (T_SMP // TM, NT_CTX) if sample else (NT_CTX, 0)
    tile = pl.BlockSpec((TM, D_MODEL), lambda i: (off + i, 0))
    return pl.pallas_call(
        _final_kernel, grid=(n,),
        in_specs=[tile, tile, pl.BlockSpec((None, 1, 6 * D_MODEL), lambda i: (_mod_row(off + i), 0, 0)),
                  pl.BlockSpec(fg.shape, lambda i: (0, 0))],
        out_specs=pl.BlockSpec((TM, D_MODEL), lambda i: (i, 0)),
        out_shape=jax.ShapeDtypeStruct((n * TM, D_MODEL), F32),
        compiler_params=_cparams(1), name="final_norm",
    )(x, moe, mod_l, fg)


def _rope_tables():
    t = jnp.arange(DEC_SEQ)
    pos = jnp.stack([t // GRID_W, t % GRID_W], axis=-1).astype(F32)
    nf = DA_HD // 4
    inv = ROPE_BASE ** (-jnp.arange(nf, dtype=F32) / nf)
    ang = pos[:, :, None] * inv
    cos = jnp.cos(ang)
    sin = jnp.sin(ang)
    cos64 = jnp.stack([cos, cos], axis=2).reshape(DEC_SEQ, DA_HD)
    sin64 = jnp.stack([-sin, sin], axis=2).reshape(DEC_SEQ, DA_HD)
    cos_t = jnp.concatenate([jnp.ones((TM, LANES), F32), jnp.tile(cos64, (1, 2))], axis=0)
    sin_t = jnp.concatenate([jnp.zeros((TM, LANES), F32), jnp.tile(sin64, (1, 2))], axis=0)
    return cos_t, sin_t


def kernel(x_prompt, x_sample, cache_k, cache_v, state_ssm, state_mlstm_c, state_mlstm_n, state_mlstm_m, c, c_ctx, ada_w, ada_b, norm1_g, norm2_g, w_in, da_lambda, da_subln_g, mb_conv_w, mb_conv_b, mb_dt_bias, mb_a_log, mb_d, mb_norm_g, ml_gate_b, ml_norm_g, w_branch, w_mgate, b_mgate, w_out, router_w, ex_w_gate, ex_w_up, ex_w_down, final_g):
    x = jnp.concatenate([x_prompt.reshape(T_CTX, D_MODEL), x_sample.reshape(T_SMP, D_MODEL)], axis=0)
    cc = jnp.concatenate([c_ctx[None, :], c, jnp.zeros((16 - 1 - DEC_BATCH, D_MODEL), F32)], axis=0)
    mod = _modulation(cc, ada_w, ada_b).reshape(DEPTH, 16, 1, 6 * D_MODEL)
    cos_t, sin_t = _rope_tables()
    cache_k2 = cache_k.reshape(DEC_BATCH, DEPTH, PAST_LEN, DA_WIDTH)
    cache_v2 = cache_v.reshape(DEC_BATCH, DEPTH, PAST_LEN, DA_WIDTH)

    outs = {n: [] for n in ("k", "v", "ssm", "C", "n", "m")}
    moe = None
    for l in range(DEPTH):
        w = w_in[l]
        wm = jnp.concatenate([w[:, :2816], w[:, 2832:4368]], axis=1).astype(BF16)
        ws = jnp.concatenate([w[:, 2816:2832], w[:, 4368:4384]], axis=1).astype(BF16)
        res = _projection(x, moe, mod[l - 1] if l else None, mod[l], norm1_g[l][None], wm, ws, ws.T,
                          cos_t, sin_t)
        q, k, v, kf, vf, z, xbc, mq, mk, mv, mo, sm, smt = res[:13]
        if l:
            x = res[13]
        outs["k"].append(kf[:T_CTX].reshape(BATCH, SEQ, DA_HEADS, 2, DA_HD))
        outs["v"].append(vf[:T_CTX].reshape(BATCH, SEQ, DA_HEADS, 2 * DA_HD))

        lp = da_lambda[l]
        sg = da_subln_g[l][None]
        cw = mb_conv_w[l]
        cb = mb_conv_b[l][None]
        dtb = mb_dt_bias[l].reshape(16)
        a_neg = -jnp.exp(mb_a_log[l]).reshape(16)
        dexp = jnp.repeat(mb_d[l], MB_HD)[None]
        mng = mb_norm_g[l][None]
        gb = ml_gate_b[l].reshape(16)
        lng = ml_norm_g[l].reshape(1, ML_WIDTH)
        ys = []
        for sample in (False, True):
            nb = DEC_BATCH if sample else BATCH
            y_da = _attention(l, q, k, v, cache_k2, cache_v2, lp, sg, sample)
            if sample:
                h0 = state_ssm[:, l]
                c0 = state_mlstm_c[:, l]
                n0 = state_mlstm_n[:, l]
                m0 = state_mlstm_m[:, l]
            else:
                h0 = jnp.zeros((nb, 2, MB_HEADS, MB_HD, MB_STATE), F32)
                c0 = jnp.zeros((nb, 2, ML_HEADS, ML_DK, ML_DV), F32)
                n0 = jnp.zeros((nb, 2, ML_HEADS, ML_DK), F32)
                m0 = jnp.zeros((nb, 2, ML_HEADS), F32)
            h0t = jnp.transpose(h0, (0, 1, 4, 2, 3)).reshape(nb, 2, MB_STATE, MB_INNER)
            y_mb, hfin = _ssd(z, xbc, sm, smt, cw, cb, dtb, a_neg, dexp, mng, h0t, sample)
            y_ml, cfin, nfin, mfin = _mlstm(
                mq, mk, mv, mo, sm, smt, gb, lng,
                c0.reshape(nb, 2, 2, LANES, ML_DV), n0.reshape(nb, 2, 2, LANES, 1),
                m0.reshape(nb, 1, 8), sample)
            ys.append((y_da, y_mb, y_ml))
            if not sample:
                outs["ssm"].append(jnp.transpose(
                    hfin.reshape(nb, 2, MB_STATE, MB_HEADS, MB_HD), (0, 1, 3, 4, 2)))
                outs["C"].append(cfin.reshape(nb, 2, ML_HEADS, ML_DK, ML_DV))
                outs["n"].append(nfin.reshape(nb, 2, ML_HEADS, ML_DK))
                outs["m"].append(mfin.reshape(nb, 2, ML_HEADS))
        yda, ymb, yml = (jnp.concatenate([ys[0][j], ys[1][j]], axis=0) for j in range(3))

        x, h2, aff = _merge(x, mod[l], norm1_g[l][None], yda, ymb, yml,
                            w_mgate[l].astype(BF16), b_mgate[l][None], w_branch[l].astype(BF16),
                            w_out[l].astype(BF16), norm2_g[l][None], router_w[l].T)

        xes, gs, idxs = [], [], []
        for lo, hi in ((0, T_CTX), (T_CTX, T_ALL)):
            cap = EC_FACTOR * (hi - lo) // N_EXPERTS
            g, idx = lax.top_k(aff[:, lo:hi], cap)
            idx = idx + lo
            xes.append(h2[idx])
            gs.append(g)
            idxs.append(idx)
        xe = jnp.concatenate(xes, axis=1)
        g = jnp.concatenate(gs, axis=1)[..., None]
        idx = jnp.concatenate(idxs, axis=1)
        ye = _experts(l, xe, g, ex_w_gate, ex_w_up, ex_w_down)
        moe = jnp.zeros((T_ALL, D_MODEL), F32).at[idx.reshape(-1)].add(ye.reshape(-1, D_MODEL))

    fg = final_g[None]
    y_prompt = _final(x, moe, mod[DEPTH - 1], fg, False).reshape(BATCH, SEQ, D_MODEL)
    y_sample = _final(x, moe, mod[DEPTH - 1], fg, True).reshape(DEC_BATCH, DEC_SEQ, D_MODEL)
    return (y_prompt, y_sample, jnp.stack(outs["k"], axis=1), jnp.stack(outs["v"], axis=1),
            jnp.stack(outs["ssm"], axis=1), jnp.stack(outs["C"], axis=1),
            jnp.stack(outs["n"], axis=1), jnp.stack(outs["m"], axis=1))
```

```python
import functools
import math

import jax
import jax.numpy as jnp
from jax import lax
from jax.experimental import pallas as pl
from jax.experimental.pallas import tpu as pltpu

F32 = jnp.float32
BF16 = jnp.bfloat16

D_MODEL = 1024
BATCH = 16
SEQ = 256
DEPTH = 2
DEC_BATCH = 8
DEC_SEQ = 2048
PAST_LEN = 512
GRID_W = 64
EPS = 1e-6
CHUNK = 128
ROPE_BASE = 10000.0
DA_HEADS = 4
DA_HD = 64
DA_WIDTH = 512
MB_INNER = 512
MB_HD = 64
MB_HEADS = 8
MB_GROUPS = 2
MB_STATE = 64
MB_CONV_DIM = 768
ML_HEADS = 4
ML_DK = 64
ML_DV = 128
ML_WIDTH = 512
N_EXPERTS = 16
EC_FACTOR = 2
EXPERT_FF = 1024

T_CTX = BATCH * SEQ
T_SMP = DEC_BATCH * DEC_SEQ
T_ALL = T_CTX + T_SMP
TM = 256
NT_CTX = T_CTX // TM
NT_ALL = T_ALL // TM
NB_CTX = T_CTX // CHUNK
NB_ALL = T_ALL // CHUNK
CAP_CTX = EC_FACTOR * T_CTX // N_EXPERTS
CAP_SMP = EC_FACTOR * T_SMP // N_EXPERTS
CAP_ALL = CAP_CTX + CAP_SMP
RB = 256
N_RB = CAP_ALL // RB
WIN = 64
TILES_PER_REQ = DEC_SEQ // TM
LANES = 128
VMEM_LIMIT = 56 * 1024 * 1024
HI = lax.Precision.HIGHEST

C_Q, C_K, C_V, C_Z, C_XBC, C_MQ, C_MK, C_MV, C_MO, C_END = (
    0, 512, 1024, 1536, 2048, 2816, 3072, 3328, 3840, 4352)


def _mod_row(i):
    return jnp.where(i < NT_CTX, 0, 1 + (i - NT_CTX) // TILES_PER_REQ)


def _rope_blk(i):
    return jnp.where(i < NT_CTX, 0, 1 + (i - NT_CTX) % TILES_PER_REQ)


def _cparams(n_grid):
    return pltpu.CompilerParams(dimension_semantics=("arbitrary",) * n_grid,
                                vmem_limit_bytes=VMEM_LIMIT)


def _silu(x):
    return x * jax.nn.sigmoid(x)


def _softplus(x):
    u = jnp.exp(-jnp.abs(x))
    w = 1.0 + u
    l1p = jnp.where(w == 1.0, u, jnp.log(w) * (u / (w - 1.0)))
    return jnp.maximum(x, 0.0) + l1p


def _dot(a, b):
    return jnp.dot(a, b, preferred_element_type=F32)


def _dot_nt(a, b):
    return lax.dot_general(a, b, (((1,), (1,)), ((), ())), preferred_element_type=F32)


def _dot_hi(a, b):
    return jnp.dot(a, b, precision=HI, preferred_element_type=F32)


def _rms(x, g):
    return x * lax.rsqrt(jnp.mean(x * x, axis=-1, keepdims=True) + EPS) * g


def _mod_kernel(c_ref, w_ref, b_ref, o_ref):
    s = _silu(c_ref[...])
    o_ref[...] = _dot(s.astype(BF16), w_ref[...].astype(BF16)) + b_ref[...]


def _modulation(cc, ada_w, ada_b):
    tn = 1536
    return pl.pallas_call(
        _mod_kernel,
        grid=(DEPTH, 6 * D_MODEL // tn),
        in_specs=[pl.BlockSpec((16, D_MODEL), lambda l, j: (0, 0)),
                  pl.BlockSpec((None, D_MODEL, tn), lambda l, j: (l, 0, j)),
                  pl.BlockSpec((None, 1, tn), lambda l, j: (l, 0, j))],
        out_specs=pl.BlockSpec((None, 16, tn), lambda l, j: (l, 0, j)),
        out_shape=jax.ShapeDtypeStruct((DEPTH, 16, 6 * D_MODEL), F32),
        compiler_params=_cparams(2),
        name="adaln_mod",
    )(cc, ada_w, ada_b.reshape(DEPTH, 1, 6 * D_MODEL))


def _rope(t, cos, sin, first_half):
    outs = []
    for c in range(DA_WIDTH // LANES):
        xc = t[:, LANES * c:LANES * (c + 1)]
        partner = jnp.where(first_half, pltpu.roll(xc, LANES - 16, 1), pltpu.roll(xc, 16, 1))
        outs.append(xc * cos + partner * sin)
    return jnp.concatenate(outs, axis=1)


def _proj_body(x, mod_ref, g_ref, wm_ref, ws_ref, wst_ref, cos_ref, sin_ref,
               q_ref, k_ref, v_ref, kf_ref, vf_ref, z_ref, xbc_ref,
               mq_ref, mk_ref, mv_ref, mo_ref, sm_ref, smt_ref):
    mod = mod_ref[...]
    sh1 = mod[:, 0:D_MODEL]
    sc1 = mod[:, D_MODEL:2 * D_MODEL]
    h = _rms(x, g_ref[...]) * (1.0 + sc1) + sh1
    hb = h.astype(BF16)

    def proj(a, b):
        return _dot(hb, wm_ref[:, a:b])

    lane = lax.broadcasted_iota(jnp.int32, (TM, LANES), 1)
    first_half = (lane % 32) < 16
    cos = cos_ref[...]
    sin = sin_ref[...]
    q = proj(C_Q, C_K)
    k = proj(C_K, C_V)
    v = proj(C_V, C_Z)
    kf_ref[...] = k
    vf_ref[...] = v
    q_ref[...] = (_rope(q, cos, sin, first_half) * (DA_HD ** -0.5)).astype(BF16)
    k_ref[...] = _rope(k, cos, sin, first_half).astype(BF16)
    v_ref[...] = v.astype(BF16)
    z_ref[...] = proj(C_Z, C_XBC)
    xbc_ref[...] = proj(C_XBC, C_MQ)
    mq_ref[...] = proj(C_MQ, C_MK).astype(BF16)
    mk_ref[...] = (proj(C_MK, C_MV) * (ML_DK ** -0.5)).astype(BF16)
    mv_ref[...] = proj(C_MV, C_MO).astype(BF16)
    mo_ref[...] = proj(C_MO, C_END)
    sm_ref[...] = _dot(hb, ws_ref[...])
    st = _dot_nt(wst_ref[...], hb)
    for j in range(TM // CHUNK):
        smt_ref[j] = st[:, CHUNK * j:CHUNK * (j + 1)]


def _proj_kernel_first(x_ref, *refs):
    _proj_body(x_ref[...], *refs)


def _proj_kernel_next(x_ref, moe_ref, modp_ref, *refs):
    g2 = modp_ref[...][:, 5 * D_MODEL:6 * D_MODEL]
    x = x_ref[...] + g2 * moe_ref[...]
    xo_ref = refs[-1]
    xo_ref[...] = x
    _proj_body(x, *refs[:-1])


def _projection(x, moe_prev, mod_prev, mod_l, g1, wm, ws, wst, cos_t, sin_t):
    tile = lambda w: pl.BlockSpec((TM, w), lambda i: (i, 0))
    full = lambda a: pl.BlockSpec(a.shape, lambda i: (0,) * a.ndim)
    modspec = pl.BlockSpec((None, 1, 6 * D_MODEL), lambda i: (_mod_row(i), 0, 0))
    in_specs = [tile(D_MODEL)]
    args = [x]
    if moe_prev is not None:
        in_specs += [tile(D_MODEL), modspec]
        args += [moe_prev, mod_prev]
    in_specs += [modspec, full(g1), full(wm), full(ws), full(wst),
                 pl.BlockSpec((TM, LANES), lambda i: (_rope_blk(i), 0)),
                 pl.BlockSpec((TM, LANES), lambda i: (_rope_blk(i), 0))]
    args += [mod_l, g1, wm, ws, wst, cos_t, sin_t]
    ctx_only = pl.BlockSpec((TM, DA_WIDTH), lambda i: (jnp.minimum(i, NT_CTX), 0))
    out_specs = [tile(512), tile(512), tile(512), ctx_only, ctx_only, tile(512), tile(768),
                 tile(256), tile(256), tile(512), tile(512), tile(32),
                 pl.BlockSpec((TM // CHUNK, 32, CHUNK), lambda i: (i, 0, 0))]
    sds = jax.ShapeDtypeStruct
    out_shape = [sds((T_ALL, 512), BF16), sds((T_ALL, 512), BF16), sds((T_ALL, 512), BF16),
                 sds((T_CTX + TM, 512), F32), sds((T_CTX + TM, 512), F32),
                 sds((T_ALL, 512), F32), sds((T_ALL, 768), F32),
                 sds((T_ALL, 256), BF16), sds((T_ALL, 256), BF16), sds((T_ALL, 512), BF16),
                 sds((T_ALL, 512), F32), sds((T_ALL, 32), F32),
                 sds((T_ALL // CHUNK, 32, CHUNK), F32)]
    kern = _proj_kernel_first
    if moe_prev is not None:
        out_specs.append(tile(D_MODEL))
        out_shape.append(sds((T_ALL, D_MODEL), F32))
        kern = _proj_kernel_next
    return pl.pallas_call(
        kern, grid=(NT_ALL,), in_specs=in_specs, out_specs=out_specs, out_shape=out_shape,
        compiler_params=_cparams(1), name="in_proj",
    )(*args)


def _attn_body(lam_init, q_ref, k_ref, v_ref, kc_ref, vc_ref, lp_ref, g_ref, o_ref):
    lp = lp_ref[...]
    s01 = jnp.sum(lp[0:1] * lp[1:2], axis=-1, keepdims=True)
    s23 = jnp.sum(lp[2:3] * lp[3:4], axis=-1, keepdims=True)
    lam = jnp.exp(s01) - jnp.exp(s23) + lam_init
    tq = q_ref.shape[0]
    lane = lax.broadcasted_iota(jnp.int32, (tq, LANES), 1)
    g = g_ref[...]
    for h in range(DA_HEADS):
        cols = slice(LANES * h, LANES * (h + 1))
        qh = q_ref[:, cols]
        kh = k_ref[:, cols]
        vh = v_ref[:, cols]
        if kc_ref is not None:
            kch = kc_ref[:, cols].astype(BF16)
            vch = vc_ref[:, cols].astype(BF16)
        parts = []
        for m in range(2):
            qm = jnp.where((lane < DA_HD) == (m == 0), qh, jnp.zeros_like(qh))
            s = _dot_nt(qm, kh)
            mx = jnp.max(s, axis=-1, keepdims=True)
            if kc_ref is not None:
                sc = _dot_nt(qm, kch)
                mx = jnp.maximum(mx, jnp.max(sc, axis=-1, keepdims=True))
                ec = jnp.exp(sc - mx)
            e = jnp.exp(s - mx)
            den = jnp.sum(e, axis=-1, keepdims=True)
            acc = _dot(e.astype(BF16), vh)
            if kc_ref is not None:
                den = den + jnp.sum(ec, axis=-1, keepdims=True)
                acc = acc + _dot(ec.astype(BF16), vch)
            parts.append(acc / den)
        att = parts[0] - lam * parts[1]
        o_ref[:, cols] = (_rms(att, g) * (1.0 - lam_init)).astype(BF16)


def _attn_kernel_ctx(lam_init, q_ref, k_ref, v_ref, lp_ref, g_ref, o_ref):
    _attn_body(lam_init, q_ref, k_ref, v_ref, None, None, lp_ref, g_ref, o_ref)


def _attn_kernel_smp(lam_init, q_ref, k_ref, v_ref, kc_ref, vc_ref, lp_ref, g_ref, o_ref):
    _attn_body(lam_init, q_ref, k_ref, v_ref, kc_ref, vc_ref, lp_ref, g_ref, o_ref)


def _attention(l, q, k, v, cache_k, cache_v, lp, g, sample):
    lam_init = 0.8 - 0.6 * math.exp(-0.3 * l)
    full = lambda a: pl.BlockSpec(a.shape, lambda *_: (0,) * a.ndim)
    sds = jax.ShapeDtypeStruct
    if not sample:
        blk = pl.BlockSpec((SEQ, DA_WIDTH), lambda b: (b, 0))
        return pl.pallas_call(
            functools.partial(_attn_kernel_ctx, lam_init), grid=(BATCH,),
            in_specs=[blk, blk, blk, full(lp), full(g)],
            out_specs=blk, out_shape=sds((T_CTX, DA_WIDTH), BF16),
            compiler_params=_cparams(1), name="diff_attn_ctx",
        )(q, k, v, lp, g)
    tq = 256
    nq = DEC_SEQ // tq
    off_q = T_CTX // tq
    off_k = T_CTX // DEC_SEQ
    qblk = pl.BlockSpec((tq, DA_WIDTH), lambda b, i: (off_q + b * nq + i, 0))
    kblk = pl.BlockSpec((DEC_SEQ, DA_WIDTH), lambda b, i: (off_k + b, 0))
    cblk = pl.BlockSpec((None, None, PAST_LEN, DA_WIDTH), lambda b, i: (b, l, 0, 0))
    return pl.pallas_call(
        functools.partial(_attn_kernel_smp, lam_init), grid=(DEC_BATCH, nq),
        in_specs=[qblk, kblk, kblk, cblk, cblk, full(lp), full(g)],
        out_specs=pl.BlockSpec((tq, DA_WIDTH), lambda b, i: (b * nq + i, 0)),
        out_shape=sds((T_SMP, DA_WIDTH), BF16),
        compiler_params=_cparams(2), name="diff_attn_smp",
    )(q, k, v, cache_k, cache_v, lp, g)


def _tri_masks():
    r = lax.broadcasted_iota(jnp.int32, (CHUNK, CHUNK), 0)
    c = lax.broadcasted_iota(jnp.int32, (CHUNK, CHUNK), 1)
    return c <= r, c >= r


def _ssd_kernel(L, z_ref, xbc_ref, sm_ref, smt_ref, cw_ref, cb_ref, dtbr_ref, dtbc_ref,
                ar_ref, ac_ref, dexp_ref, ng_ref, h0_ref,
                y_ref, hfin_ref, xc_s, yacc_s, ht_s):
    nc = L // CHUNK
    low, upp = _tri_masks()
    low_f = low.astype(F32)
    upp_f = upp.astype(F32)
    lane512 = lax.broadcasted_iota(jnp.int32, (1, MB_INNER), 1)
    row16 = lax.broadcasted_iota(jnp.int32, (16, MB_INNER), 0)
    lane16 = lax.broadcasted_iota(jnp.int32, (16, MB_INNER), 1)
    lane128 = lax.broadcasted_iota(jnp.int32, (CHUNK, LANES), 1)
    rowblk = lax.broadcasted_iota(jnp.int32, (LANES, MB_INNER), 0) // MB_STATE
    colblk = lax.broadcasted_iota(jnp.int32, (LANES, MB_INNER), 1) // (MB_INNER // MB_GROUPS)
    same_group = rowblk == colblk
    cw = cw_ref[...]
    cbias = cb_ref[...]

    def conv_chunk(c, _):
        base = pl.multiple_of(c * CHUNK, CHUNK)
        x = xbc_ref[pl.ds(base, CHUNK), :]
        prev = xbc_ref[pl.ds(jnp.maximum(base - 1, 0), 1), :]
        nxt = xbc_ref[pl.ds(jnp.minimum(base + CHUNK, L - 1), 1), :]
        prev = jnp.where(c == 0, 0.0, prev)
        nxt = jnp.where(c == nc - 1, 0.0, nxt)
        row = lax.broadcasted_iota(jnp.int32, (CHUNK, 1), 0)
        xp = jnp.where(row == 0, prev, pltpu.roll(x, 1, 0))
        xn = jnp.where(row == CHUNK - 1, nxt, pltpu.roll(x, CHUNK - 1, 0))
        conv = xp * cw[0:1] + x * cw[1:2] + xn * cw[2:3] + cbias
        xc_s[pl.ds(base, CHUNK), :] = _silu(conv)
        return 0

    lax.fori_loop(0, nc, conv_chunk, 0)

    h0 = h0_ref[...]
    for d in range(2):
        h0d = h0[d]
        ht_s[d] = jnp.concatenate(
            [jnp.where(lane512 < MB_INNER // 2, h0d, 0.0),
             jnp.where(lane512 >= MB_INNER // 2, h0d, 0.0)], axis=0)

    def chunk_dir(c, d):
        base = pl.multiple_of(c * CHUNK, CHUNK)
        rows = pl.ds(base, CHUNK)
        xs = xc_s[rows, 0:MB_INNER]
        bm = xc_s[rows, MB_INNER:MB_INNER + LANES]
        cm = xc_s[rows, MB_INNER + LANES:MB_CONV_DIM]
        dt_col = _softplus(sm_ref[rows, 0:16] + dtbr_ref[...])
        a_col = dt_col * ar_ref[...]
        dt_row = _softplus(smt_ref[c][0:16, :] + dtbc_ref[...])
        a_row = dt_row * ac_ref[...]
        if d == 0:
            cum_col = _dot_hi(low_f, a_col)
            cum_row = _dot_hi(a_row, upp_f)
            mask = low
            last = CHUNK - 1
        else:
            cum_col = _dot_hi(upp_f, a_col)
            cum_row = _dot_hi(a_row, low_f)
            mask = upp
            last = 0
        cum_last = cum_col[last:last + 1, :]
        expand = (row16 == 8 * d + lane16 // MB_HD).astype(F32)
        w_exp = _dot_hi(jnp.exp(cum_last - cum_col) * dt_col, expand)
        g_exp = _dot_hi(jnp.exp(cum_col), expand)
        cd_exp = _dot_hi(jnp.broadcast_to(jnp.exp(cum_last), (8, 16)), expand)[0:1]
        xw = (xs * w_exp).astype(BF16)
        xsb = xs.astype(BF16)
        bb = bm.astype(BF16)
        cb = cm.astype(BF16)
        ht = ht_s[d]
        y_off = _dot(cb, ht.astype(BF16)) * g_exp
        s_new = _dot(bm.T.astype(BF16), xw)
        ht_s[d] = ht * cd_exp + jnp.where(same_group, s_new, 0.0)
        pairs = []
        for g in range(MB_GROUPS):
            cg = jnp.where((lane128 < MB_STATE) == (g == 0), cb, jnp.zeros_like(cb))
            cbg = _dot_nt(cg, bb)
            for pr in range(2):
                ys = []
                for hh in range(2):
                    ci = 8 * d + 4 * g + 2 * pr + hh
                    seg = cum_col[:, ci:ci + 1] - cum_row[ci:ci + 1, :]
                    m = jnp.where(mask, jnp.exp(seg), 0.0) * cbg * dt_row[ci:ci + 1, :]
                    k = 2 * g + pr
                    ys.append(_dot(m.astype(BF16), xsb[:, LANES * k:LANES * (k + 1)]))
                pairs.append(jnp.where(lane128 < MB_HD, ys[0], ys[1]))
        y = jnp.concatenate(pairs, axis=1) + y_off
        if d == 0:
            yacc_s[rows, :] = y
        else:
            yacc_s[rows, :] = yacc_s[rows, :] + y

    def fwd(c, _):
        chunk_dir(c, 0)
        return 0

    def bwd(i, _):
        chunk_dir(nc - 1 - i, 1)
        return 0

    lax.fori_loop(0, nc, fwd, 0)
    lax.fori_loop(0, nc, bwd, 0)

    dexp = dexp_ref[...]
    ng = ng_ref[...]

    def fin(c, _):
        rows = pl.ds(pl.multiple_of(c * CHUNK, CHUNK), CHUNK)
        y = yacc_s[rows, :] + dexp * xc_s[rows, 0:MB_INNER]
        y = y * _silu(z_ref[rows, :])
        y_ref[rows, :] = _rms(y, ng).astype(BF16)
        return 0

    lax.fori_loop(0, nc, fin, 0)
    for d in range(2):
        ht = ht_s[d]
        hfin_ref[d] = ht[0:MB_STATE, :] + ht[MB_STATE:2 * MB_STATE, :]


def _ssd(z, xbc, sm, smt, cw, cb, dtb, a_neg, dexp, ng, h0t, sample):
    nb, L, off = (DEC_BATCH, DEC_SEQ, T_CTX // DEC_SEQ) if sample else (BATCH, SEQ, 0)
    full = lambda a: pl.BlockSpec(a.shape, lambda b: (0,) * a.ndim)
    seq = lambda w: pl.BlockSpec((L, w), lambda b: (off + b, 0))
    dtb_r, dtb_c = dtb.reshape(1, 16), dtb.reshape(16, 1)
    a_r, a_c = a_neg.reshape(1, 16), a_neg.reshape(16, 1)
    sds = jax.ShapeDtypeStruct
    return pl.pallas_call(
        functools.partial(_ssd_kernel, L), grid=(nb,),
        in_specs=[seq(MB_INNER), seq(MB_CONV_DIM), seq(32),
                  pl.BlockSpec((L // CHUNK, 32, CHUNK), lambda b: (off + b, 0, 0)),
                  full(cw), full(cb), full(dtb_r), full(dtb_c), full(a_r), full(a_c),
                  full(dexp), full(ng),
                  pl.BlockSpec((None, 2, MB_STATE, MB_INNER), lambda b: (b, 0, 0, 0))],
        out_specs=[pl.BlockSpec((L, MB_INNER), lambda b: (b, 0)),
                   pl.BlockSpec((None, 2, MB_STATE, MB_INNER), lambda b: (b, 0, 0, 0))],
        out_shape=[sds((nb * L, MB_INNER), BF16), sds((nb, 2, MB_STATE, MB_INNER), F32)],
        scratch_shapes=[pltpu.VMEM((L, MB_CONV_DIM), F32), pltpu.VMEM((L, MB_INNER), F32),
                        pltpu.VMEM((2, LANES, MB_INNER), F32)],
        compiler_params=_cparams(1), name="ssd_smp" if sample else "ssd_ctx",
    )(z, xbc, sm, smt, cw, cb, dtb_r, dtb_c, a_r, a_c, dexp, ng, h0t)


def _mlstm_kernel(L, q_ref, k_ref, v_ref, o_ref, sm_ref, smt_ref, gbr_ref, gbc_ref, ng_ref,
                  c0_ref, n0_ref, m0_ref,
                  y_ref, cf_ref, nf_ref, mf_ref, hacc_s, c_s, n_s):
    nc = L // CHUNK
    low, upp = _tri_masks()
    low_f = low.astype(F32)
    upp_f = upp.astype(F32)
    lane128 = lax.broadcasted_iota(jnp.int32, (CHUNK, LANES), 1)
    lane8 = lax.broadcasted_iota(jnp.int32, (1, 2 * ML_HEADS), 1)
    neg_inf = -jnp.inf

    c_s[...] = c0_ref[...]
    n_s[...] = n0_ref[...]
    m0 = m0_ref[...]

    def chunk_dir(c, d, m_in):
        base = pl.multiple_of(c * CHUNK, CHUNK)
        rows = pl.ds(base, CHUNK)
        pre_col = sm_ref[rows, 16:32] + gbr_ref[...]
        pre_row = smt_ref[c][16:32, :] + gbc_ref[...]
        lf_col = -_softplus(-pre_col)
        lf_row = -_softplus(-pre_row)
        if d == 0:
            b_col = _dot_hi(low_f, lf_col)
            b_row = _dot_hi(lf_row, upp_f)
            mask = low
            last = CHUNK - 1
        else:
            b_col = _dot_hi(upp_f, lf_col)
            b_row = _dot_hi(lf_row, low_f)
            mask = upp
            last = 0
        m_out = m_in
        for pr in range(2):
            qp = q_ref[rows, LANES * pr:LANES * (pr + 1)]
            kp = k_ref[rows, LANES * pr:LANES * (pr + 1)]
            kt = kp.astype(F32).T
            cst = c_s[d, pr]
            nst = n_s[d, pr]
            cb16 = cst.astype(BF16)
            nb16 = jnp.broadcast_to(nst, (LANES, LANES)).astype(BF16)
            c_new = []
            n_new = []
            for hh in range(2):
                h = 2 * pr + hh
                ci = 8 * d + h
                cf = 8 * d + 4 + h
                m_st = m_in[:, 4 * d + h:4 * d + h + 1]
                qm = jnp.where((lane128 < ML_DK) == (hh == 0), qp, jnp.zeros_like(qp))
                vh = v_ref[rows, ML_DV * h:ML_DV * (h + 1)]
                bcol = b_col[:, cf:cf + 1]
                brow = b_row[cf:cf + 1, :]
                li_row = pre_row[ci:ci + 1, :]
                dm = jnp.where(mask, bcol - brow + li_row, neg_inf)
                inter = bcol + m_st
                m_t = jnp.maximum(inter, jnp.max(dm, axis=-1, keepdims=True))
                s_intra = jnp.exp(dm - m_t) * _dot_nt(qm, kp)
                s_inter = jnp.exp(inter - m_t)
                num = s_inter * _dot(qm, cb16) + _dot(s_intra.astype(BF16), vh)
                den = s_inter * _dot(qm, nb16) + jnp.sum(s_intra, axis=-1, keepdims=True)
                hout = num / jnp.maximum(jnp.abs(den), jnp.exp(-m_t))
                cols = slice(ML_DV * h, ML_DV * (h + 1))
                if d == 0:
                    hacc_s[rows, cols] = hout
                else:
                    hacc_s[rows, cols] = hacc_s[rows, cols] + hout
                b_end = b_row[cf:cf + 1, last:last + 1]
                w_end = b_end - brow + li_row
                m_new = jnp.maximum(b_end + m_st, jnp.max(w_end, axis=-1, keepdims=True))
                s_old = jnp.exp(b_end + m_st - m_new)
                s_tok = jnp.exp(w_end - m_new)
                half = slice(ML_DK * hh, ML_DK * (hh + 1))
                kts = kt[half, :] * s_tok
                c_new.append(s_old * cst[half, :] + _dot(kts.astype(BF16), vh))
                n_new.append(s_old * nst[half, :] + jnp.sum(kts, axis=-1, keepdims=True))
                m_out = jnp.where(lane8 == 4 * d + h, m_new, m_out)
            c_s[d, pr] = jnp.concatenate(c_new, axis=0)
            n_s[d, pr] = jnp.concatenate(n_new, axis=0)
        return m_out

    def fwd(c, m):
        return chunk_dir(c, 0, m)

    def bwd(i, m):
        return chunk_dir(nc - 1 - i, 1, m)

    m_fin = lax.fori_loop(0, nc, bwd, lax.fori_loop(0, nc, fwd, m0))

    ng = ng_ref[...]

    def fin(c, _):
        rows = pl.ds(pl.multiple_of(c * CHUNK, CHUNK), CHUNK)
        for h in range(ML_HEADS):
            cols = slice(ML_DV * h, ML_DV * (h + 1))
            y = _rms(hacc_s[rows, cols], ng[:, cols]) * jax.nn.sigmoid(o_ref[rows, cols])
            y_ref[rows, cols] = y.astype(BF16)
        return 0

    lax.fori_loop(0, nc, fin, 0)
    cf_ref[...] = c_s[...]
    nf_ref[...] = n_s[...]
    mf_ref[...] = m_fin


def _mlstm(q, k, v, o, sm, smt, gb, ng, c0, n0, m0, sample):
    nb, L, off = (DEC_BATCH, DEC_SEQ, T_CTX // DEC_SEQ) if sample else (BATCH, SEQ, 0)
    full = lambda a: pl.BlockSpec(a.shape, lambda b: (0,) * a.ndim)
    seq = lambda w: pl.BlockSpec((L, w), lambda b: (off + b, 0))
    gb_r, gb_c = gb.reshape(1, 16), gb.reshape(16, 1)
    st_c = pl.BlockSpec((None, 2, 2, LANES, ML_DV), lambda b: (b, 0, 0, 0, 0))
    st_n = pl.BlockSpec((None, 2, 2, LANES, 1), lambda b: (b, 0, 0, 0, 0))
    st_m = pl.BlockSpec((None, 1, 8), lambda b: (b, 0, 0))
    sds = jax.ShapeDtypeStruct
    return pl.pallas_call(
        functools.partial(_mlstm_kernel, L), grid=(nb,),
        in_specs=[seq(256), seq(256), seq(512), seq(512), seq(32),
                  pl.BlockSpec((L // CHUNK, 32, CHUNK), lambda b: (off + b, 0, 0)),
                  full(gb_r), full(gb_c), full(ng), st_c, st_n, st_m],
        out_specs=[pl.BlockSpec((L, ML_WIDTH), lambda b: (b, 0)), st_c, st_n, st_m],
        out_shape=[sds((nb * L, ML_WIDTH), BF16), sds((nb, 2, 2, LANES, ML_DV), F32),
                   sds((nb, 2, 2, LANES, 1), F32), sds((nb, 1, 8), F32)],
        scratch_shapes=[pltpu.VMEM((L, ML_WIDTH), F32), pltpu.VMEM((2, 2, LANES, ML_DV), F32),
                        pltpu.VMEM((2, 2, LANES, 1), F32)],
        compiler_params=_cparams(1), name="mlstm_smp" if sample else "mlstm_ctx",
    )(q, k, v, o, sm, smt, gb_r, gb_c, ng, c0, n0, m0)


def _merge_kernel(x_ref, mod_ref, g1_ref, yda_ref, ymb_ref, yml_ref, wg_ref, bg_ref, wb_ref,
                  wo_ref, g2_ref, rw_ref, rwt_ref, xo_ref, h2_ref, affb_ref, afft_ref):
    x = x_ref[...]
    mod = mod_ref[...]
    sh1, sc1, gt1, sh2, sc2 = (mod[:, j * D_MODEL:(j + 1) * D_MODEL] for j in range(5))
    hb = (_rms(x, g1_ref[...]) * (1.0 + sc1) + sh1).astype(BF16)
    merged = None
    for n, y_ref in enumerate((yda_ref, ymb_ref, yml_ref)):
        cols = slice(n * D_MODEL, (n + 1) * D_MODEL)
        gate = jax.nn.sigmoid(_dot(hb, wg_ref[:, cols]) + bg_ref[:, cols])
        term = gate * _dot(y_ref[...], wb_ref[n])
        merged = term if merged is None else merged + term
    out = _dot(merged.astype(BF16), wo_ref[...])
    xn = x + gt1 * out
    xo_ref[...] = xn
    h2 = _rms(xn, g2_ref[...]) * (1.0 + sc2) + sh2
    h2_ref[...] = h2.astype(BF16)
    logits = lax.dot_general(rwt_ref[...], h2, (((1,), (1,)), ((), ())),
                             precision=HI, preferred_element_type=F32)
    e = jnp.exp(logits - jnp.max(logits, axis=0, keepdims=True))
    aff = e / jnp.sum(e, axis=0, keepdims=True)
    for j in range(TM // CHUNK):
        affb_ref[j] = aff[:, CHUNK * j:CHUNK * (j + 1)]
    lt = _dot_hi(h2, rw_ref[...])
    et = jnp.exp(lt - jnp.max(lt, axis=-1, keepdims=True))
    afft_ref[...] = et / jnp.sum(et, axis=-1, keepdims=True)


def _merge(x, mod_l, g1, yda, ymb, yml, wg, bg, wb, wo, g2, rw, rwt):
    tile = lambda w: pl.BlockSpec((TM, w), lambda i: (i, 0))
    full = lambda a: pl.BlockSpec(a.shape, lambda i: (0,) * a.ndim)
    sds = jax.ShapeDtypeStruct
    return pl.pallas_call(
        _merge_kernel, grid=(NT_ALL,),
        in_specs=[tile(D_MODEL), pl.BlockSpec((None, 1, 6 * D_MODEL), lambda i: (_mod_row(i), 0, 0)),
                  full(g1), tile(512), tile(512), tile(512), full(wg), full(bg), full(wb),
                  full(wo), full(g2), full(rw), full(rwt)],
        out_specs=[tile(D_MODEL), tile(D_MODEL),
                   pl.BlockSpec((TM // CHUNK, N_EXPERTS, CHUNK), lambda i: (i, 0, 0)),
                   tile(N_EXPERTS)],
        out_shape=[sds((T_ALL, D_MODEL), F32), sds((T_ALL, D_MODEL), BF16),
                   sds((NB_ALL, N_EXPERTS, CHUNK), F32), sds((T_ALL, N_EXPERTS), F32)],
        compiler_params=_cparams(1), name="merge_out",
    )(x, mod_l, g1, yda, ymb, yml, wg, bg, wb, wo, g2, rw, rwt)


def _route_kernel(affb_ref, slotb_ref, slott_ref, a_ref, jlo_ref, jhi_ref, acc_s, run_s):
    r = lax.broadcasted_iota(jnp.int32, (CHUNK, CHUNK), 0)
    c = lax.broadcasted_iota(jnp.int32, (CHUNK, CHUNK), 1)
    upper = (r <= c).astype(BF16)
    eye = (r == c).astype(F32)
    lane = lax.broadcasted_iota(jnp.int32, (N_EXPERTS, LANES), 1)
    acc_s[...] = jnp.zeros_like(acc_s)
    run_s[...] = jnp.zeros_like(run_s)
    for b0, b1, cap in ((0, NB_CTX, CAP_CTX), (NB_CTX, NB_ALL, CAP_SMP)):
        aff = affb_ref[b0:b1]

        def search(i, thr_bits):
            cand = thr_bits | lax.shift_left(jnp.int32(1), 30 - i)
            cnt = jnp.sum((aff >= pltpu.bitcast(cand, F32)[None]).astype(jnp.int32), axis=0)
            cnt = jnp.sum(cnt, axis=1, keepdims=True)
            return jnp.where(cnt >= cap, cand, thr_bits)

        thr = pltpu.bitcast(lax.fori_loop(0, 31, search, jnp.zeros((N_EXPERTS, 1), jnp.int32)), F32)
        n_gt = jnp.sum(jnp.sum((aff > thr[None]).astype(jnp.int32), axis=0), axis=1, keepdims=True)
        need = (cap - n_gt).astype(F32)

        run_s[1] = jnp.zeros((N_EXPERTS, 1), F32)

        def blk(b, _):
            run_sel = run_s[0]
            run_eq = run_s[1]
            x = affb_ref[b]
            eq = x == thr
            eq_f = jnp.where(eq, 1.0, 0.0)
            eq_incl = _dot(eq_f.astype(BF16), upper)
            sel = (x > thr) | (eq & (run_eq + eq_incl - eq_f < need))
            sel_f = jnp.where(sel, 1.0, 0.0)
            sel_incl = _dot(sel_f.astype(BF16), upper)
            slot = jnp.where(sel, run_sel + sel_incl - sel_f, -1.0)
            slotb_ref[b] = slot.astype(jnp.int32)
            slott_ref[pl.ds(pl.multiple_of(b * CHUNK, CHUNK), CHUNK), :] = lax.dot_general(
                eye, slot, (((1,), (1,)), ((), ())), precision=HI, preferred_element_type=F32)
            acc_s[...] = jnp.where((lane == b // 2) & (b % 2 == 0), run_sel, acc_s[...])
            run_s[0] = run_sel + sel_incl[:, CHUNK - 1:CHUNK]
            run_s[1] = run_eq + eq_incl[:, CHUNK - 1:CHUNK]
            return 0

        lax.fori_loop(b0, b1, blk, 0)
    a_acc = jnp.where(lane == NT_ALL, run_s[0], acc_s[...])
    a_ref[...] = a_acc.astype(jnp.int32)
    a_next = pltpu.roll(a_acc, LANES - 1, 1)
    tile_ok = lane < NT_ALL
    jlo = jnp.zeros((N_EXPERTS, LANES), jnp.int32)
    jhi = jnp.zeros((N_EXPERTS, LANES), jnp.int32)
    for k in range(N_RB):
        lo_k = jnp.sum((tile_ok & (a_next <= float(RB * k))).astype(jnp.int32), axis=1, keepdims=True)
        hi_k = jnp.sum((tile_ok & (a_acc < float(RB * (k + 1)))).astype(jnp.int32), axis=1, keepdims=True) - 1
        jlo = jnp.where(lane == k, lo_k, jlo)
        jhi = jnp.where(lane == k, hi_k, jhi)
    jlo_ref[...] = jlo
    jhi_ref[...] = jhi


def _route(affb):
    sds = jax.ShapeDtypeStruct
    small = sds((N_EXPERTS, LANES), jnp.int32)
    return pl.pallas_call(
        _route_kernel,
        out_shape=[sds((NB_ALL, N_EXPERTS, CHUNK), jnp.int32), sds((T_ALL, N_EXPERTS), F32),
                   small, small, small],
        scratch_shapes=[pltpu.VMEM((N_EXPERTS, LANES), F32), pltpu.VMEM((2, N_EXPERTS, 1), F32)],
        compiler_params=pltpu.CompilerParams(vmem_limit_bytes=VMEM_LIMIT), name="route",
    )(affb)


def _expert_kernel(jlo_ref, jhi_ref, h2_hbm, slotb_ref, wg_ref, wu_ref, wd_ref, o_ref,
                   wg_s, wu_s, wd_s, tile_s, sem, acc_s):
    e = pl.program_id(0)
    k = pl.program_id(1)

    @pl.when(k == 0)
    def _():
        wg_s[...] = wg_ref[...].astype(BF16)
        wu_s[...] = wu_ref[...].astype(BF16)
        wd_s[...] = wd_ref[...].astype(BF16)

    jlo = jnp.clip(jlo_ref[e * LANES + k], 0, NT_ALL - 1)
    n = jnp.clip(jhi_ref[e * LANES + k] - jlo + 1, 1, NT_ALL - jlo)

    def tile_copy(j, buf):
        return pltpu.make_async_copy(h2_hbm.at[pl.ds(pl.multiple_of(j * TM, TM), TM), :],
                                     tile_s.at[buf], sem.at[buf])

    tile_copy(jlo, 0).start()
    acc_s[...] = jnp.zeros_like(acc_s)
    want = lax.broadcasted_iota(jnp.int32, (RB, TM), 0) + k * RB

    def body(i, _):
        buf = i % 2
        j = jlo + i
        tile_copy(j, buf).wait()

        @pl.when(i + 1 < n)
        def _():
            tile_copy(j + 1, 1 - buf).start()

        srow = jnp.concatenate([slotb_ref[2 * j, pl.ds(e, 1), :],
                                slotb_ref[2 * j + 1, pl.ds(e, 1), :]], axis=1)
        onehot = jnp.where(srow == want, 1.0, 0.0).astype(BF16)
        acc_s[...] += _dot(onehot, tile_s[buf])
        return 0

    lax.fori_loop(0, n, body, 0)
    xe = acc_s[...].astype(BF16)
    hid = _silu(_dot(xe, wg_s[...])) * _dot(xe, wu_s[...])
    o_ref[...] = _dot(hid.astype(BF16), wd_s[...])


def _experts(l, jlo, jhi, h2, slotb, w_gate, w_up, w_down):
    wsp = pl.BlockSpec((None, None, D_MODEL, EXPERT_FF), lambda e, j, *_: (l, e, 0, 0))
    wsd = pl.BlockSpec((None, None, EXPERT_FF, D_MODEL), lambda e, j, *_: (l, e, 0, 0))
    grid_spec = pltpu.PrefetchScalarGridSpec(
        num_scalar_prefetch=2, grid=(N_EXPERTS, N_RB),
        in_specs=[pl.BlockSpec(memory_space=pl.ANY),
                  pl.BlockSpec(slotb.shape, lambda e, j, *_: (0, 0, 0)), wsp, wsp, wsd],
        out_specs=pl.BlockSpec((None, RB, D_MODEL), lambda e, j, *_: (e, j, 0)),
        scratch_shapes=[pltpu.VMEM((D_MODEL, EXPERT_FF), BF16), pltpu.VMEM((D_MODEL, EXPERT_FF), BF16),
                        pltpu.VMEM((EXPERT_FF, D_MODEL), BF16), pltpu.VMEM((2, TM, D_MODEL), BF16),
                        pltpu.SemaphoreType.DMA((2,)), pltpu.VMEM((RB, D_MODEL), F32)])
    return pl.pallas_call(
        _expert_kernel, grid_spec=grid_spec,
        out_shape=jax.ShapeDtypeStruct((N_EXPERTS, CAP_ALL, D_MODEL), F32),
        compiler_params=_cparams(2), name="expert_ffn",
    )(jlo, jhi, h2, slotb, w_gate, w_up, w_down)


def _combine_kernel(a_ref, ye_hbm, slott_ref, afft_ref, o_ref, win_s, sem, xwin_s, xsem):
    j = pl.program_id(0)
    nt = pl.num_programs(0)

    def first_row(e, jj):
        a = a_ref[e * LANES + jj]
        return pl.multiple_of(jnp.clip((a // 8) * 8, 0, CAP_ALL - WIN), 8)

    def win_copy(e, row0, buf):
        return pltpu.make_async_copy(ye_hbm.at[e, pl.ds(row0, WIN), :], win_s.at[buf, e], sem.at[buf, e])

    buf = j % 2

    @pl.when(j == 0)
    def _():
        for e in range(N_EXPERTS):
            win_copy(e, first_row(e, 0), 0).start()

    @pl.when(j + 1 < nt)
    def _():
        for e in range(N_EXPERTS):
            win_copy(e, first_row(e, j + 1), 1 - buf).start()

    lane = lax.broadcasted_iota(jnp.int32, (TM, WIN), 1).astype(F32)
    o_ref[...] = jnp.zeros_like(o_ref)

    def place(rows, row0, scol, gcol):
        onehot = jnp.where(scol - row0.astype(F32) == lane, 1.0, 0.0).astype(BF16)
        hi = rows.astype(BF16)
        lo = (rows - hi.astype(F32)).astype(BF16)
        return gcol * (_dot(onehot, hi) + _dot(onehot, lo))

    for e in range(N_EXPERTS):
        b_end = a_ref[e * LANES + j + 1]
        row0 = first_row(e, j)
        win_copy(e, row0, buf).wait()
        scol = slott_ref[:, e:e + 1]
        gcol = afft_ref[:, e:e + 1]
        o_ref[...] += place(win_s[buf, e], row0, scol, gcol)
        n_more = jnp.maximum((b_end - row0 + WIN - 1) // WIN - 1, 0)

        def more(i, _):
            lo_slot = row0 + (i + 1) * WIN
            r = pl.multiple_of(jnp.minimum(lo_slot, CAP_ALL - WIN), 8)
            cp = pltpu.make_async_copy(ye_hbm.at[e, pl.ds(r, WIN), :], xwin_s, xsem)
            cp.start()
            cp.wait()
            s_hi = jnp.where(scol >= lo_slot.astype(F32), scol, -1.0)
            o_ref[...] += place(xwin_s[...], r, s_hi, gcol)
            return 0

        lax.fori_loop(0, n_more, more, 0)


def _combine(a, ye, slott, afft):
    tile = lambda w: pl.BlockSpec((TM, w), lambda i, *_: (i, 0))
    grid_spec = pltpu.PrefetchScalarGridSpec(
        num_scalar_prefetch=1, grid=(NT_ALL,),
        in_specs=[pl.BlockSpec(memory_space=pl.ANY), tile(N_EXPERTS), tile(N_EXPERTS)],
        out_specs=tile(D_MODEL),
        scratch_shapes=[pltpu.VMEM((2, N_EXPERTS, WIN, D_MODEL), F32),
                        pltpu.SemaphoreType.DMA((2, N_EXPERTS)),
                        pltpu.VMEM((WIN, D_MODEL), F32), pltpu.SemaphoreType.DMA(())])
    return pl.pallas_call(
        _combine_kernel, grid_spec=grid_spec,
        out_shape=jax.ShapeDtypeStruct((T_ALL, D_MODEL), F32),
        compiler_params=_cparams(1), name="moe_combine",
    )(a, ye, slott, afft)


def _final_kernel(x_ref, moe_ref, mod_ref, g_ref, o_ref):
    g2 = mod_ref[...][:, 5 * D_MODEL:6 * D_MODEL]
    o_ref[...] = _rms(x_ref[...] + g2 * moe_ref[...], g_ref[...])


def _final(x, moe, mod_l, fg, sample):
    n, off = (T_SMP // TM, NT_CTX) if sample else (NT_CTX, 0)
    tile = pl.BlockSpec((TM, D_MODEL), lambda i: (off + i, 0))
    return pl.pallas_call(
        _final_kernel, grid=(n,),
        in_specs=[tile, tile, pl.BlockSpec((None, 1, 6 * D_MODEL), lambda i: (_mod_row(off + i), 0, 0)),
                  pl.BlockSpec(fg.shape, lambda i: (0, 0))],
        out_specs=pl.BlockSpec((TM, D_MODEL), lambda i: (i, 0)),
        out_shape=jax.ShapeDtypeStruct((n * TM, D_MODEL), F32),
        compiler_params=_cparams(1), name="final_norm",
    )(x, moe, mod_l, fg)


def _rope_tables():
    t = jnp.arange(DEC_SEQ)
    pos = jnp.stack([t // GRID_W, t % GRID_W], axis=-1).astype(F32)
    nf = DA_HD // 4
    inv = ROPE_BASE ** (-jnp.arange(nf, dtype=F32) / nf)
    ang = pos[:, :, None] * inv
    cos = jnp.cos(ang)
    sin = jnp.sin(ang)
    cos64 = jnp.stack([cos, cos], axis=2).reshape(DEC_SEQ, DA_HD)
    sin64 = jnp.stack([-sin, sin], axis=2).reshape(DEC_SEQ, DA_HD)
    cos_t = jnp.concatenate([jnp.ones((TM, LANES), F32), jnp.tile(cos64, (1, 2))], axis=0)
    sin_t = jnp.concatenate([jnp.zeros((TM, LANES), F32), jnp.tile(sin64, (1, 2))], axis=0)
    return cos_t, sin_t


def kernel(x_prompt, x_sample, cache_k, cache_v, state_ssm, state_mlstm_c, state_mlstm_n, state_mlstm_m, c, c_ctx, ada_w, ada_b, norm1_g, norm2_g, w_in, da_lambda, da_subln_g, mb_conv_w, mb_conv_b, mb_dt_bias, mb_a_log, mb_d, mb_norm_g, ml_gate_b, ml_norm_g, w_branch, w_mgate, b_mgate, w_out, router_w, ex_w_gate, ex_w_up, ex_w_down, final_g):
    x = jnp.concatenate([x_prompt.reshape(T_CTX, D_MODEL), x_sample.reshape(T_SMP, D_MODEL)], axis=0)
    cc = jnp.concatenate([c_ctx[None, :], c, jnp.zeros((16 - 1 - DEC_BATCH, D_MODEL), F32)], axis=0)
    mod = _modulation(cc, ada_w, ada_b).reshape(DEPTH, 16, 1, 6 * D_MODEL)
    cos_t, sin_t = _rope_tables()
    cache_k2 = cache_k.reshape(DEC_BATCH, DEPTH, PAST_LEN, DA_WIDTH)
    cache_v2 = cache_v.reshape(DEC_BATCH, DEPTH, PAST_LEN, DA_WIDTH)

    outs = {n: [] for n in ("k", "v", "ssm", "C", "n", "m")}
    moe = None
    for l in range(DEPTH):
        w = w_in[l]
        wm = jnp.concatenate([w[:, :2816], w[:, 2832:4368]], axis=1).astype(BF16)
        ws = jnp.concatenate([w[:, 2816:2832], w[:, 4368:4384]], axis=1).astype(BF16)
        res = _projection(x, moe, mod[l - 1] if l else None, mod[l], norm1_g[l][None], wm, ws, ws.T,
                          cos_t, sin_t)
        q, k, v, kf, vf, z, xbc, mq, mk, mv, mo, sm, smt = res[:13]
        if l:
            x = res[13]
        outs["k"].append(kf[:T_CTX].reshape(BATCH, SEQ, DA_HEADS, 2, DA_HD))
        outs["v"].append(vf[:T_CTX].reshape(BATCH, SEQ, DA_HEADS, 2 * DA_HD))

        lp = da_lambda[l]
        sg = da_subln_g[l][None]
        cw = mb_conv_w[l]
        cb = mb_conv_b[l][None]
        dtb = mb_dt_bias[l].reshape(16)
        a_neg = -jnp.exp(mb_a_log[l]).reshape(16)
        dexp = jnp.repeat(mb_d[l], MB_HD)[None]
        mng = mb_norm_g[l][None]
        gb = ml_gate_b[l].reshape(16)
        lng = ml_norm_g[l].reshape(1, ML_WIDTH)
        ys = []
        for sample in (False, True):
            nb = DEC_BATCH if sample else BATCH
            y_da = _attention(l, q, k, v, cache_k2, cache_v2, lp, sg, sample)
            if sample:
                h0 = state_ssm[:, l]
                c0 = state_mlstm_c[:, l]
                n0 = state_mlstm_n[:, l]
                m0 = state_mlstm_m[:, l]
            else:
                h0 = jnp.zeros((nb, 2, MB_HEADS, MB_HD, MB_STATE), F32)
                c0 = jnp.zeros((nb, 2, ML_HEADS, ML_DK, ML_DV), F32)
                n0 = jnp.zeros((nb, 2, ML_HEADS, ML_DK), F32)
                m0 = jnp.zeros((nb, 2, ML_HEADS), F32)
            h0t = jnp.transpose(h0, (0, 1, 4, 2, 3)).reshape(nb, 2, MB_STATE, MB_INNER)
            y_mb, hfin = _ssd(z, xbc, sm, smt, cw, cb, dtb, a_neg, dexp, mng, h0t, sample)
            y_ml, cfin, nfin, mfin = _mlstm(
                mq, mk, mv, mo, sm, smt, gb, lng,
                c0.reshape(nb, 2, 2, LANES, ML_DV), n0.reshape(nb, 2, 2, LANES, 1),
                m0.reshape(nb, 1, 8), sample)
            ys.append((y_da, y_mb, y_ml))
            if not sample:
                outs["ssm"].append(jnp.transpose(
                    hfin.reshape(nb, 2, MB_STATE, MB_HEADS, MB_HD), (0, 1, 3, 4, 2)))
                outs["C"].append(cfin.reshape(nb, 2, ML_HEADS, ML_DK, ML_DV))
                outs["n"].append(nfin.reshape(nb, 2, ML_HEADS, ML_DK))
                outs["m"].append(mfin.reshape(nb, 2, ML_HEADS))
        yda, ymb, yml = (jnp.concatenate([ys[0][j], ys[1][j]], axis=0) for j in range(3))

        x, h2, affb, afft = _merge(x, mod[l], norm1_g[l][None], yda, ymb, yml,
                                   w_mgate[l].astype(BF16), b_mgate[l][None], w_branch[l].astype(BF16),
                                   w_out[l].astype(BF16), norm2_g[l][None], router_w[l], router_w[l].T)
        slotb, slott, a_cnt, jlo, jhi = _route(affb)
        ye = _experts(l, jlo.reshape(-1), jhi.reshape(-1), h2, slotb, ex_w_gate, ex_w_up, ex_w_down)
        moe = _combine(a_cnt.reshape(-1), ye, slott, afft)

    fg = final_g[None]
    y_prompt = _final(x, moe, mod[DEPTH - 1], fg, False).reshape(BATCH, SEQ, D_MODEL)
    y_sample = _final(x, moe, mod[DEPTH - 1], fg, True).reshape(DEC_BATCH, DEC_SEQ, D_MODEL)
    return (y_prompt, y_sample, jnp.stack(outs["k"], axis=1), jnp.stack(outs["v"], axis=1),
            jnp.stack(outs["ssm"], axis=1), jnp.stack(outs["C"], axis=1),
            jnp.stack(outs["n"], axis=1), jnp.stack(outs["m"], axis=1))
```

```python
import functools
import math

import jax
import jax.numpy as jnp
from jax import lax
from jax.experimental import pallas as pl
from jax.experimental.pallas import tpu as pltpu

F32 = jnp.float32
BF16 = jnp.bfloat16

D_MODEL = 1024
BATCH = 16
SEQ = 256
DEPTH = 2
DEC_BATCH = 8
DEC_SEQ = 2048
PAST_LEN = 512
GRID_W = 64
EPS = 1e-6
CHUNK = 128
ROPE_BASE = 10000.0
DA_HEADS = 4
DA_HD = 64
DA_WIDTH = 512
MB_INNER = 512
MB_HD = 64
MB_HEADS = 8
MB_GROUPS = 2
MB_STATE = 64
MB_CONV_DIM = 768
ML_HEADS = 4
ML_DK = 64
ML_DV = 128
ML_WIDTH = 512
N_EXPERTS = 16
EC_FACTOR = 2
EXPERT_FF = 1024

T_CTX = BATCH * SEQ
T_SMP = DEC_BATCH * DEC_SEQ
T_ALL = T_CTX + T_SMP
TM = 256
NT_CTX = T_CTX // TM
NT_ALL = T_ALL // TM
NB_CTX = T_CTX // CHUNK
NB_ALL = T_ALL // CHUNK
CAP_CTX = EC_FACTOR * T_CTX // N_EXPERTS
CAP_SMP = EC_FACTOR * T_SMP // N_EXPERTS
CAP_ALL = CAP_CTX + CAP_SMP
RB = 256
N_RB = CAP_ALL // RB
WIN = 64
GATHER_TILES = 4
TILES_PER_REQ = DEC_SEQ // TM
LANES = 128
VMEM_LIMIT = 56 * 1024 * 1024
HI = lax.Precision.HIGHEST

C_Q, C_K, C_V, C_Z, C_XBC, C_MQ, C_MK, C_MV, C_MO, C_END = (
    0, 512, 1024, 1536, 2048, 2816, 3072, 3328, 3840, 4352)


def _mod_row(i):
    return jnp.where(i < NT_CTX, 0, 1 + (i - NT_CTX) // TILES_PER_REQ)


def _rope_blk(i):
    return jnp.where(i < NT_CTX, 0, 1 + (i - NT_CTX) % TILES_PER_REQ)


def _cparams(n_grid):
    return pltpu.CompilerParams(dimension_semantics=("arbitrary",) * n_grid,
                                vmem_limit_bytes=VMEM_LIMIT)


def _silu(x):
    return x * jax.nn.sigmoid(x)


def _softplus(x):
    u = jnp.exp(-jnp.abs(x))
    w = 1.0 + u
    l1p = jnp.where(w == 1.0, u, jnp.log(w) * (u / (w - 1.0)))
    return jnp.maximum(x, 0.0) + l1p


def _dot(a, b):
    return jnp.dot(a, b, preferred_element_type=F32)


def _dot_nt(a, b):
    return lax.dot_general(a, b, (((1,), (1,)), ((), ())), preferred_element_type=F32)


def _dot_hi(a, b):
    return jnp.dot(a, b, precision=HI, preferred_element_type=F32)


def _rms(x, g):
    return x * lax.rsqrt(jnp.mean(x * x, axis=-1, keepdims=True) + EPS) * g


def _mod_kernel(c_ref, w_ref, b_ref, o_ref):
    s = _silu(c_ref[...])
    o_ref[...] = _dot(s.astype(BF16), w_ref[...].astype(BF16)) + b_ref[...]


def _modulation(cc, ada_w, ada_b):
    tn = 1536
    return pl.pallas_call(
        _mod_kernel,
        grid=(DEPTH, 6 * D_MODEL // tn),
        in_specs=[pl.BlockSpec((16, D_MODEL), lambda l, j: (0, 0)),
                  pl.BlockSpec((None, D_MODEL, tn), lambda l, j: (l, 0, j)),
                  pl.BlockSpec((None, 1, tn), lambda l, j: (l, 0, j))],
        out_specs=pl.BlockSpec((None, 16, tn), lambda l, j: (l, 0, j)),
        out_shape=jax.ShapeDtypeStruct((DEPTH, 16, 6 * D_MODEL), F32),
        compiler_params=_cparams(2),
        name="adaln_mod",
    )(cc, ada_w, ada_b.reshape(DEPTH, 1, 6 * D_MODEL))


def _rope(t, cos, sin, first_half):
    outs = []
    for c in range(DA_WIDTH // LANES):
        xc = t[:, LANES * c:LANES * (c + 1)]
        partner = jnp.where(first_half, pltpu.roll(xc, LANES - 16, 1), pltpu.roll(xc, 16, 1))
        outs.append(xc * cos + partner * sin)
    return jnp.concatenate(outs, axis=1)


def _proj_body(x, mod_ref, g_ref, wm_ref, ws_ref, wst_ref, cos_ref, sin_ref,
               q_ref, k_ref, v_ref, kf_ref, vf_ref, z_ref, xbc_ref,
               mq_ref, mk_ref, mv_ref, mo_ref, sm_ref, smt_ref):
    mod = mod_ref[...]
    sh1 = mod[:, 0:D_MODEL]
    sc1 = mod[:, D_MODEL:2 * D_MODEL]
    h = _rms(x, g_ref[...]) * (1.0 + sc1) + sh1
    hb = h.astype(BF16)

    def proj(a, b):
        return _dot(hb, wm_ref[:, a:b])

    lane = lax.broadcasted_iota(jnp.int32, (TM, LANES), 1)
    first_half = (lane % 32) < 16
    cos = cos_ref[...]
    sin = sin_ref[...]
    q = proj(C_Q, C_K)
    k = proj(C_K, C_V)
    v = proj(C_V, C_Z)
    kf_ref[...] = k
    vf_ref[...] = v
    q_ref[...] = (_rope(q, cos, sin, first_half) * (DA_HD ** -0.5)).astype(BF16)
    k_ref[...] = _rope(k, cos, sin, first_half).astype(BF16)
    v_ref[...] = v.astype(BF16)
    z_ref[...] = proj(C_Z, C_XBC)
    xbc_ref[...] = proj(C_XBC, C_MQ)
    mq_ref[...] = proj(C_MQ, C_MK).astype(BF16)
    mk_ref[...] = (proj(C_MK, C_MV) * (ML_DK ** -0.5)).astype(BF16)
    mv_ref[...] = proj(C_MV, C_MO).astype(BF16)
    mo_ref[...] = proj(C_MO, C_END)
    sm_ref[...] = _dot(hb, ws_ref[...])
    st = _dot_nt(wst_ref[...], hb)
    for j in range(TM // CHUNK):
        smt_ref[j] = st[:, CHUNK * j:CHUNK * (j + 1)]


def _proj_kernel_first(x_ref, *refs):
    _proj_body(x_ref[...], *refs)


def _proj_kernel_next(x_ref, moe_ref, modp_ref, *refs):
    g2 = modp_ref[...][:, 5 * D_MODEL:6 * D_MODEL]
    x = x_ref[...] + g2 * moe_ref[...]
    xo_ref = refs[-1]
    xo_ref[...] = x
    _proj_body(x, *refs[:-1])


def _projection(x, moe_prev, mod_prev, mod_l, g1, wm, ws, wst, cos_t, sin_t):
    tile = lambda w: pl.BlockSpec((TM, w), lambda i: (i, 0))
    full = lambda a: pl.BlockSpec(a.shape, lambda i: (0,) * a.ndim)
    modspec = pl.BlockSpec((None, 1, 6 * D_MODEL), lambda i: (_mod_row(i), 0, 0))
    in_specs = [tile(D_MODEL)]
    args = [x]
    if moe_prev is not None:
        in_specs += [tile(D_MODEL), modspec]
        args += [moe_prev, mod_prev]
    in_specs += [modspec, full(g1), full(wm), full(ws), full(wst),
                 pl.BlockSpec((TM, LANES), lambda i: (_rope_blk(i), 0)),
                 pl.BlockSpec((TM, LANES), lambda i: (_rope_blk(i), 0))]
    args += [mod_l, g1, wm, ws, wst, cos_t, sin_t]
    ctx_only = pl.BlockSpec((TM, DA_WIDTH), lambda i: (jnp.minimum(i, NT_CTX), 0))
    out_specs = [tile(512), tile(512), tile(512), ctx_only, ctx_only, tile(512), tile(768),
                 tile(256), tile(256), tile(512), tile(512), tile(32),
                 pl.BlockSpec((TM // CHUNK, 32, CHUNK), lambda i: (i, 0, 0))]
    sds = jax.ShapeDtypeStruct
    out_shape = [sds((T_ALL, 512), BF16), sds((T_ALL, 512), BF16), sds((T_ALL, 512), BF16),
                 sds((T_CTX + TM, 512), F32), sds((T_CTX + TM, 512), F32),
                 sds((T_ALL, 512), F32), sds((T_ALL, 768), F32),
                 sds((T_ALL, 256), BF16), sds((T_ALL, 256), BF16), sds((T_ALL, 512), BF16),
                 sds((T_ALL, 512), F32), sds((T_ALL, 32), F32),
                 sds((T_ALL // CHUNK, 32, CHUNK), F32)]
    kern = _proj_kernel_first
    if moe_prev is not None:
        out_specs.append(tile(D_MODEL))
        out_shape.append(sds((T_ALL, D_MODEL), F32))
        kern = _proj_kernel_next
    return pl.pallas_call(
        kern, grid=(NT_ALL,), in_specs=in_specs, out_specs=out_specs, out_shape=out_shape,
        compiler_params=_cparams(1), name="in_proj",
    )(*args)


def _attn_body(lam_init, q_ref, k_ref, v_ref, kc_ref, vc_ref, lp_ref, g_ref, o_ref):
    lp = lp_ref[...]
    s01 = jnp.sum(lp[0:1] * lp[1:2], axis=-1, keepdims=True)
    s23 = jnp.sum(lp[2:3] * lp[3:4], axis=-1, keepdims=True)
    lam = jnp.exp(s01) - jnp.exp(s23) + lam_init
    tq = q_ref.shape[0]
    lane = lax.broadcasted_iota(jnp.int32, (tq, LANES), 1)
    g = g_ref[...]
    for h in range(DA_HEADS):
        cols = slice(LANES * h, LANES * (h + 1))
        qh = q_ref[:, cols]
        kh = k_ref[:, cols]
        vh = v_ref[:, cols]
        if kc_ref is not None:
            kch = kc_ref[:, cols].astype(BF16)
            vch = vc_ref[:, cols].astype(BF16)
        parts = []
        for m in range(2):
            qm = jnp.where((lane < DA_HD) == (m == 0), qh, jnp.zeros_like(qh))
            s = _dot_nt(qm, kh)
            mx = jnp.max(s, axis=-1, keepdims=True)
            if kc_ref is not None:
                sc = _dot_nt(qm, kch)
                mx = jnp.maximum(mx, jnp.max(sc, axis=-1, keepdims=True))
                ec = jnp.exp(sc - mx)
            e = jnp.exp(s - mx)
            den = jnp.sum(e, axis=-1, keepdims=True)
            acc = _dot(e.astype(BF16), vh)
            if kc_ref is not None:
                den = den + jnp.sum(ec, axis=-1, keepdims=True)
                acc = acc + _dot(ec.astype(BF16), vch)
            parts.append(acc / den)
        att = parts[0] - lam * parts[1]
        o_ref[:, cols] = (_rms(att, g) * (1.0 - lam_init)).astype(BF16)


def _attn_kernel_ctx(lam_init, q_ref, k_ref, v_ref, lp_ref, g_ref, o_ref):
    _attn_body(lam_init, q_ref, k_ref, v_ref, None, None, lp_ref, g_ref, o_ref)


def _attn_kernel_smp(lam_init, q_ref, k_ref, v_ref, kc_ref, vc_ref, lp_ref, g_ref, o_ref):
    _attn_body(lam_init, q_ref, k_ref, v_ref, kc_ref, vc_ref, lp_ref, g_ref, o_ref)


def _attention(l, q, k, v, cache_k, cache_v, lp, g, sample):
    lam_init = 0.8 - 0.6 * math.exp(-0.3 * l)
    full = lambda a: pl.BlockSpec(a.shape, lambda *_: (0,) * a.ndim)
    sds = jax.ShapeDtypeStruct
    if not sample:
        blk = pl.BlockSpec((SEQ, DA_WIDTH), lambda b: (b, 0))
        return pl.pallas_call(
            functools.partial(_attn_kernel_ctx, lam_init), grid=(BATCH,),
            in_specs=[blk, blk, blk, full(lp), full(g)],
            out_specs=blk, out_shape=sds((T_CTX, DA_WIDTH), BF16),
            compiler_params=_cparams(1), name="diff_attn_ctx",
        )(q, k, v, lp, g)
    tq = 256
    nq = DEC_SEQ // tq
    off_q = T_CTX // tq
    off_k = T_CTX // DEC_SEQ
    qblk = pl.BlockSpec((tq, DA_WIDTH), lambda b, i: (off_q + b * nq + i, 0))
    kblk = pl.BlockSpec((DEC_SEQ, DA_WIDTH), lambda b, i: (off_k + b, 0))
    cblk = pl.BlockSpec((None, None, PAST_LEN, DA_WIDTH), lambda b, i: (b, l, 0, 0))
    return pl.pallas_call(
        functools.partial(_attn_kernel_smp, lam_init), grid=(DEC_BATCH, nq),
        in_specs=[qblk, kblk, kblk, cblk, cblk, full(lp), full(g)],
        out_specs=pl.BlockSpec((tq, DA_WIDTH), lambda b, i: (b * nq + i, 0)),
        out_shape=sds((T_SMP, DA_WIDTH), BF16),
        compiler_params=_cparams(2), name="diff_attn_smp",
    )(q, k, v, cache_k, cache_v, lp, g)


def _tri_masks():
    r = lax.broadcasted_iota(jnp.int32, (CHUNK, CHUNK), 0)
    c = lax.broadcasted_iota(jnp.int32, (CHUNK, CHUNK), 1)
    return c <= r, c >= r


def _ssd_kernel(L, z_ref, xbc_ref, sm_ref, smt_ref, cw_ref, cb_ref, dtbr_ref, dtbc_ref,
                ar_ref, ac_ref, dexp_ref, ng_ref, h0_ref,
                y_ref, hfin_ref, xc_s, yacc_s, ht_s):
    nc = L // CHUNK
    low, upp = _tri_masks()
    low_f = low.astype(F32)
    upp_f = upp.astype(F32)
    lane512 = lax.broadcasted_iota(jnp.int32, (1, MB_INNER), 1)
    row16 = lax.broadcasted_iota(jnp.int32, (16, MB_INNER), 0)
    lane16 = lax.broadcasted_iota(jnp.int32, (16, MB_INNER), 1)
    lane128 = lax.broadcasted_iota(jnp.int32, (CHUNK, LANES), 1)
    rowblk = lax.broadcasted_iota(jnp.int32, (LANES, MB_INNER), 0) // MB_STATE
    colblk = lax.broadcasted_iota(jnp.int32, (LANES, MB_INNER), 1) // (MB_INNER // MB_GROUPS)
    same_group = rowblk == colblk
    cw = cw_ref[...]
    cbias = cb_ref[...]

    def conv_chunk(c, _):
        base = pl.multiple_of(c * CHUNK, CHUNK)
        x = xbc_ref[pl.ds(base, CHUNK), :]
        prev = xbc_ref[pl.ds(jnp.maximum(base - 1, 0), 1), :]
        nxt = xbc_ref[pl.ds(jnp.minimum(base + CHUNK, L - 1), 1), :]
        prev = jnp.where(c == 0, 0.0, prev)
        nxt = jnp.where(c == nc - 1, 0.0, nxt)
        row = lax.broadcasted_iota(jnp.int32, (CHUNK, 1), 0)
        xp = jnp.where(row == 0, prev, pltpu.roll(x, 1, 0))
        xn = jnp.where(row == CHUNK - 1, nxt, pltpu.roll(x, CHUNK - 1, 0))
        conv = xp * cw[0:1] + x * cw[1:2] + xn * cw[2:3] + cbias
        xc_s[pl.ds(base, CHUNK), :] = _silu(conv)
        return 0

    lax.fori_loop(0, nc, conv_chunk, 0)

    h0 = h0_ref[...]
    for d in range(2):
        h0d = h0[d]
        ht_s[d] = jnp.concatenate(
            [jnp.where(lane512 < MB_INNER // 2, h0d, 0.0),
             jnp.where(lane512 >= MB_INNER // 2, h0d, 0.0)], axis=0)

    def chunk_dir(c, d):
        base = pl.multiple_of(c * CHUNK, CHUNK)
        rows = pl.ds(base, CHUNK)
        xs = xc_s[rows, 0:MB_INNER]
        bm = xc_s[rows, MB_INNER:MB_INNER + LANES]
        cm = xc_s[rows, MB_INNER + LANES:MB_CONV_DIM]
        dt_col = _softplus(sm_ref[rows, 0:16] + dtbr_ref[...])
        a_col = dt_col * ar_ref[...]
        dt_row = _softplus(smt_ref[c][0:16, :] + dtbc_ref[...])
        a_row = dt_row * ac_ref[...]
        if d == 0:
            cum_col = _dot_hi(low_f, a_col)
            cum_row = _dot_hi(a_row, upp_f)
            mask = low
            last = CHUNK - 1
        else:
            cum_col = _dot_hi(upp_f, a_col)
            cum_row = _dot_hi(a_row, low_f)
            mask = upp
            last = 0
        cum_last = cum_col[last:last + 1, :]
        expand = (row16 == 8 * d + lane16 // MB_HD).astype(F32)
        w_exp = _dot_hi(jnp.exp(cum_last - cum_col) * dt_col, expand)
        g_exp = _dot_hi(jnp.exp(cum_col), expand)
        cd_exp = _dot_hi(jnp.broadcast_to(jnp.exp(cum_last), (8, 16)), expand)[0:1]
        xw = (xs * w_exp).astype(BF16)
        xsb = xs.astype(BF16)
        bb = bm.astype(BF16)
        cb = cm.astype(BF16)
        ht = ht_s[d]
        y_off = _dot(cb, ht.astype(BF16)) * g_exp
        s_new = _dot(bm.T.astype(BF16), xw)
        ht_s[d] = ht * cd_exp + jnp.where(same_group, s_new, 0.0)
        pairs = []
        for g in range(MB_GROUPS):
            cg = jnp.where((lane128 < MB_STATE) == (g == 0), cb, jnp.zeros_like(cb))
            cbg = _dot_nt(cg, bb)
            for pr in range(2):
                ys = []
                for hh in range(2):
                    ci = 8 * d + 4 * g + 2 * pr + hh
                    seg = cum_col[:, ci:ci + 1] - cum_row[ci:ci + 1, :]
                    m = jnp.where(mask, jnp.exp(seg), 0.0) * cbg * dt_row[ci:ci + 1, :]
                    k = 2 * g + pr
                    ys.append(_dot(m.astype(BF16), xsb[:, LANES * k:LANES * (k + 1)]))
                pairs.append(jnp.where(lane128 < MB_HD, ys[0], ys[1]))
        y = jnp.concatenate(pairs, axis=1) + y_off
        if d == 0:
            yacc_s[rows, :] = y
        else:
            yacc_s[rows, :] = yacc_s[rows, :] + y

    def fwd(c, _):
        chunk_dir(c, 0)
        return 0

    def bwd(i, _):
        chunk_dir(nc - 1 - i, 1)
        return 0

    lax.fori_loop(0, nc, fwd, 0)
    lax.fori_loop(0, nc, bwd, 0)

    dexp = dexp_ref[...]
    ng = ng_ref[...]

    def fin(c, _):
        rows = pl.ds(pl.multiple_of(c * CHUNK, CHUNK), CHUNK)
        y = yacc_s[rows, :] + dexp * xc_s[rows, 0:MB_INNER]
        y = y * _silu(z_ref[rows, :])
        y_ref[rows, :] = _rms(y, ng).astype(BF16)
        return 0

    lax.fori_loop(0, nc, fin, 0)
    for d in range(2):
        ht = ht_s[d]
        hfin_ref[d] = ht[0:MB_STATE, :] + ht[MB_STATE:2 * MB_STATE, :]


def _ssd(z, xbc, sm, smt, cw, cb, dtb, a_neg, dexp, ng, h0t, sample):
    nb, L, off = (DEC_BATCH, DEC_SEQ, T_CTX // DEC_SEQ) if sample else (BATCH, SEQ, 0)
    full = lambda a: pl.BlockSpec(a.shape, lambda b: (0,) * a.ndim)
    seq = lambda w: pl.BlockSpec((L, w), lambda b: (off + b, 0))
    dtb_r, dtb_c = dtb.reshape(1, 16), dtb.reshape(16, 1)
    a_r, a_c = a_neg.reshape(1, 16), a_neg.reshape(16, 1)
    sds = jax.ShapeDtypeStruct
    return pl.pallas_call(
        functools.partial(_ssd_kernel, L), grid=(nb,),
        in_specs=[seq(MB_INNER), seq(MB_CONV_DIM), seq(32),
                  pl.BlockSpec((L // CHUNK, 32, CHUNK), lambda b: (off + b, 0, 0)),
                  full(cw), full(cb), full(dtb_r), full(dtb_c), full(a_r), full(a_c),
                  full(dexp), full(ng),
                  pl.BlockSpec((None, 2, MB_STATE, MB_INNER), lambda b: (b, 0, 0, 0))],
        out_specs=[pl.BlockSpec((L, MB_INNER), lambda b: (b, 0)),
                   pl.BlockSpec((None, 2, MB_STATE, MB_INNER), lambda b: (b, 0, 0, 0))],
        out_shape=[sds((nb * L, MB_INNER), BF16), sds((nb, 2, MB_STATE, MB_INNER), F32)],
        scratch_shapes=[pltpu.VMEM((L, MB_CONV_DIM), F32), pltpu.VMEM((L, MB_INNER), F32),
                        pltpu.VMEM((2, LANES, MB_INNER), F32)],
        compiler_params=_cparams(1), name="ssd_smp" if sample else "ssd_ctx",
    )(z, xbc, sm, smt, cw, cb, dtb_r, dtb_c, a_r, a_c, dexp, ng, h0t)


def _mlstm_kernel(L, q_ref, k_ref, v_ref, o_ref, sm_ref, smt_ref, gbr_ref, gbc_ref, ng_ref,
                  c0_ref, n0_ref, m0_ref,
                  y_ref, cf_ref, nf_ref, mf_ref, hacc_s, c_s, n_s):
    nc = L // CHUNK
    low, upp = _tri_masks()
    low_f = low.astype(F32)
    upp_f = upp.astype(F32)
    lane128 = lax.broadcasted_iota(jnp.int32, (CHUNK, LANES), 1)
    lane8 = lax.broadcasted_iota(jnp.int32, (1, 2 * ML_HEADS), 1)
    neg_inf = -jnp.inf

    c_s[...] = c0_ref[...]
    n_s[...] = n0_ref[...]
    m0 = m0_ref[...]

    def chunk_dir(c, d, m_in):
        base = pl.multiple_of(c * CHUNK, CHUNK)
        rows = pl.ds(base, CHUNK)
        pre_col = sm_ref[rows, 16:32] + gbr_ref[...]
        pre_row = smt_ref[c][16:32, :] + gbc_ref[...]
        lf_col = -_softplus(-pre_col)
        lf_row = -_softplus(-pre_row)
        if d == 0:
            b_col = _dot_hi(low_f, lf_col)
            b_row = _dot_hi(lf_row, upp_f)
            mask = low
            last = CHUNK - 1
        else:
            b_col = _dot_hi(upp_f, lf_col)
            b_row = _dot_hi(lf_row, low_f)
            mask = upp
            last = 0
        m_out = m_in
        for pr in range(2):
            qp = q_ref[rows, LANES * pr:LANES * (pr + 1)]
            kp = k_ref[rows, LANES * pr:LANES * (pr + 1)]
            kt = kp.astype(F32).T
            cst = c_s[d, pr]
            nst = n_s[d, pr]
            cb16 = cst.astype(BF16)
            nb16 = jnp.broadcast_to(nst, (LANES, LANES)).astype(BF16)
            c_new = []
            n_new = []
            for hh in range(2):
                h = 2 * pr + hh
                ci = 8 * d + h
                cf = 8 * d + 4 + h
                m_st = m_in[:, 4 * d + h:4 * d + h + 1]
                qm = jnp.where((lane128 < ML_DK) == (hh == 0), qp, jnp.zeros_like(qp))
                vh = v_ref[rows, ML_DV * h:ML_DV * (h + 1)]
                bcol = b_col[:, cf:cf + 1]
                brow = b_row[cf:cf + 1, :]
                li_row = pre_row[ci:ci + 1, :]
                dm = jnp.where(mask, bcol - brow + li_row, neg_inf)
                inter = bcol + m_st
                m_t = jnp.maximum(inter, jnp.max(dm, axis=-1, keepdims=True))
                s_intra = jnp.exp(dm - m_t) * _dot_nt(qm, kp)
                s_inter = jnp.exp(inter - m_t)
                num = s_inter * _dot(qm, cb16) + _dot(s_intra.astype(BF16), vh)
                den = s_inter * _dot(qm, nb16) + jnp.sum(s_intra, axis=-1, keepdims=True)
                hout = num / jnp.maximum(jnp.abs(den), jnp.exp(-m_t))
                cols = slice(ML_DV * h, ML_DV * (h + 1))
                if d == 0:
                    hacc_s[rows, cols] = hout
                else:
                    hacc_s[rows, cols] = hacc_s[rows, cols] + hout
                b_end = b_row[cf:cf + 1, last:last + 1]
                w_end = b_end - brow + li_row
                m_new = jnp.maximum(b_end + m_st, jnp.max(w_end, axis=-1, keepdims=True))
                s_old = jnp.exp(b_end + m_st - m_new)
                s_tok = jnp.exp(w_end - m_new)
                half = slice(ML_DK * hh, ML_DK * (hh + 1))
                kts = kt[half, :] * s_tok
                c_new.append(s_old * cst[half, :] + _dot(kts.astype(BF16), vh))
                n_new.append(s_old * nst[half, :] + jnp.sum(kts, axis=-1, keepdims=True))
                m_out = jnp.where(lane8 == 4 * d + h, m_new, m_out)
            c_s[d, pr] = jnp.concatenate(c_new, axis=0)
            n_s[d, pr] = jnp.concatenate(n_new, axis=0)
        return m_out

    def fwd(c, m):
        return chunk_dir(c, 0, m)

    def bwd(i, m):
        return chunk_dir(nc - 1 - i, 1, m)

    m_fin = lax.fori_loop(0, nc, bwd, lax.fori_loop(0, nc, fwd, m0))

    ng = ng_ref[...]

    def fin(c, _):
        rows = pl.ds(pl.multiple_of(c * CHUNK, CHUNK), CHUNK)
        for h in range(ML_HEADS):
            cols = slice(ML_DV * h, ML_DV * (h + 1))
            y = _rms(hacc_s[rows, cols], ng[:, cols]) * jax.nn.sigmoid(o_ref[rows, cols])
            y_ref[rows, cols] = y.astype(BF16)
        return 0

    lax.fori_loop(0, nc, fin, 0)
    cf_ref[...] = c_s[...]
    nf_ref[...] = n_s[...]
    mf_ref[...] = m_fin


def _mlstm(q, k, v, o, sm, smt, gb, ng, c0, n0, m0, sample):
    nb, L, off = (DEC_BATCH, DEC_SEQ, T_CTX // DEC_SEQ) if sample else (BATCH, SEQ, 0)
    full = lambda a: pl.BlockSpec(a.shape, lambda b: (0,) * a.ndim)
    seq = lambda w: pl.BlockSpec((L, w), lambda b: (off + b, 0))
    gb_r, gb_c = gb.reshape(1, 16), gb.reshape(16, 1)
    st_c = pl.BlockSpec((None, 2, 2, LANES, ML_DV), lambda b: (b, 0, 0, 0, 0))
    st_n = pl.BlockSpec((None, 2, 2, LANES, 1), lambda b: (b, 0, 0, 0, 0))
    st_m = pl.BlockSpec((None, 1, 8), lambda b: (b, 0, 0))
    sds = jax.ShapeDtypeStruct
    return pl.pallas_call(
        functools.partial(_mlstm_kernel, L), grid=(nb,),
        in_specs=[seq(256), seq(256), seq(512), seq(512), seq(32),
                  pl.BlockSpec((L // CHUNK, 32, CHUNK), lambda b: (off + b, 0, 0)),
                  full(gb_r), full(gb_c), full(ng), st_c, st_n, st_m],
        out_specs=[pl.BlockSpec((L, ML_WIDTH), lambda b: (b, 0)), st_c, st_n, st_m],
        out_shape=[sds((nb * L, ML_WIDTH), BF16), sds((nb, 2, 2, LANES, ML_DV), F32),
                   sds((nb, 2, 2, LANES, 1), F32), sds((nb, 1, 8), F32)],
        scratch_shapes=[pltpu.VMEM((L, ML_WIDTH), F32), pltpu.VMEM((2, 2, LANES, ML_DV), F32),
                        pltpu.VMEM((2, 2, LANES, 1), F32)],
        compiler_params=_cparams(1), name="mlstm_smp" if sample else "mlstm_ctx",
    )(q, k, v, o, sm, smt, gb_r, gb_c, ng, c0, n0, m0)


def _merge_kernel(x_ref, mod_ref, g1_ref, yda_ref, ymb_ref, yml_ref, wg_ref, bg_ref, wb_ref,
                  wo_ref, g2_ref, rwt_ref, xo_ref, h2t_ref, affb_ref):
    x = x_ref[...]
    mod = mod_ref[...]
    sh1, sc1, gt1, sh2, sc2 = (mod[:, j * D_MODEL:(j + 1) * D_MODEL] for j in range(5))
    hb = (_rms(x, g1_ref[...]) * (1.0 + sc1) + sh1).astype(BF16)
    merged = None
    for n, y_ref in enumerate((yda_ref, ymb_ref, yml_ref)):
        cols = slice(n * D_MODEL, (n + 1) * D_MODEL)
        gate = jax.nn.sigmoid(_dot(hb, wg_ref[:, cols]) + bg_ref[:, cols])
        term = gate * _dot(y_ref[...], wb_ref[n])
        merged = term if merged is None else merged + term
    out = _dot(merged.astype(BF16), wo_ref[...])
    xn = x + gt1 * out
    xo_ref[...] = xn
    h2 = _rms(xn, g2_ref[...]) * (1.0 + sc2) + sh2
    h2t_ref[...] = h2.T.astype(BF16)
    logits = lax.dot_general(rwt_ref[...], h2, (((1,), (1,)), ((), ())),
                             precision=HI, preferred_element_type=F32)
    e = jnp.exp(logits - jnp.max(logits, axis=0, keepdims=True))
    aff = e / jnp.sum(e, axis=0, keepdims=True)
    for j in range(TM // CHUNK):
        affb_ref[j] = aff[:, CHUNK * j:CHUNK * (j + 1)]


def _merge(x, mod_l, g1, yda, ymb, yml, wg, bg, wb, wo, g2, rwt):
    tile = lambda w: pl.BlockSpec((TM, w), lambda i: (i, 0))
    full = lambda a: pl.BlockSpec(a.shape, lambda i: (0,) * a.ndim)
    sds = jax.ShapeDtypeStruct
    return pl.pallas_call(
        _merge_kernel, grid=(NT_ALL,),
        in_specs=[tile(D_MODEL), pl.BlockSpec((None, 1, 6 * D_MODEL), lambda i: (_mod_row(i), 0, 0)),
                  full(g1), tile(512), tile(512), tile(512), full(wg), full(bg), full(wb),
                  full(wo), full(g2), full(rwt)],
        out_specs=[tile(D_MODEL), pl.BlockSpec((D_MODEL, TM), lambda i: (0, i)),
                   pl.BlockSpec((TM // CHUNK, N_EXPERTS, CHUNK), lambda i: (i, 0, 0))],
        out_shape=[sds((T_ALL, D_MODEL), F32), sds((D_MODEL, T_ALL), BF16),
                   sds((NB_ALL, N_EXPERTS, CHUNK), F32)],
        compiler_params=_cparams(1), name="merge_out",
    )(x, mod_l, g1, yda, ymb, yml, wg, bg, wb, wo, g2, rwt)


def _route_kernel(affb_ref, slotb_ref, slott_ref, a_ref, jlo_ref, jhi_ref, acc_s, run_s):
    r = lax.broadcasted_iota(jnp.int32, (CHUNK, CHUNK), 0)
    c = lax.broadcasted_iota(jnp.int32, (CHUNK, CHUNK), 1)
    upper = (r <= c).astype(BF16)
    eye = (r == c).astype(F32)
    lane = lax.broadcasted_iota(jnp.int32, (N_EXPERTS, LANES), 1)
    acc_s[...] = jnp.zeros_like(acc_s)
    run_s[...] = jnp.zeros_like(run_s)
    for b0, b1, cap in ((0, NB_CTX, CAP_CTX), (NB_CTX, NB_ALL, CAP_SMP)):
        aff = affb_ref[b0:b1]

        def search(i, thr_bits):
            cand = thr_bits | lax.shift_left(jnp.int32(1), 30 - i)
            cnt = jnp.sum((aff >= pltpu.bitcast(cand, F32)[None]).astype(jnp.int32), axis=0)
            cnt = jnp.sum(cnt, axis=1, keepdims=True)
            return jnp.where(cnt >= cap, cand, thr_bits)

        thr = pltpu.bitcast(lax.fori_loop(0, 31, search, jnp.zeros((N_EXPERTS, 1), jnp.int32)), F32)
        n_gt = jnp.sum(jnp.sum((aff > thr[None]).astype(jnp.int32), axis=0), axis=1, keepdims=True)
        need = (cap - n_gt).astype(F32)

        run_s[1] = jnp.zeros((N_EXPERTS, 1), F32)

        def blk(b, _):
            run_sel = run_s[0]
            run_eq = run_s[1]
            x = affb_ref[b]
            eq = x == thr
            eq_f = jnp.where(eq, 1.0, 0.0)
            eq_incl = _dot(eq_f.astype(BF16), upper)
            sel = (x > thr) | (eq & (run_eq + eq_incl - eq_f < need))
            sel_f = jnp.where(sel, 1.0, 0.0)
            sel_incl = _dot(sel_f.astype(BF16), upper)
            slot = jnp.where(sel, run_sel + sel_incl - sel_f, -1.0)
            slotb_ref[b] = slot.astype(jnp.int32)
            slott_ref[pl.ds(pl.multiple_of(b * CHUNK, CHUNK), CHUNK), :] = lax.dot_general(
                eye, slot, (((1,), (1,)), ((), ())), precision=HI, preferred_element_type=F32)
            acc_s[...] = jnp.where((lane == b // 2) & (b % 2 == 0), run_sel, acc_s[...])
            run_s[0] = run_sel + sel_incl[:, CHUNK - 1:CHUNK]
            run_s[1] = run_eq + eq_incl[:, CHUNK - 1:CHUNK]
            return 0

        lax.fori_loop(b0, b1, blk, 0)
    a_acc = jnp.where(lane == NT_ALL, run_s[0], acc_s[...])
    a_ref[...] = a_acc.astype(jnp.int32)
    a_next = pltpu.roll(a_acc, LANES - 1, 1)
    tile_ok = lane < NT_ALL
    jlo = jnp.zeros((N_EXPERTS, LANES), jnp.int32)
    jhi = jnp.zeros((N_EXPERTS, LANES), jnp.int32)
    for k in range(N_RB):
        lo_k = jnp.sum((tile_ok & (a_next <= float(RB * k))).astype(jnp.int32), axis=1, keepdims=True)
        hi_k = jnp.sum((tile_ok & (a_acc < float(RB * (k + 1)))).astype(jnp.int32), axis=1, keepdims=True) - 1
        jlo = jnp.where(lane == k, lo_k, jlo)
        jhi = jnp.where(lane == k, hi_k, jhi)
    jlo_ref[...] = jlo
    jhi_ref[...] = jhi


def _route(affb):
    sds = jax.ShapeDtypeStruct
    small = sds((N_EXPERTS, LANES), jnp.int32)
    return pl.pallas_call(
        _route_kernel,
        out_shape=[sds((NB_ALL, N_EXPERTS, CHUNK), jnp.int32), sds((T_ALL, N_EXPERTS), F32),
                   small, small, small],
        scratch_shapes=[pltpu.VMEM((N_EXPERTS, LANES), F32), pltpu.VMEM((2, N_EXPERTS, 1), F32)],
        compiler_params=pltpu.CompilerParams(vmem_limit_bytes=VMEM_LIMIT), name="route",
    )(affb)


def _expert_kernel(jlo_ref, jhi_ref, h2t_hbm, slotb_ref, affb_ref, wg_ref, wu_ref, wd_ref, o_ref,
                   wg_s, wu_s, wd_s, chunk_s, sem, acc_s, g_s, cnt_s):
    e = pl.program_id(0)
    k = pl.program_id(1)
    step = e * N_RB + k

    @pl.when(k == 0)
    def _():
        wg_s[...] = wg_ref[...].astype(BF16)
        wu_s[...] = wu_ref[...].astype(BF16)
        wd_s[...] = wd_ref[...].astype(BF16)

    def tile_range(s):
        i = (s // N_RB) * LANES + s % N_RB
        lo = jnp.clip(jlo_ref[i], 0, NT_ALL - 1)
        return lo, jnp.clip(jhi_ref[i] - lo + 1, 1, NT_ALL - lo)

    def chunk_tile(lo, c):
        return jnp.minimum(lo + GATHER_TILES * c, NT_ALL - GATHER_TILES)

    def chunk_copy(j0, buf):
        return pltpu.make_async_copy(
            h2t_hbm.at[:, pl.ds(pl.multiple_of(j0 * TM, TM), GATHER_TILES * TM)],
            chunk_s.at[buf], sem.at[buf])

    jlo, n = tile_range(step)
    nch = (n + GATHER_TILES - 1) // GATHER_TILES

    @pl.when(step == 0)
    def _():
        cnt_s[0] = 0
        chunk_copy(chunk_tile(jlo, 0), 0).start()

    done = cnt_s[0]
    acc_s[...] = jnp.zeros_like(acc_s)
    g_s[...] = jnp.zeros_like(g_s)
    ntok = GATHER_TILES * TM
    want = lax.broadcasted_iota(jnp.int32, (RB, ntok), 0) + k * RB
    lane_tile = lax.broadcasted_iota(jnp.int32, (1, ntok), 1) // TM

    def body(c, _):
        buf = (done + c) % 2
        j0 = chunk_tile(jlo, c)
        chunk_copy(j0, buf).wait()

        @pl.when(c + 1 < nch)
        def _():
            chunk_copy(chunk_tile(jlo, c + 1), 1 - buf).start()

        @pl.when((c + 1 == nch) & (step + 1 < N_EXPERTS * N_RB))
        def _():
            chunk_copy(chunk_tile(tile_range(step + 1)[0], 0), 1 - buf).start()

        nblk = ntok // CHUNK
        srow = jnp.concatenate([slotb_ref[2 * j0 + i, pl.ds(e, 1), :] for i in range(nblk)], axis=1)
        arow = jnp.concatenate([affb_ref[2 * j0 + i, pl.ds(e, 1), :] for i in range(nblk)], axis=1)
        fresh = j0 + lane_tile >= jlo + GATHER_TILES * c
        hit = (srow == want) & fresh
        onehot = jnp.where(hit, 1.0, 0.0).astype(BF16)
        acc_s[...] += _dot_nt(chunk_s[buf], onehot)
        g_s[...] += jnp.sum(jnp.where(hit, arow, 0.0), axis=1, keepdims=True)
        return 0

    lax.fori_loop(0, nch, body, 0)
    cnt_s[0] = done + nch
    xe = acc_s[...].T.astype(BF16)
    hid = _silu(_dot(xe, wg_s[...])) * _dot(xe, wu_s[...])
    o_ref[...] = _dot(hid.astype(BF16), wd_s[...]) * g_s[...]


def _experts(l, jlo, jhi, h2t, slotb, affb, w_gate, w_up, w_down):
    wsp = pl.BlockSpec((None, None, D_MODEL, EXPERT_FF), lambda e, j, *_: (l, e, 0, 0))
    wsd = pl.BlockSpec((None, None, EXPERT_FF, D_MODEL), lambda e, j, *_: (l, e, 0, 0))
    whole = lambda a: pl.BlockSpec(a.shape, lambda e, j, *_: (0,) * a.ndim)
    grid_spec = pltpu.PrefetchScalarGridSpec(
        num_scalar_prefetch=2, grid=(N_EXPERTS, N_RB),
        in_specs=[pl.BlockSpec(memory_space=pl.ANY), whole(slotb), whole(affb), wsp, wsp, wsd],
        out_specs=pl.BlockSpec((None, RB, D_MODEL), lambda e, j, *_: (e, j, 0)),
        scratch_shapes=[pltpu.VMEM((D_MODEL, EXPERT_FF), BF16), pltpu.VMEM((D_MODEL, EXPERT_FF), BF16),
                        pltpu.VMEM((EXPERT_FF, D_MODEL), BF16),
                        pltpu.VMEM((2, D_MODEL, GATHER_TILES * TM), BF16),
                        pltpu.SemaphoreType.DMA((2,)), pltpu.VMEM((D_MODEL, RB), F32),
                        pltpu.VMEM((RB, 1), F32), pltpu.SMEM((1,), jnp.int32)])
    return pl.pallas_call(
        _expert_kernel, grid_spec=grid_spec,
        out_shape=jax.ShapeDtypeStruct((N_EXPERTS, CAP_ALL, D_MODEL), F32),
        compiler_params=_cparams(2), name="expert_ffn",
    )(jlo, jhi, h2t, slotb, affb, w_gate, w_up, w_down)


def _combine_kernel(a_ref, ye_hbm, slott_ref, o_ref, win_s, sem, xwin_s, xsem):
    j = pl.program_id(0)
    nt = pl.num_programs(0)

    def first_row(e, jj):
        a = a_ref[e * LANES + jj]
        return pl.multiple_of(jnp.clip((a // 8) * 8, 0, CAP_ALL - WIN), 8)

    def win_copy(e, row0, buf):
        return pltpu.make_async_copy(ye_hbm.at[e, pl.ds(row0, WIN), :], win_s.at[buf, e], sem.at[buf, e])

    buf = j % 2

    @pl.when(j == 0)
    def _():
        for e in range(N_EXPERTS):
            win_copy(e, first_row(e, 0), 0).start()

    @pl.when(j + 1 < nt)
    def _():
        for e in range(N_EXPERTS):
            win_copy(e, first_row(e, j + 1), 1 - buf).start()

    def split(rows):
        hi = rows.astype(BF16)
        return hi, (rows - hi.astype(F32)).astype(BF16)

    lane = lax.broadcasted_iota(jnp.int32, (TM, LANES), 1)
    lane_f = lane.astype(F32)
    rows0 = []
    pieces = []
    for e in range(0, N_EXPERTS, LANES // WIN):
        tgt = None
        for i in range(LANES // WIN):
            row0 = first_row(e + i, j)
            win_copy(e + i, row0, buf).wait()
            rows0.append(row0)
            t_i = slott_ref[:, e + i:e + i + 1] - (row0 - WIN * i).astype(F32)
            in_win = (lane >= WIN * i) & (lane < WIN * (i + 1))
            tgt = jnp.where(in_win, t_i, -1.0) if tgt is None else jnp.where(in_win, t_i, tgt)
        pieces.append(jnp.where(tgt == lane_f, 1.0, 0.0).astype(BF16))
    onehot = jnp.concatenate(pieces, axis=1)
    hi, lo = split(win_s[buf].reshape(N_EXPERTS * WIN, D_MODEL))
    o_ref[...] = _dot(onehot, hi) + _dot(onehot, lo)

    lane_w = lax.broadcasted_iota(jnp.int32, (TM, WIN), 1).astype(F32)
    for e in range(N_EXPERTS):
        row0 = rows0[e]
        n_more = jnp.maximum((a_ref[e * LANES + j + 1] - row0 + WIN - 1) // WIN - 1, 0)

        def more(i, _):
            lo_slot = row0 + (i + 1) * WIN
            r = pl.multiple_of(jnp.minimum(lo_slot, CAP_ALL - WIN), 8)
            cp = pltpu.make_async_copy(ye_hbm.at[e, pl.ds(r, WIN), :], xwin_s, xsem)
            cp.start()
            cp.wait()
            scol = slott_ref[:, e:e + 1]
            scol = jnp.where(scol >= lo_slot.astype(F32), scol, -1.0)
            oh = jnp.where(scol - r.astype(F32) == lane_w, 1.0, 0.0).astype(BF16)
            xh, xl = split(xwin_s[...])
            o_ref[...] += _dot(oh, xh) + _dot(oh, xl)
            return 0

        lax.fori_loop(0, n_more, more, 0)


def _combine(a, ye, slott):
    tile = lambda w: pl.BlockSpec((TM, w), lambda i, *_: (i, 0))
    grid_spec = pltpu.PrefetchScalarGridSpec(
        num_scalar_prefetch=1, grid=(NT_ALL,),
        in_specs=[pl.BlockSpec(memory_space=pl.ANY), tile(N_EXPERTS)],
        out_specs=tile(D_MODEL),
        scratch_shapes=[pltpu.VMEM((2, N_EXPERTS, WIN, D_MODEL), F32),
                        pltpu.SemaphoreType.DMA((2, N_EXPERTS)),
                        pltpu.VMEM((WIN, D_MODEL), F32), pltpu.SemaphoreType.DMA(())])
    return pl.pallas_call(
        _combine_kernel, grid_spec=grid_spec,
        out_shape=jax.ShapeDtypeStruct((T_ALL, D_MODEL), F32),
        compiler_params=_cparams(1), name="moe_combine",
    )(a, ye, slott)


def _final_kernel(x_ref, moe_ref, mod_ref, g_ref, o_ref):
    g2 = mod_ref[...][:, 5 * D_MODEL:6 * D_MODEL]
    o_ref[...] = _rms(x_ref[...] + g2 * moe_ref[...], g_ref[...])


def _final(x, moe, mod_l, fg, sample):
    n, off = (T_SMP // TM, NT_CTX) if sample else (NT_CTX, 0)
    tile = pl.BlockSpec((TM, D_MODEL), lambda i: (off + i, 0))
    return pl.pallas_call(
        _final_kernel, grid=(n,),
        in_specs=[tile, tile, pl.BlockSpec((None, 1, 6 * D_MODEL), lambda i: (_mod_row(off + i), 0, 0)),
                  pl.BlockSpec(fg.shape, lambda i: (0, 0))],
        out_specs=pl.BlockSpec((TM, D_MODEL), lambda i: (i, 0)),
        out_shape=jax.ShapeDtypeStruct((n * TM, D_MODEL), F32),
        compiler_params=_cparams(1), name="final_norm",
    )(x, moe, mod_l, fg)


def _rope_tables():
    t = jnp.arange(DEC_SEQ)
    pos = jnp.stack([t // GRID_W, t % GRID_W], axis=-1).astype(F32)
    nf = DA_HD // 4
    inv = ROPE_BASE ** (-jnp.arange(nf, dtype=F32) / nf)
    ang = pos[:, :, None] * inv
    cos = jnp.cos(ang)
    sin = jnp.sin(ang)
    cos64 = jnp.stack([cos, cos], axis=2).reshape(DEC_SEQ, DA_HD)
    sin64 = jnp.stack([-sin, sin], axis=2).reshape(DEC_SEQ, DA_HD)
    cos_t = jnp.concatenate([jnp.ones((TM, LANES), F32), jnp.tile(cos64, (1, 2))], axis=0)
    sin_t = jnp.concatenate([jnp.zeros((TM, LANES), F32), jnp.tile(sin64, (1, 2))], axis=0)
    return cos_t, sin_t


def kernel(x_prompt, x_sample, cache_k, cache_v, state_ssm, state_mlstm_c, state_mlstm_n, state_mlstm_m, c, c_ctx, ada_w, ada_b, norm1_g, norm2_g, w_in, da_lambda, da_subln_g, mb_conv_w, mb_conv_b, mb_dt_bias, mb_a_log, mb_d, mb_norm_g, ml_gate_b, ml_norm_g, w_branch, w_mgate, b_mgate, w_out, router_w, ex_w_gate, ex_w_up, ex_w_down, final_g):
    x = jnp.concatenate([x_prompt.reshape(T_CTX, D_MODEL), x_sample.reshape(T_SMP, D_MODEL)], axis=0)
    cc = jnp.concatenate([c_ctx[None, :], c, jnp.zeros((16 - 1 - DEC_BATCH, D_MODEL), F32)], axis=0)
    mod = _modulation(cc, ada_w, ada_b).reshape(DEPTH, 16, 1, 6 * D_MODEL)
    cos_t, sin_t = _rope_tables()
    cache_k2 = cache_k.reshape(DEC_BATCH, DEPTH, PAST_LEN, DA_WIDTH)
    cache_v2 = cache_v.reshape(DEC_BATCH, DEPTH, PAST_LEN, DA_WIDTH)

    outs = {n: [] for n in ("k", "v", "ssm", "C", "n", "m")}
    moe = None
    for l in range(DEPTH):
        w = w_in[l]
        wm = jnp.concatenate([w[:, :2816], w[:, 2832:4368]], axis=1).astype(BF16)
        ws = jnp.concatenate([w[:, 2816:2832], w[:, 4368:4384]], axis=1).astype(BF16)
        res = _projection(x, moe, mod[l - 1] if l else None, mod[l], norm1_g[l][None], wm, ws, ws.T,
                          cos_t, sin_t)
        q, k, v, kf, vf, z, xbc, mq, mk, mv, mo, sm, smt = res[:13]
        if l:
            x = res[13]
        outs["k"].append(kf[:T_CTX].reshape(BATCH, SEQ, DA_HEADS, 2, DA_HD))
        outs["v"].append(vf[:T_CTX].reshape(BATCH, SEQ, DA_HEADS, 2 * DA_HD))

        lp = da_lambda[l]
        sg = da_subln_g[l][None]
        cw = mb_conv_w[l]
        cb = mb_conv_b[l][None]
        dtb = mb_dt_bias[l].reshape(16)
        a_neg = -jnp.exp(mb_a_log[l]).reshape(16)
        dexp = jnp.repeat(mb_d[l], MB_HD)[None]
        mng = mb_norm_g[l][None]
        gb = ml_gate_b[l].reshape(16)
        lng = ml_norm_g[l].reshape(1, ML_WIDTH)
        ys = []
        for sample in (False, True):
            nb = DEC_BATCH if sample else BATCH
            y_da = _attention(l, q, k, v, cache_k2, cache_v2, lp, sg, sample)
            if sample:
                h0 = state_ssm[:, l]
                c0 = state_mlstm_c[:, l]
                n0 = state_mlstm_n[:, l]
                m0 = state_mlstm_m[:, l]
            else:
                h0 = jnp.zeros((nb, 2, MB_HEADS, MB_HD, MB_STATE), F32)
                c0 = jnp.zeros((nb, 2, ML_HEADS, ML_DK, ML_DV), F32)
                n0 = jnp.zeros((nb, 2, ML_HEADS, ML_DK), F32)
                m0 = jnp.zeros((nb, 2, ML_HEADS), F32)
            h0t = jnp.transpose(h0, (0, 1, 4, 2, 3)).reshape(nb, 2, MB_STATE, MB_INNER)
            y_mb, hfin = _ssd(z, xbc, sm, smt, cw, cb, dtb, a_neg, dexp, mng, h0t, sample)
            y_ml, cfin, nfin, mfin = _mlstm(
                mq, mk, mv, mo, sm, smt, gb, lng,
                c0.reshape(nb, 2, 2, LANES, ML_DV), n0.reshape(nb, 2, 2, LANES, 1),
                m0.reshape(nb, 1, 8), sample)
            ys.append((y_da, y_mb, y_ml))
            if not sample:
                outs["ssm"].append(jnp.transpose(
                    hfin.reshape(nb, 2, MB_STATE, MB_HEADS, MB_HD), (0, 1, 3, 4, 2)))
                outs["C"].append(cfin.reshape(nb, 2, ML_HEADS, ML_DK, ML_DV))
                outs["n"].append(nfin.reshape(nb, 2, ML_HEADS, ML_DK))
                outs["m"].append(mfin.reshape(nb, 2, ML_HEADS))
        yda, ymb, yml = (jnp.concatenate([ys[0][j], ys[1][j]], axis=0) for j in range(3))

        x, h2t, affb = _merge(x, mod[l], norm1_g[l][None], yda, ymb, yml,
                              w_mgate[l].astype(BF16), b_mgate[l][None], w_branch[l].astype(BF16),
                              w_out[l].astype(BF16), norm2_g[l][None], router_w[l].T)
        slotb, slott, a_cnt, jlo, jhi = _route(affb)
        ye = _experts(l, jlo.reshape(-1), jhi.reshape(-1), h2t, slotb, affb,
                      ex_w_gate, ex_w_up, ex_w_down)
        moe = _combine(a_cnt.reshape(-1), ye, slott)

    fg = final_g[None]
    y_prompt = _final(x, moe, mod[DEPTH - 1], fg, False).reshape(BATCH, SEQ, D_MODEL)
    y_sample = _final(x, moe, mod[DEPTH - 1], fg, True).reshape(DEC_BATCH, DEC_SEQ, D_MODEL)
    return (y_prompt, y_sample, jnp.stack(outs["k"], axis=1), jnp.stack(outs["v"], axis=1),
            jnp.stack(outs["ssm"], axis=1), jnp.stack(outs["C"], axis=1),
            jnp.stack(outs["n"], axis=1), jnp.stack(outs["m"], axis=1))
```

```python
import functools
import math

import jax
import jax.numpy as jnp
from jax import lax
from jax.experimental import pallas as pl
from jax.experimental.pallas import tpu as pltpu

F32 = jnp.float32
BF16 = jnp.bfloat16

D_MODEL = 1024
BATCH = 16
SEQ = 256
DEPTH = 2
DEC_BATCH = 8
DEC_SEQ = 2048
PAST_LEN = 512
GRID_W = 64
EPS = 1e-6
CHUNK = 128
ROPE_BASE = 10000.0
DA_HEADS = 4
DA_HD = 64
DA_WIDTH = 512
MB_INNER = 512
MB_HD = 64
MB_HEADS = 8
MB_GROUPS = 2
MB_STATE = 64
MB_CONV_DIM = 768
ML_HEADS = 4
ML_DK = 64
ML_DV = 128
ML_WIDTH = 512
N_EXPERTS = 16
EC_FACTOR = 2
EXPERT_FF = 1024

T_CTX = BATCH * SEQ
T_SMP = DEC_BATCH * DEC_SEQ
T_ALL = T_CTX + T_SMP
TM = 256
NT_CTX = T_CTX // TM
NT_ALL = T_ALL // TM
NB_CTX = T_CTX // CHUNK
NB_ALL = T_ALL // CHUNK
CAP_CTX = EC_FACTOR * T_CTX // N_EXPERTS
CAP_SMP = EC_FACTOR * T_SMP // N_EXPERTS
CAP_ALL = CAP_CTX + CAP_SMP
RB = 256
N_RB = CAP_ALL // RB
WIN = 64
GATHER_TILES = 4
GATHER_AHEAD = 2
TILES_PER_REQ = DEC_SEQ // TM
TM_MERGE = 512
BRANCH_W = 512
LANES = 128
VMEM_LIMIT = 56 * 1024 * 1024
HI = lax.Precision.HIGHEST
LOG2E = math.log2(math.e)

C_Q, C_K, C_V, C_Z, C_XBC, C_MQ, C_MK, C_MV, C_MO, C_END = (
    0, 512, 1024, 1536, 2048, 2816, 3072, 3328, 3840, 4352)


def _mod_row(i):
    return jnp.where(i < NT_CTX, 0, 1 + (i - NT_CTX) // TILES_PER_REQ)


def _rope_blk(i):
    return jnp.where(i < NT_CTX, 0, 1 + (i - NT_CTX) % TILES_PER_REQ)


def _cparams(n_grid):
    return pltpu.CompilerParams(dimension_semantics=("arbitrary",) * n_grid,
                                vmem_limit_bytes=VMEM_LIMIT)


def _silu(x):
    return x * jax.nn.sigmoid(x)


def _softplus(x):
    u = jnp.exp(-jnp.abs(x))
    w = 1.0 + u
    l1p = jnp.where(w == 1.0, u, jnp.log(w) * (u / (w - 1.0)))
    return jnp.maximum(x, 0.0) + l1p


def _dot(a, b):
    return jnp.dot(a, b, preferred_element_type=F32)


def _dot_nt(a, b):
    return lax.dot_general(a, b, (((1,), (1,)), ((), ())), preferred_element_type=F32)


def _dot_hi(a, b):
    return jnp.dot(a, b, precision=HI, preferred_element_type=F32)


def _rms(x, g):
    return x * lax.rsqrt(jnp.mean(x * x, axis=-1, keepdims=True) + EPS) * g


def _mod_kernel(c_ref, w_ref, b_ref, o_ref):
    s = _silu(c_ref[...])
    o_ref[...] = _dot(s.astype(BF16), w_ref[...].astype(BF16)) + b_ref[...]


def _modulation(cc, ada_w, ada_b):
    tn = 1536
    return pl.pallas_call(
        _mod_kernel,
        grid=(DEPTH, 6 * D_MODEL // tn),
        in_specs=[pl.BlockSpec((16, D_MODEL), lambda l, j: (0, 0)),
                  pl.BlockSpec((None, D_MODEL, tn), lambda l, j: (l, 0, j)),
                  pl.BlockSpec((None, 1, tn), lambda l, j: (l, 0, j))],
        out_specs=pl.BlockSpec((None, 16, tn), lambda l, j: (l, 0, j)),
        out_shape=jax.ShapeDtypeStruct((DEPTH, 16, 6 * D_MODEL), F32),
        compiler_params=_cparams(2),
        name="adaln_mod",
    )(cc, ada_w, ada_b.reshape(DEPTH, 1, 6 * D_MODEL))


def _rope(t, cos, sin, first_half):
    outs = []
    for c in range(DA_WIDTH // LANES):
        xc = t[:, LANES * c:LANES * (c + 1)]
        partner = jnp.where(first_half, pltpu.roll(xc, LANES - 16, 1), pltpu.roll(xc, 16, 1))
        outs.append(xc * cos + partner * sin)
    return jnp.concatenate(outs, axis=1)


def _proj_body(x, mod_ref, g_ref, wm_ref, ws_ref, wst_ref, cos_ref, sin_ref,
               q_ref, k_ref, v_ref, kf_ref, vf_ref, z_ref, xbc_ref,
               mq_ref, mk_ref, mv_ref, mo_ref, sm_ref, smt_ref):
    mod = mod_ref[...]
    sh1 = mod[:, 0:D_MODEL]
    sc1 = mod[:, D_MODEL:2 * D_MODEL]
    h = _rms(x, g_ref[...]) * (1.0 + sc1) + sh1
    hb = h.astype(BF16)

    def proj(a, b):
        return _dot(hb, wm_ref[:, a:b])

    lane = lax.broadcasted_iota(jnp.int32, (TM, LANES), 1)
    first_half = (lane % 32) < 16
    cos = cos_ref[...]
    sin = sin_ref[...]
    q = proj(C_Q, C_K)
    k = proj(C_K, C_V)
    v = proj(C_V, C_Z)
    kf_ref[...] = k
    vf_ref[...] = v
    q_ref[...] = (_rope(q, cos, sin, first_half) * (DA_HD ** -0.5 * LOG2E)).astype(BF16)
    k_ref[...] = _rope(k, cos, sin, first_half).astype(BF16)
    v_ref[...] = v.astype(BF16)
    z_ref[...] = proj(C_Z, C_XBC)
    xbc_ref[...] = proj(C_XBC, C_MQ)
    mq_ref[...] = proj(C_MQ, C_MK).astype(BF16)
    mk_ref[...] = (proj(C_MK, C_MV) * (ML_DK ** -0.5)).astype(BF16)
    mv_ref[...] = proj(C_MV, C_MO).astype(BF16)
    mo_ref[...] = proj(C_MO, C_END)
    sm_ref[...] = _dot(hb, ws_ref[...])
    st = _dot_nt(wst_ref[...], hb)
    for j in range(TM // CHUNK):
        smt_ref[j] = st[:, CHUNK * j:CHUNK * (j + 1)]


def _proj_kernel_first(xp_ref, xs_ref, *refs):
    x = jnp.where(pl.program_id(0) < NT_CTX, xp_ref[...], xs_ref[...])
    refs[-1][...] = x
    _proj_body(x, *refs[:-1])


def _proj_kernel_next(x_ref, moe_ref, modp_ref, *refs):
    g2 = modp_ref[...][:, 5 * D_MODEL:6 * D_MODEL]
    x = x_ref[...] + g2 * moe_ref[...]
    refs[-1][...] = x
    _proj_body(x, *refs[:-1])


def _ctx_tile(w):
    return pl.BlockSpec((TM, w), lambda i: (jnp.minimum(i, NT_CTX - 1), 0))


def _smp_tile(w):
    return pl.BlockSpec((TM, w), lambda i: (jnp.maximum(i - NT_CTX, 0), 0))


def _projection(xs, moe_prev, mod_prev, mod_l, g1, wm, ws, wst, cos_t, sin_t):
    tile = lambda w: pl.BlockSpec((TM, w), lambda i: (i, 0))
    full = lambda a: pl.BlockSpec(a.shape, lambda i: (0,) * a.ndim)
    modspec = pl.BlockSpec((None, 1, 6 * D_MODEL), lambda i: (_mod_row(i), 0, 0))
    if moe_prev is None:
        in_specs = [_ctx_tile(D_MODEL), _smp_tile(D_MODEL)]
        args = list(xs)
        kern = _proj_kernel_first
    else:
        in_specs = [tile(D_MODEL), tile(D_MODEL), modspec]
        args = [xs, moe_prev, mod_prev]
        kern = _proj_kernel_next
    in_specs += [modspec, full(g1), full(wm), full(ws), full(wst),
                 pl.BlockSpec((TM, LANES), lambda i: (_rope_blk(i), 0)),
                 pl.BlockSpec((TM, LANES), lambda i: (_rope_blk(i), 0))]
    args += [mod_l, g1, wm, ws, wst, cos_t, sin_t]
    ctx_only = pl.BlockSpec((TM, DA_WIDTH), lambda i: (jnp.minimum(i, NT_CTX), 0))
    out_specs = [tile(512), tile(512), tile(512), ctx_only, ctx_only, tile(512), tile(768),
                 tile(256), tile(256), tile(512), tile(512), tile(32),
                 pl.BlockSpec((TM // CHUNK, 32, CHUNK), lambda i: (i, 0, 0)), tile(D_MODEL)]
    sds = jax.ShapeDtypeStruct
    out_shape = [sds((T_ALL, 512), BF16), sds((T_ALL, 512), BF16), sds((T_ALL, 512), BF16),
                 sds((T_CTX + TM, 512), F32), sds((T_CTX + TM, 512), F32),
                 sds((T_ALL, 512), F32), sds((T_ALL, 768), F32),
                 sds((T_ALL, 256), BF16), sds((T_ALL, 256), BF16), sds((T_ALL, 512), BF16),
                 sds((T_ALL, 512), F32), sds((T_ALL, 32), F32),
                 sds((T_ALL // CHUNK, 32, CHUNK), F32), sds((T_ALL, D_MODEL), F32)]
    return pl.pallas_call(
        kern, grid=(NT_ALL,), in_specs=in_specs, out_specs=out_specs, out_shape=out_shape,
        compiler_params=_cparams(1), name="in_proj",
    )(*args)


def _attn_body(lam_init, q_ref, k_ref, v_ref, kc_ref, vc_ref, lp_ref, g_ref, o_ref):
    lp = lp_ref[...]
    s01 = jnp.sum(lp[0:1] * lp[1:2], axis=-1, keepdims=True)
    s23 = jnp.sum(lp[2:3] * lp[3:4], axis=-1, keepdims=True)
    lam = jnp.exp(s01) - jnp.exp(s23) + lam_init
    tq = q_ref.shape[0]
    lane = lax.broadcasted_iota(jnp.int32, (tq, LANES), 1)
    g = g_ref[...]
    for h in range(DA_HEADS):
        cols = slice(LANES * h, LANES * (h + 1))
        qh = q_ref[:, cols]
        kh = k_ref[:, cols]
        vh = v_ref[:, cols]
        if kc_ref is not None:
            kch = kc_ref[:, cols].astype(BF16)
            vch = vc_ref[:, cols].astype(BF16)
        parts = []
        for m in range(2):
            qm = jnp.where((lane < DA_HD) == (m == 0), qh, jnp.zeros_like(qh))
            s = _dot_nt(qm, kh)
            mx = jnp.max(s, axis=-1, keepdims=True)
            if kc_ref is not None:
                sc = _dot_nt(qm, kch)
                mx = jnp.maximum(mx, jnp.max(sc, axis=-1, keepdims=True))
                ec = jnp.exp2(sc - mx)
            e = jnp.exp2(s - mx)
            den = jnp.sum(e, axis=-1, keepdims=True)
            acc = _dot(e.astype(BF16), vh)
            if kc_ref is not None:
                den = den + jnp.sum(ec, axis=-1, keepdims=True)
                acc = acc + _dot(ec.astype(BF16), vch)
            parts.append(acc / den)
        att = parts[0] - lam * parts[1]
        o_ref[:, cols] = (_rms(att, g) * (1.0 - lam_init)).astype(BF16)


def _attn_kernel_ctx(lam_init, q_ref, k_ref, v_ref, lp_ref, g_ref, o_ref):
    _attn_body(lam_init, q_ref, k_ref, v_ref, None, None, lp_ref, g_ref, o_ref)


def _attn_kernel_smp(lam_init, q_ref, k_ref, v_ref, kc_ref, vc_ref, lp_ref, g_ref, o_ref):
    _attn_body(lam_init, q_ref, k_ref, v_ref, kc_ref, vc_ref, lp_ref, g_ref, o_ref)


def _attention(l, q, k, v, cache_k, cache_v, lp, g, sample):
    lam_init = 0.8 - 0.6 * math.exp(-0.3 * l)
    full = lambda a: pl.BlockSpec(a.shape, lambda *_: (0,) * a.ndim)
    sds = jax.ShapeDtypeStruct
    if not sample:
        blk = pl.BlockSpec((SEQ, DA_WIDTH), lambda b: (b, 0))
        return pl.pallas_call(
            functools.partial(_attn_kernel_ctx, lam_init), grid=(BATCH,),
            in_specs=[blk, blk, blk, full(lp), full(g)],
            out_specs=blk, out_shape=sds((T_CTX, DA_WIDTH), BF16),
            compiler_params=_cparams(1), name="diff_attn_ctx",
        )(q, k, v, lp, g)
    tq = 256
    nq = DEC_SEQ // tq
    off_q = T_CTX // tq
    off_k = T_CTX // DEC_SEQ
    qblk = pl.BlockSpec((tq, DA_WIDTH), lambda b, i: (off_q + b * nq + i, 0))
    kblk = pl.BlockSpec((DEC_SEQ, DA_WIDTH), lambda b, i: (off_k + b, 0))
    cblk = pl.BlockSpec((None, None, PAST_LEN, DA_WIDTH), lambda b, i: (b, l, 0, 0))
    return pl.pallas_call(
        functools.partial(_attn_kernel_smp, lam_init), grid=(DEC_BATCH, nq),
        in_specs=[qblk, kblk, kblk, cblk, cblk, full(lp), full(g)],
        out_specs=pl.BlockSpec((tq, DA_WIDTH), lambda b, i: (b * nq + i, 0)),
        out_shape=sds((T_SMP, DA_WIDTH), BF16),
        compiler_params=_cparams(2), name="diff_attn_smp",
    )(q, k, v, cache_k, cache_v, lp, g)


def _tri_masks():
    r = lax.broadcasted_iota(jnp.int32, (CHUNK, CHUNK), 0)
    c = lax.broadcasted_iota(jnp.int32, (CHUNK, CHUNK), 1)
    return c <= r, c >= r


def _ssd_kernel(L, z_ref, xbc_ref, sm_ref, smt_ref, cw_ref, cb_ref, dtbr_ref, dtbc_ref,
                ar_ref, ac_ref, dexp_ref, ng_ref, h0_ref,
                y_ref, hfin_ref, xc_s, yacc_s, ht_s):
    nc = L // CHUNK
    low, upp = _tri_masks()
    low_f = low.astype(F32)
    upp_f = upp.astype(F32)
    lane512 = lax.broadcasted_iota(jnp.int32, (1, MB_INNER), 1)
    row16 = lax.broadcasted_iota(jnp.int32, (16, MB_INNER), 0)
    lane16 = lax.broadcasted_iota(jnp.int32, (16, MB_INNER), 1)
    lane128 = lax.broadcasted_iota(jnp.int32, (CHUNK, LANES), 1)
    rowblk = lax.broadcasted_iota(jnp.int32, (LANES, MB_INNER), 0) // MB_STATE
    colblk = lax.broadcasted_iota(jnp.int32, (LANES, MB_INNER), 1) // (MB_INNER // MB_GROUPS)
    same_group = rowblk == colblk
    cw = cw_ref[...]
    cbias = cb_ref[...]

    def conv_chunk(c, _):
        base = pl.multiple_of(c * CHUNK, CHUNK)
        x = xbc_ref[pl.ds(base, CHUNK), :]
        prev = xbc_ref[pl.ds(jnp.maximum(base - 1, 0), 1), :]
        nxt = xbc_ref[pl.ds(jnp.minimum(base + CHUNK, L - 1), 1), :]
        prev = jnp.where(c == 0, 0.0, prev)
        nxt = jnp.where(c == nc - 1, 0.0, nxt)
        row = lax.broadcasted_iota(jnp.int32, (CHUNK, 1), 0)
        xp = jnp.where(row == 0, prev, pltpu.roll(x, 1, 0))
        xn = jnp.where(row == CHUNK - 1, nxt, pltpu.roll(x, CHUNK - 1, 0))
        conv = xp * cw[0:1] + x * cw[1:2] + xn * cw[2:3] + cbias
        xc_s[pl.ds(base, CHUNK), :] = _silu(conv)
        return 0

    lax.fori_loop(0, nc, conv_chunk, 0)

    h0 = h0_ref[...]
    for d in range(2):
        h0d = h0[d]
        ht_s[d] = jnp.concatenate(
            [jnp.where(lane512 < MB_INNER // 2, h0d, 0.0),
             jnp.where(lane512 >= MB_INNER // 2, h0d, 0.0)], axis=0)

    def chunk_dir(c, d):
        base = pl.multiple_of(c * CHUNK, CHUNK)
        rows = pl.ds(base, CHUNK)
        xs = xc_s[rows, 0:MB_INNER]
        bm = xc_s[rows, MB_INNER:MB_INNER + LANES]
        cm = xc_s[rows, MB_INNER + LANES:MB_CONV_DIM]
        dt_col = _softplus(sm_ref[rows, 0:16] + dtbr_ref[...])
        a_col = dt_col * ar_ref[...]
        dt_row = _softplus(smt_ref[c][0:16, :] + dtbc_ref[...])
        a_row = dt_row * ac_ref[...]
        if d == 0:
            cum_col = _dot_hi(low_f, a_col)
            cum_row = _dot_hi(a_row, upp_f)
            mask = low
            last = CHUNK - 1
        else:
            cum_col = _dot_hi(upp_f, a_col)
            cum_row = _dot_hi(a_row, low_f)
            mask = upp
            last = 0
        cum_last = cum_col[last:last + 1, :]
        expand = (row16 == 8 * d + lane16 // MB_HD).astype(F32)
        w_exp = _dot_hi(jnp.exp(cum_last - cum_col) * dt_col, expand)
        g_exp = _dot_hi(jnp.exp(cum_col), expand)
        cd_exp = _dot_hi(jnp.broadcast_to(jnp.exp(cum_last), (8, 16)), expand)[0:1]
        xw = (xs * w_exp).astype(BF16)
        xsb = xs.astype(BF16)
        bb = bm.astype(BF16)
        cb = cm.astype(BF16)
        ht = ht_s[d]
        y_off = _dot(cb, ht.astype(BF16)) * g_exp
        s_new = _dot(bm.T.astype(BF16), xw)
        ht_s[d] = ht * cd_exp + jnp.where(same_group, s_new, 0.0)
        pairs = []
        for g in range(MB_GROUPS):
            cg = jnp.where((lane128 < MB_STATE) == (g == 0), cb, jnp.zeros_like(cb))
            cbg = _dot_nt(cg, bb)
            for pr in range(2):
                ys = []
                for hh in range(2):
                    ci = 8 * d + 4 * g + 2 * pr + hh
                    seg = cum_col[:, ci:ci + 1] - cum_row[ci:ci + 1, :]
                    m = jnp.where(mask, jnp.exp(seg), 0.0) * cbg * dt_row[ci:ci + 1, :]
                    k = 2 * g + pr
                    ys.append(_dot(m.astype(BF16), xsb[:, LANES * k:LANES * (k + 1)]))
                pairs.append(jnp.where(lane128 < MB_HD, ys[0], ys[1]))
        yacc_s[d, rows, :] = jnp.concatenate(pairs, axis=1) + y_off

    def both(i, _):
        chunk_dir(i, 0)
        chunk_dir(nc - 1 - i, 1)
        return 0

    lax.fori_loop(0, nc, both, 0)

    dexp = dexp_ref[...]
    ng = ng_ref[...]

    def fin(c, _):
        rows = pl.ds(pl.multiple_of(c * CHUNK, CHUNK), CHUNK)
        y = yacc_s[0, rows, :] + yacc_s[1, rows, :] + dexp * xc_s[rows, 0:MB_INNER]
        y = y * _silu(z_ref[rows, :])
        y_ref[rows, :] = _rms(y, ng).astype(BF16)
        return 0

    lax.fori_loop(0, nc, fin, 0)
    for d in range(2):
        ht = ht_s[d]
        hfin_ref[d] = ht[0:MB_STATE, :] + ht[MB_STATE:2 * MB_STATE, :]


def _ssd(z, xbc, sm, smt, cw, cb, dtb, a_neg, dexp, ng, h0t, sample):
    nb, L, off = (DEC_BATCH, DEC_SEQ, T_CTX // DEC_SEQ) if sample else (BATCH, SEQ, 0)
    full = lambda a: pl.BlockSpec(a.shape, lambda b: (0,) * a.ndim)
    seq = lambda w: pl.BlockSpec((L, w), lambda b: (off + b, 0))
    dtb_r, dtb_c = dtb.reshape(1, 16), dtb.reshape(16, 1)
    a_r, a_c = a_neg.reshape(1, 16), a_neg.reshape(16, 1)
    sds = jax.ShapeDtypeStruct
    return pl.pallas_call(
        functools.partial(_ssd_kernel, L), grid=(nb,),
        in_specs=[seq(MB_INNER), seq(MB_CONV_DIM), seq(32),
                  pl.BlockSpec((L // CHUNK, 32, CHUNK), lambda b: (off + b, 0, 0)),
                  full(cw), full(cb), full(dtb_r), full(dtb_c), full(a_r), full(a_c),
                  full(dexp), full(ng),
                  pl.BlockSpec((None, 2, MB_STATE, MB_INNER), lambda b: (b, 0, 0, 0))],
        out_specs=[pl.BlockSpec((L, MB_INNER), lambda b: (b, 0)),
                   pl.BlockSpec((None, 2, MB_STATE, MB_INNER), lambda b: (b, 0, 0, 0))],
        out_shape=[sds((nb * L, MB_INNER), BF16), sds((nb, 2, MB_STATE, MB_INNER), F32)],
        scratch_shapes=[pltpu.VMEM((L, MB_CONV_DIM), F32), pltpu.VMEM((2, L, MB_INNER), F32),
                        pltpu.VMEM((2, LANES, MB_INNER), F32)],
        compiler_params=_cparams(1), name="ssd_smp" if sample else "ssd_ctx",
    )(z, xbc, sm, smt, cw, cb, dtb_r, dtb_c, a_r, a_c, dexp, ng, h0t)


def _mlstm_kernel(L, q_ref, k_ref, v_ref, o_ref, sm_ref, smt_ref, gbr_ref, gbc_ref, ng_ref,
                  c0_ref, n0_ref, m0_ref,
                  y_ref, cf_ref, nf_ref, mf_ref, hacc_s, c_s, n_s):
    nc = L // CHUNK
    low, upp = _tri_masks()
    low_f = low.astype(F32)
    upp_f = upp.astype(F32)
    lane128 = lax.broadcasted_iota(jnp.int32, (CHUNK, LANES), 1)
    lane8 = lax.broadcasted_iota(jnp.int32, (1, 2 * ML_HEADS), 1)
    neg_inf = -jnp.inf

    c_s[...] = c0_ref[...]
    n_s[...] = n0_ref[...]
    m0 = m0_ref[...]

    def both(i, m_in):
        jobs = ((0, i), (1, nc - 1 - i))
        heads = [(d, h) for d, _ in jobs for h in range(ML_HEADS)]
        rows, pre_row, b_col, b_row = {}, {}, {}, {}
        for d, c in jobs:
            rows[d] = pl.ds(pl.multiple_of(c * CHUNK, CHUNK), CHUNK)
            pre_col = sm_ref[rows[d], 16:32] + gbr_ref[...]
            pre_row[d] = smt_ref[c][16:32, :] + gbc_ref[...]
            lf_col = -_softplus(-pre_col)
            lf_row = -_softplus(-pre_row[d])
            b_col[d] = _dot_hi(upp_f if d else low_f, lf_col)
            b_row[d] = _dot_hi(lf_row, low_f if d else upp_f)
        kt, cst, nst, vh, qk, qc, qn = {}, {}, {}, {}, {}, {}, {}
        for d, _ in jobs:
            for pr in range(2):
                qp = q_ref[rows[d], LANES * pr:LANES * (pr + 1)]
                kp = k_ref[rows[d], LANES * pr:LANES * (pr + 1)]
                kt[d, pr] = kp.astype(F32).T
                cst[d, pr] = c_s[d, pr]
                nst[d, pr] = n_s[d, pr]
                cb16 = cst[d, pr].astype(BF16)
                nb16 = jnp.broadcast_to(nst[d, pr], (LANES, LANES)).astype(BF16)
                for hh in range(2):
                    h = 2 * pr + hh
                    qm = jnp.where((lane128 < ML_DK) == (hh == 0), qp, jnp.zeros_like(qp))
                    vh[d, h] = v_ref[rows[d], ML_DV * h:ML_DV * (h + 1)]
                    qk[d, h] = _dot_nt(qm, kp)
                    qc[d, h] = _dot(qm, cb16)
                    qn[d, h] = _dot(qm, nb16)
        m_t, s_intra, s_inter, rsum, m_new, s_old, kts = {}, {}, {}, {}, {}, {}, {}
        for d, h in heads:
            m_st = m_in[:, 4 * d + h:4 * d + h + 1]
            bcol = b_col[d][:, 8 * d + 4 + h:8 * d + 5 + h]
            brow = b_row[d][8 * d + 4 + h:8 * d + 5 + h, :]
            li_row = pre_row[d][8 * d + h:8 * d + h + 1, :]
            dm = jnp.where(upp if d else low, bcol - brow + li_row, neg_inf)
            inter = bcol + m_st
            m_t[d, h] = jnp.maximum(inter, jnp.max(dm, axis=-1, keepdims=True))
            s_intra[d, h] = jnp.exp(dm - m_t[d, h]) * qk[d, h]
            s_inter[d, h] = jnp.exp(inter - m_t[d, h])
            rsum[d, h] = jnp.sum(s_intra[d, h], axis=-1, keepdims=True)
            last = 0 if d else CHUNK - 1
            b_end = brow[:, last:last + 1]
            w_end = b_end - brow + li_row
            m_new[d, h] = jnp.maximum(b_end + m_st, jnp.max(w_end, axis=-1, keepdims=True))
            s_old[d, h] = jnp.exp(b_end + m_st - m_new[d, h])
            half = slice(ML_DK * (h % 2), ML_DK * (h % 2 + 1))
            kts[d, h] = kt[d, h // 2][half, :] * jnp.exp(w_end - m_new[d, h])
        pv, kv = {}, {}
        for d, h in heads:
            pv[d, h] = _dot(s_intra[d, h].astype(BF16), vh[d, h])
            kv[d, h] = _dot(kts[d, h].astype(BF16), vh[d, h])
        m_out = m_in
        for d, h in heads:
            num = s_inter[d, h] * qc[d, h] + pv[d, h]
            den = s_inter[d, h] * qn[d, h] + rsum[d, h]
            hout = num / jnp.maximum(jnp.abs(den), jnp.exp(-m_t[d, h]))
            hacc_s[d, rows[d], ML_DV * h:ML_DV * (h + 1)] = hout
            m_out = jnp.where(lane8 == 4 * d + h, m_new[d, h], m_out)
        for d, _ in jobs:
            for pr in range(2):
                c_new, n_new = [], []
                for hh in range(2):
                    h = 2 * pr + hh
                    half = slice(ML_DK * hh, ML_DK * (hh + 1))
                    c_new.append(s_old[d, h] * cst[d, pr][half, :] + kv[d, h])
                    n_new.append(s_old[d, h] * nst[d, pr][half, :]
                                 + jnp.sum(kts[d, h], axis=-1, keepdims=True))
                c_s[d, pr] = jnp.concatenate(c_new, axis=0)
                n_s[d, pr] = jnp.concatenate(n_new, axis=0)
        return m_out

    m_fin = lax.fori_loop(0, nc, both, m0)

    ng = ng_ref[...]

    def fin(c, _):
        rows = pl.ds(pl.multiple_of(c * CHUNK, CHUNK), CHUNK)
        for h in range(ML_HEADS):
            cols = slice(ML_DV * h, ML_DV * (h + 1))
            y = _rms(hacc_s[0, rows, cols] + hacc_s[1, rows, cols], ng[:, cols]) * jax.nn.sigmoid(o_ref[rows, cols])
            y_ref[rows, cols] = y.astype(BF16)
        return 0

    lax.fori_loop(0, nc, fin, 0)
    cf_ref[...] = c_s[...]
    nf_ref[...] = n_s[...]
    mf_ref[...] = m_fin


def _mlstm(q, k, v, o, sm, smt, gb, ng, c0, n0, m0, sample):
    nb, L, off = (DEC_BATCH, DEC_SEQ, T_CTX // DEC_SEQ) if sample else (BATCH, SEQ, 0)
    full = lambda a: pl.BlockSpec(a.shape, lambda b: (0,) * a.ndim)
    seq = lambda w: pl.BlockSpec((L, w), lambda b: (off + b, 0))
    gb_r, gb_c = gb.reshape(1, 16), gb.reshape(16, 1)
    st_c = pl.BlockSpec((None, 2, 2, LANES, ML_DV), lambda b: (b, 0, 0, 0, 0))
    st_n = pl.BlockSpec((None, 2, 2, LANES, 1), lambda b: (b, 0, 0, 0, 0))
    st_m = pl.BlockSpec((None, 1, 8), lambda b: (b, 0, 0))
    sds = jax.ShapeDtypeStruct
    return pl.pallas_call(
        functools.partial(_mlstm_kernel, L), grid=(nb,),
        in_specs=[seq(256), seq(256), seq(512), seq(512), seq(32),
                  pl.BlockSpec((L // CHUNK, 32, CHUNK), lambda b: (off + b, 0, 0)),
                  full(gb_r), full(gb_c), full(ng), st_c, st_n, st_m],
        out_specs=[pl.BlockSpec((L, ML_WIDTH), lambda b: (b, 0)), st_c, st_n, st_m],
        out_shape=[sds((nb * L, ML_WIDTH), BF16), sds((nb, 2, 2, LANES, ML_DV), F32),
                   sds((nb, 2, 2, LANES, 1), F32), sds((nb, 1, 8), F32)],
        scratch_shapes=[pltpu.VMEM((2, L, ML_WIDTH), F32), pltpu.VMEM((2, 2, LANES, ML_DV), F32),
                        pltpu.VMEM((2, 2, LANES, 1), F32)],
        compiler_params=_cparams(1), name="mlstm_smp" if sample else "mlstm_ctx",
    )(q, k, v, o, sm, smt, gb_r, gb_c, ng, c0, n0, m0)


def _merge_kernel(x_ref, mod_ref, g1_ref, ydac_ref, ymbc_ref, ymlc_ref, ydas_ref, ymbs_ref, ymls_ref,
                  wg_ref, bg_ref, wb_ref, wo_ref, g2_ref, rwt_ref, xo_ref, h2t_ref, affb_ref):
    x = x_ref[...]
    mod = mod_ref[...]
    sh1, sc1, gt1, sh2, sc2 = (mod[:, j * D_MODEL:(j + 1) * D_MODEL] for j in range(5))
    hb = (_rms(x, g1_ref[...]) * (1.0 + sc1) + sh1).astype(BF16)
    is_ctx = pl.program_id(0) < T_CTX // TM_MERGE
    merged = None
    for n, (yc_ref, ys_ref) in enumerate(((ydac_ref, ydas_ref), (ymbc_ref, ymbs_ref), (ymlc_ref, ymls_ref))):
        cols = slice(n * D_MODEL, (n + 1) * D_MODEL)
        gate = jax.nn.sigmoid(_dot(hb, wg_ref[:, cols]) + bg_ref[:, cols])
        y = jnp.where(is_ctx, yc_ref[...], ys_ref[...])
        term = gate * _dot(y, wb_ref[n])
        merged = term if merged is None else merged + term
    out = _dot(merged.astype(BF16), wo_ref[...])
    xn = x + gt1 * out
    xo_ref[...] = xn
    h2 = _rms(xn, g2_ref[...]) * (1.0 + sc2) + sh2
    h2t_ref[...] = h2.T.astype(BF16)
    logits = lax.dot_general(rwt_ref[...], h2, (((1,), (1,)), ((), ())),
                             precision=HI, preferred_element_type=F32)
    e = jnp.exp(logits - jnp.max(logits, axis=0, keepdims=True))
    aff = e / jnp.sum(e, axis=0, keepdims=True)
    for j in range(TM_MERGE // CHUNK):
        affb_ref[j] = aff[:, CHUNK * j:CHUNK * (j + 1)]


def _merge(x, mod_l, g1, ys_ctx, ys_smp, wg, bg, wb, wo, g2, rwt):
    tm = TM_MERGE
    n_ctx = T_CTX // tm
    per_req = DEC_SEQ // tm
    tile = lambda w: pl.BlockSpec((tm, w), lambda i: (i, 0))
    ctx_tile = pl.BlockSpec((tm, BRANCH_W), lambda i: (jnp.minimum(i, n_ctx - 1), 0))
    smp_tile = pl.BlockSpec((tm, BRANCH_W), lambda i: (jnp.maximum(i - n_ctx, 0), 0))
    full = lambda a: pl.BlockSpec(a.shape, lambda i: (0,) * a.ndim)
    mod_row = lambda i: jnp.where(i < n_ctx, 0, 1 + (i - n_ctx) // per_req)
    sds = jax.ShapeDtypeStruct
    return pl.pallas_call(
        _merge_kernel, grid=(T_ALL // tm,),
        in_specs=[tile(D_MODEL), pl.BlockSpec((None, 1, 6 * D_MODEL), lambda i: (mod_row(i), 0, 0)),
                  full(g1), ctx_tile, ctx_tile, ctx_tile, smp_tile, smp_tile, smp_tile,
                  full(wg), full(bg), full(wb), full(wo), full(g2), full(rwt)],
        out_specs=[tile(D_MODEL), pl.BlockSpec((D_MODEL, tm), lambda i: (0, i)),
                   pl.BlockSpec((tm // CHUNK, N_EXPERTS, CHUNK), lambda i: (i, 0, 0))],
        out_shape=[sds((T_ALL, D_MODEL), F32), sds((D_MODEL, T_ALL), BF16),
                   sds((NB_ALL, N_EXPERTS, CHUNK), F32)],
        compiler_params=_cparams(1), name="merge_out",
    )(x, mod_l, g1, *ys_ctx, *ys_smp, wg, bg, wb, wo, g2, rwt)


def _route_kernel(affb_ref, slotb_ref, slott_ref, a_ref, jlo_ref, jhi_ref, acc_s, run_s):
    r = lax.broadcasted_iota(jnp.int32, (CHUNK, CHUNK), 0)
    c = lax.broadcasted_iota(jnp.int32, (CHUNK, CHUNK), 1)
    upper = (r <= c).astype(BF16)
    eye = (r == c).astype(F32)
    lane = lax.broadcasted_iota(jnp.int32, (N_EXPERTS, LANES), 1)
    acc_s[...] = jnp.zeros_like(acc_s)
    run_s[...] = jnp.zeros_like(run_s)
    for b0, b1, cap in ((0, NB_CTX, CAP_CTX), (NB_CTX, NB_ALL, CAP_SMP)):
        aff = affb_ref[b0:b1]

        def search(i, thr_bits):
            cand = thr_bits | lax.shift_left(jnp.int32(1), 30 - i)
            cnt = jnp.sum((aff >= pltpu.bitcast(cand, F32)[None]).astype(jnp.int32), axis=0)
            cnt = jnp.sum(cnt, axis=1, keepdims=True)
            return jnp.where(cnt >= cap, cand, thr_bits)

        thr = pltpu.bitcast(lax.fori_loop(0, 31, search, jnp.zeros((N_EXPERTS, 1), jnp.int32)), F32)
        n_gt = jnp.sum(jnp.sum((aff > thr[None]).astype(jnp.int32), axis=0), axis=1, keepdims=True)
        need = (cap - n_gt).astype(F32)

        run_s[1] = jnp.zeros((N_EXPERTS, 1), F32)

        def blk(b, _):
            run_sel = run_s[0]
            run_eq = run_s[1]
            x = affb_ref[b]
            eq = x == thr
            eq_f = jnp.where(eq, 1.0, 0.0)
            eq_incl = _dot(eq_f.astype(BF16), upper)
            sel = (x > thr) | (eq & (run_eq + eq_incl - eq_f < need))
            sel_f = jnp.where(sel, 1.0, 0.0)
            sel_incl = _dot(sel_f.astype(BF16), upper)
            slot = jnp.where(sel, run_sel + sel_incl - sel_f, -1.0)
            slotb_ref[b] = slot.astype(jnp.int32)
            slott_ref[pl.ds(pl.multiple_of(b * CHUNK, CHUNK), CHUNK), :] = lax.dot_general(
                eye, slot, (((1,), (1,)), ((), ())), precision=HI, preferred_element_type=F32)
            acc_s[...] = jnp.where((lane == b // 2) & (b % 2 == 0), run_sel, acc_s[...])
            run_s[0] = run_sel + sel_incl[:, CHUNK - 1:CHUNK]
            run_s[1] = run_eq + eq_incl[:, CHUNK - 1:CHUNK]
            return 0

        lax.fori_loop(b0, b1, blk, 0)
    a_acc = jnp.where(lane == NT_ALL, run_s[0], acc_s[...])
    a_ref[...] = a_acc.astype(jnp.int32)
    a_next = pltpu.roll(a_acc, LANES - 1, 1)
    tile_ok = lane < NT_ALL
    jlo = jnp.zeros((N_EXPERTS, LANES), jnp.int32)
    jhi = jnp.zeros((N_EXPERTS, LANES), jnp.int32)
    for k in range(N_RB):
        lo_k = jnp.sum((tile_ok & (a_next <= float(RB * k))).astype(jnp.int32), axis=1, keepdims=True)
        hi_k = jnp.sum((tile_ok & (a_acc < float(RB * (k + 1)))).astype(jnp.int32), axis=1, keepdims=True) - 1
        jlo = jnp.where(lane == k, lo_k, jlo)
        jhi = jnp.where(lane == k, hi_k, jhi)
    jlo_ref[...] = jlo
    jhi_ref[...] = jhi


def _route(affb):
    sds = jax.ShapeDtypeStruct
    small = sds((N_EXPERTS, LANES), jnp.int32)
    return pl.pallas_call(
        _route_kernel,
        out_shape=[sds((NB_ALL, N_EXPERTS, CHUNK), jnp.int32), sds((T_ALL, N_EXPERTS), F32),
                   small, small, small],
        scratch_shapes=[pltpu.VMEM((N_EXPERTS, LANES), F32), pltpu.VMEM((2, N_EXPERTS, 1), F32)],
        compiler_params=pltpu.CompilerParams(vmem_limit_bytes=VMEM_LIMIT), name="route",
    )(affb)


def _expert_kernel(jlo_ref, jhi_ref, h2t_hbm, slotb_ref, affb_ref, wg_ref, wu_ref, wd_ref, o_ref,
                   wg_s, wu_s, wd_s, chunk_s, sem, acc_s, g_s, cnt_s):
    e = pl.program_id(0)
    k = pl.program_id(1)
    step = e * N_RB + k

    @pl.when(k == 0)
    def _():
        wg_s[...] = wg_ref[...].astype(BF16)
        wu_s[...] = wu_ref[...].astype(BF16)
        wd_s[...] = wd_ref[...].astype(BF16)

    def tile_range(s):
        i = (s // N_RB) * LANES + s % N_RB
        lo = jnp.clip(jlo_ref[i], 0, NT_ALL - 1)
        return lo, jnp.clip(jhi_ref[i] - lo + 1, 1, NT_ALL - lo)

    def chunk_tile(lo, c):
        return jnp.minimum(lo + GATHER_TILES * c, NT_ALL - GATHER_TILES)

    def chunk_copy(j0, buf):
        return pltpu.make_async_copy(
            h2t_hbm.at[:, pl.ds(pl.multiple_of(j0 * TM, TM), GATHER_TILES * TM)],
            chunk_s.at[buf], sem.at[buf])

    def n_chunks(s):
        return (tile_range(s)[1] + GATHER_TILES - 1) // GATHER_TILES

    n_steps = N_EXPERTS * N_RB

    def advance(s, c):
        wrap = (c + 1 >= n_chunks(jnp.minimum(s, n_steps - 1))) | (s >= n_steps)
        return jnp.where(wrap, s + 1, s), jnp.where(wrap, 0, c + 1)

    def start_at(s, c, buf):
        @pl.when(s < n_steps)
        def _():
            chunk_copy(chunk_tile(tile_range(jnp.minimum(s, n_steps - 1))[0], c), buf).start()

    jlo, n = tile_range(step)
    nch = n_chunks(step)

    @pl.when(step == 0)
    def _():
        cnt_s[0] = 0
        pos = (step, 0)
        for i in range(GATHER_AHEAD):
            start_at(pos[0], pos[1], i)
            pos = advance(*pos)

    done = cnt_s[0]
    acc_s[...] = jnp.zeros_like(acc_s)
    g_s[...] = jnp.zeros_like(g_s)
    ntok = GATHER_TILES * TM
    want = lax.broadcasted_iota(jnp.int32, (RB, ntok), 0) + k * RB
    lane_tile = lax.broadcasted_iota(jnp.int32, (1, ntok), 1) // TM

    def body(c, _):
        buf = (done + c) % (GATHER_AHEAD + 1)
        j0 = chunk_tile(jlo, c)
        chunk_copy(j0, buf).wait()
        pos = (step, c)
        for _i in range(GATHER_AHEAD):
            pos = advance(*pos)
        start_at(pos[0], pos[1], (done + c + GATHER_AHEAD) % (GATHER_AHEAD + 1))

        nblk = ntok // CHUNK
        srow = jnp.concatenate([slotb_ref[2 * j0 + i, pl.ds(e, 1), :] for i in range(nblk)], axis=1)
        arow = jnp.concatenate([affb_ref[2 * j0 + i, pl.ds(e, 1), :] for i in range(nblk)], axis=1)
        fresh = j0 + lane_tile >= jlo + GATHER_TILES * c
        hit = (srow == want) & fresh
        onehot = jnp.where(hit, 1.0, 0.0).astype(BF16)
        acc_s[...] += _dot_nt(chunk_s[buf], onehot)
        g_s[...] += jnp.sum(jnp.where(hit, arow, 0.0), axis=1, keepdims=True)
        return 0

    lax.fori_loop(0, nch, body, 0)
    cnt_s[0] = done + nch
    xe = acc_s[...].T.astype(BF16)
    hid = _silu(_dot(xe, wg_s[...])) * _dot(xe, wu_s[...])
    o_ref[...] = _dot(hid.astype(BF16), wd_s[...]) * g_s[...]


def _experts(l, jlo, jhi, h2t, slotb, affb, w_gate, w_up, w_down):
    wsp = pl.BlockSpec((None, None, D_MODEL, EXPERT_FF), lambda e, j, *_: (l, e, 0, 0))
    wsd = pl.BlockSpec((None, None, EXPERT_FF, D_MODEL), lambda e, j, *_: (l, e, 0, 0))
    whole = lambda a: pl.BlockSpec(a.shape, lambda e, j, *_: (0,) * a.ndim)
    grid_spec = pltpu.PrefetchScalarGridSpec(
        num_scalar_prefetch=2, grid=(N_EXPERTS, N_RB),
        in_specs=[pl.BlockSpec(memory_space=pl.ANY), whole(slotb), whole(affb), wsp, wsp, wsd],
        out_specs=pl.BlockSpec((None, RB, D_MODEL), lambda e, j, *_: (e, j, 0)),
        scratch_shapes=[pltpu.VMEM((D_MODEL, EXPERT_FF), BF16), pltpu.VMEM((D_MODEL, EXPERT_FF), BF16),
                        pltpu.VMEM((EXPERT_FF, D_MODEL), BF16),
                        pltpu.VMEM((GATHER_AHEAD + 1, D_MODEL, GATHER_TILES * TM), BF16),
                        pltpu.SemaphoreType.DMA((GATHER_AHEAD + 1,)), pltpu.VMEM((D_MODEL, RB), F32),
                        pltpu.VMEM((RB, 1), F32), pltpu.SMEM((1,), jnp.int32)])
    return pl.pallas_call(
        _expert_kernel, grid_spec=grid_spec,
        out_shape=jax.ShapeDtypeStruct((N_EXPERTS, CAP_ALL, D_MODEL), F32),
        compiler_params=_cparams(2), name="expert_ffn",
    )(jlo, jhi, h2t, slotb, affb, w_gate, w_up, w_down)


def _combine_kernel(a_ref, ye_hbm, slott_ref, o_ref, win_s, sem, xwin_s, xsem):
    j = pl.program_id(0)
    nt = pl.num_programs(0)

    def first_row(e, jj):
        a = a_ref[e * LANES + jj]
        return pl.multiple_of(jnp.clip((a // 8) * 8, 0, CAP_ALL - WIN), 8)

    def win_copy(e, row0, buf):
        return pltpu.make_async_copy(ye_hbm.at[e, pl.ds(row0, WIN), :], win_s.at[buf, e], sem.at[buf, e])

    buf = j % 2

    @pl.when(j == 0)
    def _():
        for e in range(N_EXPERTS):
            win_copy(e, first_row(e, 0), 0).start()

    @pl.when(j + 1 < nt)
    def _():
        for e in range(N_EXPERTS):
            win_copy(e, first_row(e, j + 1), 1 - buf).start()

    def split(rows):
        hi = rows.astype(BF16)
        return hi, (rows - hi.astype(F32)).astype(BF16)

    lane = lax.broadcasted_iota(jnp.int32, (TM, LANES), 1)
    lane_f = lane.astype(F32)
    rows0 = []
    pieces = []
    for e in range(0, N_EXPERTS, LANES // WIN):
        tgt = None
        for i in range(LANES // WIN):
            row0 = first_row(e + i, j)
            win_copy(e + i, row0, buf).wait()
            rows0.append(row0)
            t_i = slott_ref[:, e + i:e + i + 1] - (row0 - WIN * i).astype(F32)
            in_win = (lane >= WIN * i) & (lane < WIN * (i + 1))
            tgt = jnp.where(in_win, t_i, -1.0) if tgt is None else jnp.where(in_win, t_i, tgt)
        pieces.append(jnp.where(tgt == lane_f, 1.0, 0.0).astype(BF16))
    onehot = jnp.concatenate(pieces, axis=1)
    hi, lo = split(win_s[buf].reshape(N_EXPERTS * WIN, D_MODEL))
    o_ref[...] = _dot(onehot, hi) + _dot(onehot, lo)

    lane_w = lax.broadcasted_iota(jnp.int32, (TM, WIN), 1).astype(F32)
    for e in range(N_EXPERTS):
        row0 = rows0[e]
        n_more = jnp.maximum((a_ref[e * LANES + j + 1] - row0 + WIN - 1) // WIN - 1, 0)

        def more(i, _):
            lo_slot = row0 + (i + 1) * WIN
            r = pl.multiple_of(jnp.minimum(lo_slot, CAP_ALL - WIN), 8)
            cp = pltpu.make_async_copy(ye_hbm.at[e, pl.ds(r, WIN), :], xwin_s, xsem)
            cp.start()
            cp.wait()
            scol = slott_ref[:, e:e + 1]
            scol = jnp.where(scol >= lo_slot.astype(F32), scol, -1.0)
            oh = jnp.where(scol - r.astype(F32) == lane_w, 1.0, 0.0).astype(BF16)
            xh, xl = split(xwin_s[...])
            o_ref[...] += _dot(oh, xh) + _dot(oh, xl)
            return 0

        lax.fori_loop(0, n_more, more, 0)


def _combine(a, ye, slott):
    tile = lambda w: pl.BlockSpec((TM, w), lambda i, *_: (i, 0))
    grid_spec = pltpu.PrefetchScalarGridSpec(
        num_scalar_prefetch=1, grid=(NT_ALL,),
        in_specs=[pl.BlockSpec(memory_space=pl.ANY), tile(N_EXPERTS)],
        out_specs=tile(D_MODEL),
        scratch_shapes=[pltpu.VMEM((2, N_EXPERTS, WIN, D_MODEL), F32),
                        pltpu.SemaphoreType.DMA((2, N_EXPERTS)),
                        pltpu.VMEM((WIN, D_MODEL), F32), pltpu.SemaphoreType.DMA(())])
    return pl.pallas_call(
        _combine_kernel, grid_spec=grid_spec,
        out_shape=jax.ShapeDtypeStruct((T_ALL, D_MODEL), F32),
        compiler_params=_cparams(1), name="moe_combine",
    )(a, ye, slott)


def _final_kernel(x_ref, moe_ref, mod_ref, g_ref, o_ref):
    g2 = mod_ref[...][:, 5 * D_MODEL:6 * D_MODEL]
    o_ref[...] = _rms(x_ref[...] + g2 * moe_ref[...], g_ref[...])


def _final(x, moe, mod_l, fg, sample):
    n, off = (T_SMP // TM, NT_CTX) if sample else (NT_CTX, 0)
    tile = pl.BlockSpec((TM, D_MODEL), lambda i: (off + i, 0))
    return pl.pallas_call(
        _final_kernel, grid=(n,),
        in_specs=[tile, tile, pl.BlockSpec((None, 1, 6 * D_MODEL), lambda i: (_mod_row(off + i), 0, 0)),
                  pl.BlockSpec(fg.shape, lambda i: (0, 0))],
        out_specs=pl.BlockSpec((TM, D_MODEL), lambda i: (i, 0)),
        out_shape=jax.ShapeDtypeStruct((n * TM, D_MODEL), F32),
        compiler_params=_cparams(1), name="final_norm",
    )(x, moe, mod_l, fg)


def _rope_tables():
    t = jnp.arange(DEC_SEQ)
    pos = jnp.stack([t // GRID_W, t % GRID_W], axis=-1).astype(F32)
    nf = DA_HD // 4
    inv = ROPE_BASE ** (-jnp.arange(nf, dtype=F32) / nf)
    ang = pos[:, :, None] * inv
    cos = jnp.cos(ang)
    sin = jnp.sin(ang)
    cos64 = jnp.stack([cos, cos], axis=2).reshape(DEC_SEQ, DA_HD)
    sin64 = jnp.stack([-sin, sin], axis=2).reshape(DEC_SEQ, DA_HD)
    cos_t = jnp.concatenate([jnp.ones((TM, LANES), F32), jnp.tile(cos64, (1, 2))], axis=0)
    sin_t = jnp.concatenate([jnp.zeros((TM, LANES), F32), jnp.tile(sin64, (1, 2))], axis=0)
    return cos_t, sin_t


def kernel(x_prompt, x_sample, cache_k, cache_v, state_ssm, state_mlstm_c, state_mlstm_n, state_mlstm_m, c, c_ctx, ada_w, ada_b, norm1_g, norm2_g, w_in, da_lambda, da_subln_g, mb_conv_w, mb_conv_b, mb_dt_bias, mb_a_log, mb_d, mb_norm_g, ml_gate_b, ml_norm_g, w_branch, w_mgate, b_mgate, w_out, router_w, ex_w_gate, ex_w_up, ex_w_down, final_g):
    x = (x_prompt.reshape(T_CTX, D_MODEL), x_sample.reshape(T_SMP, D_MODEL))
    cc = jnp.concatenate([c_ctx[None, :], c, jnp.zeros((16 - 1 - DEC_BATCH, D_MODEL), F32)], axis=0)
    mod = _modulation(cc, ada_w, ada_b).reshape(DEPTH, 16, 1, 6 * D_MODEL)
    cos_t, sin_t = _rope_tables()
    cache_k2 = cache_k.reshape(DEC_BATCH, DEPTH, PAST_LEN, DA_WIDTH)
    cache_v2 = cache_v.reshape(DEC_BATCH, DEPTH, PAST_LEN, DA_WIDTH)

    outs = {n: [] for n in ("k", "v", "ssm", "C", "n", "m")}
    moe = None
    for l in range(DEPTH):
        w = w_in[l]
        wm = jnp.concatenate([w[:, :2816], w[:, 2832:4368]], axis=1).astype(BF16)
        ws = jnp.concatenate([w[:, 2816:2832], w[:, 4368:4384]], axis=1).astype(BF16)
        res = _projection(x, moe, mod[l - 1] if l else None, mod[l], norm1_g[l][None], wm, ws, ws.T,
                          cos_t, sin_t)
        q, k, v, kf, vf, z, xbc, mq, mk, mv, mo, sm, smt, x = res
        outs["k"].append(kf[:T_CTX].reshape(BATCH, SEQ, DA_HEADS, 2, DA_HD))
        outs["v"].append(vf[:T_CTX].reshape(BATCH, SEQ, DA_HEADS, 2 * DA_HD))

        lp = da_lambda[l]
        sg = da_subln_g[l][None]
        cw = mb_conv_w[l]
        cb = mb_conv_b[l][None]
        dtb = mb_dt_bias[l].reshape(16)
        a_neg = -jnp.exp(mb_a_log[l]).reshape(16)
        dexp = jnp.repeat(mb_d[l], MB_HD)[None]
        mng = mb_norm_g[l][None]
        gb = ml_gate_b[l].reshape(16)
        lng = ml_norm_g[l].reshape(1, ML_WIDTH)
        ys = []
        for sample in (False, True):
            nb = DEC_BATCH if sample else BATCH
            y_da = _attention(l, q, k, v, cache_k2, cache_v2, lp, sg, sample)
            if sample:
                h0 = state_ssm[:, l]
                c0 = state_mlstm_c[:, l]
                n0 = state_mlstm_n[:, l]
                m0 = state_mlstm_m[:, l]
            else:
                h0 = jnp.zeros((nb, 2, MB_HEADS, MB_HD, MB_STATE), F32)
                c0 = jnp.zeros((nb, 2, ML_HEADS, ML_DK, ML_DV), F32)
                n0 = jnp.zeros((nb, 2, ML_HEADS, ML_DK), F32)
                m0 = jnp.zeros((nb, 2, ML_HEADS), F32)
            h0t = jnp.transpose(h0, (0, 1, 4, 2, 3)).reshape(nb, 2, MB_STATE, MB_INNER)
            y_mb, hfin = _ssd(z, xbc, sm, smt, cw, cb, dtb, a_neg, dexp, mng, h0t, sample)
            y_ml, cfin, nfin, mfin = _mlstm(
                mq, mk, mv, mo, sm, smt, gb, lng,
                c0.reshape(nb, 2, 2, LANES, ML_DV), n0.reshape(nb, 2, 2, LANES, 1),
                m0.reshape(nb, 1, 8), sample)
            ys.append((y_da, y_mb, y_ml))
            if not sample:
                outs["ssm"].append(jnp.transpose(
                    hfin.reshape(nb, 2, MB_STATE, MB_HEADS, MB_HD), (0, 1, 3, 4, 2)))
                outs["C"].append(cfin.reshape(nb, 2, ML_HEADS, ML_DK, ML_DV))
                outs["n"].append(nfin.reshape(nb, 2, ML_HEADS, ML_DK))
                outs["m"].append(mfin.reshape(nb, 2, ML_HEADS))
        x, h2t, affb = _merge(x, mod[l], norm1_g[l][None], ys[0], ys[1],
                              w_mgate[l].astype(BF16), b_mgate[l][None], w_branch[l].astype(BF16),
                              w_out[l].astype(BF16), norm2_g[l][None], router_w[l].T)
        slotb, slott, a_cnt, jlo, jhi = _route(affb)
        ye = _experts(l, jlo.reshape(-1), jhi.reshape(-1), h2t, slotb, affb,
                      ex_w_gate, ex_w_up, ex_w_down)
        moe = _combine(a_cnt.reshape(-1), ye, slott)

    fg = final_g[None]
    y_prompt = _final(x, moe, mod[DEPTH - 1], fg, False).reshape(BATCH, SEQ, D_MODEL)
    y_sample = _final(x, moe, mod[DEPTH - 1], fg, True).reshape(DEC_BATCH, DEC_SEQ, D_MODEL)
    return (y_prompt, y_sample, jnp.stack(outs["k"], axis=1), jnp.stack(outs["v"], axis=1),
            jnp.stack(outs["ssm"], axis=1), jnp.stack(outs["C"], axis=1),
            jnp.stack(outs["n"], axis=1), jnp.stack(outs["m"], axis=1))
```

```python
import functools
import math

import jax
import jax.numpy as jnp
from jax import lax
from jax.experimental import pallas as pl
from jax.experimental.pallas import tpu as pltpu

F32 = jnp.float32
BF16 = jnp.bfloat16

D_MODEL = 1024
BATCH = 16
SEQ = 256
DEPTH = 2
DEC_BATCH = 8
DEC_SEQ = 2048
PAST_LEN = 512
GRID_W = 64
EPS = 1e-6
CHUNK = 128
ROPE_BASE = 10000.0
DA_HEADS = 4
DA_HD = 64
DA_WIDTH = 512
MB_INNER = 512
MB_HD = 64
MB_HEADS = 8
MB_GROUPS = 2
MB_STATE = 64
MB_CONV_DIM = 768
ML_HEADS = 4
ML_DK = 64
ML_DV = 128
ML_WIDTH = 512
N_EXPERTS = 16
EC_FACTOR = 2
EXPERT_FF = 1024

T_CTX = BATCH * SEQ
T_SMP = DEC_BATCH * DEC_SEQ
T_ALL = T_CTX + T_SMP
TM = 256
NT_CTX = T_CTX // TM
NT_ALL = T_ALL // TM
NB_CTX = T_CTX // CHUNK
NB_ALL = T_ALL // CHUNK
CAP_CTX = EC_FACTOR * T_CTX // N_EXPERTS
CAP_SMP = EC_FACTOR * T_SMP // N_EXPERTS
CAP_ALL = CAP_CTX + CAP_SMP
RB = 256
N_RB = CAP_ALL // RB
WIN = 64
GATHER_TILES = 4
GATHER_AHEAD = 2
TILES_PER_REQ = DEC_SEQ // TM
TM_MERGE = 512
BRANCH_W = 512
LANES = 128
VMEM_LIMIT = 56 * 1024 * 1024
HI = lax.Precision.HIGHEST
LOG2E = math.log2(math.e)

C_Q, C_K, C_V, C_Z, C_XBC, C_MQ, C_MK, C_MV, C_MO, C_END = (
    0, 512, 1024, 1536, 2048, 2816, 3072, 3328, 3840, 4352)


def _mod_row(i):
    return jnp.where(i < NT_CTX, 0, 1 + (i - NT_CTX) // TILES_PER_REQ)


def _rope_blk(i):
    return jnp.where(i < NT_CTX, 0, 1 + (i - NT_CTX) % TILES_PER_REQ)


def _cparams(n_grid):
    return pltpu.CompilerParams(dimension_semantics=("arbitrary",) * n_grid,
                                vmem_limit_bytes=VMEM_LIMIT)


def _silu(x):
    return x * jax.nn.sigmoid(x)


def _softplus(x):
    u = jnp.exp(-jnp.abs(x))
    w = 1.0 + u
    l1p = jnp.where(w == 1.0, u, jnp.log(w) * (u / (w - 1.0)))
    return jnp.maximum(x, 0.0) + l1p


def _dot(a, b):
    return jnp.dot(a, b, preferred_element_type=F32)


def _dot_nt(a, b):
    return lax.dot_general(a, b, (((1,), (1,)), ((), ())), preferred_element_type=F32)


def _dot_hi(a, b):
    return jnp.dot(a, b, precision=HI, preferred_element_type=F32)


def _split3(a):
    hi = a.astype(BF16)
    r = a - hi.astype(F32)
    mid = r.astype(BF16)
    return hi, mid, (r - mid.astype(F32)).astype(BF16)


def _mask3(mask, axis):
    m = jnp.where(mask, 1.0, 0.0).astype(BF16)
    return jnp.concatenate([m, m, m], axis=axis)


def _mask_dot(m3, a):
    return _dot(m3, jnp.concatenate(_split3(a), axis=0))


def _dot_mask(a, m3):
    return _dot(jnp.concatenate(_split3(a), axis=1), m3)


def _dot_mask_narrow(a, m):
    hi, mid, lo = _split3(a)
    return _dot(hi, m) + _dot(mid, m) + _dot(lo, m)


def _rms(x, g):
    return x * lax.rsqrt(jnp.mean(x * x, axis=-1, keepdims=True) + EPS) * g


def _mod_kernel(c_ref, w_ref, b_ref, o_ref):
    s = _silu(c_ref[...])
    o_ref[...] = _dot(s.astype(BF16), w_ref[...].astype(BF16)) + b_ref[...]


def _modulation(cc, ada_w, ada_b):
    tn = 1536
    return pl.pallas_call(
        _mod_kernel,
        grid=(DEPTH, 6 * D_MODEL // tn),
        in_specs=[pl.BlockSpec((16, D_MODEL), lambda l, j: (0, 0)),
                  pl.BlockSpec((None, D_MODEL, tn), lambda l, j: (l, 0, j)),
                  pl.BlockSpec((None, 1, tn), lambda l, j: (l, 0, j))],
        out_specs=pl.BlockSpec((None, 16, tn), lambda l, j: (l, 0, j)),
        out_shape=jax.ShapeDtypeStruct((DEPTH, 16, 6 * D_MODEL), F32),
        compiler_params=_cparams(2),
        name="adaln_mod",
    )(cc, ada_w, ada_b.reshape(DEPTH, 1, 6 * D_MODEL))


def _rope(t, cos, sin, first_half):
    outs = []
    for c in range(DA_WIDTH // LANES):
        xc = t[:, LANES * c:LANES * (c + 1)]
        partner = jnp.where(first_half, pltpu.roll(xc, LANES - 16, 1), pltpu.roll(xc, 16, 1))
        outs.append(xc * cos + partner * sin)
    return jnp.concatenate(outs, axis=1)


def _proj_body(x, mod_ref, g_ref, wm_ref, ws_ref, wst_ref, cos_ref, sin_ref,
               q_ref, k_ref, v_ref, kf_ref, vf_ref, z_ref, xbc_ref,
               mq_ref, mk_ref, mv_ref, mo_ref, sm_ref, smt_ref):
    mod = mod_ref[...]
    sh1 = mod[:, 0:D_MODEL]
    sc1 = mod[:, D_MODEL:2 * D_MODEL]
    h = _rms(x, g_ref[...]) * (1.0 + sc1) + sh1
    hb = h.astype(BF16)

    def proj(a, b):
        return _dot(hb, wm_ref[:, a:b])

    lane = lax.broadcasted_iota(jnp.int32, (TM, LANES), 1)
    first_half = (lane % 32) < 16
    cos = cos_ref[...]
    sin = sin_ref[...]
    q = proj(C_Q, C_K)
    k = proj(C_K, C_V)
    v = proj(C_V, C_Z)
    kf_ref[...] = k
    vf_ref[...] = v
    q_ref[...] = (_rope(q, cos, sin, first_half) * (DA_HD ** -0.5 * LOG2E)).astype(BF16)
    k_ref[...] = _rope(k, cos, sin, first_half).astype(BF16)
    v_ref[...] = v.astype(BF16)
    z_ref[...] = proj(C_Z, C_XBC)
    xbc_ref[...] = proj(C_XBC, C_MQ)
    mq_ref[...] = proj(C_MQ, C_MK).astype(BF16)
    mk_ref[...] = (proj(C_MK, C_MV) * (ML_DK ** -0.5)).astype(BF16)
    mv_ref[...] = proj(C_MV, C_MO).astype(BF16)
    mo_ref[...] = proj(C_MO, C_END)
    sm_ref[...] = _dot(hb, ws_ref[...])
    st = _dot_nt(wst_ref[...], hb)
    for j in range(TM // CHUNK):
        smt_ref[j] = st[:, CHUNK * j:CHUNK * (j + 1)]


def _proj_kernel_first(xp_ref, xs_ref, *refs):
    x = jnp.where(pl.program_id(0) < NT_CTX, xp_ref[...], xs_ref[...])
    refs[-1][...] = x
    _proj_body(x, *refs[:-1])


def _proj_kernel_next(x_ref, moe_ref, modp_ref, *refs):
    g2 = modp_ref[...][:, 5 * D_MODEL:6 * D_MODEL]
    x = x_ref[...] + g2 * moe_ref[...]
    refs[-1][...] = x
    _proj_body(x, *refs[:-1])


def _ctx_tile(w):
    return pl.BlockSpec((TM, w), lambda i: (jnp.minimum(i, NT_CTX - 1), 0))


def _smp_tile(w):
    return pl.BlockSpec((TM, w), lambda i: (jnp.maximum(i - NT_CTX, 0), 0))


def _projection(xs, moe_prev, mod_prev, mod_l, g1, wm, ws, wst, cos_t, sin_t):
    tile = lambda w: pl.BlockSpec((TM, w), lambda i: (i, 0))
    full = lambda a: pl.BlockSpec(a.shape, lambda i: (0,) * a.ndim)
    modspec = pl.BlockSpec((None, 1, 6 * D_MODEL), lambda i: (_mod_row(i), 0, 0))
    if moe_prev is None:
        in_specs = [_ctx_tile(D_MODEL), _smp_tile(D_MODEL)]
        args = list(xs)
        kern = _proj_kernel_first
    else:
        in_specs = [tile(D_MODEL), tile(D_MODEL), modspec]
        args = [xs, moe_prev, mod_prev]
        kern = _proj_kernel_next
    in_specs += [modspec, full(g1), full(wm), full(ws), full(wst),
                 pl.BlockSpec((TM, LANES), lambda i: (_rope_blk(i), 0)),
                 pl.BlockSpec((TM, LANES), lambda i: (_rope_blk(i), 0))]
    args += [mod_l, g1, wm, ws, wst, cos_t, sin_t]
    ctx_only = pl.BlockSpec((TM, DA_WIDTH), lambda i: (jnp.minimum(i, NT_CTX), 0))
    out_specs = [tile(512), tile(512), tile(512), ctx_only, ctx_only, tile(512), tile(768),
                 tile(256), tile(256), tile(512), tile(512), tile(32),
                 pl.BlockSpec((TM // CHUNK, 32, CHUNK), lambda i: (i, 0, 0)), tile(D_MODEL)]
    sds = jax.ShapeDtypeStruct
    out_shape = [sds((T_ALL, 512), BF16), sds((T_ALL, 512), BF16), sds((T_ALL, 512), BF16),
                 sds((T_CTX + TM, 512), F32), sds((T_CTX + TM, 512), F32),
                 sds((T_ALL, 512), F32), sds((T_ALL, 768), F32),
                 sds((T_ALL, 256), BF16), sds((T_ALL, 256), BF16), sds((T_ALL, 512), BF16),
                 sds((T_ALL, 512), F32), sds((T_ALL, 32), F32),
                 sds((T_ALL // CHUNK, 32, CHUNK), F32), sds((T_ALL, D_MODEL), F32)]
    return pl.pallas_call(
        kern, grid=(NT_ALL,), in_specs=in_specs, out_specs=out_specs, out_shape=out_shape,
        compiler_params=_cparams(1), name="in_proj",
    )(*args)


def _attn_body(lam_init, q_ref, k_ref, v_ref, kc_ref, vc_ref, lp_ref, g_ref, o_ref):
    lp = lp_ref[...]
    s01 = jnp.sum(lp[0:1] * lp[1:2], axis=-1, keepdims=True)
    s23 = jnp.sum(lp[2:3] * lp[3:4], axis=-1, keepdims=True)
    lam = jnp.exp(s01) - jnp.exp(s23) + lam_init
    tq = q_ref.shape[0]
    lane = lax.broadcasted_iota(jnp.int32, (tq, LANES), 1)
    g = g_ref[...]
    for h in range(DA_HEADS):
        cols = slice(LANES * h, LANES * (h + 1))
        qh = q_ref[:, cols]
        kh = k_ref[:, cols]
        vh = v_ref[:, cols]
        if kc_ref is not None:
            kch = kc_ref[:, cols].astype(BF16)
            vch = vc_ref[:, cols].astype(BF16)
        parts = []
        for m in range(2):
            qm = jnp.where((lane < DA_HD) == (m == 0), qh, jnp.zeros_like(qh))
            s = _dot_nt(qm, kh)
            mx = jnp.max(s, axis=-1, keepdims=True)
            if kc_ref is not None:
                sc = _dot_nt(qm, kch)
                mx = jnp.maximum(mx, jnp.max(sc, axis=-1, keepdims=True))
                ec = jnp.exp2(sc - mx)
            e = jnp.exp2(s - mx)
            den = jnp.sum(e, axis=-1, keepdims=True)
            acc = _dot(e.astype(BF16), vh)
            if kc_ref is not None:
                den = den + jnp.sum(ec, axis=-1, keepdims=True)
                acc = acc + _dot(ec.astype(BF16), vch)
            parts.append(acc / den)
        att = parts[0] - lam * parts[1]
        o_ref[:, cols] = (_rms(att, g) * (1.0 - lam_init)).astype(BF16)


def _attn_kernel_ctx(lam_init, q_ref, k_ref, v_ref, lp_ref, g_ref, o_ref):
    _attn_body(lam_init, q_ref, k_ref, v_ref, None, None, lp_ref, g_ref, o_ref)


def _attn_kernel_smp(lam_init, q_ref, k_ref, v_ref, kc_ref, vc_ref, lp_ref, g_ref, o_ref):
    _attn_body(lam_init, q_ref, k_ref, v_ref, kc_ref, vc_ref, lp_ref, g_ref, o_ref)


def _attention(l, q, k, v, cache_k, cache_v, lp, g, sample):
    lam_init = 0.8 - 0.6 * math.exp(-0.3 * l)
    full = lambda a: pl.BlockSpec(a.shape, lambda *_: (0,) * a.ndim)
    sds = jax.ShapeDtypeStruct
    if not sample:
        blk = pl.BlockSpec((SEQ, DA_WIDTH), lambda b: (b, 0))
        return pl.pallas_call(
            functools.partial(_attn_kernel_ctx, lam_init), grid=(BATCH,),
            in_specs=[blk, blk, blk, full(lp), full(g)],
            out_specs=blk, out_shape=sds((T_CTX, DA_WIDTH), BF16),
            compiler_params=_cparams(1), name="diff_attn_ctx",
        )(q, k, v, lp, g)
    tq = 256
    nq = DEC_SEQ // tq
    off_q = T_CTX // tq
    off_k = T_CTX // DEC_SEQ
    qblk = pl.BlockSpec((tq, DA_WIDTH), lambda b, i: (off_q + b * nq + i, 0))
    kblk = pl.BlockSpec((DEC_SEQ, DA_WIDTH), lambda b, i: (off_k + b, 0))
    cblk = pl.BlockSpec((None, None, PAST_LEN, DA_WIDTH), lambda b, i: (b, l, 0, 0))
    return pl.pallas_call(
        functools.partial(_attn_kernel_smp, lam_init), grid=(DEC_BATCH, nq),
        in_specs=[qblk, kblk, kblk, cblk, cblk, full(lp), full(g)],
        out_specs=pl.BlockSpec((tq, DA_WIDTH), lambda b, i: (b * nq + i, 0)),
        out_shape=sds((T_SMP, DA_WIDTH), BF16),
        compiler_params=_cparams(2), name="diff_attn_smp",
    )(q, k, v, cache_k, cache_v, lp, g)


def _tri_masks():
    r = lax.broadcasted_iota(jnp.int32, (CHUNK, CHUNK), 0)
    c = lax.broadcasted_iota(jnp.int32, (CHUNK, CHUNK), 1)
    return c <= r, c >= r


def _ssd_kernel(L, z_ref, xbc_ref, sm_ref, smt_ref, cw_ref, cb_ref, dtbr_ref, dtbc_ref,
                ar_ref, ac_ref, dexp_ref, ng_ref, h0_ref,
                y_ref, hfin_ref, xc_s, yacc_s, ht_s):
    nc = L // CHUNK
    low, upp = _tri_masks()
    tri_l3 = (_mask3(low, 1), _mask3(upp, 1))
    tri_r3 = (_mask3(upp, 0), _mask3(low, 0))
    lane512 = lax.broadcasted_iota(jnp.int32, (1, MB_INNER), 1)
    row16 = lax.broadcasted_iota(jnp.int32, (16, MB_INNER), 0)
    lane16 = lax.broadcasted_iota(jnp.int32, (16, MB_INNER), 1)
    expand = tuple(jnp.where(row16 == 8 * d + lane16 // MB_HD, 1.0, 0.0).astype(BF16) for d in range(2))
    lane128 = lax.broadcasted_iota(jnp.int32, (CHUNK, LANES), 1)
    rowblk = lax.broadcasted_iota(jnp.int32, (LANES, MB_INNER), 0) // MB_STATE
    colblk = lax.broadcasted_iota(jnp.int32, (LANES, MB_INNER), 1) // (MB_INNER // MB_GROUPS)
    same_group = rowblk == colblk
    cw = cw_ref[...]
    cbias = cb_ref[...]

    def conv_chunk(c, _):
        base = pl.multiple_of(c * CHUNK, CHUNK)
        x = xbc_ref[pl.ds(base, CHUNK), :]
        prev = xbc_ref[pl.ds(jnp.maximum(base - 1, 0), 1), :]
        nxt = xbc_ref[pl.ds(jnp.minimum(base + CHUNK, L - 1), 1), :]
        prev = jnp.where(c == 0, 0.0, prev)
        nxt = jnp.where(c == nc - 1, 0.0, nxt)
        row = lax.broadcasted_iota(jnp.int32, (CHUNK, 1), 0)
        xp = jnp.where(row == 0, prev, pltpu.roll(x, 1, 0))
        xn = jnp.where(row == CHUNK - 1, nxt, pltpu.roll(x, CHUNK - 1, 0))
        conv = xp * cw[0:1] + x * cw[1:2] + xn * cw[2:3] + cbias
        xc_s[pl.ds(base, CHUNK), :] = _silu(conv)
        return 0

    lax.fori_loop(0, nc, conv_chunk, 0)

    h0 = h0_ref[...]
    for d in range(2):
        h0d = h0[d]
        ht_s[d] = jnp.concatenate(
            [jnp.where(lane512 < MB_INNER // 2, h0d, 0.0),
             jnp.where(lane512 >= MB_INNER // 2, h0d, 0.0)], axis=0)

    def both(i, _):
        jobs = ((0, i), (1, nc - 1 - i))
        rows, xsb, bb, cb, dt_row, cum_col, cum_row = {}, {}, {}, {}, {}, {}, {}
        y_off, cbg = {}, {}
        for d, c in jobs:
            rows[d] = pl.ds(pl.multiple_of(c * CHUNK, CHUNK), CHUNK)
            xs = xc_s[rows[d], 0:MB_INNER]
            bm = xc_s[rows[d], MB_INNER:MB_INNER + LANES]
            cm = xc_s[rows[d], MB_INNER + LANES:MB_CONV_DIM]
            dt_col = _softplus(sm_ref[rows[d], 0:16] + dtbr_ref[...])
            a_col = dt_col * ar_ref[...]
            dt_row[d] = _softplus(smt_ref[c][0:16, :] + dtbc_ref[...])
            a_row = dt_row[d] * ac_ref[...]
            cum_col[d] = _mask_dot(tri_l3[d], a_col)
            cum_row[d] = _dot_mask(a_row, tri_r3[d])
            last = 0 if d else CHUNK - 1
            cum_last = cum_col[d][last:last + 1, :]
            w_exp = _dot_mask_narrow(jnp.exp(cum_last - cum_col[d]) * dt_col, expand[d])
            g_exp = _dot_mask_narrow(jnp.exp(cum_col[d]), expand[d])
            cd_exp = _dot_mask_narrow(jnp.broadcast_to(jnp.exp(cum_last), (8, 16)), expand[d])[0:1]
            xw = (xs * w_exp).astype(BF16)
            xsb[d] = xs.astype(BF16)
            bb[d] = bm.astype(BF16)
            cb[d] = cm.astype(BF16)
            ht = ht_s[d]
            y_off[d] = _dot(cb[d], ht.astype(BF16)) * g_exp
            s_new = _dot(bm.T.astype(BF16), xw)
            ht_s[d] = ht * cd_exp + jnp.where(same_group, s_new, 0.0)
            for g in range(MB_GROUPS):
                cg = jnp.where((lane128 < MB_STATE) == (g == 0), cb[d], jnp.zeros_like(cb[d]))
                cbg[d, g] = _dot_nt(cg, bb[d])
        m16 = {}
        for d, _c in jobs:
            for h in range(MB_HEADS):
                ci = 8 * d + h
                seg = cum_col[d][:, ci:ci + 1] - cum_row[d][ci:ci + 1, :]
                m = jnp.where(upp if d else low, jnp.exp(seg), 0.0) * cbg[d, h // 4] * dt_row[d][ci:ci + 1, :]
                m16[d, h] = m.astype(BF16)
        yd = {}
        for d, _c in jobs:
            for h in range(MB_HEADS):
                k = h // 2
                yd[d, h] = _dot(m16[d, h], xsb[d][:, LANES * k:LANES * (k + 1)])
        for d, _c in jobs:
            pairs = [jnp.where(lane128 < MB_HD, yd[d, 2 * k], yd[d, 2 * k + 1]) for k in range(MB_HEADS // 2)]
            yacc_s[d, rows[d], :] = jnp.concatenate(pairs, axis=1) + y_off[d]
        return 0

    lax.fori_loop(0, nc, both, 0)

    dexp = dexp_ref[...]
    ng = ng_ref[...]

    def fin(c, _):
        rows = pl.ds(pl.multiple_of(c * CHUNK, CHUNK), CHUNK)
        y = yacc_s[0, rows, :] + yacc_s[1, rows, :] + dexp * xc_s[rows, 0:MB_INNER]
        y = y * _silu(z_ref[rows, :])
        y_ref[rows, :] = _rms(y, ng).astype(BF16)
        return 0

    lax.fori_loop(0, nc, fin, 0)
    for d in range(2):
        ht = ht_s[d]
        hfin_ref[d] = ht[0:MB_STATE, :] + ht[MB_STATE:2 * MB_STATE, :]


def _ssd(z, xbc, sm, smt, cw, cb, dtb, a_neg, dexp, ng, h0t, sample):
    nb, L, off = (DEC_BATCH, DEC_SEQ, T_CTX // DEC_SEQ) if sample else (BATCH, SEQ, 0)
    full = lambda a: pl.BlockSpec(a.shape, lambda b: (0,) * a.ndim)
    seq = lambda w: pl.BlockSpec((L, w), lambda b: (off + b, 0))
    dtb_r, dtb_c = dtb.reshape(1, 16), dtb.reshape(16, 1)
    a_r, a_c = a_neg.reshape(1, 16), a_neg.reshape(16, 1)
    sds = jax.ShapeDtypeStruct
    return pl.pallas_call(
        functools.partial(_ssd_kernel, L), grid=(nb,),
        in_specs=[seq(MB_INNER), seq(MB_CONV_DIM), seq(32),
                  pl.BlockSpec((L // CHUNK, 32, CHUNK), lambda b: (off + b, 0, 0)),
                  full(cw), full(cb), full(dtb_r), full(dtb_c), full(a_r), full(a_c),
                  full(dexp), full(ng),
                  pl.BlockSpec((None, 2, MB_STATE, MB_INNER), lambda b: (b, 0, 0, 0))],
        out_specs=[pl.BlockSpec((L, MB_INNER), lambda b: (b, 0)),
                   pl.BlockSpec((None, 2, MB_STATE, MB_INNER), lambda b: (b, 0, 0, 0))],
        out_shape=[sds((nb * L, MB_INNER), BF16), sds((nb, 2, MB_STATE, MB_INNER), F32)],
        scratch_shapes=[pltpu.VMEM((L, MB_CONV_DIM), F32), pltpu.VMEM((2, L, MB_INNER), F32),
                        pltpu.VMEM((2, LANES, MB_INNER), F32)],
        compiler_params=_cparams(1), name="ssd_smp" if sample else "ssd_ctx",
    )(z, xbc, sm, smt, cw, cb, dtb_r, dtb_c, a_r, a_c, dexp, ng, h0t)


def _mlstm_kernel(L, q_ref, k_ref, v_ref, o_ref, sm_ref, smt_ref, gbr_ref, gbc_ref, ng_ref,
                  c0_ref, n0_ref, m0_ref,
                  y_ref, cf_ref, nf_ref, mf_ref, hacc_s, c_s, n_s):
    nc = L // CHUNK
    low, upp = _tri_masks()
    tri_l = (low.astype(F32), upp.astype(F32))
    tri_r = (upp.astype(F32), low.astype(F32))
    lane128 = lax.broadcasted_iota(jnp.int32, (CHUNK, LANES), 1)
    lane8 = lax.broadcasted_iota(jnp.int32, (1, 2 * ML_HEADS), 1)
    neg_inf = -jnp.inf

    c_s[...] = c0_ref[...]
    n_s[...] = n0_ref[...]
    m0 = m0_ref[...]

    def both(i, m_in):
        jobs = ((0, i), (1, nc - 1 - i))
        heads = [(d, h) for d, _ in jobs for h in range(ML_HEADS)]
        rows, pre_row, b_col, b_row = {}, {}, {}, {}
        for d, c in jobs:
            rows[d] = pl.ds(pl.multiple_of(c * CHUNK, CHUNK), CHUNK)
            pre_col = sm_ref[rows[d], 16:32] + gbr_ref[...]
            pre_row[d] = smt_ref[c][16:32, :] + gbc_ref[...]
            lf_col = -_softplus(-pre_col)
            lf_row = -_softplus(-pre_row[d])
            b_col[d] = _dot_hi(tri_l[d], lf_col)
            b_row[d] = _dot_hi(lf_row, tri_r[d])
        kt, cst, nst, vh, qk, qc, qn = {}, {}, {}, {}, {}, {}, {}
        for d, _ in jobs:
            for pr in range(2):
                qp = q_ref[rows[d], LANES * pr:LANES * (pr + 1)]
                kp = k_ref[rows[d], LANES * pr:LANES * (pr + 1)]
                kt[d, pr] = kp.astype(F32).T
                cst[d, pr] = c_s[d, pr]
                nst[d, pr] = n_s[d, pr]
                cb16 = cst[d, pr].astype(BF16)
                nb16 = jnp.broadcast_to(nst[d, pr], (LANES, LANES)).astype(BF16)
                for hh in range(2):
                    h = 2 * pr + hh
                    qm = jnp.where((lane128 < ML_DK) == (hh == 0), qp, jnp.zeros_like(qp))
                    vh[d, h] = v_ref[rows[d], ML_DV * h:ML_DV * (h + 1)]
                    qk[d, h] = _dot_nt(qm, kp)
                    qc[d, h] = _dot(qm, cb16)
                    qn[d, h] = _dot(qm, nb16)
        m_t, s_intra, s_inter, rsum, m_new, s_old, kts = {}, {}, {}, {}, {}, {}, {}
        for d, h in heads:
            m_st = m_in[:, 4 * d + h:4 * d + h + 1]
            bcol = b_col[d][:, 8 * d + 4 + h:8 * d + 5 + h]
            brow = b_row[d][8 * d + 4 + h:8 * d + 5 + h, :]
            li_row = pre_row[d][8 * d + h:8 * d + h + 1, :]
            dm = jnp.where(upp if d else low, bcol - brow + li_row, neg_inf)
            inter = bcol + m_st
            m_t[d, h] = jnp.maximum(inter, jnp.max(dm, axis=-1, keepdims=True))
            s_intra[d, h] = jnp.exp(dm - m_t[d, h]) * qk[d, h]
            s_inter[d, h] = jnp.exp(inter - m_t[d, h])
            rsum[d, h] = jnp.sum(s_intra[d, h], axis=-1, keepdims=True)
            last = 0 if d else CHUNK - 1
            b_end = brow[:, last:last + 1]
            w_end = b_end - brow + li_row
            m_new[d, h] = jnp.maximum(b_end + m_st, jnp.max(w_end, axis=-1, keepdims=True))
            s_old[d, h] = jnp.exp(b_end + m_st - m_new[d, h])
            half = slice(ML_DK * (h % 2), ML_DK * (h % 2 + 1))
            kts[d, h] = kt[d, h // 2][half, :] * jnp.exp(w_end - m_new[d, h])
        pv, kv = {}, {}
        for d, h in heads:
            pv[d, h] = _dot(s_intra[d, h].astype(BF16), vh[d, h])
            kv[d, h] = _dot(kts[d, h].astype(BF16), vh[d, h])
        m_out = m_in
        for d, h in heads:
            num = s_inter[d, h] * qc[d, h] + pv[d, h]
            den = s_inter[d, h] * qn[d, h] + rsum[d, h]
            hout = num / jnp.maximum(jnp.abs(den), jnp.exp(-m_t[d, h]))
            hacc_s[d, rows[d], ML_DV * h:ML_DV * (h + 1)] = hout
            m_out = jnp.where(lane8 == 4 * d + h, m_new[d, h], m_out)
        for d, _ in jobs:
            for pr in range(2):
                c_new, n_new = [], []
                for hh in range(2):
                    h = 2 * pr + hh
                    half = slice(ML_DK * hh, ML_DK * (hh + 1))
                    c_new.append(s_old[d, h] * cst[d, pr][half, :] + kv[d, h])
                    n_new.append(s_old[d, h] * nst[d, pr][half, :]
                                 + jnp.sum(kts[d, h], axis=-1, keepdims=True))
                c_s[d, pr] = jnp.concatenate(c_new, axis=0)
                n_s[d, pr] = jnp.concatenate(n_new, axis=0)
        return m_out

    m_fin = lax.fori_loop(0, nc, both, m0)

    ng = ng_ref[...]

    def fin(c, _):
        rows = pl.ds(pl.multiple_of(c * CHUNK, CHUNK), CHUNK)
        for h in range(ML_HEADS):
            cols = slice(ML_DV * h, ML_DV * (h + 1))
            y = _rms(hacc_s[0, rows, cols] + hacc_s[1, rows, cols], ng[:, cols]) * jax.nn.sigmoid(o_ref[rows, cols])
            y_ref[rows, cols] = y.astype(BF16)
        return 0

    lax.fori_loop(0, nc, fin, 0)
    cf_ref[...] = c_s[...]
    nf_ref[...] = n_s[...]
    mf_ref[...] = m_fin


def _mlstm(q, k, v, o, sm, smt, gb, ng, c0, n0, m0, sample):
    nb, L, off = (DEC_BATCH, DEC_SEQ, T_CTX // DEC_SEQ) if sample else (BATCH, SEQ, 0)
    full = lambda a: pl.BlockSpec(a.shape, lambda b: (0,) * a.ndim)
    seq = lambda w: pl.BlockSpec((L, w), lambda b: (off + b, 0))
    gb_r, gb_c = gb.reshape(1, 16), gb.reshape(16, 1)
    st_c = pl.BlockSpec((None, 2, 2, LANES, ML_DV), lambda b: (b, 0, 0, 0, 0))
    st_n = pl.BlockSpec((None, 2, 2, LANES, 1), lambda b: (b, 0, 0, 0, 0))
    st_m = pl.BlockSpec((None, 1, 8), lambda b: (b, 0, 0))
    sds = jax.ShapeDtypeStruct
    return pl.pallas_call(
        functools.partial(_mlstm_kernel, L), grid=(nb,),
        in_specs=[seq(256), seq(256), seq(512), seq(512), seq(32),
                  pl.BlockSpec((L // CHUNK, 32, CHUNK), lambda b: (off + b, 0, 0)),
                  full(gb_r), full(gb_c), full(ng), st_c, st_n, st_m],
        out_specs=[pl.BlockSpec((L, ML_WIDTH), lambda b: (b, 0)), st_c, st_n, st_m],
        out_shape=[sds((nb * L, ML_WIDTH), BF16), sds((nb, 2, 2, LANES, ML_DV), F32),
                   sds((nb, 2, 2, LANES, 1), F32), sds((nb, 1, 8), F32)],
        scratch_shapes=[pltpu.VMEM((2, L, ML_WIDTH), F32), pltpu.VMEM((2, 2, LANES, ML_DV), F32),
                        pltpu.VMEM((2, 2, LANES, 1), F32)],
        compiler_params=_cparams(1), name="mlstm_smp" if sample else "mlstm_ctx",
    )(q, k, v, o, sm, smt, gb_r, gb_c, ng, c0, n0, m0)


def _merge_kernel(x_ref, mod_ref, g1_ref, ydac_ref, ymbc_ref, ymlc_ref, ydas_ref, ymbs_ref, ymls_ref,
                  wg_ref, bg_ref, wb_ref, wo_ref, g2_ref, rwt_ref, xo_ref, h2t_ref, affb_ref):
    x = x_ref[...]
    mod = mod_ref[...]
    sh1, sc1, gt1, sh2, sc2 = (mod[:, j * D_MODEL:(j + 1) * D_MODEL] for j in range(5))
    hb = (_rms(x, g1_ref[...]) * (1.0 + sc1) + sh1).astype(BF16)
    is_ctx = pl.program_id(0) < T_CTX // TM_MERGE
    merged = None
    for n, (yc_ref, ys_ref) in enumerate(((ydac_ref, ydas_ref), (ymbc_ref, ymbs_ref), (ymlc_ref, ymls_ref))):
        cols = slice(n * D_MODEL, (n + 1) * D_MODEL)
        gate = jax.nn.sigmoid(_dot(hb, wg_ref[:, cols]) + bg_ref[:, cols])
        y = jnp.where(is_ctx, yc_ref[...], ys_ref[...])
        term = gate * _dot(y, wb_ref[n])
        merged = term if merged is None else merged + term
    out = _dot(merged.astype(BF16), wo_ref[...])
    xn = x + gt1 * out
    xo_ref[...] = xn
    h2 = _rms(xn, g2_ref[...]) * (1.0 + sc2) + sh2
    h2t_ref[...] = h2.T.astype(BF16)
    logits = lax.dot_general(rwt_ref[...], h2, (((1,), (1,)), ((), ())),
                             precision=HI, preferred_element_type=F32)
    e = jnp.exp(logits - jnp.max(logits, axis=0, keepdims=True))
    aff = e / jnp.sum(e, axis=0, keepdims=True)
    for j in range(TM_MERGE // CHUNK):
        affb_ref[j] = aff[:, CHUNK * j:CHUNK * (j + 1)]


def _merge(x, mod_l, g1, ys_ctx, ys_smp, wg, bg, wb, wo, g2, rwt):
    tm = TM_MERGE
    n_ctx = T_CTX // tm
    per_req = DEC_SEQ // tm
    tile = lambda w: pl.BlockSpec((tm, w), lambda i: (i, 0))
    ctx_tile = pl.BlockSpec((tm, BRANCH_W), lambda i: (jnp.minimum(i, n_ctx - 1), 0))
    smp_tile = pl.BlockSpec((tm, BRANCH_W), lambda i: (jnp.maximum(i - n_ctx, 0), 0))
    full = lambda a: pl.BlockSpec(a.shape, lambda i: (0,) * a.ndim)
    mod_row = lambda i: jnp.where(i < n_ctx, 0, 1 + (i - n_ctx) // per_req)
    sds = jax.ShapeDtypeStruct
    return pl.pallas_call(
        _merge_kernel, grid=(T_ALL // tm,),
        in_specs=[tile(D_MODEL), pl.BlockSpec((None, 1, 6 * D_MODEL), lambda i: (mod_row(i), 0, 0)),
                  full(g1), ctx_tile, ctx_tile, ctx_tile, smp_tile, smp_tile, smp_tile,
                  full(wg), full(bg), full(wb), full(wo), full(g2), full(rwt)],
        out_specs=[tile(D_MODEL), pl.BlockSpec((D_MODEL, tm), lambda i: (0, i)),
                   pl.BlockSpec((tm // CHUNK, N_EXPERTS, CHUNK), lambda i: (i, 0, 0))],
        out_shape=[sds((T_ALL, D_MODEL), F32), sds((D_MODEL, T_ALL), BF16),
                   sds((NB_ALL, N_EXPERTS, CHUNK), F32)],
        compiler_params=_cparams(1), name="merge_out",
    )(x, mod_l, g1, *ys_ctx, *ys_smp, wg, bg, wb, wo, g2, rwt)


def _route_kernel(affb_ref, slotb_ref, slott_ref, a_ref, jlo_ref, jhi_ref, acc_s, run_s):
    r = lax.broadcasted_iota(jnp.int32, (CHUNK, CHUNK), 0)
    c = lax.broadcasted_iota(jnp.int32, (CHUNK, CHUNK), 1)
    upper = (r <= c).astype(BF16)
    eye = (r == c).astype(F32)
    lane = lax.broadcasted_iota(jnp.int32, (N_EXPERTS, LANES), 1)
    acc_s[...] = jnp.zeros_like(acc_s)
    run_s[...] = jnp.zeros_like(run_s)
    for b0, b1, cap in ((0, NB_CTX, CAP_CTX), (NB_CTX, NB_ALL, CAP_SMP)):
        aff = affb_ref[b0:b1]

        def search(i, thr_bits):
            cand = thr_bits | lax.shift_left(jnp.int32(1), 30 - i)
            cnt = jnp.sum((aff >= pltpu.bitcast(cand, F32)[None]).astype(jnp.int32), axis=0)
            cnt = jnp.sum(cnt, axis=1, keepdims=True)
            return jnp.where(cnt >= cap, cand, thr_bits)

        thr = pltpu.bitcast(lax.fori_loop(0, 31, search, jnp.zeros((N_EXPERTS, 1), jnp.int32)), F32)
        n_gt = jnp.sum(jnp.sum((aff > thr[None]).astype(jnp.int32), axis=0), axis=1, keepdims=True)
        need = (cap - n_gt).astype(F32)

        run_s[1] = jnp.zeros((N_EXPERTS, 1), F32)

        def blk(b, _):
            run_sel = run_s[0]
            run_eq = run_s[1]
            x = affb_ref[b]
            eq = x == thr
            eq_f = jnp.where(eq, 1.0, 0.0)
            eq_incl = _dot(eq_f.astype(BF16), upper)
            sel = (x > thr) | (eq & (run_eq + eq_incl - eq_f < need))
            sel_f = jnp.where(sel, 1.0, 0.0)
            sel_incl = _dot(sel_f.astype(BF16), upper)
            slot = jnp.where(sel, run_sel + sel_incl - sel_f, -1.0)
            slotb_ref[b] = slot.astype(jnp.int32)
            slott_ref[pl.ds(pl.multiple_of(b * CHUNK, CHUNK), CHUNK), :] = lax.dot_general(
                eye, slot, (((1,), (1,)), ((), ())), precision=HI, preferred_element_type=F32)
            acc_s[...] = jnp.where((lane == b // 2) & (b % 2 == 0), run_sel, acc_s[...])
            run_s[0] = run_sel + sel_incl[:, CHUNK - 1:CHUNK]
            run_s[1] = run_eq + eq_incl[:, CHUNK - 1:CHUNK]
            return 0

        lax.fori_loop(b0, b1, blk, 0)
    a_acc = jnp.where(lane == NT_ALL, run_s[0], acc_s[...])
    a_ref[...] = a_acc.astype(jnp.int32)
    a_next = pltpu.roll(a_acc, LANES - 1, 1)
    tile_ok = lane < NT_ALL
    jlo = jnp.zeros((N_EXPERTS, LANES), jnp.int32)
    jhi = jnp.zeros((N_EXPERTS, LANES), jnp.int32)
    for k in range(N_RB):
        lo_k = jnp.sum((tile_ok & (a_next <= float(RB * k))).astype(jnp.int32), axis=1, keepdims=True)
        hi_k = jnp.sum((tile_ok & (a_acc < float(RB * (k + 1)))).astype(jnp.int32), axis=1, keepdims=True) - 1
        jlo = jnp.where(lane == k, lo_k, jlo)
        jhi = jnp.where(lane == k, hi_k, jhi)
    jlo_ref[...] = jlo
    jhi_ref[...] = jhi


def _route(affb):
    sds = jax.ShapeDtypeStruct
    small = sds((N_EXPERTS, LANES), jnp.int32)
    return pl.pallas_call(
        _route_kernel,
        out_shape=[sds((NB_ALL, N_EXPERTS, CHUNK), jnp.int32), sds((T_ALL, N_EXPERTS), F32),
                   small, small, small],
        scratch_shapes=[pltpu.VMEM((N_EXPERTS, LANES), F32), pltpu.VMEM((2, N_EXPERTS, 1), F32)],
        compiler_params=pltpu.CompilerParams(vmem_limit_bytes=VMEM_LIMIT), name="route",
    )(affb)


def _expert_kernel(jlo_ref, jhi_ref, h2t_hbm, slotb_ref, affb_ref, wg_ref, wu_ref, wd_ref, o_ref,
                   wg_s, wu_s, wd_s, chunk_s, sem, acc_s, g_s, cnt_s):
    e = pl.program_id(0)
    k = pl.program_id(1)
    step = e * N_RB + k

    @pl.when(k == 0)
    def _():
        wg_s[...] = wg_ref[...].astype(BF16)
        wu_s[...] = wu_ref[...].astype(BF16)
        wd_s[...] = wd_ref[...].astype(BF16)

    def tile_range(s):
        i = (s // N_RB) * LANES + s % N_RB
        lo = jnp.clip(jlo_ref[i], 0, NT_ALL - 1)
        return lo, jnp.clip(jhi_ref[i] - lo + 1, 1, NT_ALL - lo)

    def chunk_tile(lo, c):
        return jnp.minimum(lo + GATHER_TILES * c, NT_ALL - GATHER_TILES)

    def chunk_copy(j0, buf):
        return pltpu.make_async_copy(
            h2t_hbm.at[:, pl.ds(pl.multiple_of(j0 * TM, TM), GATHER_TILES * TM)],
            chunk_s.at[buf], sem.at[buf])

    def n_chunks(s):
        return (tile_range(s)[1] + GATHER_TILES - 1) // GATHER_TILES

    n_steps = N_EXPERTS * N_RB

    def advance(s, c):
        wrap = (c + 1 >= n_chunks(jnp.minimum(s, n_steps - 1))) | (s >= n_steps)
        return jnp.where(wrap, s + 1, s), jnp.where(wrap, 0, c + 1)

    def start_at(s, c, buf):
        @pl.when(s < n_steps)
        def _():
            chunk_copy(chunk_tile(tile_range(jnp.minimum(s, n_steps - 1))[0], c), buf).start()

    jlo, n = tile_range(step)
    nch = n_chunks(step)

    @pl.when(step == 0)
    def _():
        cnt_s[0] = 0
        pos = (step, 0)
        for i in range(GATHER_AHEAD):
            start_at(pos[0], pos[1], i)
            pos = advance(*pos)

    done = cnt_s[0]
    acc_s[...] = jnp.zeros_like(acc_s)
    g_s[...] = jnp.zeros_like(g_s)
    ntok = GATHER_TILES * TM
    want = lax.broadcasted_iota(jnp.int32, (RB, ntok), 0) + k * RB
    lane_tile = lax.broadcasted_iota(jnp.int32, (1, ntok), 1) // TM

    def body(c, _):
        buf = (done + c) % (GATHER_AHEAD + 1)
        j0 = chunk_tile(jlo, c)
        chunk_copy(j0, buf).wait()
        pos = (step, c)
        for _i in range(GATHER_AHEAD):
            pos = advance(*pos)
        start_at(pos[0], pos[1], (done + c + GATHER_AHEAD) % (GATHER_AHEAD + 1))

        nblk = ntok // CHUNK
        srow = jnp.concatenate([slotb_ref[2 * j0 + i, pl.ds(e, 1), :] for i in range(nblk)], axis=1)
        arow = jnp.concatenate([affb_ref[2 * j0 + i, pl.ds(e, 1), :] for i in range(nblk)], axis=1)
        fresh = j0 + lane_tile >= jlo + GATHER_TILES * c
        hit = (srow == want) & fresh
        onehot = jnp.where(hit, 1.0, 0.0).astype(BF16)
        acc_s[...] += _dot_nt(chunk_s[buf], onehot)
        g_s[...] += jnp.sum(jnp.where(hit, arow, 0.0), axis=1, keepdims=True)
        return 0

    lax.fori_loop(0, nch, body, 0)
    cnt_s[0] = done + nch
    xe = acc_s[...].T.astype(BF16)
    hid = _silu(_dot(xe, wg_s[...])) * _dot(xe, wu_s[...])
    o_ref[...] = _dot(hid.astype(BF16), wd_s[...]) * g_s[...]


def _experts(l, jlo, jhi, h2t, slotb, affb, w_gate, w_up, w_down):
    wsp = pl.BlockSpec((None, None, D_MODEL, EXPERT_FF), lambda e, j, *_: (l, e, 0, 0))
    wsd = pl.BlockSpec((None, None, EXPERT_FF, D_MODEL), lambda e, j, *_: (l, e, 0, 0))
    whole = lambda a: pl.BlockSpec(a.shape, lambda e, j, *_: (0,) * a.ndim)
    grid_spec = pltpu.PrefetchScalarGridSpec(
        num_scalar_prefetch=2, grid=(N_EXPERTS, N_RB),
        in_specs=[pl.BlockSpec(memory_space=pl.ANY), whole(slotb), whole(affb), wsp, wsp, wsd],
        out_specs=pl.BlockSpec((None, RB, D_MODEL), lambda e, j, *_: (e, j, 0)),
        scratch_shapes=[pltpu.VMEM((D_MODEL, EXPERT_FF), BF16), pltpu.VMEM((D_MODEL, EXPERT_FF), BF16),
                        pltpu.VMEM((EXPERT_FF, D_MODEL), BF16),
                        pltpu.VMEM((GATHER_AHEAD + 1, D_MODEL, GATHER_TILES * TM), BF16),
                        pltpu.SemaphoreType.DMA((GATHER_AHEAD + 1,)), pltpu.VMEM((D_MODEL, RB), F32),
                        pltpu.VMEM((RB, 1), F32), pltpu.SMEM((1,), jnp.int32)])
    return pl.pallas_call(
        _expert_kernel, grid_spec=grid_spec,
        out_shape=jax.ShapeDtypeStruct((N_EXPERTS, CAP_ALL, D_MODEL), F32),
        compiler_params=_cparams(2), name="expert_ffn",
    )(jlo, jhi, h2t, slotb, affb, w_gate, w_up, w_down)


def _combine_kernel(a_ref, ye_hbm, slott_ref, o_ref, win_s, sem, xwin_s, xsem):
    j = pl.program_id(0)
    nt = pl.num_programs(0)

    def first_row(e, jj):
        a = a_ref[e * LANES + jj]
        return pl.multiple_of(jnp.clip((a // 8) * 8, 0, CAP_ALL - WIN), 8)

    def win_copy(e, row0, buf):
        return pltpu.make_async_copy(ye_hbm.at[e, pl.ds(row0, WIN), :], win_s.at[buf, e], sem.at[buf, e])

    buf = j % 2

    @pl.when(j == 0)
    def _():
        for e in range(N_EXPERTS):
            win_copy(e, first_row(e, 0), 0).start()

    @pl.when(j + 1 < nt)
    def _():
        for e in range(N_EXPERTS):
            win_copy(e, first_row(e, j + 1), 1 - buf).start()

    def split(rows):
        hi = rows.astype(BF16)
        return hi, (rows - hi.astype(F32)).astype(BF16)

    lane = lax.broadcasted_iota(jnp.int32, (TM, LANES), 1)
    lane_f = lane.astype(F32)
    rows0 = []
    pieces = []
    for e in range(0, N_EXPERTS, LANES // WIN):
        tgt = None
        for i in range(LANES // WIN):
            row0 = first_row(e + i, j)
            win_copy(e + i, row0, buf).wait()
            rows0.append(row0)
            t_i = slott_ref[:, e + i:e + i + 1] - (row0 - WIN * i).astype(F32)
            in_win = (lane >= WIN * i) & (lane < WIN * (i + 1))
            tgt = jnp.where(in_win, t_i, -1.0) if tgt is None else jnp.where(in_win, t_i, tgt)
        pieces.append(jnp.where(tgt == lane_f, 1.0, 0.0).astype(BF16))
    onehot = jnp.concatenate(pieces, axis=1)
    hi, lo = split(win_s[buf].reshape(N_EXPERTS * WIN, D_MODEL))
    o_ref[...] = _dot(onehot, hi) + _dot(onehot, lo)

    lane_w = lax.broadcasted_iota(jnp.int32, (TM, WIN), 1).astype(F32)
    for e in range(N_EXPERTS):
        row0 = rows0[e]
        n_more = jnp.maximum((a_ref[e * LANES + j + 1] - row0 + WIN - 1) // WIN - 1, 0)

        def more(i, _):
            lo_slot = row0 + (i + 1) * WIN
            r = pl.multiple_of(jnp.minimum(lo_slot, CAP_ALL - WIN), 8)
            cp = pltpu.make_async_copy(ye_hbm.at[e, pl.ds(r, WIN), :], xwin_s, xsem)
            cp.start()
            cp.wait()
            scol = slott_ref[:, e:e + 1]
            scol = jnp.where(scol >= lo_slot.astype(F32), scol, -1.0)
            oh = jnp.where(scol - r.astype(F32) == lane_w, 1.0, 0.0).astype(BF16)
            xh, xl = split(xwin_s[...])
            o_ref[...] += _dot(oh, xh) + _dot(oh, xl)
            return 0

        lax.fori_loop(0, n_more, more, 0)


def _combine(a, ye, slott):
    tile = lambda w: pl.BlockSpec((TM, w), lambda i, *_: (i, 0))
    grid_spec = pltpu.PrefetchScalarGridSpec(
        num_scalar_prefetch=1, grid=(NT_ALL,),
        in_specs=[pl.BlockSpec(memory_space=pl.ANY), tile(N_EXPERTS)],
        out_specs=tile(D_MODEL),
        scratch_shapes=[pltpu.VMEM((2, N_EXPERTS, WIN, D_MODEL), F32),
                        pltpu.SemaphoreType.DMA((2, N_EXPERTS)),
                        pltpu.VMEM((WIN, D_MODEL), F32), pltpu.SemaphoreType.DMA(())])
    return pl.pallas_call(
        _combine_kernel, grid_spec=grid_spec,
        out_shape=jax.ShapeDtypeStruct((T_ALL, D_MODEL), F32),
        compiler_params=_cparams(1), name="moe_combine",
    )(a, ye, slott)


def _final_kernel(x_ref, moe_ref, mod_ref, g_ref, o_ref):
    g2 = mod_ref[...][:, 5 * D_MODEL:6 * D_MODEL]
    o_ref[...] = _rms(x_ref[...] + g2 * moe_ref[...], g_ref[...])


def _final(x, moe, mod_l, fg, sample):
    n, off = (T_SMP // TM, NT_CTX) if sample else (NT_CTX, 0)
    tile = pl.BlockSpec((TM, D_MODEL), lambda i: (off + i, 0))
    return pl.pallas_call(
        _final_kernel, grid=(n,),
        in_specs=[tile, tile, pl.BlockSpec((None, 1, 6 * D_MODEL), lambda i: (_mod_row(off + i), 0, 0)),
                  pl.BlockSpec(fg.shape, lambda i: (0, 0))],
        out_specs=pl.BlockSpec((TM, D_MODEL), lambda i: (i, 0)),
        out_shape=jax.ShapeDtypeStruct((n * TM, D_MODEL), F32),
        compiler_params=_cparams(1), name="final_norm",
    )(x, moe, mod_l, fg)


def _rope_tables():
    t = jnp.arange(DEC_SEQ)
    pos = jnp.stack([t // GRID_W, t % GRID_W], axis=-1).astype(F32)
    nf = DA_HD // 4
    inv = ROPE_BASE ** (-jnp.arange(nf, dtype=F32) / nf)
    ang = pos[:, :, None] * inv
    cos = jnp.cos(ang)
    sin = jnp.sin(ang)
    cos64 = jnp.stack([cos, cos], axis=2).reshape(DEC_SEQ, DA_HD)
    sin64 = jnp.stack([-sin, sin], axis=2).reshape(DEC_SEQ, DA_HD)
    cos_t = jnp.concatenate([jnp.ones((TM, LANES), F32), jnp.tile(cos64, (1, 2))], axis=0)
    sin_t = jnp.concatenate([jnp.zeros((TM, LANES), F32), jnp.tile(sin64, (1, 2))], axis=0)
    return cos_t, sin_t


def kernel(x_prompt, x_sample, cache_k, cache_v, state_ssm, state_mlstm_c, state_mlstm_n, state_mlstm_m, c, c_ctx, ada_w, ada_b, norm1_g, norm2_g, w_in, da_lambda, da_subln_g, mb_conv_w, mb_conv_b, mb_dt_bias, mb_a_log, mb_d, mb_norm_g, ml_gate_b, ml_norm_g, w_branch, w_mgate, b_mgate, w_out, router_w, ex_w_gate, ex_w_up, ex_w_down, final_g):
    x = (x_prompt.reshape(T_CTX, D_MODEL), x_sample.reshape(T_SMP, D_MODEL))
    cc = jnp.concatenate([c_ctx[None, :], c, jnp.zeros((16 - 1 - DEC_BATCH, D_MODEL), F32)], axis=0)
    mod = _modulation(cc, ada_w, ada_b).reshape(DEPTH, 16, 1, 6 * D_MODEL)
    cos_t, sin_t = _rope_tables()
    cache_k2 = cache_k.reshape(DEC_BATCH, DEPTH, PAST_LEN, DA_WIDTH)
    cache_v2 = cache_v.reshape(DEC_BATCH, DEPTH, PAST_LEN, DA_WIDTH)

    outs = {n: [] for n in ("k", "v", "ssm", "C", "n", "m")}
    moe = None
    for l in range(DEPTH):
        w = w_in[l]
        wm = jnp.concatenate([w[:, :2816], w[:, 2832:4368]], axis=1).astype(BF16)
        ws = jnp.concatenate([w[:, 2816:2832], w[:, 4368:4384]], axis=1).astype(BF16)
        res = _projection(x, moe, mod[l - 1] if l else None, mod[l], norm1_g[l][None], wm, ws, ws.T,
                          cos_t, sin_t)
        q, k, v, kf, vf, z, xbc, mq, mk, mv, mo, sm, smt, x = res
        outs["k"].append(kf[:T_CTX].reshape(BATCH, SEQ, DA_HEADS, 2, DA_HD))
        outs["v"].append(vf[:T_CTX].reshape(BATCH, SEQ, DA_HEADS, 2 * DA_HD))

        lp = da_lambda[l]
        sg = da_subln_g[l][None]
        cw = mb_conv_w[l]
        cb = mb_conv_b[l][None]
        dtb = mb_dt_bias[l].reshape(16)
        a_neg = -jnp.exp(mb_a_log[l]).reshape(16)
        dexp = jnp.repeat(mb_d[l], MB_HD)[None]
        mng = mb_norm_g[l][None]
        gb = ml_gate_b[l].reshape(16)
        lng = ml_norm_g[l].reshape(1, ML_WIDTH)
        ys = []
        for sample in (False, True):
            nb = DEC_BATCH if sample else BATCH
            y_da = _attention(l, q, k, v, cache_k2, cache_v2, lp, sg, sample)
            if sample:
                h0 = state_ssm[:, l]
                c0 = state_mlstm_c[:, l]
                n0 = state_mlstm_n[:, l]
                m0 = state_mlstm_m[:, l]
            else:
                h0 = jnp.zeros((nb, 2, MB_HEADS, MB_HD, MB_STATE), F32)
                c0 = jnp.zeros((nb, 2, ML_HEADS, ML_DK, ML_DV), F32)
                n0 = jnp.zeros((nb, 2, ML_HEADS, ML_DK), F32)
                m0 = jnp.zeros((nb, 2, ML_HEADS), F32)
            h0t = jnp.transpose(h0, (0, 1, 4, 2, 3)).reshape(nb, 2, MB_STATE, MB_INNER)
            y_mb, hfin = _ssd(z, xbc, sm, smt, cw, cb, dtb, a_neg, dexp, mng, h0t, sample)
            y_ml, cfin, nfin, mfin = _mlstm(
                mq, mk, mv, mo, sm, smt, gb, lng,
                c0.reshape(nb, 2, 2, LANES, ML_DV), n0.reshape(nb, 2, 2, LANES, 1),
                m0.reshape(nb, 1, 8), sample)
            ys.append((y_da, y_mb, y_ml))
            if not sample:
                outs["ssm"].append(jnp.transpose(
                    hfin.reshape(nb, 2, MB_STATE, MB_HEADS, MB_HD), (0, 1, 3, 4, 2)))
                outs["C"].append(cfin.reshape(nb, 2, ML_HEADS, ML_DK, ML_DV))
                outs["n"].append(nfin.reshape(nb, 2, ML_HEADS, ML_DK))
                outs["m"].append(mfin.reshape(nb, 2, ML_HEADS))
        x, h2t, affb = _merge(x, mod[l], norm1_g[l][None], ys[0], ys[1],
                              w_mgate[l].astype(BF16), b_mgate[l][None], w_branch[l].astype(BF16),
                              w_out[l].astype(BF16), norm2_g[l][None], router_w[l].T)
        slotb, slott, a_cnt, jlo, jhi = _route(affb)
        ye = _experts(l, jlo.reshape(-1), jhi.reshape(-1), h2t, slotb, affb,
                      ex_w_gate, ex_w_up, ex_w_down)
        moe = _combine(a_cnt.reshape(-1), ye, slott)

    fg = final_g[None]
    y_prompt = _final(x, moe, mod[DEPTH - 1], fg, False).reshape(BATCH, SEQ, D_MODEL)
    y_sample = _final(x, moe, mod[DEPTH - 1], fg, True).reshape(DEC_BATCH, DEC_SEQ, D_MODEL)
    return (y_prompt, y_sample, jnp.stack(outs["k"], axis=1), jnp.stack(outs["v"], axis=1),
            jnp.stack(outs["ssm"], axis=1), jnp.stack(outs["C"], axis=1),
            jnp.stack(outs["n"], axis=1), jnp.stack(outs["m"], axis=1))
```

```python
import functools
import math

import jax
import jax.numpy as jnp
from jax import lax
from jax.experimental import pallas as pl
from jax.experimental.pallas import tpu as pltpu

F32 = jnp.float32
BF16 = jnp.bfloat16

D_MODEL = 1024
BATCH = 16
SEQ = 256
DEPTH = 2
DEC_BATCH = 8
DEC_SEQ = 2048
PAST_LEN = 512
GRID_W = 64
EPS = 1e-6
CHUNK = 128
ROPE_BASE = 10000.0
DA_HEADS = 4
DA_HD = 64
DA_WIDTH = 512
MB_INNER = 512
MB_HD = 64
MB_HEADS = 8
MB_GROUPS = 2
MB_STATE = 64
MB_CONV_DIM = 768
ML_HEADS = 4
ML_DK = 64
ML_DV = 128
ML_WIDTH = 512
N_EXPERTS = 16
EC_FACTOR = 2
EXPERT_FF = 1024

T_CTX = BATCH * SEQ
T_SMP = DEC_BATCH * DEC_SEQ
T_ALL = T_CTX + T_SMP
TM = 256
NT_CTX = T_CTX // TM
NT_ALL = T_ALL // TM
NB_CTX = T_CTX // CHUNK
NB_ALL = T_ALL // CHUNK
CAP_CTX = EC_FACTOR * T_CTX // N_EXPERTS
CAP_SMP = EC_FACTOR * T_SMP // N_EXPERTS
CAP_ALL = CAP_CTX + CAP_SMP
RB = 256
N_RB = CAP_ALL // RB
WIN = 64
GATHER_TILES = 5
GATHER_AHEAD = 2
TILES_PER_REQ = DEC_SEQ // TM
TM_MERGE = 512
BRANCH_W = 512
LANES = 128
VMEM_LIMIT = 56 * 1024 * 1024
HI = lax.Precision.HIGHEST
LOG2E = math.log2(math.e)

C_Q, C_K, C_V, C_Z, C_XBC, C_MQ, C_MK, C_MV, C_MO, C_END = (
    0, 512, 1024, 1536, 2048, 2816, 3072, 3328, 3840, 4352)


def _mod_row(i):
    return jnp.where(i < NT_CTX, 0, 1 + (i - NT_CTX) // TILES_PER_REQ)


def _rope_blk(i):
    return jnp.where(i < NT_CTX, 0, 1 + (i - NT_CTX) % TILES_PER_REQ)


def _cparams(n_grid):
    return pltpu.CompilerParams(dimension_semantics=("arbitrary",) * n_grid,
                                vmem_limit_bytes=VMEM_LIMIT)


def _silu(x):
    return x * jax.nn.sigmoid(x)


def _softplus(x):
    u = jnp.exp(-jnp.abs(x))
    w = 1.0 + u
    l1p = jnp.where(w == 1.0, u, jnp.log(w) * (u / (w - 1.0)))
    return jnp.maximum(x, 0.0) + l1p


def _dot(a, b):
    return jnp.dot(a, b, preferred_element_type=F32)


def _dot_nt(a, b):
    return lax.dot_general(a, b, (((1,), (1,)), ((), ())), preferred_element_type=F32)


def _dot_hi(a, b):
    return jnp.dot(a, b, precision=HI, preferred_element_type=F32)


def _split3(a):
    hi = a.astype(BF16)
    r = a - hi.astype(F32)
    mid = r.astype(BF16)
    return hi, mid, (r - mid.astype(F32)).astype(BF16)


def _mask3(mask, axis):
    m = jnp.where(mask, 1.0, 0.0).astype(BF16)
    return jnp.concatenate([m, m, m], axis=axis)


def _mask_dot(m3, a):
    return _dot(m3, jnp.concatenate(_split3(a), axis=0))


def _dot_mask(a, m3):
    return _dot(jnp.concatenate(_split3(a), axis=1), m3)


def _dot_mask_narrow(a, m):
    hi, mid, lo = _split3(a)
    return _dot(hi, m) + _dot(mid, m) + _dot(lo, m)


def _rms(x, g):
    return x * lax.rsqrt(jnp.mean(x * x, axis=-1, keepdims=True) + EPS) * g


def _mod_kernel(c_ref, w_ref, b_ref, o_ref):
    s = _silu(c_ref[...])
    o_ref[...] = _dot(s.astype(BF16), w_ref[...].astype(BF16)) + b_ref[...]


def _modulation(cc, ada_w, ada_b):
    tn = 1536
    return pl.pallas_call(
        _mod_kernel,
        grid=(DEPTH, 6 * D_MODEL // tn),
        in_specs=[pl.BlockSpec((16, D_MODEL), lambda l, j: (0, 0)),
                  pl.BlockSpec((None, D_MODEL, tn), lambda l, j: (l, 0, j)),
                  pl.BlockSpec((None, 1, tn), lambda l, j: (l, 0, j))],
        out_specs=pl.BlockSpec((None, 16, tn), lambda l, j: (l, 0, j)),
        out_shape=jax.ShapeDtypeStruct((DEPTH, 16, 6 * D_MODEL), F32),
        compiler_params=_cparams(2),
        name="adaln_mod",
    )(cc, ada_w, ada_b.reshape(DEPTH, 1, 6 * D_MODEL))


def _rope(t, cos, sin, first_half):
    outs = []
    for c in range(DA_WIDTH // LANES):
        xc = t[:, LANES * c:LANES * (c + 1)]
        partner = jnp.where(first_half, pltpu.roll(xc, LANES - 16, 1), pltpu.roll(xc, 16, 1))
        outs.append(xc * cos + partner * sin)
    return jnp.concatenate(outs, axis=1)


def _proj_body(x, mod_ref, g_ref, wm_ref, ws_ref, wst_ref, cos_ref, sin_ref,
               q_ref, k_ref, v_ref, kf_ref, vf_ref, z_ref, xbc_ref,
               mq_ref, mk_ref, mv_ref, mo_ref, sm_ref, smt_ref):
    mod = mod_ref[...]
    sh1 = mod[:, 0:D_MODEL]
    sc1 = mod[:, D_MODEL:2 * D_MODEL]
    h = _rms(x, g_ref[...]) * (1.0 + sc1) + sh1
    hb = h.astype(BF16)

    def proj(a, b):
        return _dot(hb, wm_ref[:, a:b])

    lane = lax.broadcasted_iota(jnp.int32, (TM, LANES), 1)
    first_half = (lane % 32) < 16
    cos = cos_ref[...]
    sin = sin_ref[...]
    q = proj(C_Q, C_K)
    k = proj(C_K, C_V)
    v = proj(C_V, C_Z)
    kf_ref[...] = k
    vf_ref[...] = v
    q_ref[...] = (_rope(q, cos, sin, first_half) * (DA_HD ** -0.5 * LOG2E)).astype(BF16)
    k_ref[...] = _rope(k, cos, sin, first_half).astype(BF16)
    v_ref[...] = v.astype(BF16)
    z_ref[...] = proj(C_Z, C_XBC)
    xbc_ref[...] = proj(C_XBC, C_MQ)
    mq_ref[...] = proj(C_MQ, C_MK).astype(BF16)
    mk_ref[...] = (proj(C_MK, C_MV) * (ML_DK ** -0.5)).astype(BF16)
    mv_ref[...] = proj(C_MV, C_MO).astype(BF16)
    mo_ref[...] = proj(C_MO, C_END)
    sm_ref[...] = _dot(hb, ws_ref[...])
    st = _dot_nt(wst_ref[...], hb)
    for j in range(TM // CHUNK):
        smt_ref[j] = st[:, CHUNK * j:CHUNK * (j + 1)]


def _proj_kernel_first(xp_ref, xs_ref, *refs):
    x = jnp.where(pl.program_id(0) < NT_CTX, xp_ref[...], xs_ref[...])
    refs[-1][...] = x
    _proj_body(x, *refs[:-1])


def _proj_kernel_next(x_ref, moe_ref, modp_ref, *refs):
    g2 = modp_ref[...][:, 5 * D_MODEL:6 * D_MODEL]
    x = x_ref[...] + g2 * moe_ref[...]
    refs[-1][...] = x
    _proj_body(x, *refs[:-1])


def _ctx_tile(w):
    return pl.BlockSpec((TM, w), lambda i: (jnp.minimum(i, NT_CTX - 1), 0))


def _smp_tile(w):
    return pl.BlockSpec((TM, w), lambda i: (jnp.maximum(i - NT_CTX, 0), 0))


def _projection(xs, moe_prev, mod_prev, mod_l, g1, wm, ws, wst, cos_t, sin_t):
    tile = lambda w: pl.BlockSpec((TM, w), lambda i: (i, 0))
    full = lambda a: pl.BlockSpec(a.shape, lambda i: (0,) * a.ndim)
    modspec = pl.BlockSpec((None, 1, 6 * D_MODEL), lambda i: (_mod_row(i), 0, 0))
    if moe_prev is None:
        in_specs = [_ctx_tile(D_MODEL), _smp_tile(D_MODEL)]
        args = list(xs)
        kern = _proj_kernel_first
    else:
        in_specs = [tile(D_MODEL), tile(D_MODEL), modspec]
        args = [xs, moe_prev, mod_prev]
        kern = _proj_kernel_next
    in_specs += [modspec, full(g1), full(wm), full(ws), full(wst),
                 pl.BlockSpec((TM, LANES), lambda i: (_rope_blk(i), 0)),
                 pl.BlockSpec((TM, LANES), lambda i: (_rope_blk(i), 0))]
    args += [mod_l, g1, wm, ws, wst, cos_t, sin_t]
    ctx_only = pl.BlockSpec((TM, DA_WIDTH), lambda i: (jnp.minimum(i, NT_CTX), 0))
    out_specs = [tile(512), tile(512), tile(512), ctx_only, ctx_only, tile(512), tile(768),
                 tile(256), tile(256), tile(512), tile(512), tile(32),
                 pl.BlockSpec((TM // CHUNK, 32, CHUNK), lambda i: (i, 0, 0)), tile(D_MODEL)]
    sds = jax.ShapeDtypeStruct
    out_shape = [sds((T_ALL, 512), BF16), sds((T_ALL, 512), BF16), sds((T_ALL, 512), BF16),
                 sds((T_CTX + TM, 512), F32), sds((T_CTX + TM, 512), F32),
                 sds((T_ALL, 512), F32), sds((T_ALL, 768), F32),
                 sds((T_ALL, 256), BF16), sds((T_ALL, 256), BF16), sds((T_ALL, 512), BF16),
                 sds((T_ALL, 512), F32), sds((T_ALL, 32), F32),
                 sds((T_ALL // CHUNK, 32, CHUNK), F32), sds((T_ALL, D_MODEL), F32)]
    return pl.pallas_call(
        kern, grid=(NT_ALL,), in_specs=in_specs, out_specs=out_specs, out_shape=out_shape,
        compiler_params=_cparams(1), name="in_proj",
    )(*args)


def _attn_body(lam_init, q_ref, k_ref, v_ref, kc_ref, vc_ref, lp_ref, g_ref, o_ref):
    lp = lp_ref[...]
    s01 = jnp.sum(lp[0:1] * lp[1:2], axis=-1, keepdims=True)
    s23 = jnp.sum(lp[2:3] * lp[3:4], axis=-1, keepdims=True)
    lam = jnp.exp(s01) - jnp.exp(s23) + lam_init
    tq = q_ref.shape[0]
    lane = lax.broadcasted_iota(jnp.int32, (tq, LANES), 1)
    g = g_ref[...]
    for h in range(DA_HEADS):
        cols = slice(LANES * h, LANES * (h + 1))
        qh = q_ref[:, cols]
        kh = k_ref[:, cols]
        vh = v_ref[:, cols]
        if kc_ref is not None:
            kch = kc_ref[:, cols].astype(BF16)
            vch = vc_ref[:, cols].astype(BF16)
        parts = []
        for m in range(2):
            qm = jnp.where((lane < DA_HD) == (m == 0), qh, jnp.zeros_like(qh))
            s = _dot_nt(qm, kh)
            mx = jnp.max(s, axis=-1, keepdims=True)
            if kc_ref is not None:
                sc = _dot_nt(qm, kch)
                mx = jnp.maximum(mx, jnp.max(sc, axis=-1, keepdims=True))
                ec = jnp.exp2(sc - mx)
            e = jnp.exp2(s - mx)
            den = jnp.sum(e, axis=-1, keepdims=True)
            acc = _dot(e.astype(BF16), vh)
            if kc_ref is not None:
                den = den + jnp.sum(ec, axis=-1, keepdims=True)
                acc = acc + _dot(ec.astype(BF16), vch)
            parts.append(acc / den)
        att = parts[0] - lam * parts[1]
        o_ref[:, cols] = (_rms(att, g) * (1.0 - lam_init)).astype(BF16)


def _attn_kernel_ctx(lam_init, q_ref, k_ref, v_ref, lp_ref, g_ref, o_ref):
    _attn_body(lam_init, q_ref, k_ref, v_ref, None, None, lp_ref, g_ref, o_ref)


def _attn_kernel_smp(lam_init, q_ref, k_ref, v_ref, kc_ref, vc_ref, lp_ref, g_ref, o_ref):
    _attn_body(lam_init, q_ref, k_ref, v_ref, kc_ref, vc_ref, lp_ref, g_ref, o_ref)


def _attention(l, q, k, v, cache_k, cache_v, lp, g, sample):
    lam_init = 0.8 - 0.6 * math.exp(-0.3 * l)
    full = lambda a: pl.BlockSpec(a.shape, lambda *_: (0,) * a.ndim)
    sds = jax.ShapeDtypeStruct
    if not sample:
        blk = pl.BlockSpec((SEQ, DA_WIDTH), lambda b: (b, 0))
        return pl.pallas_call(
            functools.partial(_attn_kernel_ctx, lam_init), grid=(BATCH,),
            in_specs=[blk, blk, blk, full(lp), full(g)],
            out_specs=blk, out_shape=sds((T_CTX, DA_WIDTH), BF16),
            compiler_params=_cparams(1), name="diff_attn_ctx",
        )(q, k, v, lp, g)
    tq = 256
    nq = DEC_SEQ // tq
    off_q = T_CTX // tq
    off_k = T_CTX // DEC_SEQ
    qblk = pl.BlockSpec((tq, DA_WIDTH), lambda b, i: (off_q + b * nq + i, 0))
    kblk = pl.BlockSpec((DEC_SEQ, DA_WIDTH), lambda b, i: (off_k + b, 0))
    cblk = pl.BlockSpec((None, None, PAST_LEN, DA_WIDTH), lambda b, i: (b, l, 0, 0))
    return pl.pallas_call(
        functools.partial(_attn_kernel_smp, lam_init), grid=(DEC_BATCH, nq),
        in_specs=[qblk, kblk, kblk, cblk, cblk, full(lp), full(g)],
        out_specs=pl.BlockSpec((tq, DA_WIDTH), lambda b, i: (b * nq + i, 0)),
        out_shape=sds((T_SMP, DA_WIDTH), BF16),
        compiler_params=_cparams(2), name="diff_attn_smp",
    )(q, k, v, cache_k, cache_v, lp, g)


def _tri_masks():
    r = lax.broadcasted_iota(jnp.int32, (CHUNK, CHUNK), 0)
    c = lax.broadcasted_iota(jnp.int32, (CHUNK, CHUNK), 1)
    return c <= r, c >= r


def _ssd_kernel(L, z_ref, xbc_ref, sm_ref, smt_ref, cw_ref, cb_ref, dtbr_ref, dtbc_ref,
                ar_ref, ac_ref, dexp_ref, ng_ref, h0_ref,
                y_ref, hfin_ref, xc_s, yacc_s, ht_s):
    nc = L // CHUNK
    low, upp = _tri_masks()
    tri_l3 = (_mask3(low, 1), _mask3(upp, 1))
    tri_r3 = (_mask3(upp, 0), _mask3(low, 0))
    lane512 = lax.broadcasted_iota(jnp.int32, (1, MB_INNER), 1)
    row16 = lax.broadcasted_iota(jnp.int32, (16, MB_INNER), 0)
    lane16 = lax.broadcasted_iota(jnp.int32, (16, MB_INNER), 1)
    expand = tuple(jnp.where(row16 == 8 * d + lane16 // MB_HD, 1.0, 0.0).astype(BF16) for d in range(2))
    lane128 = lax.broadcasted_iota(jnp.int32, (CHUNK, LANES), 1)
    rowblk = lax.broadcasted_iota(jnp.int32, (LANES, MB_INNER), 0) // MB_STATE
    colblk = lax.broadcasted_iota(jnp.int32, (LANES, MB_INNER), 1) // (MB_INNER // MB_GROUPS)
    same_group = rowblk == colblk
    cw = cw_ref[...]
    cbias = cb_ref[...]

    def conv_chunk(c, _):
        base = pl.multiple_of(c * CHUNK, CHUNK)
        x = xbc_ref[pl.ds(base, CHUNK), :]
        prev = xbc_ref[pl.ds(jnp.maximum(base - 1, 0), 1), :]
        nxt = xbc_ref[pl.ds(jnp.minimum(base + CHUNK, L - 1), 1), :]
        prev = jnp.where(c == 0, 0.0, prev)
        nxt = jnp.where(c == nc - 1, 0.0, nxt)
        row = lax.broadcasted_iota(jnp.int32, (CHUNK, 1), 0)
        xp = jnp.where(row == 0, prev, pltpu.roll(x, 1, 0))
        xn = jnp.where(row == CHUNK - 1, nxt, pltpu.roll(x, CHUNK - 1, 0))
        conv = xp * cw[0:1] + x * cw[1:2] + xn * cw[2:3] + cbias
        xc_s[pl.ds(base, CHUNK), :] = _silu(conv)
        return 0

    lax.fori_loop(0, nc, conv_chunk, 0)

    h0 = h0_ref[...]
    for d in range(2):
        h0d = h0[d]
        ht_s[d] = jnp.concatenate(
            [jnp.where(lane512 < MB_INNER // 2, h0d, 0.0),
             jnp.where(lane512 >= MB_INNER // 2, h0d, 0.0)], axis=0)

    def both(i, _):
        jobs = ((0, i), (1, nc - 1 - i))
        rows, xsb, bb, cb, dt_row, cum_col, cum_row = {}, {}, {}, {}, {}, {}, {}
        y_off, cbg = {}, {}
        for d, c in jobs:
            rows[d] = pl.ds(pl.multiple_of(c * CHUNK, CHUNK), CHUNK)
            xs = xc_s[rows[d], 0:MB_INNER]
            bm = xc_s[rows[d], MB_INNER:MB_INNER + LANES]
            cm = xc_s[rows[d], MB_INNER + LANES:MB_CONV_DIM]
            dt_col = _softplus(sm_ref[rows[d], 0:16] + dtbr_ref[...])
            a_col = dt_col * ar_ref[...]
            dt_row[d] = _softplus(smt_ref[c][0:16, :] + dtbc_ref[...])
            a_row = dt_row[d] * ac_ref[...]
            cum_col[d] = _mask_dot(tri_l3[d], a_col)
            cum_row[d] = _dot_mask(a_row, tri_r3[d])
            last = 0 if d else CHUNK - 1
            cum_last = cum_col[d][last:last + 1, :]
            w_exp = _dot_mask_narrow(jnp.exp(cum_last - cum_col[d]) * dt_col, expand[d])
            g_exp = _dot_mask_narrow(jnp.exp(cum_col[d]), expand[d])
            cd_exp = _dot_mask_narrow(jnp.broadcast_to(jnp.exp(cum_last), (8, 16)), expand[d])[0:1]
            xw = (xs * w_exp).astype(BF16)
            xsb[d] = xs.astype(BF16)
            bb[d] = bm.astype(BF16)
            cb[d] = cm.astype(BF16)
            ht = ht_s[d]
            y_off[d] = _dot(cb[d], ht.astype(BF16)) * g_exp
            s_new = _dot(bm.T.astype(BF16), xw)
            ht_s[d] = ht * cd_exp + jnp.where(same_group, s_new, 0.0)
            for g in range(MB_GROUPS):
                cg = jnp.where((lane128 < MB_STATE) == (g == 0), cb[d], jnp.zeros_like(cb[d]))
                cbg[d, g] = _dot_nt(cg, bb[d])
        m16 = {}
        for d, _c in jobs:
            for h in range(MB_HEADS):
                ci = 8 * d + h
                seg = cum_col[d][:, ci:ci + 1] - cum_row[d][ci:ci + 1, :]
                m = jnp.where(upp if d else low, jnp.exp(seg), 0.0) * cbg[d, h // 4] * dt_row[d][ci:ci + 1, :]
                m16[d, h] = m.astype(BF16)
        yd = {}
        for d, _c in jobs:
            for h in range(MB_HEADS):
                k = h // 2
                yd[d, h] = _dot(m16[d, h], xsb[d][:, LANES * k:LANES * (k + 1)])
        for d, _c in jobs:
            pairs = [jnp.where(lane128 < MB_HD, yd[d, 2 * k], yd[d, 2 * k + 1]) for k in range(MB_HEADS // 2)]
            yacc_s[d, rows[d], :] = jnp.concatenate(pairs, axis=1) + y_off[d]
        return 0

    lax.fori_loop(0, nc, both, 0)

    dexp = dexp_ref[...]
    ng = ng_ref[...]

    def fin(c, _):
        rows = pl.ds(pl.multiple_of(c * CHUNK, CHUNK), CHUNK)
        y = yacc_s[0, rows, :] + yacc_s[1, rows, :] + dexp * xc_s[rows, 0:MB_INNER]
        y = y * _silu(z_ref[rows, :])
        y_ref[rows, :] = _rms(y, ng).astype(BF16)
        return 0

    lax.fori_loop(0, nc, fin, 0)
    for d in range(2):
        ht = ht_s[d]
        hfin_ref[d] = ht[0:MB_STATE, :] + ht[MB_STATE:2 * MB_STATE, :]


def _ssd(z, xbc, sm, smt, cw, cb, dtb, a_neg, dexp, ng, h0t, sample):
    nb, L, off = (DEC_BATCH, DEC_SEQ, T_CTX // DEC_SEQ) if sample else (BATCH, SEQ, 0)
    full = lambda a: pl.BlockSpec(a.shape, lambda b: (0,) * a.ndim)
    seq = lambda w: pl.BlockSpec((L, w), lambda b: (off + b, 0))
    dtb_r, dtb_c = dtb.reshape(1, 16), dtb.reshape(16, 1)
    a_r, a_c = a_neg.reshape(1, 16), a_neg.reshape(16, 1)
    sds = jax.ShapeDtypeStruct
    return pl.pallas_call(
        functools.partial(_ssd_kernel, L), grid=(nb,),
        in_specs=[seq(MB_INNER), seq(MB_CONV_DIM), seq(32),
                  pl.BlockSpec((L // CHUNK, 32, CHUNK), lambda b: (off + b, 0, 0)),
                  full(cw), full(cb), full(dtb_r), full(dtb_c), full(a_r), full(a_c),
                  full(dexp), full(ng),
                  pl.BlockSpec((None, 2, MB_STATE, MB_INNER), lambda b: (b, 0, 0, 0))],
        out_specs=[pl.BlockSpec((L, MB_INNER), lambda b: (b, 0)),
                   pl.BlockSpec((None, 2, MB_STATE, MB_INNER), lambda b: (b, 0, 0, 0))],
        out_shape=[sds((nb * L, MB_INNER), BF16), sds((nb, 2, MB_STATE, MB_INNER), F32)],
        scratch_shapes=[pltpu.VMEM((L, MB_CONV_DIM), F32), pltpu.VMEM((2, L, MB_INNER), F32),
                        pltpu.VMEM((2, LANES, MB_INNER), F32)],
        compiler_params=_cparams(1), name="ssd_smp" if sample else "ssd_ctx",
    )(z, xbc, sm, smt, cw, cb, dtb_r, dtb_c, a_r, a_c, dexp, ng, h0t)


def _mlstm_kernel(L, q_ref, k_ref, v_ref, o_ref, sm_ref, smt_ref, gbr_ref, gbc_ref, ng_ref,
                  c0_ref, n0_ref, m0_ref,
                  y_ref, cf_ref, nf_ref, mf_ref, hacc_s, c_s, n_s):
    nc = L // CHUNK
    low, upp = _tri_masks()
    tri_l = (low.astype(F32), upp.astype(F32))
    tri_r = (upp.astype(F32), low.astype(F32))
    lane128 = lax.broadcasted_iota(jnp.int32, (CHUNK, LANES), 1)
    lane8 = lax.broadcasted_iota(jnp.int32, (1, 2 * ML_HEADS), 1)
    neg_inf = -jnp.inf

    ones16 = jnp.ones((CHUNK, LANES), BF16)
    c_s[...] = c0_ref[...]
    n_s[...] = jnp.broadcast_to(n0_ref[...], n_s.shape)
    m0 = m0_ref[...]

    def both(i, m_in):
        jobs = ((0, i), (1, nc - 1 - i))
        heads = [(d, h) for d, _ in jobs for h in range(ML_HEADS)]
        rows, pre_row, b_col, b_row = {}, {}, {}, {}
        for d, c in jobs:
            rows[d] = pl.ds(pl.multiple_of(c * CHUNK, CHUNK), CHUNK)
            pre_col = sm_ref[rows[d], 16:32] + gbr_ref[...]
            pre_row[d] = smt_ref[c][16:32, :] + gbc_ref[...]
            lf_col = -_softplus(-pre_col)
            lf_row = -_softplus(-pre_row[d])
            b_col[d] = _dot_hi(tri_l[d], lf_col)
            b_row[d] = _dot_hi(lf_row, tri_r[d])
        kt, cst, nst, vh, qk, qc, qn = {}, {}, {}, {}, {}, {}, {}
        for d, _ in jobs:
            for pr in range(2):
                qp = q_ref[rows[d], LANES * pr:LANES * (pr + 1)]
                kp = k_ref[rows[d], LANES * pr:LANES * (pr + 1)]
                kt[d, pr] = kp.astype(F32).T
                cst[d, pr] = c_s[d, pr]
                nst[d, pr] = n_s[d, pr]
                cb16 = cst[d, pr].astype(BF16)
                nb16 = nst[d, pr].astype(BF16)
                for hh in range(2):
                    h = 2 * pr + hh
                    qm = jnp.where((lane128 < ML_DK) == (hh == 0), qp, jnp.zeros_like(qp))
                    vh[d, h] = jnp.concatenate([v_ref[rows[d], ML_DV * h:ML_DV * (h + 1)], ones16], axis=1)
                    qk[d, h] = _dot_nt(qm, kp)
                    qc[d, h] = _dot(qm, cb16)
                    qn[d, h] = _dot(qm, nb16)
        m_t, s_intra, s_inter, m_new, s_old, kts = {}, {}, {}, {}, {}, {}
        for d, h in heads:
            m_st = m_in[:, 4 * d + h:4 * d + h + 1]
            bcol = jnp.broadcast_to(b_col[d][:, 8 * d + 4 + h:8 * d + 5 + h], (CHUNK, CHUNK))
            brow = b_row[d][8 * d + 4 + h:8 * d + 5 + h, :]
            li_row = pre_row[d][8 * d + h:8 * d + h + 1, :]
            dm = jnp.where(upp if d else low, bcol - brow + li_row, neg_inf)
            inter = bcol + m_st
            m_t[d, h] = jnp.maximum(inter, jnp.max(dm, axis=-1, keepdims=True))
            s_intra[d, h] = jnp.exp(dm - m_t[d, h]) * qk[d, h]
            s_inter[d, h] = jnp.exp(inter - m_t[d, h])
            last = 0 if d else CHUNK - 1
            b_end = brow[:, last:last + 1]
            w_end = b_end - brow + li_row
            m_new[d, h] = jnp.maximum(b_end + m_st, jnp.max(w_end, axis=-1, keepdims=True))
            s_old[d, h] = jnp.exp(b_end + m_st - m_new[d, h])
            half = slice(ML_DK * (h % 2), ML_DK * (h % 2 + 1))
            kts[d, h] = kt[d, h // 2][half, :] * jnp.exp(w_end - m_new[d, h])
        pv, kv = {}, {}
        for d, h in heads:
            pv[d, h] = _dot(s_intra[d, h].astype(BF16), vh[d, h])
            kv[d, h] = _dot(kts[d, h].astype(BF16), vh[d, h])
        m_out = m_in
        for d, h in heads:
            num = s_inter[d, h] * qc[d, h] + pv[d, h][:, :ML_DV]
            den = s_inter[d, h] * qn[d, h] + pv[d, h][:, ML_DV:]
            hout = num / jnp.maximum(jnp.abs(den), jnp.exp(-m_t[d, h]))
            hacc_s[d, rows[d], ML_DV * h:ML_DV * (h + 1)] = hout
            m_out = jnp.where(lane8 == 4 * d + h, m_new[d, h], m_out)
        for d, _ in jobs:
            for pr in range(2):
                c_new, n_new = [], []
                for hh in range(2):
                    h = 2 * pr + hh
                    half = slice(ML_DK * hh, ML_DK * (hh + 1))
                    c_new.append(s_old[d, h] * cst[d, pr][half, :] + kv[d, h][:, :ML_DV])
                    n_new.append(s_old[d, h] * nst[d, pr][half, :] + kv[d, h][:, ML_DV:])
                c_s[d, pr] = jnp.concatenate(c_new, axis=0)
                n_s[d, pr] = jnp.concatenate(n_new, axis=0)
        return m_out

    m_fin = lax.fori_loop(0, nc, both, m0)

    ng = ng_ref[...]

    def fin(c, _):
        rows = pl.ds(pl.multiple_of(c * CHUNK, CHUNK), CHUNK)
        for h in range(ML_HEADS):
            cols = slice(ML_DV * h, ML_DV * (h + 1))
            y = _rms(hacc_s[0, rows, cols] + hacc_s[1, rows, cols], ng[:, cols]) * jax.nn.sigmoid(o_ref[rows, cols])
            y_ref[rows, cols] = y.astype(BF16)
        return 0

    lax.fori_loop(0, nc, fin, 0)
    cf_ref[...] = c_s[...]
    nf_ref[...] = n_s[:, :, :, 0:1]
    mf_ref[...] = m_fin


def _mlstm(q, k, v, o, sm, smt, gb, ng, c0, n0, m0, sample):
    nb, L, off = (DEC_BATCH, DEC_SEQ, T_CTX // DEC_SEQ) if sample else (BATCH, SEQ, 0)
    full = lambda a: pl.BlockSpec(a.shape, lambda b: (0,) * a.ndim)
    seq = lambda w: pl.BlockSpec((L, w), lambda b: (off + b, 0))
    gb_r, gb_c = gb.reshape(1, 16), gb.reshape(16, 1)
    st_c = pl.BlockSpec((None, 2, 2, LANES, ML_DV), lambda b: (b, 0, 0, 0, 0))
    st_n = pl.BlockSpec((None, 2, 2, LANES, 1), lambda b: (b, 0, 0, 0, 0))
    st_m = pl.BlockSpec((None, 1, 8), lambda b: (b, 0, 0))
    sds = jax.ShapeDtypeStruct
    return pl.pallas_call(
        functools.partial(_mlstm_kernel, L), grid=(nb,),
        in_specs=[seq(256), seq(256), seq(512), seq(512), seq(32),
                  pl.BlockSpec((L // CHUNK, 32, CHUNK), lambda b: (off + b, 0, 0)),
                  full(gb_r), full(gb_c), full(ng), st_c, st_n, st_m],
        out_specs=[pl.BlockSpec((L, ML_WIDTH), lambda b: (b, 0)), st_c, st_n, st_m],
        out_shape=[sds((nb * L, ML_WIDTH), BF16), sds((nb, 2, 2, LANES, ML_DV), F32),
                   sds((nb, 2, 2, LANES, 1), F32), sds((nb, 1, 8), F32)],
        scratch_shapes=[pltpu.VMEM((2, L, ML_WIDTH), F32), pltpu.VMEM((2, 2, LANES, ML_DV), F32),
                        pltpu.VMEM((2, 2, LANES, LANES), F32)],
        compiler_params=_cparams(1), name="mlstm_smp" if sample else "mlstm_ctx",
    )(q, k, v, o, sm, smt, gb_r, gb_c, ng, c0, n0, m0)


def _merge_kernel(x_ref, mod_ref, g1_ref, ydac_ref, ymbc_ref, ymlc_ref, ydas_ref, ymbs_ref, ymls_ref,
                  wg_ref, bg_ref, wb_ref, wo_ref, g2_ref, rwt_ref, xo_ref, h2t_ref, affb_ref):
    x = x_ref[...]
    mod = mod_ref[...]
    sh1, sc1, gt1, sh2, sc2 = (mod[:, j * D_MODEL:(j + 1) * D_MODEL] for j in range(5))
    hb = (_rms(x, g1_ref[...]) * (1.0 + sc1) + sh1).astype(BF16)
    is_ctx = pl.program_id(0) < T_CTX // TM_MERGE
    merged = None
    for n, (yc_ref, ys_ref) in enumerate(((ydac_ref, ydas_ref), (ymbc_ref, ymbs_ref), (ymlc_ref, ymls_ref))):
        cols = slice(n * D_MODEL, (n + 1) * D_MODEL)
        gate = jax.nn.sigmoid(_dot(hb, wg_ref[:, cols]) + bg_ref[:, cols])
        y = jnp.where(is_ctx, yc_ref[...], ys_ref[...])
        term = gate * _dot(y, wb_ref[n])
        merged = term if merged is None else merged + term
    out = _dot(merged.astype(BF16), wo_ref[...])
    xn = x + gt1 * out
    xo_ref[...] = xn
    h2 = _rms(xn, g2_ref[...]) * (1.0 + sc2) + sh2
    h2t_ref[...] = h2.T.astype(BF16)
    logits = lax.dot_general(rwt_ref[...], h2, (((1,), (1,)), ((), ())),
                             precision=HI, preferred_element_type=F32)
    e = jnp.exp(logits - jnp.max(logits, axis=0, keepdims=True))
    aff = e / jnp.sum(e, axis=0, keepdims=True)
    for j in range(TM_MERGE // CHUNK):
        affb_ref[j] = aff[:, CHUNK * j:CHUNK * (j + 1)]


def _merge(x, mod_l, g1, ys_ctx, ys_smp, wg, bg, wb, wo, g2, rwt):
    tm = TM_MERGE
    n_ctx = T_CTX // tm
    per_req = DEC_SEQ // tm
    tile = lambda w: pl.BlockSpec((tm, w), lambda i: (i, 0))
    ctx_tile = pl.BlockSpec((tm, BRANCH_W), lambda i: (jnp.minimum(i, n_ctx - 1), 0))
    smp_tile = pl.BlockSpec((tm, BRANCH_W), lambda i: (jnp.maximum(i - n_ctx, 0), 0))
    full = lambda a: pl.BlockSpec(a.shape, lambda i: (0,) * a.ndim)
    mod_row = lambda i: jnp.where(i < n_ctx, 0, 1 + (i - n_ctx) // per_req)
    sds = jax.ShapeDtypeStruct
    return pl.pallas_call(
        _merge_kernel, grid=(T_ALL // tm,),
        in_specs=[tile(D_MODEL), pl.BlockSpec((None, 1, 6 * D_MODEL), lambda i: (mod_row(i), 0, 0)),
                  full(g1), ctx_tile, ctx_tile, ctx_tile, smp_tile, smp_tile, smp_tile,
                  full(wg), full(bg), full(wb), full(wo), full(g2), full(rwt)],
        out_specs=[tile(D_MODEL), pl.BlockSpec((D_MODEL, tm), lambda i: (0, i)),
                   pl.BlockSpec((tm // CHUNK, N_EXPERTS, CHUNK), lambda i: (i, 0, 0))],
        out_shape=[sds((T_ALL, D_MODEL), F32), sds((D_MODEL, T_ALL), BF16),
                   sds((NB_ALL, N_EXPERTS, CHUNK), F32)],
        compiler_params=_cparams(1), name="merge_out",
    )(x, mod_l, g1, *ys_ctx, *ys_smp, wg, bg, wb, wo, g2, rwt)


def _route_kernel(affb_ref, slotb_ref, slott_ref, a_ref, jlo_ref, jhi_ref, acc_s, run_s):
    r = lax.broadcasted_iota(jnp.int32, (CHUNK, CHUNK), 0)
    c = lax.broadcasted_iota(jnp.int32, (CHUNK, CHUNK), 1)
    upper = (r <= c).astype(BF16)
    eye = (r == c).astype(F32)
    lane = lax.broadcasted_iota(jnp.int32, (N_EXPERTS, LANES), 1)
    acc_s[...] = jnp.zeros_like(acc_s)
    run_s[...] = jnp.zeros_like(run_s)
    for b0, b1, cap in ((0, NB_CTX, CAP_CTX), (NB_CTX, NB_ALL, CAP_SMP)):
        aff = affb_ref[b0:b1]

        def search(i, thr_bits):
            cand = thr_bits | lax.shift_left(jnp.int32(1), 30 - i)
            cnt = jnp.sum((aff >= pltpu.bitcast(cand, F32)[None]).astype(jnp.int32), axis=0)
            cnt = jnp.sum(cnt, axis=1, keepdims=True)
            return jnp.where(cnt >= cap, cand, thr_bits)

        thr = pltpu.bitcast(lax.fori_loop(0, 31, search, jnp.zeros((N_EXPERTS, 1), jnp.int32)), F32)
        n_gt = jnp.sum(jnp.sum((aff > thr[None]).astype(jnp.int32), axis=0), axis=1, keepdims=True)
        need = (cap - n_gt).astype(F32)

        run_s[1] = jnp.zeros((N_EXPERTS, 1), F32)

        def blk(b, _):
            run_sel = run_s[0]
            run_eq = run_s[1]
            x = affb_ref[b]
            eq = x == thr
            eq_f = jnp.where(eq, 1.0, 0.0)
            eq_incl = _dot(eq_f.astype(BF16), upper)
            sel = (x > thr) | (eq & (run_eq + eq_incl - eq_f < need))
            sel_f = jnp.where(sel, 1.0, 0.0)
            sel_incl = _dot(sel_f.astype(BF16), upper)
            slot = jnp.where(sel, run_sel + sel_incl - sel_f, -1.0)
            slotb_ref[b] = slot.astype(jnp.int32)
            slott_ref[pl.ds(pl.multiple_of(b * CHUNK, CHUNK), CHUNK), :] = lax.dot_general(
                eye, slot, (((1,), (1,)), ((), ())), precision=HI, preferred_element_type=F32)
            acc_s[...] = jnp.where((lane == b // 2) & (b % 2 == 0), run_sel, acc_s[...])
            run_s[0] = run_sel + sel_incl[:, CHUNK - 1:CHUNK]
            run_s[1] = run_eq + eq_incl[:, CHUNK - 1:CHUNK]
            return 0

        lax.fori_loop(b0, b1, blk, 0)
    a_acc = jnp.where(lane == NT_ALL, run_s[0], acc_s[...])
    a_ref[...] = a_acc.astype(jnp.int32)
    a_next = pltpu.roll(a_acc, LANES - 1, 1)
    tile_ok = lane < NT_ALL
    jlo = jnp.zeros((N_EXPERTS, LANES), jnp.int32)
    jhi = jnp.zeros((N_EXPERTS, LANES), jnp.int32)
    for k in range(N_RB):
        lo_k = jnp.sum((tile_ok & (a_next <= float(RB * k))).astype(jnp.int32), axis=1, keepdims=True)
        hi_k = jnp.sum((tile_ok & (a_acc < float(RB * (k + 1)))).astype(jnp.int32), axis=1, keepdims=True) - 1
        jlo = jnp.where(lane == k, lo_k, jlo)
        jhi = jnp.where(lane == k, hi_k, jhi)
    jlo_ref[...] = jlo
    jhi_ref[...] = jhi


def _route(affb):
    sds = jax.ShapeDtypeStruct
    small = sds((N_EXPERTS, LANES), jnp.int32)
    return pl.pallas_call(
        _route_kernel,
        out_shape=[sds((NB_ALL, N_EXPERTS, CHUNK), jnp.int32), sds((T_ALL, N_EXPERTS), F32),
                   small, small, small],
        scratch_shapes=[pltpu.VMEM((N_EXPERTS, LANES), F32), pltpu.VMEM((2, N_EXPERTS, 1), F32)],
        compiler_params=pltpu.CompilerParams(vmem_limit_bytes=VMEM_LIMIT), name="route",
    )(affb)


def _expert_kernel(jlo_ref, jhi_ref, h2t_hbm, slotb_ref, affb_ref, wg_ref, wu_ref, wd_ref, o_ref,
                   wg_s, wu_s, wd_s, chunk_s, sem, acc_s, g_s, cnt_s):
    e = pl.program_id(0)
    k = pl.program_id(1)
    step = e * N_RB + k

    @pl.when(k == 0)
    def _():
        wg_s[...] = wg_ref[...].astype(BF16)
        wu_s[...] = wu_ref[...].astype(BF16)
        wd_s[...] = wd_ref[...].astype(BF16)

    def tile_range(s):
        i = (s // N_RB) * LANES + s % N_RB
        lo = jnp.clip(jlo_ref[i], 0, NT_ALL - 1)
        return lo, jnp.clip(jhi_ref[i] - lo + 1, 1, NT_ALL - lo)

    def chunk_tile(lo, c):
        return jnp.minimum(lo + GATHER_TILES * c, NT_ALL - GATHER_TILES)

    def chunk_copy(j0, buf):
        return pltpu.make_async_copy(
            h2t_hbm.at[:, pl.ds(pl.multiple_of(j0 * TM, TM), GATHER_TILES * TM)],
            chunk_s.at[buf], sem.at[buf])

    def n_chunks(s):
        return (tile_range(s)[1] + GATHER_TILES - 1) // GATHER_TILES

    n_steps = N_EXPERTS * N_RB

    def advance(s, c):
        wrap = (c + 1 >= n_chunks(jnp.minimum(s, n_steps - 1))) | (s >= n_steps)
        return jnp.where(wrap, s + 1, s), jnp.where(wrap, 0, c + 1)

    def start_at(s, c, buf):
        @pl.when(s < n_steps)
        def _():
            chunk_copy(chunk_tile(tile_range(jnp.minimum(s, n_steps - 1))[0], c), buf).start()

    jlo, n = tile_range(step)
    nch = n_chunks(step)

    @pl.when(step == 0)
    def _():
        cnt_s[0] = 0
        pos = (step, 0)
        for i in range(GATHER_AHEAD):
            start_at(pos[0], pos[1], i)
            pos = advance(*pos)

    done = cnt_s[0]
    acc_s[...] = jnp.zeros_like(acc_s)
    g_s[...] = jnp.zeros_like(g_s)
    ntok = GATHER_TILES * TM
    want = lax.broadcasted_iota(jnp.int32, (RB, ntok), 0) + k * RB
    lane_tile = lax.broadcasted_iota(jnp.int32, (1, ntok), 1) // TM

    def body(c, _):
        buf = (done + c) % (GATHER_AHEAD + 1)
        j0 = chunk_tile(jlo, c)
        chunk_copy(j0, buf).wait()
        pos = (step, c)
        for _i in range(GATHER_AHEAD):
            pos = advance(*pos)
        start_at(pos[0], pos[1], (done + c + GATHER_AHEAD) % (GATHER_AHEAD + 1))

        nblk = ntok // CHUNK
        srow = jnp.concatenate([slotb_ref[2 * j0 + i, pl.ds(e, 1), :] for i in range(nblk)], axis=1)
        arow = jnp.concatenate([affb_ref[2 * j0 + i, pl.ds(e, 1), :] for i in range(nblk)], axis=1)
        fresh = j0 + lane_tile >= jlo + GATHER_TILES * c
        hit = (srow == want) & fresh
        onehot = jnp.where(hit, 1.0, 0.0).astype(BF16)
        acc_s[...] += _dot_nt(chunk_s[buf], onehot)
        g_s[...] += jnp.sum(jnp.where(hit, arow, 0.0), axis=1, keepdims=True)
        return 0

    lax.fori_loop(0, nch, body, 0)
    cnt_s[0] = done + nch
    xe = acc_s[...].T.astype(BF16)
    hid = _silu(_dot(xe, wg_s[...])) * _dot(xe, wu_s[...])
    o_ref[...] = _dot(hid.astype(BF16), wd_s[...]) * g_s[...]


def _experts(l, jlo, jhi, h2t, slotb, affb, w_gate, w_up, w_down):
    wsp = pl.BlockSpec((None, None, D_MODEL, EXPERT_FF), lambda e, j, *_: (l, e, 0, 0))
    wsd = pl.BlockSpec((None, None, EXPERT_FF, D_MODEL), lambda e, j, *_: (l, e, 0, 0))
    whole = lambda a: pl.BlockSpec(a.shape, lambda e, j, *_: (0,) * a.ndim)
    grid_spec = pltpu.PrefetchScalarGridSpec(
        num_scalar_prefetch=2, grid=(N_EXPERTS, N_RB),
        in_specs=[pl.BlockSpec(memory_space=pl.ANY), whole(slotb), whole(affb), wsp, wsp, wsd],
        out_specs=pl.BlockSpec((None, RB, D_MODEL), lambda e, j, *_: (e, j, 0)),
        scratch_shapes=[pltpu.VMEM((D_MODEL, EXPERT_FF), BF16), pltpu.VMEM((D_MODEL, EXPERT_FF), BF16),
                        pltpu.VMEM((EXPERT_FF, D_MODEL), BF16),
                        pltpu.VMEM((GATHER_AHEAD + 1, D_MODEL, GATHER_TILES * TM), BF16),
                        pltpu.SemaphoreType.DMA((GATHER_AHEAD + 1,)), pltpu.VMEM((D_MODEL, RB), F32),
                        pltpu.VMEM((RB, 1), F32), pltpu.SMEM((1,), jnp.int32)])
    return pl.pallas_call(
        _expert_kernel, grid_spec=grid_spec,
        out_shape=jax.ShapeDtypeStruct((N_EXPERTS, CAP_ALL, D_MODEL), F32),
        compiler_params=_cparams(2), name="expert_ffn",
    )(jlo, jhi, h2t, slotb, affb, w_gate, w_up, w_down)


def _combine_kernel(a_ref, ye_hbm, slott_ref, o_ref, win_s, sem, xwin_s, xsem):
    j = pl.program_id(0)
    nt = pl.num_programs(0)

    def first_row(e, jj):
        a = a_ref[e * LANES + jj]
        return pl.multiple_of(jnp.clip((a // 8) * 8, 0, CAP_ALL - WIN), 8)

    def win_copy(e, row0, buf):
        return pltpu.make_async_copy(ye_hbm.at[e, pl.ds(row0, WIN), :], win_s.at[buf, e], sem.at[buf, e])

    buf = j % 2

    @pl.when(j == 0)
    def _():
        for e in range(N_EXPERTS):
            win_copy(e, first_row(e, 0), 0).start()

    @pl.when(j + 1 < nt)
    def _():
        for e in range(N_EXPERTS):
            win_copy(e, first_row(e, j + 1), 1 - buf).start()

    lane = lax.broadcasted_iota(jnp.int32, (TM, LANES), 1)
    lane_f = lane.astype(F32)
    rows0 = []
    pieces = []
    for e in range(0, N_EXPERTS, LANES // WIN):
        tgt = None
        for i in range(LANES // WIN):
            row0 = first_row(e + i, j)
            win_copy(e + i, row0, buf).wait()
            rows0.append(row0)
            t_i = slott_ref[:, e + i:e + i + 1] - (row0 - WIN * i).astype(F32)
            in_win = (lane >= WIN * i) & (lane < WIN * (i + 1))
            tgt = jnp.where(in_win, t_i, -1.0) if tgt is None else jnp.where(in_win, t_i, tgt)
        pieces.append(jnp.where(tgt == lane_f, 1.0, 0.0).astype(BF16))
    onehot = jnp.concatenate(pieces, axis=1)
    o_ref[...] = _dot(onehot, win_s[buf].reshape(N_EXPERTS * WIN, D_MODEL).astype(BF16))

    lane_w = lax.broadcasted_iota(jnp.int32, (TM, WIN), 1).astype(F32)
    for e in range(N_EXPERTS):
        row0 = rows0[e]
        n_more = jnp.maximum((a_ref[e * LANES + j + 1] - row0 + WIN - 1) // WIN - 1, 0)

        def more(i, _):
            lo_slot = row0 + (i + 1) * WIN
            r = pl.multiple_of(jnp.minimum(lo_slot, CAP_ALL - WIN), 8)
            cp = pltpu.make_async_copy(ye_hbm.at[e, pl.ds(r, WIN), :], xwin_s, xsem)
            cp.start()
            cp.wait()
            scol = slott_ref[:, e:e + 1]
            scol = jnp.where(scol >= lo_slot.astype(F32), scol, -1.0)
            oh = jnp.where(scol - r.astype(F32) == lane_w, 1.0, 0.0).astype(BF16)
            o_ref[...] += _dot(oh, xwin_s[...].astype(BF16))
            return 0

        lax.fori_loop(0, n_more, more, 0)


def _combine(a, ye, slott):
    tile = lambda w: pl.BlockSpec((TM, w), lambda i, *_: (i, 0))
    grid_spec = pltpu.PrefetchScalarGridSpec(
        num_scalar_prefetch=1, grid=(NT_ALL,),
        in_specs=[pl.BlockSpec(memory_space=pl.ANY), tile(N_EXPERTS)],
        out_specs=tile(D_MODEL),
        scratch_shapes=[pltpu.VMEM((2, N_EXPERTS, WIN, D_MODEL), F32),
                        pltpu.SemaphoreType.DMA((2, N_EXPERTS)),
                        pltpu.VMEM((WIN, D_MODEL), F32), pltpu.SemaphoreType.DMA(())])
    return pl.pallas_call(
        _combine_kernel, grid_spec=grid_spec,
        out_shape=jax.ShapeDtypeStruct((T_ALL, D_MODEL), F32),
        compiler_params=_cparams(1), name="moe_combine",
    )(a, ye, slott)


def _final_kernel(x_ref, moe_ref, mod_ref, g_ref, o_ref):
    g2 = mod_ref[...][:, 5 * D_MODEL:6 * D_MODEL]
    o_ref[...] = _rms(x_ref[...] + g2 * moe_ref[...], g_ref[...])


def _final(x, moe, mod_l, fg, sample):
    n, off = (T_SMP // TM, NT_CTX) if sample else (NT_CTX, 0)
    tile = pl.BlockSpec((TM, D_MODEL), lambda i: (off + i, 0))
    return pl.pallas_call(
        _final_kernel, grid=(n,),
        in_specs=[tile, tile, pl.BlockSpec((None, 1, 6 * D_MODEL), lambda i: (_mod_row(off + i), 0, 0)),
                  pl.BlockSpec(fg.shape, lambda i: (0, 0))],
        out_specs=pl.BlockSpec((TM, D_MODEL), lambda i: (i, 0)),
        out_shape=jax.ShapeDtypeStruct((n * TM, D_MODEL), F32),
        compiler_params=_cparams(1), name="final_norm",
    )(x, moe, mod_l, fg)


def _rope_tables():
    t = jnp.arange(DEC_SEQ)
    pos = jnp.stack([t // GRID_W, t % GRID_W], axis=-1).astype(F32)
    nf = DA_HD // 4
    inv = ROPE_BASE ** (-jnp.arange(nf, dtype=F32) / nf)
    ang = pos[:, :, None] * inv
    cos = jnp.cos(ang)
    sin = jnp.sin(ang)
    cos64 = jnp.stack([cos, cos], axis=2).reshape(DEC_SEQ, DA_HD)
    sin64 = jnp.stack([-sin, sin], axis=2).reshape(DEC_SEQ, DA_HD)
    cos_t = jnp.concatenate([jnp.ones((TM, LANES), F32), jnp.tile(cos64, (1, 2))], axis=0)
    sin_t = jnp.concatenate([jnp.zeros((TM, LANES), F32), jnp.tile(sin64, (1, 2))], axis=0)
    return cos_t, sin_t


def kernel(x_prompt, x_sample, cache_k, cache_v, state_ssm, state_mlstm_c, state_mlstm_n, state_mlstm_m, c, c_ctx, ada_w, ada_b, norm1_g, norm2_g, w_in, da_lambda, da_subln_g, mb_conv_w, mb_conv_b, mb_dt_bias, mb_a_log, mb_d, mb_norm_g, ml_gate_b, ml_norm_g, w_branch, w_mgate, b_mgate, w_out, router_w, ex_w_gate, ex_w_up, ex_w_down, final_g):
    x = (x_prompt.reshape(T_CTX, D_MODEL), x_sample.reshape(T_SMP, D_MODEL))
    cc = jnp.concatenate([c_ctx[None, :], c, jnp.zeros((16 - 1 - DEC_BATCH, D_MODEL), F32)], axis=0)
    mod = _modulation(cc, ada_w, ada_b).reshape(DEPTH, 16, 1, 6 * D_MODEL)
    cos_t, sin_t = _rope_tables()
    cache_k2 = cache_k.reshape(DEC_BATCH, DEPTH, PAST_LEN, DA_WIDTH)
    cache_v2 = cache_v.reshape(DEC_BATCH, DEPTH, PAST_LEN, DA_WIDTH)

    outs = {n: [] for n in ("k", "v", "ssm", "C", "n", "m")}
    moe = None
    for l in range(DEPTH):
        w = w_in[l]
        wm = jnp.concatenate([w[:, :2816], w[:, 2832:4368]], axis=1).astype(BF16)
        ws = jnp.concatenate([w[:, 2816:2832], w[:, 4368:4384]], axis=1).astype(BF16)
        res = _projection(x, moe, mod[l - 1] if l else None, mod[l], norm1_g[l][None], wm, ws, ws.T,
                          cos_t, sin_t)
        q, k, v, kf, vf, z, xbc, mq, mk, mv, mo, sm, smt, x = res
        outs["k"].append(kf[:T_CTX].reshape(BATCH, SEQ, DA_HEADS, 2, DA_HD))
        outs["v"].append(vf[:T_CTX].reshape(BATCH, SEQ, DA_HEADS, 2 * DA_HD))

        lp = da_lambda[l]
        sg = da_subln_g[l][None]
        cw = mb_conv_w[l]
        cb = mb_conv_b[l][None]
        dtb = mb_dt_bias[l].reshape(16)
        a_neg = -jnp.exp(mb_a_log[l]).reshape(16)
        dexp = jnp.repeat(mb_d[l], MB_HD)[None]
        mng = mb_norm_g[l][None]
        gb = ml_gate_b[l].reshape(16)
        lng = ml_norm_g[l].reshape(1, ML_WIDTH)
        ys = []
        for sample in (False, True):
            nb = DEC_BATCH if sample else BATCH
            y_da = _attention(l, q, k, v, cache_k2, cache_v2, lp, sg, sample)
            if sample:
                h0 = state_ssm[:, l]
                c0 = state_mlstm_c[:, l]
                n0 = state_mlstm_n[:, l]
                m0 = state_mlstm_m[:, l]
            else:
                h0 = jnp.zeros((nb, 2, MB_HEADS, MB_HD, MB_STATE), F32)
                c0 = jnp.zeros((nb, 2, ML_HEADS, ML_DK, ML_DV), F32)
                n0 = jnp.zeros((nb, 2, ML_HEADS, ML_DK), F32)
                m0 = jnp.zeros((nb, 2, ML_HEADS), F32)
            h0t = jnp.transpose(h0, (0, 1, 4, 2, 3)).reshape(nb, 2, MB_STATE, MB_INNER)
            y_mb, hfin = _ssd(z, xbc, sm, smt, cw, cb, dtb, a_neg, dexp, mng, h0t, sample)
            y_ml, cfin, nfin, mfin = _mlstm(
                mq, mk, mv, mo, sm, smt, gb, lng,
                c0.reshape(nb, 2, 2, LANES, ML_DV), n0.reshape(nb, 2, 2, LANES, 1),
                m0.reshape(nb, 1, 8), sample)
            ys.append((y_da, y_mb, y_ml))
            if not sample:
                outs["ssm"].append(jnp.transpose(
                    hfin.reshape(nb, 2, MB_STATE, MB_HEADS, MB_HD), (0, 1, 3, 4, 2)))
                outs["C"].append(cfin.reshape(nb, 2, ML_HEADS, ML_DK, ML_DV))
                outs["n"].append(nfin.reshape(nb, 2, ML_HEADS, ML_DK))
                outs["m"].append(mfin.reshape(nb, 2, ML_HEADS))
        x, h2t, affb = _merge(x, mod[l], norm1_g[l][None], ys[0], ys[1],
                              w_mgate[l].astype(BF16), b_mgate[l][None], w_branch[l].astype(BF16),
                              w_out[l].astype(BF16), norm2_g[l][None], router_w[l].T)
        slotb, slott, a_cnt, jlo, jhi = _route(affb)
        ye = _experts(l, jlo.reshape(-1), jhi.reshape(-1), h2t, slotb, affb,
                      ex_w_gate, ex_w_up, ex_w_down)
        moe = _combine(a_cnt.reshape(-1), ye, slott)

    fg = final_g[None]
    y_prompt = _final(x, moe, mod[DEPTH - 1], fg, False).reshape(BATCH, SEQ, D_MODEL)
    y_sample = _final(x, moe, mod[DEPTH - 1], fg, True).reshape(DEC_BATCH, DEC_SEQ, D_MODEL)
    return (y_prompt, y_sample, jnp.stack(outs["k"], axis=1), jnp.stack(outs["v"], axis=1),
            jnp.stack(outs["ssm"], axis=1), jnp.stack(outs["C"], axis=1),
            jnp.stack(outs["n"], axis=1), jnp.stack(outs["m"], axis=1))
```

```python
import functools
import math

import jax
import jax.numpy as jnp
from jax import lax
from jax.experimental import pallas as pl
from jax.experimental.pallas import tpu as pltpu

F32 = jnp.float32
BF16 = jnp.bfloat16

D_MODEL = 1024
BATCH = 16
SEQ = 256
DEPTH = 2
DEC_BATCH = 8
DEC_SEQ = 2048
PAST_LEN = 512
GRID_W = 64
EPS = 1e-6
CHUNK = 128
ROPE_BASE = 10000.0
DA_HEADS = 4
DA_HD = 64
DA_WIDTH = 512
MB_INNER = 512
MB_HD = 64
MB_HEADS = 8
MB_GROUPS = 2
MB_STATE = 64
MB_CONV_DIM = 768
ML_HEADS = 4
ML_DK = 64
ML_DV = 128
ML_WIDTH = 512
N_EXPERTS = 16
EC_FACTOR = 2
EXPERT_FF = 1024

T_CTX = BATCH * SEQ
T_SMP = DEC_BATCH * DEC_SEQ
T_ALL = T_CTX + T_SMP
TM = 256
NT_CTX = T_CTX // TM
NT_ALL = T_ALL // TM
NB_CTX = T_CTX // CHUNK
NB_ALL = T_ALL // CHUNK
CAP_CTX = EC_FACTOR * T_CTX // N_EXPERTS
CAP_SMP = EC_FACTOR * T_SMP // N_EXPERTS
CAP_ALL = CAP_CTX + CAP_SMP
RB = 256
N_RB = CAP_ALL // RB
WIN = 64
GATHER_TILES = 5
GATHER_AHEAD = 2
TILES_PER_REQ = DEC_SEQ // TM
TM_MERGE = 512
BRANCH_W = 512
LANES = 128
BF16_ROWS = 16
VMEM_LIMIT = 56 * 1024 * 1024
HI = lax.Precision.HIGHEST
LOG2E = math.log2(math.e)

C_Q, C_K, C_V, C_Z, C_XBC, C_MQ, C_MK, C_MV, C_MO, C_END = (
    0, 512, 1024, 1536, 2048, 2816, 3072, 3328, 3840, 4352)


def _mod_row(i):
    return jnp.where(i < NT_CTX, 0, 1 + (i - NT_CTX) // TILES_PER_REQ)


def _rope_blk(i):
    return jnp.where(i < NT_CTX, 0, 1 + (i - NT_CTX) % TILES_PER_REQ)


def _cparams(n_grid):
    return pltpu.CompilerParams(dimension_semantics=("arbitrary",) * n_grid,
                                vmem_limit_bytes=VMEM_LIMIT)


def _silu(x):
    return x * jax.nn.sigmoid(x)


def _softplus(x):
    u = jnp.exp(-jnp.abs(x))
    w = 1.0 + u
    l1p = jnp.where(w == 1.0, u, jnp.log(w) * (u / (w - 1.0)))
    return jnp.maximum(x, 0.0) + l1p


def _dot(a, b):
    return jnp.dot(a, b, preferred_element_type=F32)


def _dot_nt(a, b):
    return lax.dot_general(a, b, (((1,), (1,)), ((), ())), preferred_element_type=F32)


def _dot_hi(a, b):
    return jnp.dot(a, b, precision=HI, preferred_element_type=F32)


def _split3(a):
    hi = a.astype(BF16)
    r = a - hi.astype(F32)
    mid = r.astype(BF16)
    return hi, mid, (r - mid.astype(F32)).astype(BF16)


def _mask3(mask, axis):
    m = jnp.where(mask, 1.0, 0.0).astype(BF16)
    return jnp.concatenate([m, m, m], axis=axis)


def _mask_dot(m3, a):
    return _dot(m3, jnp.concatenate(_split3(a), axis=0))


def _dot_mask(a, m3):
    return _dot(jnp.concatenate(_split3(a), axis=1), m3)


def _dot_mask_narrow(a, m):
    hi, mid, lo = _split3(a)
    return _dot(hi, m) + _dot(mid, m) + _dot(lo, m)


def _rms(x, g):
    return x * lax.rsqrt(jnp.mean(x * x, axis=-1, keepdims=True) + EPS) * g


def _mod_kernel(c_ref, w_ref, b_ref, o_ref):
    s = _silu(c_ref[...])
    o_ref[...] = _dot(s.astype(BF16), w_ref[...].astype(BF16)) + b_ref[...]


def _modulation(cc, ada_w, ada_b):
    tn = 1536
    return pl.pallas_call(
        _mod_kernel,
        grid=(DEPTH, 6 * D_MODEL // tn),
        in_specs=[pl.BlockSpec((16, D_MODEL), lambda l, j: (0, 0)),
                  pl.BlockSpec((None, D_MODEL, tn), lambda l, j: (l, 0, j)),
                  pl.BlockSpec((None, 1, tn), lambda l, j: (l, 0, j))],
        out_specs=pl.BlockSpec((None, 16, tn), lambda l, j: (l, 0, j)),
        out_shape=jax.ShapeDtypeStruct((DEPTH, 16, 6 * D_MODEL), F32),
        compiler_params=_cparams(2),
        name="adaln_mod",
    )(cc, ada_w, ada_b.reshape(DEPTH, 1, 6 * D_MODEL))


def _rope(t, cos, sin, first_half):
    outs = []
    for c in range(DA_WIDTH // LANES):
        xc = t[:, LANES * c:LANES * (c + 1)]
        partner = jnp.where(first_half, pltpu.roll(xc, LANES - 16, 1), pltpu.roll(xc, 16, 1))
        outs.append(xc * cos + partner * sin)
    return jnp.concatenate(outs, axis=1)


def _proj_body(x, mod_ref, g_ref, wm_ref, ws_ref, wst_ref, cos_ref, sin_ref,
               q_ref, k_ref, v_ref, kf_ref, vf_ref, z_ref, xbc_ref,
               mq_ref, mk_ref, mv_ref, mo_ref, sm_ref, smt_ref):
    mod = mod_ref[...]
    sh1 = mod[:, 0:D_MODEL]
    sc1 = mod[:, D_MODEL:2 * D_MODEL]
    h = _rms(x, g_ref[...]) * (1.0 + sc1) + sh1
    hb = h.astype(BF16)

    def proj(a, b):
        return _dot(hb, wm_ref[:, a:b])

    lane = lax.broadcasted_iota(jnp.int32, (TM, LANES), 1)
    first_half = (lane % 32) < 16
    cos = cos_ref[...]
    sin = sin_ref[...]
    q = proj(C_Q, C_K)
    k = proj(C_K, C_V)
    v = proj(C_V, C_Z)
    kf_ref[...] = k
    vf_ref[...] = v
    q_ref[...] = (_rope(q, cos, sin, first_half) * (DA_HD ** -0.5 * LOG2E)).astype(BF16)
    k_ref[...] = _rope(k, cos, sin, first_half).astype(BF16)
    v_ref[...] = v.astype(BF16)
    z_ref[...] = proj(C_Z, C_XBC)
    xbc_ref[...] = proj(C_XBC, C_MQ)
    mq_ref[...] = proj(C_MQ, C_MK).astype(BF16)
    mk_ref[...] = (proj(C_MK, C_MV) * (ML_DK ** -0.5)).astype(BF16)
    mv_ref[...] = proj(C_MV, C_MO).astype(BF16)
    mo_ref[...] = proj(C_MO, C_END)
    sm_ref[...] = _dot(hb, ws_ref[...])
    st = _dot_nt(wst_ref[...], hb)
    for j in range(TM // CHUNK):
        smt_ref[j] = st[:, CHUNK * j:CHUNK * (j + 1)]


def _proj_kernel_first(xp_ref, xs_ref, *refs):
    x = jnp.where(pl.program_id(0) < NT_CTX, xp_ref[...], xs_ref[...])
    refs[-1][...] = x
    _proj_body(x, *refs[:-1])


def _proj_kernel_next(x_ref, *refs):
    _proj_body(x_ref[...], *refs)


def _ctx_tile(w):
    return pl.BlockSpec((TM, w), lambda i: (jnp.minimum(i, NT_CTX - 1), 0))


def _smp_tile(w):
    return pl.BlockSpec((TM, w), lambda i: (jnp.maximum(i - NT_CTX, 0), 0))


def _projection(xs, mod_l, g1, wm, ws, wst, cos_t, sin_t):
    tile = lambda w: pl.BlockSpec((TM, w), lambda i: (i, 0))
    full = lambda a: pl.BlockSpec(a.shape, lambda i: (0,) * a.ndim)
    modspec = pl.BlockSpec((None, 1, 6 * D_MODEL), lambda i: (_mod_row(i), 0, 0))
    first = isinstance(xs, tuple)
    if first:
        in_specs = [_ctx_tile(D_MODEL), _smp_tile(D_MODEL)]
        args = list(xs)
        kern = _proj_kernel_first
    else:
        in_specs = [tile(D_MODEL)]
        args = [xs]
        kern = _proj_kernel_next
    in_specs += [modspec, full(g1), full(wm), full(ws), full(wst),
                 pl.BlockSpec((TM, LANES), lambda i: (_rope_blk(i), 0)),
                 pl.BlockSpec((TM, LANES), lambda i: (_rope_blk(i), 0))]
    args += [mod_l, g1, wm, ws, wst, cos_t, sin_t]
    ctx_only = pl.BlockSpec((TM, DA_WIDTH), lambda i: (jnp.minimum(i, NT_CTX), 0))
    out_specs = [tile(512), tile(512), tile(512), ctx_only, ctx_only, tile(512), tile(768),
                 tile(256), tile(256), tile(512), tile(512), tile(32),
                 pl.BlockSpec((TM // CHUNK, 32, CHUNK), lambda i: (i, 0, 0))]
    sds = jax.ShapeDtypeStruct
    out_shape = [sds((T_ALL, 512), BF16), sds((T_ALL, 512), BF16), sds((T_ALL, 512), BF16),
                 sds((T_CTX + TM, 512), F32), sds((T_CTX + TM, 512), F32),
                 sds((T_ALL, 512), F32), sds((T_ALL, 768), F32),
                 sds((T_ALL, 256), BF16), sds((T_ALL, 256), BF16), sds((T_ALL, 512), BF16),
                 sds((T_ALL, 512), F32), sds((T_ALL, 32), F32),
                 sds((T_ALL // CHUNK, 32, CHUNK), F32)]
    if first:
        out_specs.append(tile(D_MODEL))
        out_shape.append(sds((T_ALL, D_MODEL), F32))
    return pl.pallas_call(
        kern, grid=(NT_ALL,), in_specs=in_specs, out_specs=out_specs, out_shape=out_shape,
        compiler_params=_cparams(1), name="in_proj",
    )(*args)


def _attn_body(lam_init, q_ref, k_ref, v_ref, kc_ref, vc_ref, lp_ref, g_ref, o_ref):
    lp = lp_ref[...]
    s01 = jnp.sum(lp[0:1] * lp[1:2], axis=-1, keepdims=True)
    s23 = jnp.sum(lp[2:3] * lp[3:4], axis=-1, keepdims=True)
    lam = jnp.exp(s01) - jnp.exp(s23) + lam_init
    tq = q_ref.shape[0]
    lane = lax.broadcasted_iota(jnp.int32, (tq, LANES), 1)
    g = g_ref[...]
    for h in range(DA_HEADS):
        cols = slice(LANES * h, LANES * (h + 1))
        qh = q_ref[:, cols]
        kh = k_ref[:, cols]
        vh = v_ref[:, cols]
        if kc_ref is not None:
            kch = kc_ref[:, cols].astype(BF16)
            vch = vc_ref[:, cols].astype(BF16)
        parts = []
        for m in range(2):
            qm = jnp.where((lane < DA_HD) == (m == 0), qh, jnp.zeros_like(qh))
            s = _dot_nt(qm, kh)
            mx = jnp.max(s, axis=-1, keepdims=True)
            if kc_ref is not None:
                sc = _dot_nt(qm, kch)
                mx = jnp.maximum(mx, jnp.max(sc, axis=-1, keepdims=True))
                ec = jnp.exp2(sc - mx)
            e = jnp.exp2(s - mx)
            den = jnp.sum(e, axis=-1, keepdims=True)
            acc = _dot(e.astype(BF16), vh)
            if kc_ref is not None:
                den = den + jnp.sum(ec, axis=-1, keepdims=True)
                acc = acc + _dot(ec.astype(BF16), vch)
            parts.append(acc / den)
        att = parts[0] - lam * parts[1]
        o_ref[:, cols] = (_rms(att, g) * (1.0 - lam_init)).astype(BF16)


def _attn_kernel_ctx(lam_init, q_ref, k_ref, v_ref, lp_ref, g_ref, o_ref):
    _attn_body(lam_init, q_ref, k_ref, v_ref, None, None, lp_ref, g_ref, o_ref)


def _attn_kernel_smp(lam_init, q_ref, k_ref, v_ref, kc_ref, vc_ref, lp_ref, g_ref, o_ref):
    _attn_body(lam_init, q_ref, k_ref, v_ref, kc_ref, vc_ref, lp_ref, g_ref, o_ref)


def _attention(l, q, k, v, cache_k, cache_v, lp, g, sample):
    lam_init = 0.8 - 0.6 * math.exp(-0.3 * l)
    full = lambda a: pl.BlockSpec(a.shape, lambda *_: (0,) * a.ndim)
    sds = jax.ShapeDtypeStruct
    if not sample:
        blk = pl.BlockSpec((SEQ, DA_WIDTH), lambda b: (b, 0))
        return pl.pallas_call(
            functools.partial(_attn_kernel_ctx, lam_init), grid=(BATCH,),
            in_specs=[blk, blk, blk, full(lp), full(g)],
            out_specs=blk, out_shape=sds((T_CTX, DA_WIDTH), BF16),
            compiler_params=_cparams(1), name="diff_attn_ctx",
        )(q, k, v, lp, g)
    tq = 256
    nq = DEC_SEQ // tq
    off_q = T_CTX // tq
    off_k = T_CTX // DEC_SEQ
    qblk = pl.BlockSpec((tq, DA_WIDTH), lambda b, i: (off_q + b * nq + i, 0))
    kblk = pl.BlockSpec((DEC_SEQ, DA_WIDTH), lambda b, i: (off_k + b, 0))
    cblk = pl.BlockSpec((None, None, PAST_LEN, DA_WIDTH), lambda b, i: (b, l, 0, 0))
    return pl.pallas_call(
        functools.partial(_attn_kernel_smp, lam_init), grid=(DEC_BATCH, nq),
        in_specs=[qblk, kblk, kblk, cblk, cblk, full(lp), full(g)],
        out_specs=pl.BlockSpec((tq, DA_WIDTH), lambda b, i: (b * nq + i, 0)),
        out_shape=sds((T_SMP, DA_WIDTH), BF16),
        compiler_params=_cparams(2), name="diff_attn_smp",
    )(q, k, v, cache_k, cache_v, lp, g)


def _tri_masks():
    r = lax.broadcasted_iota(jnp.int32, (CHUNK, CHUNK), 0)
    c = lax.broadcasted_iota(jnp.int32, (CHUNK, CHUNK), 1)
    return c <= r, c >= r


def _ssd_kernel(L, z_ref, xbc_ref, sm_ref, smt_ref, cw_ref, cb_ref, dtbr_ref, dtbc_ref,
                ar_ref, ac_ref, dexp_ref, ng_ref, h0_ref,
                y_ref, hfin_ref, xc_s, yacc_s, ht_s):
    nc = L // CHUNK
    low, upp = _tri_masks()
    tri_l3 = (_mask3(low, 1), _mask3(upp, 1))
    tri_r3 = (_mask3(upp, 0), _mask3(low, 0))
    lane512 = lax.broadcasted_iota(jnp.int32, (1, MB_INNER), 1)
    row16 = lax.broadcasted_iota(jnp.int32, (16, MB_INNER), 0)
    lane16 = lax.broadcasted_iota(jnp.int32, (16, MB_INNER), 1)
    expand = tuple(_mask3(row16 == 8 * d + lane16 // MB_HD, 0) for d in range(2))
    lane128 = lax.broadcasted_iota(jnp.int32, (CHUNK, LANES), 1)
    rowblk = lax.broadcasted_iota(jnp.int32, (LANES, MB_INNER), 0) // MB_STATE
    colblk = lax.broadcasted_iota(jnp.int32, (LANES, MB_INNER), 1) // (MB_INNER // MB_GROUPS)
    same_group = rowblk == colblk
    cw = cw_ref[...]
    cbias = cb_ref[...]

    def conv_chunk(c, _):
        base = pl.multiple_of(c * CHUNK, CHUNK)
        x = xbc_ref[pl.ds(base, CHUNK), :]
        prev = xbc_ref[pl.ds(jnp.maximum(base - 1, 0), 1), :]
        nxt = xbc_ref[pl.ds(jnp.minimum(base + CHUNK, L - 1), 1), :]
        prev = jnp.where(c == 0, 0.0, prev)
        nxt = jnp.where(c == nc - 1, 0.0, nxt)
        row = lax.broadcasted_iota(jnp.int32, (CHUNK, 1), 0)
        xp = jnp.where(row == 0, prev, pltpu.roll(x, 1, 0))
        xn = jnp.where(row == CHUNK - 1, nxt, pltpu.roll(x, CHUNK - 1, 0))
        conv = xp * cw[0:1] + x * cw[1:2] + xn * cw[2:3] + cbias
        xc_s[pl.ds(base, CHUNK), :] = _silu(conv)
        return 0

    lax.fori_loop(0, nc, conv_chunk, 0)

    h0 = h0_ref[...]
    for d in range(2):
        h0d = h0[d]
        ht_s[d] = jnp.concatenate(
            [jnp.where(lane512 < MB_INNER // 2, h0d, 0.0),
             jnp.where(lane512 >= MB_INNER // 2, h0d, 0.0)], axis=0)

    def both(i, _):
        jobs = ((0, i), (1, nc - 1 - i))
        rows, xsb, bb, cb, dt_row, cum_col, cum_row = {}, {}, {}, {}, {}, {}, {}
        y_off, cbg = {}, {}
        for d, c in jobs:
            rows[d] = pl.ds(pl.multiple_of(c * CHUNK, CHUNK), CHUNK)
            xs = xc_s[rows[d], 0:MB_INNER]
            bm = xc_s[rows[d], MB_INNER:MB_INNER + LANES]
            cm = xc_s[rows[d], MB_INNER + LANES:MB_CONV_DIM]
            dt_col = _softplus(sm_ref[rows[d], 0:16] + dtbr_ref[...])
            a_col = dt_col * ar_ref[...]
            dt_row[d] = _softplus(smt_ref[c][0:16, :] + dtbc_ref[...])
            a_row = dt_row[d] * ac_ref[...]
            cum_col[d] = _mask_dot(tri_l3[d], a_col)
            cum_row[d] = _dot_mask(a_row, tri_r3[d])
            last = 0 if d else CHUNK - 1
            cum_last = cum_col[d][last:last + 1, :]
            w_exp = _dot_mask(jnp.exp(cum_last - cum_col[d]) * dt_col, expand[d])
            g_exp = _dot_mask(jnp.exp(cum_col[d]), expand[d])
            cd_exp = _dot_mask(jnp.broadcast_to(jnp.exp(cum_last), (8, 16)), expand[d])[0:1]
            xw = (xs * w_exp).astype(BF16)
            xsb[d] = xs.astype(BF16)
            bb[d] = bm.astype(BF16)
            cb[d] = cm.astype(BF16)
            ht = ht_s[d]
            y_off[d] = _dot(cb[d], ht.astype(BF16)) * g_exp
            s_new = _dot(bm.T.astype(BF16), xw)
            ht_s[d] = ht * cd_exp + jnp.where(same_group, s_new, 0.0)
            for g in range(MB_GROUPS):
                cg = jnp.where((lane128 < MB_STATE) == (g == 0), cb[d], jnp.zeros_like(cb[d]))
                cbg[d, g] = _dot_nt(cg, bb[d])
        m16 = {}
        for d, _c in jobs:
            for h in range(MB_HEADS):
                ci = 8 * d + h
                seg = cum_col[d][:, ci:ci + 1] - cum_row[d][ci:ci + 1, :]
                m = jnp.where(upp if d else low, jnp.exp(seg), 0.0) * cbg[d, h // 4] * dt_row[d][ci:ci + 1, :]
                m16[d, h] = m.astype(BF16)
        yd = {}
        for d, _c in jobs:
            for h in range(MB_HEADS):
                k = h // 2
                yd[d, h] = _dot(m16[d, h], xsb[d][:, LANES * k:LANES * (k + 1)])
        for d, _c in jobs:
            pairs = [jnp.where(lane128 < MB_HD, yd[d, 2 * k], yd[d, 2 * k + 1]) for k in range(MB_HEADS // 2)]
            yacc_s[d, rows[d], :] = jnp.concatenate(pairs, axis=1) + y_off[d]
        return 0

    lax.fori_loop(0, nc, both, 0)

    dexp = dexp_ref[...]
    ng = ng_ref[...]

    def fin(c, _):
        rows = pl.ds(pl.multiple_of(c * CHUNK, CHUNK), CHUNK)
        y = yacc_s[0, rows, :] + yacc_s[1, rows, :] + dexp * xc_s[rows, 0:MB_INNER]
        y = y * _silu(z_ref[rows, :])
        y_ref[rows, :] = _rms(y, ng).astype(BF16)
        return 0

    lax.fori_loop(0, nc, fin, 0)
    for d in range(2):
        ht = ht_s[d]
        hfin_ref[d] = ht[0:MB_STATE, :] + ht[MB_STATE:2 * MB_STATE, :]


def _ssd(z, xbc, sm, smt, cw, cb, dtb, a_neg, dexp, ng, h0t, sample):
    nb, L, off = (DEC_BATCH, DEC_SEQ, T_CTX // DEC_SEQ) if sample else (BATCH, SEQ, 0)
    full = lambda a: pl.BlockSpec(a.shape, lambda b: (0,) * a.ndim)
    seq = lambda w: pl.BlockSpec((L, w), lambda b: (off + b, 0))
    dtb_r, dtb_c = dtb.reshape(1, 16), dtb.reshape(16, 1)
    a_r, a_c = a_neg.reshape(1, 16), a_neg.reshape(16, 1)
    sds = jax.ShapeDtypeStruct
    return pl.pallas_call(
        functools.partial(_ssd_kernel, L), grid=(nb,),
        in_specs=[seq(MB_INNER), seq(MB_CONV_DIM), seq(32),
                  pl.BlockSpec((L // CHUNK, 32, CHUNK), lambda b: (off + b, 0, 0)),
                  full(cw), full(cb), full(dtb_r), full(dtb_c), full(a_r), full(a_c),
                  full(dexp), full(ng),
                  pl.BlockSpec((None, 2, MB_STATE, MB_INNER), lambda b: (b, 0, 0, 0))],
        out_specs=[pl.BlockSpec((L, MB_INNER), lambda b: (b, 0)),
                   pl.BlockSpec((None, 2, MB_STATE, MB_INNER), lambda b: (b, 0, 0, 0))],
        out_shape=[sds((nb * L, MB_INNER), BF16), sds((nb, 2, MB_STATE, MB_INNER), F32)],
        scratch_shapes=[pltpu.VMEM((L, MB_CONV_DIM), F32), pltpu.VMEM((2, L, MB_INNER), F32),
                        pltpu.VMEM((2, LANES, MB_INNER), F32)],
        compiler_params=_cparams(1), name="ssd_smp" if sample else "ssd_ctx",
    )(z, xbc, sm, smt, cw, cb, dtb_r, dtb_c, a_r, a_c, dexp, ng, h0t)


def _mlstm_kernel(L, q_ref, k_ref, v_ref, o_ref, sm_ref, smt_ref, gbr_ref, gbc_ref, ng_ref,
                  c0_ref, n0_ref, m0_ref,
                  y_ref, cf_ref, nf_ref, mf_ref, hacc_s, c_s, n_s):
    nc = L // CHUNK
    low, upp = _tri_masks()
    tri_l = (low.astype(F32), upp.astype(F32))
    tri_r = (upp.astype(F32), low.astype(F32))
    lane128 = lax.broadcasted_iota(jnp.int32, (CHUNK, LANES), 1)
    lane8 = lax.broadcasted_iota(jnp.int32, (1, 2 * ML_HEADS), 1)
    neg_inf = -jnp.inf

    ones16 = jnp.ones((CHUNK, LANES), BF16)
    c_s[...] = c0_ref[...]
    n_s[...] = jnp.broadcast_to(n0_ref[...], n_s.shape)
    m0 = m0_ref[...]

    def both(i, m_in):
        jobs = ((0, i), (1, nc - 1 - i))
        heads = [(d, h) for d, _ in jobs for h in range(ML_HEADS)]
        rows, pre_row, b_col, b_row = {}, {}, {}, {}
        for d, c in jobs:
            rows[d] = pl.ds(pl.multiple_of(c * CHUNK, CHUNK), CHUNK)
            pre_col = sm_ref[rows[d], 16:32] + gbr_ref[...]
            pre_row[d] = smt_ref[c][16:32, :] + gbc_ref[...]
            lf_col = -_softplus(-pre_col)
            lf_row = -_softplus(-pre_row[d])
            b_col[d] = _dot_hi(tri_l[d], lf_col)
            b_row[d] = _dot_hi(lf_row, tri_r[d])
        kt, cst, nst, vh, qk, qc, qn = {}, {}, {}, {}, {}, {}, {}
        for d, _ in jobs:
            for pr in range(2):
                qp = q_ref[rows[d], LANES * pr:LANES * (pr + 1)]
                kp = k_ref[rows[d], LANES * pr:LANES * (pr + 1)]
                kt[d, pr] = kp.astype(F32).T
                cst[d, pr] = c_s[d, pr]
                nst[d, pr] = n_s[d, pr]
                cb16 = cst[d, pr].astype(BF16)
                nb16 = nst[d, pr].astype(BF16)
                for hh in range(2):
                    h = 2 * pr + hh
                    qm = jnp.where((lane128 < ML_DK) == (hh == 0), qp, jnp.zeros_like(qp))
                    vh[d, h] = jnp.concatenate([v_ref[rows[d], ML_DV * h:ML_DV * (h + 1)], ones16], axis=1)
                    qk[d, h] = _dot_nt(qm, kp)
                    qc[d, h] = _dot(qm, cb16)
                    qn[d, h] = _dot(qm, nb16)
        m_t, s_intra, s_inter, m_new, s_old, kts = {}, {}, {}, {}, {}, {}
        for d, h in heads:
            m_st = m_in[:, 4 * d + h:4 * d + h + 1]
            bcol = jnp.broadcast_to(b_col[d][:, 8 * d + 4 + h:8 * d + 5 + h], (CHUNK, CHUNK))
            brow = b_row[d][8 * d + 4 + h:8 * d + 5 + h, :]
            li_row = pre_row[d][8 * d + h:8 * d + h + 1, :]
            dm = jnp.where(upp if d else low, bcol - brow + li_row, neg_inf)
            inter = bcol + m_st
            m_t[d, h] = jnp.maximum(inter, jnp.max(dm, axis=-1, keepdims=True))
            s_intra[d, h] = jnp.exp(dm - m_t[d, h]) * qk[d, h]
            s_inter[d, h] = jnp.exp(inter - m_t[d, h])
            last = 0 if d else CHUNK - 1
            b_end = brow[:, last:last + 1]
            w_end = b_end - brow + li_row
            m_new[d, h] = jnp.maximum(b_end + m_st, jnp.max(w_end, axis=-1, keepdims=True))
            s_old[d, h] = jnp.exp(b_end + m_st - m_new[d, h])
            half = slice(ML_DK * (h % 2), ML_DK * (h % 2 + 1))
            kts[d, h] = kt[d, h // 2][half, :] * jnp.exp(w_end - m_new[d, h])
        pv, kv = {}, {}
        for d, h in heads:
            pv[d, h] = _dot(s_intra[d, h].astype(BF16), vh[d, h])
            kv[d, h] = _dot(kts[d, h].astype(BF16), vh[d, h])
        m_out = m_in
        for d, h in heads:
            num = s_inter[d, h] * qc[d, h] + pv[d, h][:, :ML_DV]
            den = s_inter[d, h] * qn[d, h] + pv[d, h][:, ML_DV:]
            hout = num / jnp.maximum(jnp.abs(den), jnp.exp(-m_t[d, h]))
            hacc_s[d, rows[d], ML_DV * h:ML_DV * (h + 1)] = hout
            m_out = jnp.where(lane8 == 4 * d + h, m_new[d, h], m_out)
        for d, _ in jobs:
            for pr in range(2):
                c_new, n_new = [], []
                for hh in range(2):
                    h = 2 * pr + hh
                    half = slice(ML_DK * hh, ML_DK * (hh + 1))
                    c_new.append(s_old[d, h] * cst[d, pr][half, :] + kv[d, h][:, :ML_DV])
                    n_new.append(s_old[d, h] * nst[d, pr][half, :] + kv[d, h][:, ML_DV:])
                c_s[d, pr] = jnp.concatenate(c_new, axis=0)
                n_s[d, pr] = jnp.concatenate(n_new, axis=0)
        return m_out

    m_fin = lax.fori_loop(0, nc, both, m0)

    ng = ng_ref[...]

    def fin(c, _):
        rows = pl.ds(pl.multiple_of(c * CHUNK, CHUNK), CHUNK)
        for h in range(ML_HEADS):
            cols = slice(ML_DV * h, ML_DV * (h + 1))
            y = _rms(hacc_s[0, rows, cols] + hacc_s[1, rows, cols], ng[:, cols]) * jax.nn.sigmoid(o_ref[rows, cols])
            y_ref[rows, cols] = y.astype(BF16)
        return 0

    lax.fori_loop(0, nc, fin, 0)
    cf_ref[...] = c_s[...]
    nf_ref[...] = n_s[:, :, :, 0:1]
    mf_ref[...] = m_fin


def _mlstm(q, k, v, o, sm, smt, gb, ng, c0, n0, m0, sample):
    nb, L, off = (DEC_BATCH, DEC_SEQ, T_CTX // DEC_SEQ) if sample else (BATCH, SEQ, 0)
    full = lambda a: pl.BlockSpec(a.shape, lambda b: (0,) * a.ndim)
    seq = lambda w: pl.BlockSpec((L, w), lambda b: (off + b, 0))
    gb_r, gb_c = gb.reshape(1, 16), gb.reshape(16, 1)
    st_c = pl.BlockSpec((None, 2, 2, LANES, ML_DV), lambda b: (b, 0, 0, 0, 0))
    st_n = pl.BlockSpec((None, 2, 2, LANES, 1), lambda b: (b, 0, 0, 0, 0))
    st_m = pl.BlockSpec((None, 1, 8), lambda b: (b, 0, 0))
    sds = jax.ShapeDtypeStruct
    return pl.pallas_call(
        functools.partial(_mlstm_kernel, L), grid=(nb,),
        in_specs=[seq(256), seq(256), seq(512), seq(512), seq(32),
                  pl.BlockSpec((L // CHUNK, 32, CHUNK), lambda b: (off + b, 0, 0)),
                  full(gb_r), full(gb_c), full(ng), st_c, st_n, st_m],
        out_specs=[pl.BlockSpec((L, ML_WIDTH), lambda b: (b, 0)), st_c, st_n, st_m],
        out_shape=[sds((nb * L, ML_WIDTH), BF16), sds((nb, 2, 2, LANES, ML_DV), F32),
                   sds((nb, 2, 2, LANES, 1), F32), sds((nb, 1, 8), F32)],
        scratch_shapes=[pltpu.VMEM((2, L, ML_WIDTH), F32), pltpu.VMEM((2, 2, LANES, ML_DV), F32),
                        pltpu.VMEM((2, 2, LANES, LANES), F32)],
        compiler_params=_cparams(1), name="mlstm_smp" if sample else "mlstm_ctx",
    )(q, k, v, o, sm, smt, gb_r, gb_c, ng, c0, n0, m0)


def _merge_kernel(x_ref, mod_ref, g1_ref, ydac_ref, ymbc_ref, ymlc_ref, ydas_ref, ymbs_ref, ymls_ref,
                  wg_ref, bg_ref, wb_ref, wo_ref, g2_ref, rwt_ref, xo_ref, h2t_ref, affb_ref):
    x = x_ref[...]
    mod = mod_ref[...]
    sh1, sc1, gt1, sh2, sc2 = (mod[:, j * D_MODEL:(j + 1) * D_MODEL] for j in range(5))
    hb = (_rms(x, g1_ref[...]) * (1.0 + sc1) + sh1).astype(BF16)
    is_ctx = pl.program_id(0) < T_CTX // TM_MERGE
    merged = None
    for n, (yc_ref, ys_ref) in enumerate(((ydac_ref, ydas_ref), (ymbc_ref, ymbs_ref), (ymlc_ref, ymls_ref))):
        cols = slice(n * D_MODEL, (n + 1) * D_MODEL)
        gate = jax.nn.sigmoid(_dot(hb, wg_ref[:, cols]) + bg_ref[:, cols])
        y = jnp.where(is_ctx, yc_ref[...], ys_ref[...])
        term = gate * _dot(y, wb_ref[n])
        merged = term if merged is None else merged + term
    out = _dot(merged.astype(BF16), wo_ref[...])
    xn = x + gt1 * out
    xo_ref[...] = xn
    h2 = _rms(xn, g2_ref[...]) * (1.0 + sc2) + sh2
    h2t_ref[...] = h2.T.astype(BF16)
    logits = lax.dot_general(rwt_ref[...], h2, (((1,), (1,)), ((), ())),
                             precision=HI, preferred_element_type=F32)
    e = jnp.exp(logits - jnp.max(logits, axis=0, keepdims=True))
    aff = e / jnp.sum(e, axis=0, keepdims=True)
    for j in range(TM_MERGE // CHUNK):
        affb_ref[j] = aff[:, CHUNK * j:CHUNK * (j + 1)]


def _merge(x, mod_l, g1, ys_ctx, ys_smp, wg, bg, wb, wo, g2, rwt):
    tm = TM_MERGE
    n_ctx = T_CTX // tm
    per_req = DEC_SEQ // tm
    tile = lambda w: pl.BlockSpec((tm, w), lambda i: (i, 0))
    ctx_tile = pl.BlockSpec((tm, BRANCH_W), lambda i: (jnp.minimum(i, n_ctx - 1), 0))
    smp_tile = pl.BlockSpec((tm, BRANCH_W), lambda i: (jnp.maximum(i - n_ctx, 0), 0))
    full = lambda a: pl.BlockSpec(a.shape, lambda i: (0,) * a.ndim)
    mod_row = lambda i: jnp.where(i < n_ctx, 0, 1 + (i - n_ctx) // per_req)
    sds = jax.ShapeDtypeStruct
    return pl.pallas_call(
        _merge_kernel, grid=(T_ALL // tm,),
        in_specs=[tile(D_MODEL), pl.BlockSpec((None, 1, 6 * D_MODEL), lambda i: (mod_row(i), 0, 0)),
                  full(g1), ctx_tile, ctx_tile, ctx_tile, smp_tile, smp_tile, smp_tile,
                  full(wg), full(bg), full(wb), full(wo), full(g2), full(rwt)],
        out_specs=[tile(D_MODEL), pl.BlockSpec((D_MODEL, tm), lambda i: (0, i)),
                   pl.BlockSpec((tm // CHUNK, N_EXPERTS, CHUNK), lambda i: (i, 0, 0))],
        out_shape=[sds((T_ALL, D_MODEL), F32), sds((D_MODEL, T_ALL), BF16),
                   sds((NB_ALL, N_EXPERTS, CHUNK), F32)],
        compiler_params=_cparams(1), name="merge_out",
    )(x, mod_l, g1, *ys_ctx, *ys_smp, wg, bg, wb, wo, g2, rwt)


def _route_kernel(affb_ref, slotb_ref, slott_ref, a_ref, jlo_ref, jhi_ref, acc_s, run_s):
    r = lax.broadcasted_iota(jnp.int32, (CHUNK, CHUNK), 0)
    c = lax.broadcasted_iota(jnp.int32, (CHUNK, CHUNK), 1)
    upper = (r <= c).astype(BF16)
    eye = (r == c).astype(F32)
    lane = lax.broadcasted_iota(jnp.int32, (N_EXPERTS, LANES), 1)
    acc_s[...] = jnp.zeros_like(acc_s)
    run_s[...] = jnp.zeros_like(run_s)
    for b0, b1, cap in ((0, NB_CTX, CAP_CTX), (NB_CTX, NB_ALL, CAP_SMP)):
        aff = affb_ref[b0:b1]

        def search(i, thr_bits):
            cand = thr_bits | lax.shift_left(jnp.int32(1), 30 - i)
            cnt = jnp.sum((aff >= pltpu.bitcast(cand, F32)[None]).astype(jnp.int32), axis=0)
            cnt = jnp.sum(cnt, axis=1, keepdims=True)
            return jnp.where(cnt >= cap, cand, thr_bits)

        thr = pltpu.bitcast(lax.fori_loop(0, 31, search, jnp.zeros((N_EXPERTS, 1), jnp.int32)), F32)
        n_gt = jnp.sum(jnp.sum((aff > thr[None]).astype(jnp.int32), axis=0), axis=1, keepdims=True)
        need = (cap - n_gt).astype(F32)

        run_s[1] = jnp.zeros((N_EXPERTS, 1), F32)

        def blk(b, _):
            run_sel = run_s[0]
            run_eq = run_s[1]
            x = affb_ref[b]
            eq = x == thr
            eq_f = jnp.where(eq, 1.0, 0.0)
            eq_incl = _dot(eq_f.astype(BF16), upper)
            sel = (x > thr) | (eq & (run_eq + eq_incl - eq_f < need))
            sel_f = jnp.where(sel, 1.0, 0.0)
            sel_incl = _dot(sel_f.astype(BF16), upper)
            slot = jnp.where(sel, run_sel + sel_incl - sel_f, -1.0)
            slotb_ref[b] = slot.astype(jnp.int32)
            slott_ref[pl.ds(pl.multiple_of(b * CHUNK, CHUNK), CHUNK), :] = lax.dot_general(
                eye, slot, (((1,), (1,)), ((), ())), precision=HI, preferred_element_type=F32)
            acc_s[...] = jnp.where((lane == b // 2) & (b % 2 == 0), run_sel, acc_s[...])
            run_s[0] = run_sel + sel_incl[:, CHUNK - 1:CHUNK]
            run_s[1] = run_eq + eq_incl[:, CHUNK - 1:CHUNK]
            return 0

        lax.fori_loop(b0, b1, blk, 0)
    a_acc = jnp.where(lane == NT_ALL, run_s[0], acc_s[...])
    a_ref[...] = a_acc.astype(jnp.int32)
    a_next = pltpu.roll(a_acc, LANES - 1, 1)
    tile_ok = lane < NT_ALL
    jlo = jnp.zeros((N_EXPERTS, LANES), jnp.int32)
    jhi = jnp.zeros((N_EXPERTS, LANES), jnp.int32)
    for k in range(N_RB):
        lo_k = jnp.sum((tile_ok & (a_next <= float(RB * k))).astype(jnp.int32), axis=1, keepdims=True)
        hi_k = jnp.sum((tile_ok & (a_acc < float(RB * (k + 1)))).astype(jnp.int32), axis=1, keepdims=True) - 1
        jlo = jnp.where(lane == k, lo_k, jlo)
        jhi = jnp.where(lane == k, hi_k, jhi)
    jlo_ref[...] = jlo
    jhi_ref[...] = jhi


def _route(affb):
    sds = jax.ShapeDtypeStruct
    small = sds((N_EXPERTS, LANES), jnp.int32)
    return pl.pallas_call(
        _route_kernel,
        out_shape=[sds((NB_ALL, N_EXPERTS, CHUNK), jnp.int32), sds((T_ALL, N_EXPERTS), F32),
                   small, small, small],
        scratch_shapes=[pltpu.VMEM((N_EXPERTS, LANES), F32), pltpu.VMEM((2, N_EXPERTS, 1), F32)],
        compiler_params=pltpu.CompilerParams(vmem_limit_bytes=VMEM_LIMIT), name="route",
    )(affb)


def _expert_kernel(jlo_ref, jhi_ref, h2t_hbm, slotb_ref, affb_ref, wg_ref, wu_ref, wd_ref, o_ref,
                   wg_s, wu_s, wd_s, chunk_s, sem, acc_s, g_s, cnt_s):
    e = pl.program_id(0)
    k = pl.program_id(1)
    step = e * N_RB + k

    @pl.when(k == 0)
    def _():
        wg_s[...] = wg_ref[...].astype(BF16)
        wu_s[...] = wu_ref[...].astype(BF16)
        wd_s[...] = wd_ref[...].astype(BF16)

    def tile_range(s):
        i = (s // N_RB) * LANES + s % N_RB
        lo = jnp.clip(jlo_ref[i], 0, NT_ALL - 1)
        return lo, jnp.clip(jhi_ref[i] - lo + 1, 1, NT_ALL - lo)

    def chunk_tile(lo, c):
        return jnp.minimum(lo + GATHER_TILES * c, NT_ALL - GATHER_TILES)

    def chunk_copy(j0, buf):
        return pltpu.make_async_copy(
            h2t_hbm.at[:, pl.ds(pl.multiple_of(j0 * TM, TM), GATHER_TILES * TM)],
            chunk_s.at[buf], sem.at[buf])

    def n_chunks(s):
        return (tile_range(s)[1] + GATHER_TILES - 1) // GATHER_TILES

    n_steps = N_EXPERTS * N_RB

    def advance(s, c):
        wrap = (c + 1 >= n_chunks(jnp.minimum(s, n_steps - 1))) | (s >= n_steps)
        return jnp.where(wrap, s + 1, s), jnp.where(wrap, 0, c + 1)

    def start_at(s, c, buf):
        @pl.when(s < n_steps)
        def _():
            chunk_copy(chunk_tile(tile_range(jnp.minimum(s, n_steps - 1))[0], c), buf).start()

    jlo, n = tile_range(step)
    nch = n_chunks(step)

    @pl.when(step == 0)
    def _():
        cnt_s[0] = 0
        pos = (step, 0)
        for i in range(GATHER_AHEAD):
            start_at(pos[0], pos[1], i)
            pos = advance(*pos)

    done = cnt_s[0]
    acc_s[...] = jnp.zeros_like(acc_s)
    g_s[...] = jnp.zeros_like(g_s)
    ntok = GATHER_TILES * TM
    want = lax.broadcasted_iota(jnp.int32, (RB, ntok), 0) + k * RB
    lane_tile = lax.broadcasted_iota(jnp.int32, (1, ntok), 1) // TM

    def body(c, _):
        buf = (done + c) % (GATHER_AHEAD + 1)
        j0 = chunk_tile(jlo, c)
        chunk_copy(j0, buf).wait()
        pos = (step, c)
        for _i in range(GATHER_AHEAD):
            pos = advance(*pos)
        start_at(pos[0], pos[1], (done + c + GATHER_AHEAD) % (GATHER_AHEAD + 1))

        nblk = ntok // CHUNK
        srow = jnp.concatenate([slotb_ref[2 * j0 + i, pl.ds(e, 1), :] for i in range(nblk)], axis=1)
        arow = jnp.concatenate([affb_ref[2 * j0 + i, pl.ds(e, 1), :] for i in range(nblk)], axis=1)
        fresh = j0 + lane_tile >= jlo + GATHER_TILES * c
        hit = (srow == want) & fresh
        onehot = jnp.where(hit, 1.0, 0.0).astype(BF16)
        acc_s[...] += _dot_nt(chunk_s[buf], onehot)
        g_s[...] += jnp.sum(jnp.where(hit, arow, 0.0), axis=1, keepdims=True)
        return 0

    lax.fori_loop(0, nch, body, 0)
    cnt_s[0] = done + nch
    xe = acc_s[...].T.astype(BF16)
    hid = _silu(_dot(xe, wg_s[...])) * _dot(xe, wu_s[...])
    o_ref[...] = (_dot(hid.astype(BF16), wd_s[...]) * g_s[...]).astype(BF16)


def _experts(l, jlo, jhi, h2t, slotb, affb, w_gate, w_up, w_down):
    wsp = pl.BlockSpec((None, None, D_MODEL, EXPERT_FF), lambda e, j, *_: (l, e, 0, 0))
    wsd = pl.BlockSpec((None, None, EXPERT_FF, D_MODEL), lambda e, j, *_: (l, e, 0, 0))
    whole = lambda a: pl.BlockSpec(a.shape, lambda e, j, *_: (0,) * a.ndim)
    grid_spec = pltpu.PrefetchScalarGridSpec(
        num_scalar_prefetch=2, grid=(N_EXPERTS, N_RB),
        in_specs=[pl.BlockSpec(memory_space=pl.ANY), whole(slotb), whole(affb), wsp, wsp, wsd],
        out_specs=pl.BlockSpec((None, RB, D_MODEL), lambda e, j, *_: (e, j, 0)),
        scratch_shapes=[pltpu.VMEM((D_MODEL, EXPERT_FF), BF16), pltpu.VMEM((D_MODEL, EXPERT_FF), BF16),
                        pltpu.VMEM((EXPERT_FF, D_MODEL), BF16),
                        pltpu.VMEM((GATHER_AHEAD + 1, D_MODEL, GATHER_TILES * TM), BF16),
                        pltpu.SemaphoreType.DMA((GATHER_AHEAD + 1,)), pltpu.VMEM((D_MODEL, RB), F32),
                        pltpu.VMEM((RB, 1), F32), pltpu.SMEM((1,), jnp.int32)])
    return pl.pallas_call(
        _expert_kernel, grid_spec=grid_spec,
        out_shape=jax.ShapeDtypeStruct((N_EXPERTS, CAP_ALL, D_MODEL), BF16),
        compiler_params=_cparams(2), name="expert_ffn",
    )(jlo, jhi, h2t, slotb, affb, w_gate, w_up, w_down)


def _combine_body(a_ref, ye_hbm, slott_ref, win_s, sem, xwin_s, xsem, o_ref):
    j = pl.program_id(0)
    nt = pl.num_programs(0)

    def first_row(e, jj):
        a = a_ref[e * LANES + jj]
        return pl.multiple_of(jnp.clip((a // BF16_ROWS) * BF16_ROWS, 0, CAP_ALL - WIN), BF16_ROWS)

    def win_copy(e, row0, buf):
        return pltpu.make_async_copy(ye_hbm.at[e, pl.ds(row0, WIN), :], win_s.at[buf, e], sem.at[buf, e])

    buf = j % 2

    @pl.when(j == 0)
    def _():
        for e in range(N_EXPERTS):
            win_copy(e, first_row(e, 0), 0).start()

    @pl.when(j + 1 < nt)
    def _():
        for e in range(N_EXPERTS):
            win_copy(e, first_row(e, j + 1), 1 - buf).start()

    lane = lax.broadcasted_iota(jnp.int32, (TM, LANES), 1)
    lane_f = lane.astype(F32)
    rows0 = []
    pieces = []
    for e in range(0, N_EXPERTS, LANES // WIN):
        tgt = None
        for i in range(LANES // WIN):
            row0 = first_row(e + i, j)
            win_copy(e + i, row0, buf).wait()
            rows0.append(row0)
            t_i = slott_ref[:, e + i:e + i + 1] - (row0 - WIN * i).astype(F32)
            in_win = (lane >= WIN * i) & (lane < WIN * (i + 1))
            tgt = jnp.where(in_win, t_i, -1.0) if tgt is None else jnp.where(in_win, t_i, tgt)
        pieces.append(jnp.where(tgt == lane_f, 1.0, 0.0).astype(BF16))
    onehot = jnp.concatenate(pieces, axis=1)
    o_ref[...] = _dot(onehot, win_s[buf].reshape(N_EXPERTS * WIN, D_MODEL))

    lane_w = lax.broadcasted_iota(jnp.int32, (TM, WIN), 1).astype(F32)
    for e in range(N_EXPERTS):
        row0 = rows0[e]
        n_more = jnp.maximum((a_ref[e * LANES + j + 1] - row0 + WIN - 1) // WIN - 1, 0)

        def more(i, _):
            lo_slot = row0 + (i + 1) * WIN
            r = pl.multiple_of(jnp.minimum(lo_slot, CAP_ALL - WIN), BF16_ROWS)
            cp = pltpu.make_async_copy(ye_hbm.at[e, pl.ds(r, WIN), :], xwin_s, xsem)
            cp.start()
            cp.wait()
            scol = slott_ref[:, e:e + 1]
            scol = jnp.where(scol >= lo_slot.astype(F32), scol, -1.0)
            oh = jnp.where(scol - r.astype(F32) == lane_w, 1.0, 0.0).astype(BF16)
            o_ref[...] += _dot(oh, xwin_s[...])
            return 0

        lax.fori_loop(0, n_more, more, 0)


def _combine_kernel_mid(a_ref, ye_hbm, slott_ref, x_ref, mod_ref, xo_ref, moe_s, win_s, sem, xwin_s, xsem):
    _combine_body(a_ref, ye_hbm, slott_ref, win_s, sem, xwin_s, xsem, moe_s)
    xo_ref[...] = x_ref[...] + mod_ref[...][:, 5 * D_MODEL:6 * D_MODEL] * moe_s[...]


def _combine_kernel_last(a_ref, ye_hbm, slott_ref, x_ref, mod_ref, fg_ref, yp_ref, ys_ref,
                         moe_s, win_s, sem, xwin_s, xsem):
    _combine_body(a_ref, ye_hbm, slott_ref, win_s, sem, xwin_s, xsem, moe_s)
    y = _rms(x_ref[...] + mod_ref[...][:, 5 * D_MODEL:6 * D_MODEL] * moe_s[...], fg_ref[...])

    @pl.when(pl.program_id(0) < NT_CTX)
    def _():
        yp_ref[...] = y

    @pl.when(pl.program_id(0) >= NT_CTX)
    def _():
        ys_ref[...] = y


def _combine(a, ye, slott, x, mod_l, fg):
    tile = lambda w: pl.BlockSpec((TM, w), lambda i, *_: (i, 0))
    in_specs = [pl.BlockSpec(memory_space=pl.ANY), tile(N_EXPERTS), tile(D_MODEL),
                pl.BlockSpec((None, 1, 6 * D_MODEL), lambda i, *_: (_mod_row(i), 0, 0))]
    args = [a, ye, slott, x, mod_l]
    sds = jax.ShapeDtypeStruct
    if fg is None:
        kern, out_specs, out_shape = _combine_kernel_mid, tile(D_MODEL), sds((T_ALL, D_MODEL), F32)
    else:
        kern = _combine_kernel_last
        in_specs.append(pl.BlockSpec(fg.shape, lambda i, *_: (0, 0)))
        args.append(fg)
        out_specs = [pl.BlockSpec((TM, D_MODEL), lambda i, *_: (jnp.minimum(i, NT_CTX - 1), 0)),
                     pl.BlockSpec((TM, D_MODEL), lambda i, *_: (jnp.maximum(i - NT_CTX, 0), 0))]
        out_shape = [sds((T_CTX, D_MODEL), F32), sds((T_SMP, D_MODEL), F32)]
    grid_spec = pltpu.PrefetchScalarGridSpec(
        num_scalar_prefetch=1, grid=(NT_ALL,), in_specs=in_specs, out_specs=out_specs,
        scratch_shapes=[pltpu.VMEM((TM, D_MODEL), F32),
                        pltpu.VMEM((2, N_EXPERTS, WIN, D_MODEL), BF16),
                        pltpu.SemaphoreType.DMA((2, N_EXPERTS)),
                        pltpu.VMEM((WIN, D_MODEL), BF16), pltpu.SemaphoreType.DMA(())])
    return pl.pallas_call(
        kern, grid_spec=grid_spec, out_shape=out_shape,
        compiler_params=_cparams(1), name="moe_combine",
    )(*args)


def _rope_tables():
    t = jnp.arange(DEC_SEQ)
    pos = jnp.stack([t // GRID_W, t % GRID_W], axis=-1).astype(F32)
    nf = DA_HD // 4
    inv = ROPE_BASE ** (-jnp.arange(nf, dtype=F32) / nf)
    ang = pos[:, :, None] * inv
    cos = jnp.cos(ang)
    sin = jnp.sin(ang)
    cos64 = jnp.stack([cos, cos], axis=2).reshape(DEC_SEQ, DA_HD)
    sin64 = jnp.stack([-sin, sin], axis=2).reshape(DEC_SEQ, DA_HD)
    cos_t = jnp.concatenate([jnp.ones((TM, LANES), F32), jnp.tile(cos64, (1, 2))], axis=0)
    sin_t = jnp.concatenate([jnp.zeros((TM, LANES), F32), jnp.tile(sin64, (1, 2))], axis=0)
    return cos_t, sin_t


def kernel(x_prompt, x_sample, cache_k, cache_v, state_ssm, state_mlstm_c, state_mlstm_n, state_mlstm_m, c, c_ctx, ada_w, ada_b, norm1_g, norm2_g, w_in, da_lambda, da_subln_g, mb_conv_w, mb_conv_b, mb_dt_bias, mb_a_log, mb_d, mb_norm_g, ml_gate_b, ml_norm_g, w_branch, w_mgate, b_mgate, w_out, router_w, ex_w_gate, ex_w_up, ex_w_down, final_g):
    x = (x_prompt.reshape(T_CTX, D_MODEL), x_sample.reshape(T_SMP, D_MODEL))
    cc = jnp.concatenate([c_ctx[None, :], c, jnp.zeros((16 - 1 - DEC_BATCH, D_MODEL), F32)], axis=0)
    mod = _modulation(cc, ada_w, ada_b).reshape(DEPTH, 16, 1, 6 * D_MODEL)
    cos_t, sin_t = _rope_tables()
    cache_k2 = cache_k.reshape(DEC_BATCH, DEPTH, PAST_LEN, DA_WIDTH)
    cache_v2 = cache_v.reshape(DEC_BATCH, DEPTH, PAST_LEN, DA_WIDTH)

    outs = {n: [] for n in ("k", "v", "ssm", "C", "n", "m")}
    fg = final_g[None]
    for l in range(DEPTH):
        w = w_in[l]
        wm = jnp.concatenate([w[:, :2816], w[:, 2832:4368]], axis=1).astype(BF16)
        ws = jnp.concatenate([w[:, 2816:2832], w[:, 4368:4384]], axis=1).astype(BF16)
        res = _projection(x, mod[l], norm1_g[l][None], wm, ws, ws.T, cos_t, sin_t)
        q, k, v, kf, vf, z, xbc, mq, mk, mv, mo, sm, smt = res[:13]
        if l == 0:
            x = res[13]
        outs["k"].append(kf[:T_CTX].reshape(BATCH, SEQ, DA_HEADS, 2, DA_HD))
        outs["v"].append(vf[:T_CTX].reshape(BATCH, SEQ, DA_HEADS, 2 * DA_HD))

        lp = da_lambda[l]
        sg = da_subln_g[l][None]
        cw = mb_conv_w[l]
        cb = mb_conv_b[l][None]
        dtb = mb_dt_bias[l].reshape(16)
        a_neg = -jnp.exp(mb_a_log[l]).reshape(16)
        dexp = jnp.repeat(mb_d[l], MB_HD)[None]
        mng = mb_norm_g[l][None]
        gb = ml_gate_b[l].reshape(16)
        lng = ml_norm_g[l].reshape(1, ML_WIDTH)
        ys = []
        for sample in (False, True):
            nb = DEC_BATCH if sample else BATCH
            y_da = _attention(l, q, k, v, cache_k2, cache_v2, lp, sg, sample)
            if sample:
                h0 = state_ssm[:, l]
                c0 = state_mlstm_c[:, l]
                n0 = state_mlstm_n[:, l]
                m0 = state_mlstm_m[:, l]
            else:
                h0 = jnp.zeros((nb, 2, MB_HEADS, MB_HD, MB_STATE), F32)
                c0 = jnp.zeros((nb, 2, ML_HEADS, ML_DK, ML_DV), F32)
                n0 = jnp.zeros((nb, 2, ML_HEADS, ML_DK), F32)
                m0 = jnp.zeros((nb, 2, ML_HEADS), F32)
            h0t = jnp.transpose(h0, (0, 1, 4, 2, 3)).reshape(nb, 2, MB_STATE, MB_INNER)
            y_mb, hfin = _ssd(z, xbc, sm, smt, cw, cb, dtb, a_neg, dexp, mng, h0t, sample)
            y_ml, cfin, nfin, mfin = _mlstm(
                mq, mk, mv, mo, sm, smt, gb, lng,
                c0.reshape(nb, 2, 2, LANES, ML_DV), n0.reshape(nb, 2, 2, LANES, 1),
                m0.reshape(nb, 1, 8), sample)
            ys.append((y_da, y_mb, y_ml))
            if not sample:
                outs["ssm"].append(jnp.transpose(
                    hfin.reshape(nb, 2, MB_STATE, MB_HEADS, MB_HD), (0, 1, 3, 4, 2)))
                outs["C"].append(cfin.reshape(nb, 2, ML_HEADS, ML_DK, ML_DV))
                outs["n"].append(nfin.reshape(nb, 2, ML_HEADS, ML_DK))
                outs["m"].append(mfin.reshape(nb, 2, ML_HEADS))
        x, h2t, affb = _merge(x, mod[l], norm1_g[l][None], ys[0], ys[1],
                              w_mgate[l].astype(BF16), b_mgate[l][None], w_branch[l].astype(BF16),
                              w_out[l].astype(BF16), norm2_g[l][None], router_w[l].T)
        slotb, slott, a_cnt, jlo, jhi = _route(affb)
        ye = _experts(l, jlo.reshape(-1), jhi.reshape(-1), h2t, slotb, affb,
                      ex_w_gate, ex_w_up, ex_w_down)
        if l + 1 < DEPTH:
            x = _combine(a_cnt.reshape(-1), ye, slott, x, mod[l], None)
        else:
            y_prompt, y_sample = _combine(a_cnt.reshape(-1), ye, slott, x, mod[l], fg)

    y_prompt = y_prompt.reshape(BATCH, SEQ, D_MODEL)
    y_sample = y_sample.reshape(DEC_BATCH, DEC_SEQ, D_MODEL)
    return (y_prompt, y_sample, jnp.stack(outs["k"], axis=1), jnp.stack(outs["v"], axis=1),
            jnp.stack(outs["ssm"], axis=1), jnp.stack(outs["C"], axis=1),
            jnp.stack(outs["n"], axis=1), jnp.stack(outs["m"], axis=1))
```

```python
import functools
import math

import jax
import jax.numpy as jnp
from jax import lax
from jax.experimental import pallas as pl
from jax.experimental.pallas import tpu as pltpu

F32 = jnp.float32
BF16 = jnp.bfloat16

D_MODEL = 1024
BATCH = 16
SEQ = 256
DEPTH = 2
DEC_BATCH = 8
DEC_SEQ = 2048
PAST_LEN = 512
GRID_W = 64
EPS = 1e-6
CHUNK = 128
ROPE_BASE = 10000.0
DA_HEADS = 4
DA_HD = 64
DA_WIDTH = 512
MB_INNER = 512
MB_HD = 64
MB_HEADS = 8
MB_GROUPS = 2
MB_STATE = 64
MB_CONV_DIM = 768
ML_HEADS = 4
ML_DK = 64
ML_DV = 128
ML_WIDTH = 512
N_EXPERTS = 16
EC_FACTOR = 2
EXPERT_FF = 1024

T_CTX = BATCH * SEQ
T_SMP = DEC_BATCH * DEC_SEQ
T_ALL = T_CTX + T_SMP
TM = 256
NT_CTX = T_CTX // TM
NT_ALL = T_ALL // TM
NB_CTX = T_CTX // CHUNK
NB_ALL = T_ALL // CHUNK
CAP_CTX = EC_FACTOR * T_CTX // N_EXPERTS
CAP_SMP = EC_FACTOR * T_SMP // N_EXPERTS
CAP_ALL = CAP_CTX + CAP_SMP
RB = 256
N_RB = CAP_ALL // RB
WIN = 64
GATHER_TILES = 5
GATHER_AHEAD = 3
ROUTE_GROUP = 4
TILES_PER_REQ = DEC_SEQ // TM
TM_MERGE = 512
BRANCH_W = 512
LANES = 128
BF16_ROWS = 16
VMEM_LIMIT = 56 * 1024 * 1024
HI = lax.Precision.HIGHEST
LOG2E = math.log2(math.e)

C_Q, C_K, C_V, C_Z, C_XBC, C_MQ, C_MK, C_MV, C_MO, C_END = (
    0, 512, 1024, 1536, 2048, 2816, 3072, 3328, 3840, 4352)


def _mod_row(i):
    return jnp.where(i < NT_CTX, 0, 1 + (i - NT_CTX) // TILES_PER_REQ)


def _rope_blk(i):
    return jnp.where(i < NT_CTX, 0, 1 + (i - NT_CTX) % TILES_PER_REQ)


def _cparams(n_grid):
    return pltpu.CompilerParams(dimension_semantics=("arbitrary",) * n_grid,
                                vmem_limit_bytes=VMEM_LIMIT)


def _silu(x):
    return x * jax.nn.sigmoid(x)


def _softplus(x):
    u = jnp.exp(-jnp.abs(x))
    w = 1.0 + u
    l1p = jnp.where(w == 1.0, u, jnp.log(w) * (u / (w - 1.0)))
    return jnp.maximum(x, 0.0) + l1p


def _dot(a, b):
    return jnp.dot(a, b, preferred_element_type=F32)


def _dot_nt(a, b):
    return lax.dot_general(a, b, (((1,), (1,)), ((), ())), preferred_element_type=F32)


def _dot_hi(a, b):
    return jnp.dot(a, b, precision=HI, preferred_element_type=F32)


def _split3(a):
    hi = a.astype(BF16)
    r = a - hi.astype(F32)
    mid = r.astype(BF16)
    return hi, mid, (r - mid.astype(F32)).astype(BF16)


def _mask3(mask, axis):
    m = jnp.where(mask, 1.0, 0.0).astype(BF16)
    return jnp.concatenate([m, m, m], axis=axis)


def _mask_dot(m3, a):
    return _dot(m3, jnp.concatenate(_split3(a), axis=0))


def _dot_mask(a, m3):
    return _dot(jnp.concatenate(_split3(a), axis=1), m3)


def _dot_mask_narrow(a, m):
    hi, mid, lo = _split3(a)
    return _dot(hi, m) + _dot(mid, m) + _dot(lo, m)


def _rms(x, g):
    return x * lax.rsqrt(jnp.mean(x * x, axis=-1, keepdims=True) + EPS) * g


def _mod_kernel(c_ref, w_ref, b_ref, o_ref):
    s = _silu(c_ref[...])
    o_ref[...] = _dot(s.astype(BF16), w_ref[...].astype(BF16)) + b_ref[...]


def _modulation(cc, ada_w, ada_b):
    tn = 1536
    return pl.pallas_call(
        _mod_kernel,
        grid=(DEPTH, 6 * D_MODEL // tn),
        in_specs=[pl.BlockSpec((16, D_MODEL), lambda l, j: (0, 0)),
                  pl.BlockSpec((None, D_MODEL, tn), lambda l, j: (l, 0, j)),
                  pl.BlockSpec((None, 1, tn), lambda l, j: (l, 0, j))],
        out_specs=pl.BlockSpec((None, 16, tn), lambda l, j: (l, 0, j)),
        out_shape=jax.ShapeDtypeStruct((DEPTH, 16, 6 * D_MODEL), F32),
        compiler_params=_cparams(2),
        name="adaln_mod",
    )(cc, ada_w, ada_b.reshape(DEPTH, 1, 6 * D_MODEL))


def _rope(t, cos, sin, first_half):
    outs = []
    for c in range(DA_WIDTH // LANES):
        xc = t[:, LANES * c:LANES * (c + 1)]
        partner = jnp.where(first_half, pltpu.roll(xc, LANES - 16, 1), pltpu.roll(xc, 16, 1))
        outs.append(xc * cos + partner * sin)
    return jnp.concatenate(outs, axis=1)


def _proj_body(x, mod_ref, g_ref, wm_ref, ws_ref, wst_ref, cos_ref, sin_ref,
               q_ref, k_ref, v_ref, kf_ref, vf_ref, z_ref, xbc_ref,
               mq_ref, mk_ref, mv_ref, mo_ref, sm_ref, smt_ref):
    mod = mod_ref[...]
    sh1 = mod[:, 0:D_MODEL]
    sc1 = mod[:, D_MODEL:2 * D_MODEL]
    h = _rms(x, g_ref[...]) * (1.0 + sc1) + sh1
    hb = h.astype(BF16)

    def proj(a, b):
        return _dot(hb, wm_ref[:, a:b])

    lane = lax.broadcasted_iota(jnp.int32, (TM, LANES), 1)
    first_half = (lane % 32) < 16
    cos = cos_ref[...]
    sin = sin_ref[...]
    q = proj(C_Q, C_K)
    k = proj(C_K, C_V)
    v = proj(C_V, C_Z)
    kf_ref[...] = k
    vf_ref[...] = v
    q_ref[...] = (_rope(q, cos, sin, first_half) * (DA_HD ** -0.5 * LOG2E)).astype(BF16)
    k_ref[...] = _rope(k, cos, sin, first_half).astype(BF16)
    v_ref[...] = v.astype(BF16)
    z_ref[...] = proj(C_Z, C_XBC)
    xbc_ref[...] = proj(C_XBC, C_MQ)
    mq_ref[...] = proj(C_MQ, C_MK).astype(BF16)
    mk_ref[...] = (proj(C_MK, C_MV) * (ML_DK ** -0.5)).astype(BF16)
    mv_ref[...] = proj(C_MV, C_MO).astype(BF16)
    mo_ref[...] = proj(C_MO, C_END)
    sm_ref[...] = _dot(hb, ws_ref[...])
    st = _dot_nt(wst_ref[...], hb)
    for j in range(TM // CHUNK):
        smt_ref[j] = st[:, CHUNK * j:CHUNK * (j + 1)]


def _proj_kernel_first(xp_ref, xs_ref, *refs):
    x = jnp.where(pl.program_id(0) < NT_CTX, xp_ref[...], xs_ref[...])
    refs[-1][...] = x
    _proj_body(x, *refs[:-1])


def _proj_kernel_next(x_ref, *refs):
    _proj_body(x_ref[...], *refs)


def _ctx_tile(w):
    return pl.BlockSpec((TM, w), lambda i: (jnp.minimum(i, NT_CTX - 1), 0))


def _smp_tile(w):
    return pl.BlockSpec((TM, w), lambda i: (jnp.maximum(i - NT_CTX, 0), 0))


def _projection(xs, mod_l, g1, wm, ws, wst, cos_t, sin_t):
    tile = lambda w: pl.BlockSpec((TM, w), lambda i: (i, 0))
    full = lambda a: pl.BlockSpec(a.shape, lambda i: (0,) * a.ndim)
    modspec = pl.BlockSpec((None, 1, 6 * D_MODEL), lambda i: (_mod_row(i), 0, 0))
    first = isinstance(xs, tuple)
    if first:
        in_specs = [_ctx_tile(D_MODEL), _smp_tile(D_MODEL)]
        args = list(xs)
        kern = _proj_kernel_first
    else:
        in_specs = [tile(D_MODEL)]
        args = [xs]
        kern = _proj_kernel_next
    in_specs += [modspec, full(g1), full(wm), full(ws), full(wst),
                 pl.BlockSpec((TM, LANES), lambda i: (_rope_blk(i), 0)),
                 pl.BlockSpec((TM, LANES), lambda i: (_rope_blk(i), 0))]
    args += [mod_l, g1, wm, ws, wst, cos_t, sin_t]
    ctx_only = pl.BlockSpec((TM, DA_WIDTH), lambda i: (jnp.minimum(i, NT_CTX), 0))
    out_specs = [tile(512), tile(512), tile(512), ctx_only, ctx_only, tile(512), tile(768),
                 tile(256), tile(256), tile(512), tile(512), tile(32),
                 pl.BlockSpec((TM // CHUNK, 32, CHUNK), lambda i: (i, 0, 0))]
    sds = jax.ShapeDtypeStruct
    out_shape = [sds((T_ALL, 512), BF16), sds((T_ALL, 512), BF16), sds((T_ALL, 512), BF16),
                 sds((T_CTX + TM, 512), F32), sds((T_CTX + TM, 512), F32),
                 sds((T_ALL, 512), F32), sds((T_ALL, 768), F32),
                 sds((T_ALL, 256), BF16), sds((T_ALL, 256), BF16), sds((T_ALL, 512), BF16),
                 sds((T_ALL, 512), F32), sds((T_ALL, 32), F32),
                 sds((T_ALL // CHUNK, 32, CHUNK), F32)]
    if first:
        out_specs.append(tile(D_MODEL))
        out_shape.append(sds((T_ALL, D_MODEL), F32))
    return pl.pallas_call(
        kern, grid=(NT_ALL,), in_specs=in_specs, out_specs=out_specs, out_shape=out_shape,
        compiler_params=_cparams(1), name="in_proj",
    )(*args)


def _attn_body(lam_init, q_ref, k_ref, v_ref, kc_ref, vc_ref, lp_ref, g_ref, o_ref):
    lp = lp_ref[...]
    s01 = jnp.sum(lp[0:1] * lp[1:2], axis=-1, keepdims=True)
    s23 = jnp.sum(lp[2:3] * lp[3:4], axis=-1, keepdims=True)
    lam = jnp.exp(s01) - jnp.exp(s23) + lam_init
    tq = q_ref.shape[0]
    lane = lax.broadcasted_iota(jnp.int32, (tq, LANES), 1)
    g = g_ref[...]
    for h in range(DA_HEADS):
        cols = slice(LANES * h, LANES * (h + 1))
        qh = q_ref[:, cols]
        kh = k_ref[:, cols]
        vh = jnp.concatenate([v_ref[:, cols], jnp.ones((k_ref.shape[0], LANES), BF16)], axis=1)
        if kc_ref is not None:
            kch = kc_ref[:, cols].astype(BF16)
            vch = jnp.concatenate([vc_ref[:, cols].astype(BF16), jnp.ones((PAST_LEN, LANES), BF16)], axis=1)
        parts = []
        for m in range(2):
            qm = jnp.where((lane < DA_HD) == (m == 0), qh, jnp.zeros_like(qh))
            s = _dot_nt(qm, kh)
            mx = jnp.max(s, axis=-1, keepdims=True)
            if kc_ref is not None:
                sc = _dot_nt(qm, kch)
                mx = jnp.maximum(mx, jnp.max(sc, axis=-1, keepdims=True))
                ec = jnp.exp2(sc - mx)
            e = jnp.exp2(s - mx)
            acc = _dot(e.astype(BF16), vh)
            if kc_ref is not None:
                acc = acc + _dot(ec.astype(BF16), vch)
            parts.append(acc[:, :LANES] / acc[:, LANES:])
        att = parts[0] - lam * parts[1]
        o_ref[:, cols] = (_rms(att, g) * (1.0 - lam_init)).astype(BF16)


def _attn_kernel_ctx(lam_init, q_ref, k_ref, v_ref, lp_ref, g_ref, o_ref):
    _attn_body(lam_init, q_ref, k_ref, v_ref, None, None, lp_ref, g_ref, o_ref)


def _attn_kernel_smp(lam_init, q_ref, k_ref, v_ref, kc_ref, vc_ref, lp_ref, g_ref, o_ref):
    _attn_body(lam_init, q_ref, k_ref, v_ref, kc_ref, vc_ref, lp_ref, g_ref, o_ref)


def _attention(l, q, k, v, cache_k, cache_v, lp, g, sample):
    lam_init = 0.8 - 0.6 * math.exp(-0.3 * l)
    full = lambda a: pl.BlockSpec(a.shape, lambda *_: (0,) * a.ndim)
    sds = jax.ShapeDtypeStruct
    if not sample:
        blk = pl.BlockSpec((SEQ, DA_WIDTH), lambda b: (b, 0))
        return pl.pallas_call(
            functools.partial(_attn_kernel_ctx, lam_init), grid=(BATCH,),
            in_specs=[blk, blk, blk, full(lp), full(g)],
            out_specs=blk, out_shape=sds((T_CTX, DA_WIDTH), BF16),
            compiler_params=_cparams(1), name="diff_attn_ctx",
        )(q, k, v, lp, g)
    tq = 256
    nq = DEC_SEQ // tq
    off_q = T_CTX // tq
    off_k = T_CTX // DEC_SEQ
    qblk = pl.BlockSpec((tq, DA_WIDTH), lambda b, i: (off_q + b * nq + i, 0))
    kblk = pl.BlockSpec((DEC_SEQ, DA_WIDTH), lambda b, i: (off_k + b, 0))
    cblk = pl.BlockSpec((None, None, PAST_LEN, DA_WIDTH), lambda b, i: (b, l, 0, 0))
    return pl.pallas_call(
        functools.partial(_attn_kernel_smp, lam_init), grid=(DEC_BATCH, nq),
        in_specs=[qblk, kblk, kblk, cblk, cblk, full(lp), full(g)],
        out_specs=pl.BlockSpec((tq, DA_WIDTH), lambda b, i: (b * nq + i, 0)),
        out_shape=sds((T_SMP, DA_WIDTH), BF16),
        compiler_params=_cparams(2), name="diff_attn_smp",
    )(q, k, v, cache_k, cache_v, lp, g)


def _tri_masks():
    r = lax.broadcasted_iota(jnp.int32, (CHUNK, CHUNK), 0)
    c = lax.broadcasted_iota(jnp.int32, (CHUNK, CHUNK), 1)
    return c <= r, c >= r


def _ssd_kernel(L, z_ref, xbc_ref, sm_ref, smt_ref, cw_ref, cb_ref, dtbr_ref, dtbc_ref,
                ar_ref, ac_ref, dexp_ref, ng_ref, h0_ref,
                y_ref, hfin_ref, xc_s, yacc_s, ht_s):
    nc = L // CHUNK
    low, upp = _tri_masks()
    tri_l3 = (_mask3(low, 1), _mask3(upp, 1))
    tri_r3 = (_mask3(upp, 0), _mask3(low, 0))
    lane512 = lax.broadcasted_iota(jnp.int32, (1, MB_INNER), 1)
    row16 = lax.broadcasted_iota(jnp.int32, (16, MB_INNER), 0)
    lane16 = lax.broadcasted_iota(jnp.int32, (16, MB_INNER), 1)
    expand = tuple(_mask3(row16 == 8 * d + lane16 // MB_HD, 0) for d in range(2))
    lane128 = lax.broadcasted_iota(jnp.int32, (CHUNK, LANES), 1)
    rowblk = lax.broadcasted_iota(jnp.int32, (LANES, MB_INNER), 0) // MB_STATE
    colblk = lax.broadcasted_iota(jnp.int32, (LANES, MB_INNER), 1) // (MB_INNER // MB_GROUPS)
    same_group = rowblk == colblk
    cw = cw_ref[...]
    cbias = cb_ref[...]

    def conv_chunk(c, _):
        base = pl.multiple_of(c * CHUNK, CHUNK)
        x = xbc_ref[pl.ds(base, CHUNK), :]
        prev = xbc_ref[pl.ds(jnp.maximum(base - 1, 0), 1), :]
        nxt = xbc_ref[pl.ds(jnp.minimum(base + CHUNK, L - 1), 1), :]
        prev = jnp.where(c == 0, 0.0, prev)
        nxt = jnp.where(c == nc - 1, 0.0, nxt)
        row = lax.broadcasted_iota(jnp.int32, (CHUNK, 1), 0)
        xp = jnp.where(row == 0, prev, pltpu.roll(x, 1, 0))
        xn = jnp.where(row == CHUNK - 1, nxt, pltpu.roll(x, CHUNK - 1, 0))
        conv = xp * cw[0:1] + x * cw[1:2] + xn * cw[2:3] + cbias
        xc_s[pl.ds(base, CHUNK), :] = _silu(conv)
        return 0

    lax.fori_loop(0, nc, conv_chunk, 0)

    h0 = h0_ref[...]
    for d in range(2):
        h0d = h0[d]
        ht_s[d] = jnp.concatenate(
            [jnp.where(lane512 < MB_INNER // 2, h0d, 0.0),
             jnp.where(lane512 >= MB_INNER // 2, h0d, 0.0)], axis=0)

    def both(i, _):
        jobs = ((0, i), (1, nc - 1 - i))
        rows, xsb, bb, cb, dt_row, cum_col, cum_row = {}, {}, {}, {}, {}, {}, {}
        y_off, cbg = {}, {}
        for d, c in jobs:
            rows[d] = pl.ds(pl.multiple_of(c * CHUNK, CHUNK), CHUNK)
            xs = xc_s[rows[d], 0:MB_INNER]
            bm = xc_s[rows[d], MB_INNER:MB_INNER + LANES]
            cm = xc_s[rows[d], MB_INNER + LANES:MB_CONV_DIM]
            dt_col = _softplus(sm_ref[rows[d], 0:16] + dtbr_ref[...])
            a_col = dt_col * ar_ref[...]
            dt_row[d] = _softplus(smt_ref[c][0:16, :] + dtbc_ref[...])
            a_row = dt_row[d] * ac_ref[...]
            cum_col[d] = _mask_dot(tri_l3[d], a_col)
            cum_row[d] = _dot_mask(a_row, tri_r3[d])
            last = 0 if d else CHUNK - 1
            cum_last = cum_col[d][last:last + 1, :]
            w_exp = _dot_mask(jnp.exp(cum_last - cum_col[d]) * dt_col, expand[d])
            g_exp = _dot_mask(jnp.exp(cum_col[d]), expand[d])
            cd_exp = _dot_mask(jnp.broadcast_to(jnp.exp(cum_last), (8, 16)), expand[d])[0:1]
            xw = (xs * w_exp).astype(BF16)
            xsb[d] = xs.astype(BF16)
            bb[d] = bm.astype(BF16)
            cb[d] = cm.astype(BF16)
            ht = ht_s[d]
            y_off[d] = _dot(cb[d], ht.astype(BF16)) * g_exp
            s_new = _dot(bm.T.astype(BF16), xw)
            ht_s[d] = ht * cd_exp + jnp.where(same_group, s_new, 0.0)
            for g in range(MB_GROUPS):
                cg = jnp.where((lane128 < MB_STATE) == (g == 0), cb[d], jnp.zeros_like(cb[d]))
                cbg[d, g] = _dot_nt(cg, bb[d])
        m16 = {}
        for d, _c in jobs:
            for h in range(MB_HEADS):
                ci = 8 * d + h
                seg = cum_col[d][:, ci:ci + 1] - cum_row[d][ci:ci + 1, :]
                m = jnp.where(upp if d else low, jnp.exp(seg), 0.0) * cbg[d, h // 4] * dt_row[d][ci:ci + 1, :]
                m16[d, h] = m.astype(BF16)
        yd = {}
        for d, _c in jobs:
            for h in range(MB_HEADS):
                k = h // 2
                yd[d, h] = _dot(m16[d, h], xsb[d][:, LANES * k:LANES * (k + 1)])
        for d, _c in jobs:
            pairs = [jnp.where(lane128 < MB_HD, yd[d, 2 * k], yd[d, 2 * k + 1]) for k in range(MB_HEADS // 2)]
            yacc_s[d, rows[d], :] = jnp.concatenate(pairs, axis=1) + y_off[d]
        return 0

    lax.fori_loop(0, nc, both, 0)

    dexp = dexp_ref[...]
    ng = ng_ref[...]

    def fin(c, _):
        rows = pl.ds(pl.multiple_of(c * CHUNK, CHUNK), CHUNK)
        y = yacc_s[0, rows, :] + yacc_s[1, rows, :] + dexp * xc_s[rows, 0:MB_INNER]
        y = y * _silu(z_ref[rows, :])
        y_ref[rows, :] = _rms(y, ng).astype(BF16)
        return 0

    lax.fori_loop(0, nc, fin, 0)
    for d in range(2):
        ht = ht_s[d]
        hfin_ref[d] = ht[0:MB_STATE, :] + ht[MB_STATE:2 * MB_STATE, :]


def _ssd(z, xbc, sm, smt, cw, cb, dtb, a_neg, dexp, ng, h0t, sample):
    nb, L, off = (DEC_BATCH, DEC_SEQ, T_CTX // DEC_SEQ) if sample else (BATCH, SEQ, 0)
    full = lambda a: pl.BlockSpec(a.shape, lambda b: (0,) * a.ndim)
    seq = lambda w: pl.BlockSpec((L, w), lambda b: (off + b, 0))
    dtb_r, dtb_c = dtb.reshape(1, 16), dtb.reshape(16, 1)
    a_r, a_c = a_neg.reshape(1, 16), a_neg.reshape(16, 1)
    sds = jax.ShapeDtypeStruct
    return pl.pallas_call(
        functools.partial(_ssd_kernel, L), grid=(nb,),
        in_specs=[seq(MB_INNER), seq(MB_CONV_DIM), seq(32),
                  pl.BlockSpec((L // CHUNK, 32, CHUNK), lambda b: (off + b, 0, 0)),
                  full(cw), full(cb), full(dtb_r), full(dtb_c), full(a_r), full(a_c),
                  full(dexp), full(ng),
                  pl.BlockSpec((None, 2, MB_STATE, MB_INNER), lambda b: (b, 0, 0, 0))],
        out_specs=[pl.BlockSpec((L, MB_INNER), lambda b: (b, 0)),
                   pl.BlockSpec((None, 2, MB_STATE, MB_INNER), lambda b: (b, 0, 0, 0))],
        out_shape=[sds((nb * L, MB_INNER), BF16), sds((nb, 2, MB_STATE, MB_INNER), F32)],
        scratch_shapes=[pltpu.VMEM((L, MB_CONV_DIM), F32), pltpu.VMEM((2, L, MB_INNER), F32),
                        pltpu.VMEM((2, LANES, MB_INNER), F32)],
        compiler_params=_cparams(1), name="ssd_smp" if sample else "ssd_ctx",
    )(z, xbc, sm, smt, cw, cb, dtb_r, dtb_c, a_r, a_c, dexp, ng, h0t)


def _mlstm_kernel(L, q_ref, k_ref, v_ref, o_ref, sm_ref, smt_ref, gbr_ref, gbc_ref, ng_ref,
                  c0_ref, n0_ref, m0_ref,
                  y_ref, cf_ref, nf_ref, mf_ref, hacc_s, c_s, n_s):
    nc = L // CHUNK
    low, upp = _tri_masks()
    tri_l = (low.astype(F32), upp.astype(F32))
    tri_r = (upp.astype(F32), low.astype(F32))
    lane128 = lax.broadcasted_iota(jnp.int32, (CHUNK, LANES), 1)
    lane8 = lax.broadcasted_iota(jnp.int32, (1, 2 * ML_HEADS), 1)
    neg_inf = -jnp.inf

    ones16 = jnp.ones((CHUNK, LANES), BF16)
    c_s[...] = c0_ref[...]
    n_s[...] = jnp.broadcast_to(n0_ref[...], n_s.shape)
    m0 = m0_ref[...]

    def both(i, m_in):
        jobs = ((0, i), (1, nc - 1 - i))
        heads = [(d, h) for d, _ in jobs for h in range(ML_HEADS)]
        rows, pre_row, b_col, b_row = {}, {}, {}, {}
        for d, c in jobs:
            rows[d] = pl.ds(pl.multiple_of(c * CHUNK, CHUNK), CHUNK)
            pre_col = sm_ref[rows[d], 16:32] + gbr_ref[...]
            pre_row[d] = smt_ref[c][16:32, :] + gbc_ref[...]
            lf_col = -_softplus(-pre_col)
            lf_row = -_softplus(-pre_row[d])
            b_col[d] = _dot_hi(tri_l[d], lf_col)
            b_row[d] = _dot_hi(lf_row, tri_r[d])
        kt, cst, nst, vh, qk, qc, qn = {}, {}, {}, {}, {}, {}, {}
        for d, _ in jobs:
            for pr in range(2):
                qp = q_ref[rows[d], LANES * pr:LANES * (pr + 1)]
                kp = k_ref[rows[d], LANES * pr:LANES * (pr + 1)]
                kt[d, pr] = kp.astype(F32).T
                cst[d, pr] = c_s[d, pr]
                nst[d, pr] = n_s[d, pr]
                cb16 = cst[d, pr].astype(BF16)
                nb16 = nst[d, pr].astype(BF16)
                for hh in range(2):
                    h = 2 * pr + hh
                    qm = jnp.where((lane128 < ML_DK) == (hh == 0), qp, jnp.zeros_like(qp))
                    vh[d, h] = jnp.concatenate([v_ref[rows[d], ML_DV * h:ML_DV * (h + 1)], ones16], axis=1)
                    qk[d, h] = _dot_nt(qm, kp)
                    qc[d, h] = _dot(qm, cb16)
                    qn[d, h] = _dot(qm, nb16)
        m_t, s_intra, s_inter, m_new, s_old, kts = {}, {}, {}, {}, {}, {}
        for d, h in heads:
            m_st = m_in[:, 4 * d + h:4 * d + h + 1]
            bcol = jnp.broadcast_to(b_col[d][:, 8 * d + 4 + h:8 * d + 5 + h], (CHUNK, CHUNK))
            brow = b_row[d][8 * d + 4 + h:8 * d + 5 + h, :]
            li_row = pre_row[d][8 * d + h:8 * d + h + 1, :]
            dm = jnp.where(upp if d else low, bcol - brow + li_row, neg_inf)
            inter = bcol + m_st
            m_t[d, h] = jnp.maximum(inter, jnp.max(dm, axis=-1, keepdims=True))
            s_intra[d, h] = jnp.exp(dm - m_t[d, h]) * qk[d, h]
            s_inter[d, h] = jnp.exp(inter - m_t[d, h])
            last = 0 if d else CHUNK - 1
            b_end = brow[:, last:last + 1]
            w_end = b_end - brow + li_row
            m_new[d, h] = jnp.maximum(b_end + m_st, jnp.max(w_end, axis=-1, keepdims=True))
            s_old[d, h] = jnp.exp(b_end + m_st - m_new[d, h])
            half = slice(ML_DK * (h % 2), ML_DK * (h % 2 + 1))
            kts[d, h] = kt[d, h // 2][half, :] * jnp.exp(w_end - m_new[d, h])
        pv, kv = {}, {}
        for d, h in heads:
            pv[d, h] = _dot(s_intra[d, h].astype(BF16), vh[d, h])
            kv[d, h] = _dot(kts[d, h].astype(BF16), vh[d, h])
        m_out = m_in
        for d, h in heads:
            num = s_inter[d, h] * qc[d, h] + pv[d, h][:, :ML_DV]
            den = s_inter[d, h] * qn[d, h] + pv[d, h][:, ML_DV:]
            hout = num / jnp.maximum(jnp.abs(den), jnp.exp(-m_t[d, h]))
            hacc_s[d, rows[d], ML_DV * h:ML_DV * (h + 1)] = hout
            m_out = jnp.where(lane8 == 4 * d + h, m_new[d, h], m_out)
        for d, _ in jobs:
            for pr in range(2):
                c_new, n_new = [], []
                for hh in range(2):
                    h = 2 * pr + hh
                    half = slice(ML_DK * hh, ML_DK * (hh + 1))
                    c_new.append(s_old[d, h] * cst[d, pr][half, :] + kv[d, h][:, :ML_DV])
                    n_new.append(s_old[d, h] * nst[d, pr][half, :] + kv[d, h][:, ML_DV:])
                c_s[d, pr] = jnp.concatenate(c_new, axis=0)
                n_s[d, pr] = jnp.concatenate(n_new, axis=0)
        return m_out

    m_fin = lax.fori_loop(0, nc, both, m0)

    ng = ng_ref[...]

    def fin(c, _):
        rows = pl.ds(pl.multiple_of(c * CHUNK, CHUNK), CHUNK)
        for h in range(ML_HEADS):
            cols = slice(ML_DV * h, ML_DV * (h + 1))
            y = _rms(hacc_s[0, rows, cols] + hacc_s[1, rows, cols], ng[:, cols]) * jax.nn.sigmoid(o_ref[rows, cols])
            y_ref[rows, cols] = y.astype(BF16)
        return 0

    lax.fori_loop(0, nc, fin, 0)
    cf_ref[...] = c_s[...]
    nf_ref[...] = n_s[:, :, :, 0:1]
    mf_ref[...] = m_fin


def _mlstm(q, k, v, o, sm, smt, gb, ng, c0, n0, m0, sample):
    nb, L, off = (DEC_BATCH, DEC_SEQ, T_CTX // DEC_SEQ) if sample else (BATCH, SEQ, 0)
    full = lambda a: pl.BlockSpec(a.shape, lambda b: (0,) * a.ndim)
    seq = lambda w: pl.BlockSpec((L, w), lambda b: (off + b, 0))
    gb_r, gb_c = gb.reshape(1, 16), gb.reshape(16, 1)
    st_c = pl.BlockSpec((None, 2, 2, LANES, ML_DV), lambda b: (b, 0, 0, 0, 0))
    st_n = pl.BlockSpec((None, 2, 2, LANES, 1), lambda b: (b, 0, 0, 0, 0))
    st_m = pl.BlockSpec((None, 1, 8), lambda b: (b, 0, 0))
    sds = jax.ShapeDtypeStruct
    return pl.pallas_call(
        functools.partial(_mlstm_kernel, L), grid=(nb,),
        in_specs=[seq(256), seq(256), seq(512), seq(512), seq(32),
                  pl.BlockSpec((L // CHUNK, 32, CHUNK), lambda b: (off + b, 0, 0)),
                  full(gb_r), full(gb_c), full(ng), st_c, st_n, st_m],
        out_specs=[pl.BlockSpec((L, ML_WIDTH), lambda b: (b, 0)), st_c, st_n, st_m],
        out_shape=[sds((nb * L, ML_WIDTH), BF16), sds((nb, 2, 2, LANES, ML_DV), F32),
                   sds((nb, 2, 2, LANES, 1), F32), sds((nb, 1, 8), F32)],
        scratch_shapes=[pltpu.VMEM((2, L, ML_WIDTH), F32), pltpu.VMEM((2, 2, LANES, ML_DV), F32),
                        pltpu.VMEM((2, 2, LANES, LANES), F32)],
        compiler_params=_cparams(1), name="mlstm_smp" if sample else "mlstm_ctx",
    )(q, k, v, o, sm, smt, gb_r, gb_c, ng, c0, n0, m0)


def _merge_kernel(x_ref, mod_ref, g1_ref, ydac_ref, ymbc_ref, ymlc_ref, ydas_ref, ymbs_ref, ymls_ref,
                  wg_ref, bg_ref, wb_ref, wo_ref, g2_ref, rwt_ref, xo_ref, h2t_ref, affb_ref):
    x = x_ref[...]
    mod = mod_ref[...]
    sh1, sc1, gt1, sh2, sc2 = (mod[:, j * D_MODEL:(j + 1) * D_MODEL] for j in range(5))
    hb = (_rms(x, g1_ref[...]) * (1.0 + sc1) + sh1).astype(BF16)
    is_ctx = pl.program_id(0) < T_CTX // TM_MERGE
    merged = None
    for n, (yc_ref, ys_ref) in enumerate(((ydac_ref, ydas_ref), (ymbc_ref, ymbs_ref), (ymlc_ref, ymls_ref))):
        cols = slice(n * D_MODEL, (n + 1) * D_MODEL)
        gate = jax.nn.sigmoid(_dot(hb, wg_ref[:, cols]) + bg_ref[:, cols])
        y = jnp.where(is_ctx, yc_ref[...], ys_ref[...])
        term = gate * _dot(y, wb_ref[n])
        merged = term if merged is None else merged + term
    out = _dot(merged.astype(BF16), wo_ref[...])
    xn = x + gt1 * out
    xo_ref[...] = xn
    h2 = _rms(xn, g2_ref[...]) * (1.0 + sc2) + sh2
    h2t_ref[...] = h2.T.astype(BF16)
    logits = lax.dot_general(rwt_ref[...], h2, (((1,), (1,)), ((), ())),
                             precision=HI, preferred_element_type=F32)
    e = jnp.exp(logits - jnp.max(logits, axis=0, keepdims=True))
    aff = e / jnp.sum(e, axis=0, keepdims=True)
    for j in range(TM_MERGE // CHUNK):
        affb_ref[j] = aff[:, CHUNK * j:CHUNK * (j + 1)]


def _merge(x, mod_l, g1, ys_ctx, ys_smp, wg, bg, wb, wo, g2, rwt):
    tm = TM_MERGE
    n_ctx = T_CTX // tm
    per_req = DEC_SEQ // tm
    tile = lambda w: pl.BlockSpec((tm, w), lambda i: (i, 0))
    ctx_tile = pl.BlockSpec((tm, BRANCH_W), lambda i: (jnp.minimum(i, n_ctx - 1), 0))
    smp_tile = pl.BlockSpec((tm, BRANCH_W), lambda i: (jnp.maximum(i - n_ctx, 0), 0))
    full = lambda a: pl.BlockSpec(a.shape, lambda i: (0,) * a.ndim)
    mod_row = lambda i: jnp.where(i < n_ctx, 0, 1 + (i - n_ctx) // per_req)
    sds = jax.ShapeDtypeStruct
    return pl.pallas_call(
        _merge_kernel, grid=(T_ALL // tm,),
        in_specs=[tile(D_MODEL), pl.BlockSpec((None, 1, 6 * D_MODEL), lambda i: (mod_row(i), 0, 0)),
                  full(g1), ctx_tile, ctx_tile, ctx_tile, smp_tile, smp_tile, smp_tile,
                  full(wg), full(bg), full(wb), full(wo), full(g2), full(rwt)],
        out_specs=[tile(D_MODEL), pl.BlockSpec((D_MODEL, tm), lambda i: (0, i)),
                   pl.BlockSpec((tm // CHUNK, N_EXPERTS, CHUNK), lambda i: (i, 0, 0))],
        out_shape=[sds((T_ALL, D_MODEL), F32), sds((D_MODEL, T_ALL), BF16),
                   sds((NB_ALL, N_EXPERTS, CHUNK), F32)],
        compiler_params=_cparams(1), name="merge_out",
    )(x, mod_l, g1, *ys_ctx, *ys_smp, wg, bg, wb, wo, g2, rwt)


def _route_kernel(affb_ref, slotb_ref, slott_ref, a_ref, jlo_ref, jhi_ref, acc_s, run_s):
    r = lax.broadcasted_iota(jnp.int32, (CHUNK, CHUNK), 0)
    c = lax.broadcasted_iota(jnp.int32, (CHUNK, CHUNK), 1)
    upper = jnp.where(r <= c, 1.0, 0.0).astype(BF16)
    eye = jnp.where(r == c, 1.0, 0.0).astype(BF16)
    eye2 = jnp.concatenate([eye, eye], axis=1)
    lane = lax.broadcasted_iota(jnp.int32, (N_EXPERTS, LANES), 1)
    acc_s[...] = jnp.zeros_like(acc_s)
    run_s[...] = jnp.zeros_like(run_s)
    for b0, b1, cap in ((0, NB_CTX, CAP_CTX), (NB_CTX, NB_ALL, CAP_SMP)):
        aff = affb_ref[b0:b1]

        def search(i, thr_bits):
            cand = thr_bits | lax.shift_left(jnp.int32(1), 30 - i)
            cnt = jnp.sum((aff >= pltpu.bitcast(cand, F32)[None]).astype(jnp.int32), axis=0)
            cnt = jnp.sum(cnt, axis=1, keepdims=True)
            return jnp.where(cnt >= cap, cand, thr_bits)

        thr = pltpu.bitcast(lax.fori_loop(0, 31, search, jnp.zeros((N_EXPERTS, 1), jnp.int32)), F32)
        n_gt = jnp.sum(jnp.sum((aff > thr[None]).astype(jnp.int32), axis=0), axis=1, keepdims=True)
        need = (cap - n_gt).astype(F32)

        run_s[1] = jnp.zeros((N_EXPERTS, 1), F32)

        def blocks(gi, _):
            b = pl.multiple_of(b0 + gi * ROUTE_GROUP, ROUTE_GROUP)
            run_sel = run_s[0]
            run_eq = run_s[1]
            xs = [affb_ref[b + i] for i in range(ROUTE_GROUP)]
            eqs = [x == thr for x in xs]
            eq_fs = [jnp.where(eq, 1.0, 0.0) for eq in eqs]
            eq_incls = [_dot(f.astype(BF16), upper) for f in eq_fs]
            sels = []
            for x, eq, f, incl in zip(xs, eqs, eq_fs, eq_incls):
                sels.append((x > thr) | (eq & (run_eq + incl - f < need)))
                run_eq = run_eq + incl[:, CHUNK - 1:CHUNK]
            sel_fs = [jnp.where(sel, 1.0, 0.0) for sel in sels]
            sel_incls = [_dot(f.astype(BF16), upper) for f in sel_fs]
            acc = acc_s[...]
            slots = []
            for i, (sel, f, incl) in enumerate(zip(sels, sel_fs, sel_incls)):
                slots.append(jnp.where(sel, run_sel + incl - f, -1.0))
                if i % 2 == 0:
                    acc = jnp.where(lane == b // 2 + i // 2, run_sel, acc)
                run_sel = run_sel + incl[:, CHUNK - 1:CHUNK]
            acc_s[...] = acc
            run_s[0] = run_sel
            run_s[1] = run_eq
            for i, slot in enumerate(slots):
                slotb_ref[b + i] = slot.astype(jnp.int32)
                hi64 = jnp.floor(slot * (1.0 / 64.0)) * 64.0
                parts = jnp.concatenate([hi64, slot - hi64], axis=1).astype(BF16)
                slott_ref[pl.ds(pl.multiple_of((b + i) * CHUNK, CHUNK), CHUNK), :] = _dot_nt(eye2, parts)
            return 0

        lax.fori_loop(0, (b1 - b0) // ROUTE_GROUP, blocks, 0)
    a_acc = jnp.where(lane == NT_ALL, run_s[0], acc_s[...])
    a_ref[...] = a_acc.astype(jnp.int32)
    a_next = pltpu.roll(a_acc, LANES - 1, 1)
    tile_ok = lane < NT_ALL
    jlo = jnp.zeros((N_EXPERTS, LANES), jnp.int32)
    jhi = jnp.zeros((N_EXPERTS, LANES), jnp.int32)
    for k in range(N_RB):
        lo_k = jnp.sum((tile_ok & (a_next <= float(RB * k))).astype(jnp.int32), axis=1, keepdims=True)
        hi_k = jnp.sum((tile_ok & (a_acc < float(RB * (k + 1)))).astype(jnp.int32), axis=1, keepdims=True) - 1
        jlo = jnp.where(lane == k, lo_k, jlo)
        jhi = jnp.where(lane == k, hi_k, jhi)
    jlo_ref[...] = jlo
    jhi_ref[...] = jhi


def _route(affb):
    sds = jax.ShapeDtypeStruct
    small = sds((N_EXPERTS, LANES), jnp.int32)
    return pl.pallas_call(
        _route_kernel,
        out_shape=[sds((NB_ALL, N_EXPERTS, CHUNK), jnp.int32), sds((T_ALL, N_EXPERTS), F32),
                   small, small, small],
        scratch_shapes=[pltpu.VMEM((N_EXPERTS, LANES), F32), pltpu.VMEM((2, N_EXPERTS, 1), F32)],
        compiler_params=pltpu.CompilerParams(vmem_limit_bytes=VMEM_LIMIT), name="route",
    )(affb)


def _expert_kernel(jlo_ref, jhi_ref, h2t_hbm, slotb_ref, affb_ref, wg_ref, wu_ref, wd_ref, o_ref,
                   wg_s, wu_s, wd_s, chunk_s, sem, acc_s, g_s, cnt_s):
    e = pl.program_id(0)
    k = pl.program_id(1)
    step = e * N_RB + k

    @pl.when(k == 0)
    def _():
        wg_s[...] = wg_ref[...].astype(BF16)
        wu_s[...] = wu_ref[...].astype(BF16)
        wd_s[...] = wd_ref[...].astype(BF16)

    def tile_range(s):
        i = (s // N_RB) * LANES + s % N_RB
        lo = jnp.clip(jlo_ref[i], 0, NT_ALL - 1)
        return lo, jnp.clip(jhi_ref[i] - lo + 1, 1, NT_ALL - lo)

    def chunk_tile(lo, c):
        return jnp.minimum(lo + GATHER_TILES * c, NT_ALL - GATHER_TILES)

    def chunk_copy(j0, buf):
        return pltpu.make_async_copy(
            h2t_hbm.at[:, pl.ds(pl.multiple_of(j0 * TM, TM), GATHER_TILES * TM)],
            chunk_s.at[buf], sem.at[buf])

    def n_chunks(s):
        return (tile_range(s)[1] + GATHER_TILES - 1) // GATHER_TILES

    n_steps = N_EXPERTS * N_RB

    def advance(s, c):
        wrap = (c + 1 >= n_chunks(jnp.minimum(s, n_steps - 1))) | (s >= n_steps)
        return jnp.where(wrap, s + 1, s), jnp.where(wrap, 0, c + 1)

    def start_at(s, c, buf):
        @pl.when(s < n_steps)
        def _():
            chunk_copy(chunk_tile(tile_range(jnp.minimum(s, n_steps - 1))[0], c), buf).start()

    jlo, n = tile_range(step)
    nch = n_chunks(step)

    @pl.when(step == 0)
    def _():
        cnt_s[0] = 0
        pos = (step, 0)
        for i in range(GATHER_AHEAD):
            start_at(pos[0], pos[1], i)
            pos = advance(*pos)

    done = cnt_s[0]
    acc_s[...] = jnp.zeros_like(acc_s)
    g_s[...] = jnp.zeros_like(g_s)
    ntok = GATHER_TILES * TM
    want = lax.broadcasted_iota(jnp.int32, (RB, ntok), 0) + k * RB
    lane_tile = lax.broadcasted_iota(jnp.int32, (1, ntok), 1) // TM

    def body(c, _):
        buf = (done + c) % (GATHER_AHEAD + 1)
        j0 = chunk_tile(jlo, c)
        chunk_copy(j0, buf).wait()
        pos = (step, c)
        for _i in range(GATHER_AHEAD):
            pos = advance(*pos)
        start_at(pos[0], pos[1], (done + c + GATHER_AHEAD) % (GATHER_AHEAD + 1))

        nblk = ntok // CHUNK
        srow = jnp.concatenate([slotb_ref[2 * j0 + i, pl.ds(e, 1), :] for i in range(nblk)], axis=1)
        arow = jnp.concatenate([affb_ref[2 * j0 + i, pl.ds(e, 1), :] for i in range(nblk)], axis=1)
        fresh = j0 + lane_tile >= jlo + GATHER_TILES * c
        hit = (srow == want) & fresh
        onehot = jnp.where(hit, 1.0, 0.0).astype(BF16)
        acc_s[...] += _dot_nt(chunk_s[buf], onehot)
        g_s[...] += jnp.sum(jnp.where(hit, arow, 0.0), axis=1, keepdims=True)
        return 0

    lax.fori_loop(0, nch, body, 0)
    cnt_s[0] = done + nch
    xe = acc_s[...].T.astype(BF16)
    hid = _silu(_dot(xe, wg_s[...])) * _dot(xe, wu_s[...])
    o_ref[...] = (_dot(hid.astype(BF16), wd_s[...]) * g_s[...]).astype(BF16)


def _experts(l, jlo, jhi, h2t, slotb, affb, w_gate, w_up, w_down):
    wsp = pl.BlockSpec((None, None, D_MODEL, EXPERT_FF), lambda e, j, *_: (l, e, 0, 0))
    wsd = pl.BlockSpec((None, None, EXPERT_FF, D_MODEL), lambda e, j, *_: (l, e, 0, 0))
    whole = lambda a: pl.BlockSpec(a.shape, lambda e, j, *_: (0,) * a.ndim)
    grid_spec = pltpu.PrefetchScalarGridSpec(
        num_scalar_prefetch=2, grid=(N_EXPERTS, N_RB),
        in_specs=[pl.BlockSpec(memory_space=pl.ANY), whole(slotb), whole(affb), wsp, wsp, wsd],
        out_specs=pl.BlockSpec((None, RB, D_MODEL), lambda e, j, *_: (e, j, 0)),
        scratch_shapes=[pltpu.VMEM((D_MODEL, EXPERT_FF), BF16), pltpu.VMEM((D_MODEL, EXPERT_FF), BF16),
                        pltpu.VMEM((EXPERT_FF, D_MODEL), BF16),
                        pltpu.VMEM((GATHER_AHEAD + 1, D_MODEL, GATHER_TILES * TM), BF16),
                        pltpu.SemaphoreType.DMA((GATHER_AHEAD + 1,)), pltpu.VMEM((D_MODEL, RB), F32),
                        pltpu.VMEM((RB, 1), F32), pltpu.SMEM((1,), jnp.int32)])
    return pl.pallas_call(
        _expert_kernel, grid_spec=grid_spec,
        out_shape=jax.ShapeDtypeStruct((N_EXPERTS, CAP_ALL, D_MODEL), BF16),
        compiler_params=_cparams(2), name="expert_ffn",
    )(jlo, jhi, h2t, slotb, affb, w_gate, w_up, w_down)


def _combine_body(a_ref, ye_hbm, slott_ref, win_s, sem, xwin_s, xsem, o_ref):
    j = pl.program_id(0)
    nt = pl.num_programs(0)

    def first_row(e, jj):
        a = a_ref[e * LANES + jj]
        return pl.multiple_of(jnp.clip((a // BF16_ROWS) * BF16_ROWS, 0, CAP_ALL - WIN), BF16_ROWS)

    def win_copy(e, row0, buf):
        return pltpu.make_async_copy(ye_hbm.at[e, pl.ds(row0, WIN), :], win_s.at[buf, e], sem.at[buf, e])

    buf = j % 2

    @pl.when(j == 0)
    def _():
        for e in range(N_EXPERTS):
            win_copy(e, first_row(e, 0), 0).start()

    @pl.when(j + 1 < nt)
    def _():
        for e in range(N_EXPERTS):
            win_copy(e, first_row(e, j + 1), 1 - buf).start()

    lane = lax.broadcasted_iota(jnp.int32, (TM, LANES), 1)
    lane_f = lane.astype(F32)
    rows0 = []
    pieces = []
    for e in range(0, N_EXPERTS, LANES // WIN):
        tgt = None
        for i in range(LANES // WIN):
            row0 = first_row(e + i, j)
            win_copy(e + i, row0, buf).wait()
            rows0.append(row0)
            t_i = slott_ref[:, e + i:e + i + 1] - (row0 - WIN * i).astype(F32)
            in_win = (lane >= WIN * i) & (lane < WIN * (i + 1))
            tgt = jnp.where(in_win, t_i, -1.0) if tgt is None else jnp.where(in_win, t_i, tgt)
        pieces.append(jnp.where(tgt == lane_f, 1.0, 0.0).astype(BF16))
    onehot = jnp.concatenate(pieces, axis=1)
    o_ref[...] = _dot(onehot, win_s[buf].reshape(N_EXPERTS * WIN, D_MODEL))

    lane_w = lax.broadcasted_iota(jnp.int32, (TM, WIN), 1).astype(F32)
    for e in range(N_EXPERTS):
        row0 = rows0[e]
        n_more = jnp.maximum((a_ref[e * LANES + j + 1] - row0 + WIN - 1) // WIN - 1, 0)

        def more(i, _):
            lo_slot = row0 + (i + 1) * WIN
            r = pl.multiple_of(jnp.minimum(lo_slot, CAP_ALL - WIN), BF16_ROWS)
            cp = pltpu.make_async_copy(ye_hbm.at[e, pl.ds(r, WIN), :], xwin_s, xsem)
            cp.start()
            cp.wait()
            scol = slott_ref[:, e:e + 1]
            scol = jnp.where(scol >= lo_slot.astype(F32), scol, -1.0)
            oh = jnp.where(scol - r.astype(F32) == lane_w, 1.0, 0.0).astype(BF16)
            o_ref[...] += _dot(oh, xwin_s[...])
            return 0

        lax.fori_loop(0, n_more, more, 0)


def _combine_kernel_mid(a_ref, ye_hbm, slott_ref, x_ref, mod_ref, xo_ref, moe_s, win_s, sem, xwin_s, xsem):
    _combine_body(a_ref, ye_hbm, slott_ref, win_s, sem, xwin_s, xsem, moe_s)
    xo_ref[...] = x_ref[...] + mod_ref[...][:, 5 * D_MODEL:6 * D_MODEL] * moe_s[...]


def _combine_kernel_last(a_ref, ye_hbm, slott_ref, x_ref, mod_ref, fg_ref, yp_ref, ys_ref,
                         moe_s, win_s, sem, xwin_s, xsem):
    _combine_body(a_ref, ye_hbm, slott_ref, win_s, sem, xwin_s, xsem, moe_s)
    y = _rms(x_ref[...] + mod_ref[...][:, 5 * D_MODEL:6 * D_MODEL] * moe_s[...], fg_ref[...])

    @pl.when(pl.program_id(0) < NT_CTX)
    def _():
        yp_ref[...] = y

    @pl.when(pl.program_id(0) >= NT_CTX)
    def _():
        ys_ref[...] = y


def _combine(a, ye, slott, x, mod_l, fg):
    tile = lambda w: pl.BlockSpec((TM, w), lambda i, *_: (i, 0))
    in_specs = [pl.BlockSpec(memory_space=pl.ANY), tile(N_EXPERTS), tile(D_MODEL),
                pl.BlockSpec((None, 1, 6 * D_MODEL), lambda i, *_: (_mod_row(i), 0, 0))]
    args = [a, ye, slott, x, mod_l]
    sds = jax.ShapeDtypeStruct
    if fg is None:
        kern, out_specs, out_shape = _combine_kernel_mid, tile(D_MODEL), sds((T_ALL, D_MODEL), F32)
    else:
        kern = _combine_kernel_last
        in_specs.append(pl.BlockSpec(fg.shape, lambda i, *_: (0, 0)))
        args.append(fg)
        out_specs = [pl.BlockSpec((TM, D_MODEL), lambda i, *_: (jnp.minimum(i, NT_CTX - 1), 0)),
                     pl.BlockSpec((TM, D_MODEL), lambda i, *_: (jnp.maximum(i - NT_CTX, 0), 0))]
        out_shape = [sds((T_CTX, D_MODEL), F32), sds((T_SMP, D_MODEL), F32)]
    grid_spec = pltpu.PrefetchScalarGridSpec(
        num_scalar_prefetch=1, grid=(NT_ALL,), in_specs=in_specs, out_specs=out_specs,
        scratch_shapes=[pltpu.VMEM((TM, D_MODEL), F32),
                        pltpu.VMEM((2, N_EXPERTS, WIN, D_MODEL), BF16),
                        pltpu.SemaphoreType.DMA((2, N_EXPERTS)),
                        pltpu.VMEM((WIN, D_MODEL), BF16), pltpu.SemaphoreType.DMA(())])
    return pl.pallas_call(
        kern, grid_spec=grid_spec, out_shape=out_shape,
        compiler_params=_cparams(1), name="moe_combine",
    )(*args)


def _rope_tables():
    t = jnp.arange(DEC_SEQ)
    pos = jnp.stack([t // GRID_W, t % GRID_W], axis=-1).astype(F32)
    nf = DA_HD // 4
    inv = ROPE_BASE ** (-jnp.arange(nf, dtype=F32) / nf)
    ang = pos[:, :, None] * inv
    cos = jnp.cos(ang)
    sin = jnp.sin(ang)
    cos64 = jnp.stack([cos, cos], axis=2).reshape(DEC_SEQ, DA_HD)
    sin64 = jnp.stack([-sin, sin], axis=2).reshape(DEC_SEQ, DA_HD)
    cos_t = jnp.concatenate([jnp.ones((TM, LANES), F32), jnp.tile(cos64, (1, 2))], axis=0)
    sin_t = jnp.concatenate([jnp.zeros((TM, LANES), F32), jnp.tile(sin64, (1, 2))], axis=0)
    return cos_t, sin_t


def kernel(x_prompt, x_sample, cache_k, cache_v, state_ssm, state_mlstm_c, state_mlstm_n, state_mlstm_m, c, c_ctx, ada_w, ada_b, norm1_g, norm2_g, w_in, da_lambda, da_subln_g, mb_conv_w, mb_conv_b, mb_dt_bias, mb_a_log, mb_d, mb_norm_g, ml_gate_b, ml_norm_g, w_branch, w_mgate, b_mgate, w_out, router_w, ex_w_gate, ex_w_up, ex_w_down, final_g):
    x = (x_prompt.reshape(T_CTX, D_MODEL), x_sample.reshape(T_SMP, D_MODEL))
    cc = jnp.concatenate([c_ctx[None, :], c, jnp.zeros((16 - 1 - DEC_BATCH, D_MODEL), F32)], axis=0)
    mod = _modulation(cc, ada_w, ada_b).reshape(DEPTH, 16, 1, 6 * D_MODEL)
    cos_t, sin_t = _rope_tables()
    cache_k2 = cache_k.reshape(DEC_BATCH, DEPTH, PAST_LEN, DA_WIDTH)
    cache_v2 = cache_v.reshape(DEC_BATCH, DEPTH, PAST_LEN, DA_WIDTH)

    outs = {n: [] for n in ("k", "v", "ssm", "C", "n", "m")}
    fg = final_g[None]
    for l in range(DEPTH):
        w = w_in[l]
        wm = jnp.concatenate([w[:, :2816], w[:, 2832:4368]], axis=1).astype(BF16)
        ws = jnp.concatenate([w[:, 2816:2832], w[:, 4368:4384]], axis=1).astype(BF16)
        res = _projection(x, mod[l], norm1_g[l][None], wm, ws, ws.T, cos_t, sin_t)
        q, k, v, kf, vf, z, xbc, mq, mk, mv, mo, sm, smt = res[:13]
        if l == 0:
            x = res[13]
        outs["k"].append(kf[:T_CTX].reshape(BATCH, SEQ, DA_HEADS, 2, DA_HD))
        outs["v"].append(vf[:T_CTX].reshape(BATCH, SEQ, DA_HEADS, 2 * DA_HD))

        lp = da_lambda[l]
        sg = da_subln_g[l][None]
        cw = mb_conv_w[l]
        cb = mb_conv_b[l][None]
        dtb = mb_dt_bias[l].reshape(16)
        a_neg = -jnp.exp(mb_a_log[l]).reshape(16)
        dexp = jnp.repeat(mb_d[l], MB_HD)[None]
        mng = mb_norm_g[l][None]
        gb = ml_gate_b[l].reshape(16)
        lng = ml_norm_g[l].reshape(1, ML_WIDTH)
        ys = []
        for sample in (False, True):
            nb = DEC_BATCH if sample else BATCH
            y_da = _attention(l, q, k, v, cache_k2, cache_v2, lp, sg, sample)
            if sample:
                h0 = state_ssm[:, l]
                c0 = state_mlstm_c[:, l]
                n0 = state_mlstm_n[:, l]
                m0 = state_mlstm_m[:, l]
            else:
                h0 = jnp.zeros((nb, 2, MB_HEADS, MB_HD, MB_STATE), F32)
                c0 = jnp.zeros((nb, 2, ML_HEADS, ML_DK, ML_DV), F32)
                n0 = jnp.zeros((nb, 2, ML_HEADS, ML_DK), F32)
                m0 = jnp.zeros((nb, 2, ML_HEADS), F32)
            h0t = jnp.transpose(h0, (0, 1, 4, 2, 3)).reshape(nb, 2, MB_STATE, MB_INNER)
            y_mb, hfin = _ssd(z, xbc, sm, smt, cw, cb, dtb, a_neg, dexp, mng, h0t, sample)
            y_ml, cfin, nfin, mfin = _mlstm(
                mq, mk, mv, mo, sm, smt, gb, lng,
                c0.reshape(nb, 2, 2, LANES, ML_DV), n0.reshape(nb, 2, 2, LANES, 1),
                m0.reshape(nb, 1, 8), sample)
            ys.append((y_da, y_mb, y_ml))
            if not sample:
                outs["ssm"].append(jnp.transpose(
                    hfin.reshape(nb, 2, MB_STATE, MB_HEADS, MB_HD), (0, 1, 3, 4, 2)))
                outs["C"].append(cfin.reshape(nb, 2, ML_HEADS, ML_DK, ML_DV))
                outs["n"].append(nfin.reshape(nb, 2, ML_HEADS, ML_DK))
                outs["m"].append(mfin.reshape(nb, 2, ML_HEADS))
        x, h2t, affb = _merge(x, mod[l], norm1_g[l][None], ys[0], ys[1],
                              w_mgate[l].astype(BF16), b_mgate[l][None], w_branch[l].astype(BF16),
                              w_out[l].astype(BF16), norm2_g[l][None], router_w[l].T)
        slotb, slott, a_cnt, jlo, jhi = _route(affb)
        ye = _experts(l, jlo.reshape(-1), jhi.reshape(-1), h2t, slotb, affb,
                      ex_w_gate, ex_w_up, ex_w_down)
        if l + 1 < DEPTH:
            x = _combine(a_cnt.reshape(-1), ye, slott, x, mod[l], None)
        else:
            y_prompt, y_sample = _combine(a_cnt.reshape(-1), ye, slott, x, mod[l], fg)

    y_prompt = y_prompt.reshape(BATCH, SEQ, D_MODEL)
    y_sample = y_sample.reshape(DEC_BATCH, DEC_SEQ, D_MODEL)
    return (y_prompt, y_sample, jnp.stack(outs["k"], axis=1), jnp.stack(outs["v"], axis=1),
            jnp.stack(outs["ssm"], axis=1), jnp.stack(outs["C"], axis=1),
            jnp.stack(outs["n"], axis=1), jnp.stack(outs["m"], axis=1))
```

```python
import functools
import math

import jax
import jax.numpy as jnp
from jax import lax
from jax.experimental import pallas as pl
from jax.experimental.pallas import tpu as pltpu

F32 = jnp.float32
BF16 = jnp.bfloat16

D_MODEL = 1024
BATCH = 16
SEQ = 256
DEPTH = 2
DEC_BATCH = 8
DEC_SEQ = 2048
PAST_LEN = 512
GRID_W = 64
EPS = 1e-6
CHUNK = 128
ROPE_BASE = 10000.0
DA_HEADS = 4
DA_HD = 64
DA_WIDTH = 512
MB_INNER = 512
MB_HD = 64
MB_HEADS = 8
MB_GROUPS = 2
MB_STATE = 64
MB_CONV_DIM = 768
ML_HEADS = 4
ML_DK = 64
ML_DV = 128
ML_WIDTH = 512
N_EXPERTS = 16
EC_FACTOR = 2
EXPERT_FF = 1024

T_CTX = BATCH * SEQ
T_SMP = DEC_BATCH * DEC_SEQ
T_ALL = T_CTX + T_SMP
TM = 256
NT_CTX = T_CTX // TM
NT_ALL = T_ALL // TM
NB_CTX = T_CTX // CHUNK
NB_ALL = T_ALL // CHUNK
CAP_CTX = EC_FACTOR * T_CTX // N_EXPERTS
CAP_SMP = EC_FACTOR * T_SMP // N_EXPERTS
CAP_ALL = CAP_CTX + CAP_SMP
RB = 256
N_RB = CAP_ALL // RB
WIN = 64
GATHER_TILES = 5
GATHER_AHEAD = 3
ROUTE_GROUP = 4
TILES_PER_REQ = DEC_SEQ // TM
TM_MERGE = 512
BRANCH_W = 512
LANES = 128
BF16_ROWS = 16
VMEM_LIMIT = 56 * 1024 * 1024
HI = lax.Precision.HIGHEST
LOG2E = math.log2(math.e)

C_Q, C_K, C_V, C_Z, C_XBC, C_MQ, C_MK, C_MV, C_MO, C_END = (
    0, 512, 1024, 1536, 2048, 2816, 3072, 3328, 3840, 4352)


def _mod_row(i):
    return jnp.where(i < NT_CTX, 0, 1 + (i - NT_CTX) // TILES_PER_REQ)


def _rope_blk(i):
    return jnp.where(i < NT_CTX, 0, 1 + (i - NT_CTX) % TILES_PER_REQ)


def _cparams(n_grid):
    return pltpu.CompilerParams(dimension_semantics=("arbitrary",) * n_grid,
                                vmem_limit_bytes=VMEM_LIMIT)


def _silu(x):
    return x * jax.nn.sigmoid(x)


def _softplus(x):
    u = jnp.exp(-jnp.abs(x))
    w = 1.0 + u
    l1p = jnp.where(w == 1.0, u, jnp.log(w) * (u / (w - 1.0)))
    return jnp.maximum(x, 0.0) + l1p


def _dot(a, b):
    return jnp.dot(a, b, preferred_element_type=F32)


def _dot_nt(a, b):
    return lax.dot_general(a, b, (((1,), (1,)), ((), ())), preferred_element_type=F32)


def _dot_hi(a, b):
    return jnp.dot(a, b, precision=HI, preferred_element_type=F32)


def _split3(a):
    hi = a.astype(BF16)
    r = a - hi.astype(F32)
    mid = r.astype(BF16)
    return hi, mid, (r - mid.astype(F32)).astype(BF16)


def _mask3(mask, axis):
    m = jnp.where(mask, 1.0, 0.0).astype(BF16)
    return jnp.concatenate([m, m, m], axis=axis)


def _mask_dot(m3, a):
    return _dot(m3, jnp.concatenate(_split3(a), axis=0))


def _dot_mask(a, m3):
    return _dot(jnp.concatenate(_split3(a), axis=1), m3)


def _dot_mask_narrow(a, m):
    hi, mid, lo = _split3(a)
    return _dot(hi, m) + _dot(mid, m) + _dot(lo, m)


def _rms(x, g):
    return x * lax.rsqrt(jnp.mean(x * x, axis=-1, keepdims=True) + EPS) * g


def _mod_kernel(c_ref, w_ref, b_ref, o_ref):
    s = _silu(c_ref[...])
    o_ref[...] = _dot(s.astype(BF16), w_ref[...].astype(BF16)) + b_ref[...]


def _modulation(cc, ada_w, ada_b):
    tn = 1536
    return pl.pallas_call(
        _mod_kernel,
        grid=(DEPTH, 6 * D_MODEL // tn),
        in_specs=[pl.BlockSpec((16, D_MODEL), lambda l, j: (0, 0)),
                  pl.BlockSpec((None, D_MODEL, tn), lambda l, j: (l, 0, j)),
                  pl.BlockSpec((None, 1, tn), lambda l, j: (l, 0, j))],
        out_specs=pl.BlockSpec((None, 16, tn), lambda l, j: (l, 0, j)),
        out_shape=jax.ShapeDtypeStruct((DEPTH, 16, 6 * D_MODEL), F32),
        compiler_params=_cparams(2),
        name="adaln_mod",
    )(cc, ada_w, ada_b.reshape(DEPTH, 1, 6 * D_MODEL))


def _rope(t, cos, sin, first_half):
    outs = []
    for c in range(DA_WIDTH // LANES):
        xc = t[:, LANES * c:LANES * (c + 1)]
        partner = jnp.where(first_half, pltpu.roll(xc, LANES - 16, 1), pltpu.roll(xc, 16, 1))
        outs.append(xc * cos + partner * sin)
    return jnp.concatenate(outs, axis=1)


def _proj_body(x, mod_ref, g_ref, wm_ref, ws_ref, wst_ref, cos_ref, sin_ref,
               q_ref, k_ref, v_ref, kf_ref, vf_ref, z_ref, xbc_ref,
               mq_ref, mk_ref, mv_ref, mo_ref, sm_ref, smt_ref):
    mod = mod_ref[...]
    sh1 = mod[:, 0:D_MODEL]
    sc1 = mod[:, D_MODEL:2 * D_MODEL]
    h = _rms(x, g_ref[...]) * (1.0 + sc1) + sh1
    hb = h.astype(BF16)

    def proj(a, b):
        return _dot(hb, wm_ref[:, a:b])

    lane = lax.broadcasted_iota(jnp.int32, (TM, LANES), 1)
    first_half = (lane % 32) < 16
    cos = cos_ref[...]
    sin = sin_ref[...]
    q = proj(C_Q, C_K)
    k = proj(C_K, C_V)
    v = proj(C_V, C_Z)
    kf_ref[...] = k
    vf_ref[...] = v
    q_ref[...] = (_rope(q, cos, sin, first_half) * (DA_HD ** -0.5 * LOG2E)).astype(BF16)
    k_ref[...] = _rope(k, cos, sin, first_half).astype(BF16)
    v_ref[...] = v.astype(BF16)
    z_ref[...] = proj(C_Z, C_XBC)
    xbc_ref[...] = proj(C_XBC, C_MQ)
    mq_ref[...] = proj(C_MQ, C_MK).astype(BF16)
    mk_ref[...] = (proj(C_MK, C_MV) * (ML_DK ** -0.5)).astype(BF16)
    mv_ref[...] = proj(C_MV, C_MO).astype(BF16)
    mo_ref[...] = proj(C_MO, C_END)
    sm_ref[...] = _dot(hb, ws_ref[...])
    st = _dot_nt(wst_ref[...], hb)
    for j in range(TM // CHUNK):
        smt_ref[j] = st[:, CHUNK * j:CHUNK * (j + 1)]


def _proj_kernel_first(xp_ref, xs_ref, *refs):
    x = jnp.where(pl.program_id(0) < NT_CTX, xp_ref[...], xs_ref[...])
    refs[-1][...] = x
    _proj_body(x, *refs[:-1])


def _proj_kernel_next(x_ref, *refs):
    _proj_body(x_ref[...], *refs)


def _ctx_tile(w):
    return pl.BlockSpec((TM, w), lambda i: (jnp.minimum(i, NT_CTX - 1), 0))


def _smp_tile(w):
    return pl.BlockSpec((TM, w), lambda i: (jnp.maximum(i - NT_CTX, 0), 0))


def _projection(xs, mod_l, g1, wm, ws, wst, cos_t, sin_t):
    tile = lambda w: pl.BlockSpec((TM, w), lambda i: (i, 0))
    full = lambda a: pl.BlockSpec(a.shape, lambda i: (0,) * a.ndim)
    modspec = pl.BlockSpec((None, 1, 6 * D_MODEL), lambda i: (_mod_row(i), 0, 0))
    first = isinstance(xs, tuple)
    if first:
        in_specs = [_ctx_tile(D_MODEL), _smp_tile(D_MODEL)]
        args = list(xs)
        kern = _proj_kernel_first
    else:
        in_specs = [tile(D_MODEL)]
        args = [xs]
        kern = _proj_kernel_next
    in_specs += [modspec, full(g1), full(wm), full(ws), full(wst),
                 pl.BlockSpec((TM, LANES), lambda i: (_rope_blk(i), 0)),
                 pl.BlockSpec((TM, LANES), lambda i: (_rope_blk(i), 0))]
    args += [mod_l, g1, wm, ws, wst, cos_t, sin_t]
    ctx_only = pl.BlockSpec((TM, DA_WIDTH), lambda i: (jnp.minimum(i, NT_CTX), 0))
    out_specs = [tile(512), tile(512), tile(512), ctx_only, ctx_only, tile(512), tile(768),
                 tile(256), tile(256), tile(512), tile(512), tile(32),
                 pl.BlockSpec((TM // CHUNK, 32, CHUNK), lambda i: (i, 0, 0))]
    sds = jax.ShapeDtypeStruct
    out_shape = [sds((T_ALL, 512), BF16), sds((T_ALL, 512), BF16), sds((T_ALL, 512), BF16),
                 sds((T_CTX + TM, 512), F32), sds((T_CTX + TM, 512), F32),
                 sds((T_ALL, 512), F32), sds((T_ALL, 768), F32),
                 sds((T_ALL, 256), BF16), sds((T_ALL, 256), BF16), sds((T_ALL, 512), BF16),
                 sds((T_ALL, 512), F32), sds((T_ALL, 32), F32),
                 sds((T_ALL // CHUNK, 32, CHUNK), F32)]
    if first:
        out_specs.append(tile(D_MODEL))
        out_shape.append(sds((T_ALL, D_MODEL), F32))
    return pl.pallas_call(
        kern, grid=(NT_ALL,), in_specs=in_specs, out_specs=out_specs, out_shape=out_shape,
        compiler_params=_cparams(1), name="in_proj",
    )(*args)


def _attn_body(lam_init, q_ref, k_ref, v_ref, kc_ref, vc_ref, lp_ref, g_ref, o_ref):
    lp = lp_ref[...]
    s01 = jnp.sum(lp[0:1] * lp[1:2], axis=-1, keepdims=True)
    s23 = jnp.sum(lp[2:3] * lp[3:4], axis=-1, keepdims=True)
    lam = jnp.exp(s01) - jnp.exp(s23) + lam_init
    tq = q_ref.shape[0]
    lane = lax.broadcasted_iota(jnp.int32, (tq, LANES), 1)
    g = g_ref[...]
    def scores(h, m):
        cols = slice(LANES * h, LANES * (h + 1))
        qh = q_ref[:, cols]
        qm = jnp.where((lane < DA_HD) == (m == 0), qh, jnp.zeros_like(qh))
        s = _dot_nt(qm, k_ref[:, cols])
        sc = None if kc_ref is None else _dot_nt(qm, kc_ref[:, cols].astype(BF16))
        return s, sc

    units = [(h, m) for h in range(DA_HEADS) for m in range(2)]
    nxt = scores(*units[0])
    parts = []
    for u, (h, m) in enumerate(units):
        cols = slice(LANES * h, LANES * (h + 1))
        s, sc = nxt
        if u + 1 < len(units):
            nxt = scores(*units[u + 1])
        vh = jnp.concatenate([v_ref[:, cols], jnp.ones((k_ref.shape[0], LANES), BF16)], axis=1)
        mx = jnp.max(s, axis=-1, keepdims=True)
        if sc is not None:
            mx = jnp.maximum(mx, jnp.max(sc, axis=-1, keepdims=True))
        acc = _dot(jnp.exp2(s - mx).astype(BF16), vh)
        if sc is not None:
            vch = jnp.concatenate([vc_ref[:, cols].astype(BF16), jnp.ones((PAST_LEN, LANES), BF16)], axis=1)
            acc = acc + _dot(jnp.exp2(sc - mx).astype(BF16), vch)
        parts.append(acc[:, :LANES] / acc[:, LANES:])
        if m == 1:
            att = parts[-2] - lam * parts[-1]
            o_ref[:, cols] = (_rms(att, g) * (1.0 - lam_init)).astype(BF16)


def _attn_kernel_ctx(lam_init, q_ref, k_ref, v_ref, lp_ref, g_ref, o_ref):
    _attn_body(lam_init, q_ref, k_ref, v_ref, None, None, lp_ref, g_ref, o_ref)


def _attn_kernel_smp(lam_init, q_ref, k_ref, v_ref, kc_ref, vc_ref, lp_ref, g_ref, o_ref):
    _attn_body(lam_init, q_ref, k_ref, v_ref, kc_ref, vc_ref, lp_ref, g_ref, o_ref)


def _attention(l, q, k, v, cache_k, cache_v, lp, g, sample):
    lam_init = 0.8 - 0.6 * math.exp(-0.3 * l)
    full = lambda a: pl.BlockSpec(a.shape, lambda *_: (0,) * a.ndim)
    sds = jax.ShapeDtypeStruct
    if not sample:
        blk = pl.BlockSpec((SEQ, DA_WIDTH), lambda b: (b, 0))
        return pl.pallas_call(
            functools.partial(_attn_kernel_ctx, lam_init), grid=(BATCH,),
            in_specs=[blk, blk, blk, full(lp), full(g)],
            out_specs=blk, out_shape=sds((T_CTX, DA_WIDTH), BF16),
            compiler_params=_cparams(1), name="diff_attn_ctx",
        )(q, k, v, lp, g)
    tq = 256
    nq = DEC_SEQ // tq
    off_q = T_CTX // tq
    off_k = T_CTX // DEC_SEQ
    qblk = pl.BlockSpec((tq, DA_WIDTH), lambda b, i: (off_q + b * nq + i, 0))
    kblk = pl.BlockSpec((DEC_SEQ, DA_WIDTH), lambda b, i: (off_k + b, 0))
    cblk = pl.BlockSpec((None, None, PAST_LEN, DA_WIDTH), lambda b, i: (b, l, 0, 0))
    return pl.pallas_call(
        functools.partial(_attn_kernel_smp, lam_init), grid=(DEC_BATCH, nq),
        in_specs=[qblk, kblk, kblk, cblk, cblk, full(lp), full(g)],
        out_specs=pl.BlockSpec((tq, DA_WIDTH), lambda b, i: (b * nq + i, 0)),
        out_shape=sds((T_SMP, DA_WIDTH), BF16),
        compiler_params=_cparams(2), name="diff_attn_smp",
    )(q, k, v, cache_k, cache_v, lp, g)


def _tri_masks():
    r = lax.broadcasted_iota(jnp.int32, (CHUNK, CHUNK), 0)
    c = lax.broadcasted_iota(jnp.int32, (CHUNK, CHUNK), 1)
    return c <= r, c >= r


def _ssd_kernel(L, z_ref, xbc_ref, sm_ref, smt_ref, cw_ref, cb_ref, dtbr_ref, dtbc_ref,
                ar_ref, ac_ref, dexp_ref, ng_ref, h0_ref,
                y_ref, hfin_ref, xc_s, yacc_s, ht_s):
    nc = L // CHUNK
    low, upp = _tri_masks()
    tri_l3 = (_mask3(low, 1), _mask3(upp, 1))
    tri_r3 = (_mask3(upp, 0), _mask3(low, 0))
    lane512 = lax.broadcasted_iota(jnp.int32, (1, MB_INNER), 1)
    row16 = lax.broadcasted_iota(jnp.int32, (16, MB_INNER), 0)
    lane16 = lax.broadcasted_iota(jnp.int32, (16, MB_INNER), 1)
    expand = tuple(_mask3(row16 == 8 * d + lane16 // MB_HD, 0) for d in range(2))
    lane128 = lax.broadcasted_iota(jnp.int32, (CHUNK, LANES), 1)
    rowblk = lax.broadcasted_iota(jnp.int32, (LANES, MB_INNER), 0) // MB_STATE
    colblk = lax.broadcasted_iota(jnp.int32, (LANES, MB_INNER), 1) // (MB_INNER // MB_GROUPS)
    same_group = rowblk == colblk
    cw = cw_ref[...]
    cbias = cb_ref[...]

    def conv_chunk(c, _):
        base = pl.multiple_of(c * CHUNK, CHUNK)
        x = xbc_ref[pl.ds(base, CHUNK), :]
        prev = xbc_ref[pl.ds(jnp.maximum(base - 1, 0), 1), :]
        nxt = xbc_ref[pl.ds(jnp.minimum(base + CHUNK, L - 1), 1), :]
        prev = jnp.where(c == 0, 0.0, prev)
        nxt = jnp.where(c == nc - 1, 0.0, nxt)
        row = lax.broadcasted_iota(jnp.int32, (CHUNK, 1), 0)
        xp = jnp.where(row == 0, prev, pltpu.roll(x, 1, 0))
        xn = jnp.where(row == CHUNK - 1, nxt, pltpu.roll(x, CHUNK - 1, 0))
        conv = xp * cw[0:1] + x * cw[1:2] + xn * cw[2:3] + cbias
        xc_s[pl.ds(base, CHUNK), :] = _silu(conv)
        return 0

    lax.fori_loop(0, nc, conv_chunk, 0)

    h0 = h0_ref[...]
    for d in range(2):
        h0d = h0[d]
        ht_s[d] = jnp.concatenate(
            [jnp.where(lane512 < MB_INNER // 2, h0d, 0.0),
             jnp.where(lane512 >= MB_INNER // 2, h0d, 0.0)], axis=0)

    def both(i, _):
        jobs = ((0, i), (1, nc - 1 - i))
        rows, xsb, bb, cb, dt_row, cum_col, cum_row = {}, {}, {}, {}, {}, {}, {}
        y_off, cbg = {}, {}
        for d, c in jobs:
            rows[d] = pl.ds(pl.multiple_of(c * CHUNK, CHUNK), CHUNK)
            xs = xc_s[rows[d], 0:MB_INNER]
            bm = xc_s[rows[d], MB_INNER:MB_INNER + LANES]
            cm = xc_s[rows[d], MB_INNER + LANES:MB_CONV_DIM]
            dt_col = _softplus(sm_ref[rows[d], 0:16] + dtbr_ref[...])
            a_col = dt_col * ar_ref[...]
            dt_row[d] = _softplus(smt_ref[c][0:16, :] + dtbc_ref[...])
            a_row = dt_row[d] * ac_ref[...]
            cum_col[d] = _mask_dot(tri_l3[d], a_col)
            cum_row[d] = _dot_mask(a_row, tri_r3[d])
            last = 0 if d else CHUNK - 1
            cum_last = cum_col[d][last:last + 1, :]
            w_exp = _dot_mask(jnp.exp(cum_last - cum_col[d]) * dt_col, expand[d])
            g_exp = _dot_mask(jnp.exp(cum_col[d]), expand[d])
            cd_exp = _dot_mask(jnp.broadcast_to(jnp.exp(cum_last), (8, 16)), expand[d])[0:1]
            xw = (xs * w_exp).astype(BF16)
            xsb[d] = xs.astype(BF16)
            bb[d] = bm.astype(BF16)
            cb[d] = cm.astype(BF16)
            ht = ht_s[d]
            y_off[d] = _dot(cb[d], ht.astype(BF16)) * g_exp
            s_new = _dot(bm.T.astype(BF16), xw)
            ht_s[d] = ht * cd_exp + jnp.where(same_group, s_new, 0.0)
            for g in range(MB_GROUPS):
                cg = jnp.where((lane128 < MB_STATE) == (g == 0), cb[d], jnp.zeros_like(cb[d]))
                cbg[d, g] = _dot_nt(cg, bb[d])
        m16 = {}
        for d, _c in jobs:
            for h in range(MB_HEADS):
                ci = 8 * d + h
                seg = cum_col[d][:, ci:ci + 1] - cum_row[d][ci:ci + 1, :]
                m = jnp.where(upp if d else low, jnp.exp(seg), 0.0) * cbg[d, h // 4] * dt_row[d][ci:ci + 1, :]
                m16[d, h] = m.astype(BF16)
        yd = {}
        for d, _c in jobs:
            for h in range(MB_HEADS):
                k = h // 2
                yd[d, h] = _dot(m16[d, h], xsb[d][:, LANES * k:LANES * (k + 1)])
        for d, _c in jobs:
            pairs = [jnp.where(lane128 < MB_HD, yd[d, 2 * k], yd[d, 2 * k + 1]) for k in range(MB_HEADS // 2)]
            yacc_s[d, rows[d], :] = jnp.concatenate(pairs, axis=1) + y_off[d]
        return 0

    lax.fori_loop(0, nc, both, 0)

    dexp = dexp_ref[...]
    ng = ng_ref[...]

    def fin(c, _):
        rows = pl.ds(pl.multiple_of(c * CHUNK, CHUNK), CHUNK)
        y = yacc_s[0, rows, :] + yacc_s[1, rows, :] + dexp * xc_s[rows, 0:MB_INNER]
        y = y * _silu(z_ref[rows, :])
        y_ref[rows, :] = _rms(y, ng).astype(BF16)
        return 0

    lax.fori_loop(0, nc, fin, 0)
    for d in range(2):
        ht = ht_s[d]
        hfin_ref[d] = ht[0:MB_STATE, :] + ht[MB_STATE:2 * MB_STATE, :]


def _ssd(z, xbc, sm, smt, cw, cb, dtb, a_neg, dexp, ng, h0t, sample):
    nb, L, off = (DEC_BATCH, DEC_SEQ, T_CTX // DEC_SEQ) if sample else (BATCH, SEQ, 0)
    full = lambda a: pl.BlockSpec(a.shape, lambda b: (0,) * a.ndim)
    seq = lambda w: pl.BlockSpec((L, w), lambda b: (off + b, 0))
    dtb_r, dtb_c = dtb.reshape(1, 16), dtb.reshape(16, 1)
    a_r, a_c = a_neg.reshape(1, 16), a_neg.reshape(16, 1)
    sds = jax.ShapeDtypeStruct
    return pl.pallas_call(
        functools.partial(_ssd_kernel, L), grid=(nb,),
        in_specs=[seq(MB_INNER), seq(MB_CONV_DIM), seq(32),
                  pl.BlockSpec((L // CHUNK, 32, CHUNK), lambda b: (off + b, 0, 0)),
                  full(cw), full(cb), full(dtb_r), full(dtb_c), full(a_r), full(a_c),
                  full(dexp), full(ng),
                  pl.BlockSpec((None, 2, MB_STATE, MB_INNER), lambda b: (b, 0, 0, 0))],
        out_specs=[pl.BlockSpec((L, MB_INNER), lambda b: (b, 0)),
                   pl.BlockSpec((None, 2, MB_STATE, MB_INNER), lambda b: (b, 0, 0, 0))],
        out_shape=[sds((nb * L, MB_INNER), BF16), sds((nb, 2, MB_STATE, MB_INNER), F32)],
        scratch_shapes=[pltpu.VMEM((L, MB_CONV_DIM), F32), pltpu.VMEM((2, L, MB_INNER), F32),
                        pltpu.VMEM((2, LANES, MB_INNER), F32)],
        compiler_params=_cparams(1), name="ssd_smp" if sample else "ssd_ctx",
    )(z, xbc, sm, smt, cw, cb, dtb_r, dtb_c, a_r, a_c, dexp, ng, h0t)


def _mlstm_kernel(L, q_ref, k_ref, v_ref, o_ref, sm_ref, smt_ref, gbr_ref, gbc_ref, ng_ref,
                  c0_ref, n0_ref, m0_ref,
                  y_ref, cf_ref, nf_ref, mf_ref, hacc_s, c_s, n_s):
    nc = L // CHUNK
    low, upp = _tri_masks()
    tri_l = (low.astype(F32), upp.astype(F32))
    tri_r = (upp.astype(F32), low.astype(F32))
    lane128 = lax.broadcasted_iota(jnp.int32, (CHUNK, LANES), 1)
    lane8 = lax.broadcasted_iota(jnp.int32, (1, 2 * ML_HEADS), 1)
    neg_inf = -jnp.inf

    ones16 = jnp.ones((CHUNK, LANES), BF16)
    c_s[...] = c0_ref[...]
    n_s[...] = jnp.broadcast_to(n0_ref[...], n_s.shape)
    m0 = m0_ref[...]

    def both(i, m_in):
        jobs = ((0, i), (1, nc - 1 - i))
        heads = [(d, h) for d, _ in jobs for h in range(ML_HEADS)]
        rows, pre_row, b_col, b_row = {}, {}, {}, {}
        for d, c in jobs:
            rows[d] = pl.ds(pl.multiple_of(c * CHUNK, CHUNK), CHUNK)
            pre_col = sm_ref[rows[d], 16:32] + gbr_ref[...]
            pre_row[d] = smt_ref[c][16:32, :] + gbc_ref[...]
            lf_col = -_softplus(-pre_col)
            lf_row = -_softplus(-pre_row[d])
            b_col[d] = _dot_hi(tri_l[d], lf_col)
            b_row[d] = _dot_hi(lf_row, tri_r[d])
        kt, cst, nst, vh, qk, qc, qn = {}, {}, {}, {}, {}, {}, {}
        for d, _ in jobs:
            for pr in range(2):
                qp = q_ref[rows[d], LANES * pr:LANES * (pr + 1)]
                kp = k_ref[rows[d], LANES * pr:LANES * (pr + 1)]
                kt[d, pr] = kp.astype(F32).T
                cst[d, pr] = c_s[d, pr]
                nst[d, pr] = n_s[d, pr]
                cb16 = cst[d, pr].astype(BF16)
                nb16 = nst[d, pr].astype(BF16)
                for hh in range(2):
                    h = 2 * pr + hh
                    qm = jnp.where((lane128 < ML_DK) == (hh == 0), qp, jnp.zeros_like(qp))
                    vh[d, h] = jnp.concatenate([v_ref[rows[d], ML_DV * h:ML_DV * (h + 1)], ones16], axis=1)
                    qk[d, h] = _dot_nt(qm, kp)
                    qc[d, h] = _dot(qm, cb16)
                    qn[d, h] = _dot(qm, nb16)
        m_t, s_intra, s_inter, m_new, s_old, kts = {}, {}, {}, {}, {}, {}
        for d, h in heads:
            m_st = m_in[:, 4 * d + h:4 * d + h + 1]
            bcol = jnp.broadcast_to(b_col[d][:, 8 * d + 4 + h:8 * d + 5 + h], (CHUNK, CHUNK))
            brow = b_row[d][8 * d + 4 + h:8 * d + 5 + h, :]
            li_row = pre_row[d][8 * d + h:8 * d + h + 1, :]
            dm = jnp.where(upp if d else low, bcol - brow + li_row, neg_inf)
            inter = bcol + m_st
            m_t[d, h] = jnp.maximum(inter, jnp.max(dm, axis=-1, keepdims=True))
            s_intra[d, h] = jnp.exp(dm - m_t[d, h]) * qk[d, h]
            s_inter[d, h] = jnp.exp(inter - m_t[d, h])
            last = 0 if d else CHUNK - 1
            b_end = brow[:, last:last + 1]
            w_end = b_end - brow + li_row
            m_new[d, h] = jnp.maximum(b_end + m_st, jnp.max(w_end, axis=-1, keepdims=True))
            s_old[d, h] = jnp.exp(b_end + m_st - m_new[d, h])
            half = slice(ML_DK * (h % 2), ML_DK * (h % 2 + 1))
            kts[d, h] = kt[d, h // 2][half, :] * jnp.exp(w_end - m_new[d, h])
        pv, kv = {}, {}
        for d, h in heads:
            pv[d, h] = _dot(s_intra[d, h].astype(BF16), vh[d, h])
            kv[d, h] = _dot(kts[d, h].astype(BF16), vh[d, h])
        m_out = m_in
        for d, h in heads:
            num = s_inter[d, h] * qc[d, h] + pv[d, h][:, :ML_DV]
            den = s_inter[d, h] * qn[d, h] + pv[d, h][:, ML_DV:]
            hout = num / jnp.maximum(jnp.abs(den), jnp.exp(-m_t[d, h]))
            hacc_s[d, rows[d], ML_DV * h:ML_DV * (h + 1)] = hout
            m_out = jnp.where(lane8 == 4 * d + h, m_new[d, h], m_out)
        for d, _ in jobs:
            for pr in range(2):
                c_new, n_new = [], []
                for hh in range(2):
                    h = 2 * pr + hh
                    half = slice(ML_DK * hh, ML_DK * (hh + 1))
                    c_new.append(s_old[d, h] * cst[d, pr][half, :] + kv[d, h][:, :ML_DV])
                    n_new.append(s_old[d, h] * nst[d, pr][half, :] + kv[d, h][:, ML_DV:])
                c_s[d, pr] = jnp.concatenate(c_new, axis=0)
                n_s[d, pr] = jnp.concatenate(n_new, axis=0)
        return m_out

    m_fin = lax.fori_loop(0, nc, both, m0)

    ng = ng_ref[...]

    def fin(c, _):
        rows = pl.ds(pl.multiple_of(c * CHUNK, CHUNK), CHUNK)
        for h in range(ML_HEADS):
            cols = slice(ML_DV * h, ML_DV * (h + 1))
            y = _rms(hacc_s[0, rows, cols] + hacc_s[1, rows, cols], ng[:, cols]) * jax.nn.sigmoid(o_ref[rows, cols])
            y_ref[rows, cols] = y.astype(BF16)
        return 0

    lax.fori_loop(0, nc, fin, 0)
    cf_ref[...] = c_s[...]
    nf_ref[...] = n_s[:, :, :, 0:1]
    mf_ref[...] = m_fin


def _mlstm(q, k, v, o, sm, smt, gb, ng, c0, n0, m0, sample):
    nb, L, off = (DEC_BATCH, DEC_SEQ, T_CTX // DEC_SEQ) if sample else (BATCH, SEQ, 0)
    full = lambda a: pl.BlockSpec(a.shape, lambda b: (0,) * a.ndim)
    seq = lambda w: pl.BlockSpec((L, w), lambda b: (off + b, 0))
    gb_r, gb_c = gb.reshape(1, 16), gb.reshape(16, 1)
    st_c = pl.BlockSpec((None, 2, 2, LANES, ML_DV), lambda b: (b, 0, 0, 0, 0))
    st_n = pl.BlockSpec((None, 2, 2, LANES, 1), lambda b: (b, 0, 0, 0, 0))
    st_m = pl.BlockSpec((None, 1, 8), lambda b: (b, 0, 0))
    sds = jax.ShapeDtypeStruct
    return pl.pallas_call(
        functools.partial(_mlstm_kernel, L), grid=(nb,),
        in_specs=[seq(256), seq(256), seq(512), seq(512), seq(32),
                  pl.BlockSpec((L // CHUNK, 32, CHUNK), lambda b: (off + b, 0, 0)),
                  full(gb_r), full(gb_c), full(ng), st_c, st_n, st_m],
        out_specs=[pl.BlockSpec((L, ML_WIDTH), lambda b: (b, 0)), st_c, st_n, st_m],
        out_shape=[sds((nb * L, ML_WIDTH), BF16), sds((nb, 2, 2, LANES, ML_DV), F32),
                   sds((nb, 2, 2, LANES, 1), F32), sds((nb, 1, 8), F32)],
        scratch_shapes=[pltpu.VMEM((2, L, ML_WIDTH), F32), pltpu.VMEM((2, 2, LANES, ML_DV), F32),
                        pltpu.VMEM((2, 2, LANES, LANES), F32)],
        compiler_params=_cparams(1), name="mlstm_smp" if sample else "mlstm_ctx",
    )(q, k, v, o, sm, smt, gb_r, gb_c, ng, c0, n0, m0)


def _merge_kernel(x_ref, mod_ref, g1_ref, ydac_ref, ymbc_ref, ymlc_ref, ydas_ref, ymbs_ref, ymls_ref,
                  wg_ref, bg_ref, wb_ref, wo_ref, g2_ref, rwt_ref, xo_ref, h2t_ref, affb_ref):
    x = x_ref[...]
    mod = mod_ref[...]
    sh1, sc1, gt1, sh2, sc2 = (mod[:, j * D_MODEL:(j + 1) * D_MODEL] for j in range(5))
    hb = (_rms(x, g1_ref[...]) * (1.0 + sc1) + sh1).astype(BF16)
    is_ctx = pl.program_id(0) < T_CTX // TM_MERGE
    merged = None
    for n, (yc_ref, ys_ref) in enumerate(((ydac_ref, ydas_ref), (ymbc_ref, ymbs_ref), (ymlc_ref, ymls_ref))):
        cols = slice(n * D_MODEL, (n + 1) * D_MODEL)
        gate = jax.nn.sigmoid(_dot(hb, wg_ref[:, cols]) + bg_ref[:, cols])
        y = jnp.where(is_ctx, yc_ref[...], ys_ref[...])
        term = gate * _dot(y, wb_ref[n])
        merged = term if merged is None else merged + term
    out = _dot(merged.astype(BF16), wo_ref[...])
    xn = x + gt1 * out
    xo_ref[...] = xn
    h2 = _rms(xn, g2_ref[...]) * (1.0 + sc2) + sh2
    h2t_ref[...] = h2.T.astype(BF16)
    logits = lax.dot_general(rwt_ref[...], h2, (((1,), (1,)), ((), ())),
                             precision=HI, preferred_element_type=F32)
    e = jnp.exp(logits - jnp.max(logits, axis=0, keepdims=True))
    aff = e / jnp.sum(e, axis=0, keepdims=True)
    for j in range(TM_MERGE // CHUNK):
        affb_ref[j] = aff[:, CHUNK * j:CHUNK * (j + 1)]


def _merge(x, mod_l, g1, ys_ctx, ys_smp, wg, bg, wb, wo, g2, rwt):
    tm = TM_MERGE
    n_ctx = T_CTX // tm
    per_req = DEC_SEQ // tm
    tile = lambda w: pl.BlockSpec((tm, w), lambda i: (i, 0))
    ctx_tile = pl.BlockSpec((tm, BRANCH_W), lambda i: (jnp.minimum(i, n_ctx - 1), 0))
    smp_tile = pl.BlockSpec((tm, BRANCH_W), lambda i: (jnp.maximum(i - n_ctx, 0), 0))
    full = lambda a: pl.BlockSpec(a.shape, lambda i: (0,) * a.ndim)
    mod_row = lambda i: jnp.where(i < n_ctx, 0, 1 + (i - n_ctx) // per_req)
    sds = jax.ShapeDtypeStruct
    return pl.pallas_call(
        _merge_kernel, grid=(T_ALL // tm,),
        in_specs=[tile(D_MODEL), pl.BlockSpec((None, 1, 6 * D_MODEL), lambda i: (mod_row(i), 0, 0)),
                  full(g1), ctx_tile, ctx_tile, ctx_tile, smp_tile, smp_tile, smp_tile,
                  full(wg), full(bg), full(wb), full(wo), full(g2), full(rwt)],
        out_specs=[tile(D_MODEL), pl.BlockSpec((D_MODEL, tm), lambda i: (0, i)),
                   pl.BlockSpec((tm // CHUNK, N_EXPERTS, CHUNK), lambda i: (i, 0, 0))],
        out_shape=[sds((T_ALL, D_MODEL), F32), sds((D_MODEL, T_ALL), BF16),
                   sds((NB_ALL, N_EXPERTS, CHUNK), F32)],
        compiler_params=_cparams(1), name="merge_out",
    )(x, mod_l, g1, *ys_ctx, *ys_smp, wg, bg, wb, wo, g2, rwt)


def _route_kernel(affb_ref, slotb_ref, slott_ref, a_ref, jlo_ref, jhi_ref, acc_s, run_s):
    r = lax.broadcasted_iota(jnp.int32, (CHUNK, CHUNK), 0)
    c = lax.broadcasted_iota(jnp.int32, (CHUNK, CHUNK), 1)
    upper = jnp.where(r <= c, 1.0, 0.0).astype(BF16)
    eye = jnp.where(r == c, 1.0, 0.0).astype(BF16)
    eye2 = jnp.concatenate([eye, eye], axis=1)
    lane = lax.broadcasted_iota(jnp.int32, (N_EXPERTS, LANES), 1)
    acc_s[...] = jnp.zeros_like(acc_s)
    run_s[...] = jnp.zeros_like(run_s)
    for b0, b1, cap in ((0, NB_CTX, CAP_CTX), (NB_CTX, NB_ALL, CAP_SMP)):
        aff = affb_ref[b0:b1]

        def search(i, thr_bits):
            cand = thr_bits | lax.shift_left(jnp.int32(1), 30 - i)
            cnt = jnp.sum((aff >= pltpu.bitcast(cand, F32)[None]).astype(jnp.int32), axis=0)
            cnt = jnp.sum(cnt, axis=1, keepdims=True)
            return jnp.where(cnt >= cap, cand, thr_bits)

        thr = pltpu.bitcast(lax.fori_loop(0, 31, search, jnp.zeros((N_EXPERTS, 1), jnp.int32)), F32)
        n_gt = jnp.sum(jnp.sum((aff > thr[None]).astype(jnp.int32), axis=0), axis=1, keepdims=True)
        need = (cap - n_gt).astype(F32)

        run_s[1] = jnp.zeros((N_EXPERTS, 1), F32)

        def blocks(gi, _):
            b = pl.multiple_of(b0 + gi * ROUTE_GROUP, ROUTE_GROUP)
            run_sel = run_s[0]
            run_eq = run_s[1]
            xs = [affb_ref[b + i] for i in range(ROUTE_GROUP)]
            eqs = [x == thr for x in xs]
            eq_fs = [jnp.where(eq, 1.0, 0.0) for eq in eqs]
            eq_incls = [_dot(f.astype(BF16), upper) for f in eq_fs]
            sels = []
            for x, eq, f, incl in zip(xs, eqs, eq_fs, eq_incls):
                sels.append((x > thr) | (eq & (run_eq + incl - f < need)))
                run_eq = run_eq + incl[:, CHUNK - 1:CHUNK]
            sel_fs = [jnp.where(sel, 1.0, 0.0) for sel in sels]
            sel_incls = [_dot(f.astype(BF16), upper) for f in sel_fs]
            acc = acc_s[...]
            slots = []
            for i, (sel, f, incl) in enumerate(zip(sels, sel_fs, sel_incls)):
                slots.append(jnp.where(sel, run_sel + incl - f, -1.0))
                if i % 2 == 0:
                    acc = jnp.where(lane == b // 2 + i // 2, run_sel, acc)
                run_sel = run_sel + incl[:, CHUNK - 1:CHUNK]
            acc_s[...] = acc
            run_s[0] = run_sel
            run_s[1] = run_eq
            for i, slot in enumerate(slots):
                slotb_ref[b + i] = slot.astype(jnp.int32)
                hi64 = jnp.floor(slot * (1.0 / 64.0)) * 64.0
                parts = jnp.concatenate([hi64, slot - hi64], axis=1).astype(BF16)
                slott_ref[pl.ds(pl.multiple_of((b + i) * CHUNK, CHUNK), CHUNK), :] = _dot_nt(eye2, parts)
            return 0

        lax.fori_loop(0, (b1 - b0) // ROUTE_GROUP, blocks, 0)
    a_acc = jnp.where(lane == NT_ALL, run_s[0], acc_s[...])
    a_ref[...] = a_acc.astype(jnp.int32)
    a_next = pltpu.roll(a_acc, LANES - 1, 1)
    tile_ok = lane < NT_ALL
    jlo = jnp.zeros((N_EXPERTS, LANES), jnp.int32)
    jhi = jnp.zeros((N_EXPERTS, LANES), jnp.int32)
    for k in range(N_RB):
        lo_k = jnp.sum((tile_ok & (a_next <= float(RB * k))).astype(jnp.int32), axis=1, keepdims=True)
        hi_k = jnp.sum((tile_ok & (a_acc < float(RB * (k + 1)))).astype(jnp.int32), axis=1, keepdims=True) - 1
        jlo = jnp.where(lane == k, lo_k, jlo)
        jhi = jnp.where(lane == k, hi_k, jhi)
    jlo_ref[...] = jlo
    jhi_ref[...] = jhi


def _route(affb):
    sds = jax.ShapeDtypeStruct
    small = sds((N_EXPERTS, LANES), jnp.int32)
    return pl.pallas_call(
        _route_kernel,
        out_shape=[sds((NB_ALL, N_EXPERTS, CHUNK), jnp.int32), sds((T_ALL, N_EXPERTS), F32),
                   small, small, small],
        scratch_shapes=[pltpu.VMEM((N_EXPERTS, LANES), F32), pltpu.VMEM((2, N_EXPERTS, 1), F32)],
        compiler_params=pltpu.CompilerParams(vmem_limit_bytes=VMEM_LIMIT), name="route",
    )(affb)


def _expert_kernel(jlo_ref, jhi_ref, h2t_hbm, slotb_ref, affb_ref, wg_ref, wu_ref, wd_ref, o_ref,
                   wg_s, wu_s, wd_s, chunk_s, sem, acc_s, g_s, cnt_s):
    e = pl.program_id(0)
    k = pl.program_id(1)
    step = e * N_RB + k

    @pl.when(k == 0)
    def _():
        wg_s[...] = wg_ref[...].astype(BF16)
        wu_s[...] = wu_ref[...].astype(BF16)
        wd_s[...] = wd_ref[...].astype(BF16)

    def tile_range(s):
        i = (s // N_RB) * LANES + s % N_RB
        lo = jnp.clip(jlo_ref[i], 0, NT_ALL - 1)
        return lo, jnp.clip(jhi_ref[i] - lo + 1, 1, NT_ALL - lo)

    def chunk_tile(lo, c):
        return jnp.minimum(lo + GATHER_TILES * c, NT_ALL - GATHER_TILES)

    def chunk_copy(j0, buf):
        return pltpu.make_async_copy(
            h2t_hbm.at[:, pl.ds(pl.multiple_of(j0 * TM, TM), GATHER_TILES * TM)],
            chunk_s.at[buf], sem.at[buf])

    def n_chunks(s):
        return (tile_range(s)[1] + GATHER_TILES - 1) // GATHER_TILES

    n_steps = N_EXPERTS * N_RB

    def advance(s, c):
        wrap = (c + 1 >= n_chunks(jnp.minimum(s, n_steps - 1))) | (s >= n_steps)
        return jnp.where(wrap, s + 1, s), jnp.where(wrap, 0, c + 1)

    def start_at(s, c, buf):
        @pl.when(s < n_steps)
        def _():
            chunk_copy(chunk_tile(tile_range(jnp.minimum(s, n_steps - 1))[0], c), buf).start()

    jlo, n = tile_range(step)
    nch = n_chunks(step)

    @pl.when(step == 0)
    def _():
        cnt_s[0] = 0
        pos = (step, 0)
        for i in range(GATHER_AHEAD):
            start_at(pos[0], pos[1], i)
            pos = advance(*pos)

    done = cnt_s[0]
    acc_s[...] = jnp.zeros_like(acc_s)
    g_s[...] = jnp.zeros_like(g_s)
    ntok = GATHER_TILES * TM
    want = lax.broadcasted_iota(jnp.int32, (RB, ntok), 0) + k * RB
    lane_tile = lax.broadcasted_iota(jnp.int32, (1, ntok), 1) // TM

    def body(c, _):
        buf = (done + c) % (GATHER_AHEAD + 1)
        j0 = chunk_tile(jlo, c)
        chunk_copy(j0, buf).wait()
        pos = (step, c)
        for _i in range(GATHER_AHEAD):
            pos = advance(*pos)
        start_at(pos[0], pos[1], (done + c + GATHER_AHEAD) % (GATHER_AHEAD + 1))

        nblk = ntok // CHUNK
        srow = jnp.concatenate([slotb_ref[2 * j0 + i, pl.ds(e, 1), :] for i in range(nblk)], axis=1)
        arow = jnp.concatenate([affb_ref[2 * j0 + i, pl.ds(e, 1), :] for i in range(nblk)], axis=1)
        fresh = j0 + lane_tile >= jlo + GATHER_TILES * c
        hit = (srow == want) & fresh
        onehot = jnp.where(hit, 1.0, 0.0).astype(BF16)
        acc_s[...] += _dot_nt(chunk_s[buf], onehot)
        g_s[...] += jnp.sum(jnp.where(hit, arow, 0.0), axis=1, keepdims=True)
        return 0

    lax.fori_loop(0, nch, body, 0)
    cnt_s[0] = done + nch
    xe = acc_s[...].T.astype(BF16)
    hid = _silu(_dot(xe, wg_s[...])) * _dot(xe, wu_s[...])
    o_ref[...] = (_dot(hid.astype(BF16), wd_s[...]) * g_s[...]).astype(BF16)


def _experts(l, jlo, jhi, h2t, slotb, affb, w_gate, w_up, w_down):
    wsp = pl.BlockSpec((None, None, D_MODEL, EXPERT_FF), lambda e, j, *_: (l, e, 0, 0))
    wsd = pl.BlockSpec((None, None, EXPERT_FF, D_MODEL), lambda e, j, *_: (l, e, 0, 0))
    whole = lambda a: pl.BlockSpec(a.shape, lambda e, j, *_: (0,) * a.ndim)
    grid_spec = pltpu.PrefetchScalarGridSpec(
        num_scalar_prefetch=2, grid=(N_EXPERTS, N_RB),
        in_specs=[pl.BlockSpec(memory_space=pl.ANY), whole(slotb), whole(affb), wsp, wsp, wsd],
        out_specs=pl.BlockSpec((None, RB, D_MODEL), lambda e, j, *_: (e, j, 0)),
        scratch_shapes=[pltpu.VMEM((D_MODEL, EXPERT_FF), BF16), pltpu.VMEM((D_MODEL, EXPERT_FF), BF16),
                        pltpu.VMEM((EXPERT_FF, D_MODEL), BF16),
                        pltpu.VMEM((GATHER_AHEAD + 1, D_MODEL, GATHER_TILES * TM), BF16),
                        pltpu.SemaphoreType.DMA((GATHER_AHEAD + 1,)), pltpu.VMEM((D_MODEL, RB), F32),
                        pltpu.VMEM((RB, 1), F32), pltpu.SMEM((1,), jnp.int32)])
    return pl.pallas_call(
        _expert_kernel, grid_spec=grid_spec,
        out_shape=jax.ShapeDtypeStruct((N_EXPERTS, CAP_ALL, D_MODEL), BF16),
        compiler_params=_cparams(2), name="expert_ffn",
    )(jlo, jhi, h2t, slotb, affb, w_gate, w_up, w_down)


def _combine_body(a_ref, ye_hbm, slott_ref, win_s, sem, xwin_s, xsem, o_ref):
    j = pl.program_id(0)
    nt = pl.num_programs(0)

    def first_row(e, jj):
        a = a_ref[e * LANES + jj]
        return pl.multiple_of(jnp.clip((a // BF16_ROWS) * BF16_ROWS, 0, CAP_ALL - WIN), BF16_ROWS)

    def win_copy(e, row0, buf):
        return pltpu.make_async_copy(ye_hbm.at[e, pl.ds(row0, WIN), :], win_s.at[buf, e], sem.at[buf, e])

    buf = j % 2

    @pl.when(j == 0)
    def _():
        for e in range(N_EXPERTS):
            win_copy(e, first_row(e, 0), 0).start()

    @pl.when(j + 1 < nt)
    def _():
        for e in range(N_EXPERTS):
            win_copy(e, first_row(e, j + 1), 1 - buf).start()

    lane = lax.broadcasted_iota(jnp.int32, (TM, LANES), 1)
    lane_f = lane.astype(F32)
    rows0 = []
    pieces = []
    for e in range(0, N_EXPERTS, LANES // WIN):
        tgt = None
        for i in range(LANES // WIN):
            row0 = first_row(e + i, j)
            win_copy(e + i, row0, buf).wait()
            rows0.append(row0)
            t_i = slott_ref[:, e + i:e + i + 1] - (row0 - WIN * i).astype(F32)
            in_win = (lane >= WIN * i) & (lane < WIN * (i + 1))
            tgt = jnp.where(in_win, t_i, -1.0) if tgt is None else jnp.where(in_win, t_i, tgt)
        pieces.append(jnp.where(tgt == lane_f, 1.0, 0.0).astype(BF16))
    onehot = jnp.concatenate(pieces, axis=1)
    o_ref[...] = _dot(onehot, win_s[buf].reshape(N_EXPERTS * WIN, D_MODEL))

    lane_w = lax.broadcasted_iota(jnp.int32, (TM, WIN), 1).astype(F32)
    for e in range(N_EXPERTS):
        row0 = rows0[e]
        n_more = jnp.maximum((a_ref[e * LANES + j + 1] - row0 + WIN - 1) // WIN - 1, 0)

        def more(i, _):
            lo_slot = row0 + (i + 1) * WIN
            r = pl.multiple_of(jnp.minimum(lo_slot, CAP_ALL - WIN), BF16_ROWS)
            cp = pltpu.make_async_copy(ye_hbm.at[e, pl.ds(r, WIN), :], xwin_s, xsem)
            cp.start()
            cp.wait()
            scol = slott_ref[:, e:e + 1]
            scol = jnp.where(scol >= lo_slot.astype(F32), scol, -1.0)
            oh = jnp.where(scol - r.astype(F32) == lane_w, 1.0, 0.0).astype(BF16)
            o_ref[...] += _dot(oh, xwin_s[...])
            return 0

        lax.fori_loop(0, n_more, more, 0)


def _combine_kernel_mid(a_ref, ye_hbm, slott_ref, x_ref, mod_ref, xo_ref, moe_s, win_s, sem, xwin_s, xsem):
    _combine_body(a_ref, ye_hbm, slott_ref, win_s, sem, xwin_s, xsem, moe_s)
    xo_ref[...] = x_ref[...] + mod_ref[...][:, 5 * D_MODEL:6 * D_MODEL] * moe_s[...]


def _combine_kernel_last(a_ref, ye_hbm, slott_ref, x_ref, mod_ref, fg_ref, yp_ref, ys_ref,
                         moe_s, win_s, sem, xwin_s, xsem):
    _combine_body(a_ref, ye_hbm, slott_ref, win_s, sem, xwin_s, xsem, moe_s)
    y = _rms(x_ref[...] + mod_ref[...][:, 5 * D_MODEL:6 * D_MODEL] * moe_s[...], fg_ref[...])

    @pl.when(pl.program_id(0) < NT_CTX)
    def _():
        yp_ref[...] = y

    @pl.when(pl.program_id(0) >= NT_CTX)
    def _():
        ys_ref[...] = y


def _combine(a, ye, slott, x, mod_l, fg):
    tile = lambda w: pl.BlockSpec((TM, w), lambda i, *_: (i, 0))
    in_specs = [pl.BlockSpec(memory_space=pl.ANY), tile(N_EXPERTS), tile(D_MODEL),
                pl.BlockSpec((None, 1, 6 * D_MODEL), lambda i, *_: (_mod_row(i), 0, 0))]
    args = [a, ye, slott, x, mod_l]
    sds = jax.ShapeDtypeStruct
    if fg is None:
        kern, out_specs, out_shape = _combine_kernel_mid, tile(D_MODEL), sds((T_ALL, D_MODEL), F32)
    else:
        kern = _combine_kernel_last
        in_specs.append(pl.BlockSpec(fg.shape, lambda i, *_: (0, 0)))
        args.append(fg)
        out_specs = [pl.BlockSpec((TM, D_MODEL), lambda i, *_: (jnp.minimum(i, NT_CTX - 1), 0)),
                     pl.BlockSpec((TM, D_MODEL), lambda i, *_: (jnp.maximum(i - NT_CTX, 0), 0))]
        out_shape = [sds((T_CTX, D_MODEL), F32), sds((T_SMP, D_MODEL), F32)]
    grid_spec = pltpu.PrefetchScalarGridSpec(
        num_scalar_prefetch=1, grid=(NT_ALL,), in_specs=in_specs, out_specs=out_specs,
        scratch_shapes=[pltpu.VMEM((TM, D_MODEL), F32),
                        pltpu.VMEM((2, N_EXPERTS, WIN, D_MODEL), BF16),
                        pltpu.SemaphoreType.DMA((2, N_EXPERTS)),
                        pltpu.VMEM((WIN, D_MODEL), BF16), pltpu.SemaphoreType.DMA(())])
    return pl.pallas_call(
        kern, grid_spec=grid_spec, out_shape=out_shape,
        compiler_params=_cparams(1), name="moe_combine",
    )(*args)


def _rope_tables():
    t = jnp.arange(DEC_SEQ)
    pos = jnp.stack([t // GRID_W, t % GRID_W], axis=-1).astype(F32)
    nf = DA_HD // 4
    inv = ROPE_BASE ** (-jnp.arange(nf, dtype=F32) / nf)
    ang = pos[:, :, None] * inv
    cos = jnp.cos(ang)
    sin = jnp.sin(ang)
    cos64 = jnp.stack([cos, cos], axis=2).reshape(DEC_SEQ, DA_HD)
    sin64 = jnp.stack([-sin, sin], axis=2).reshape(DEC_SEQ, DA_HD)
    cos_t = jnp.concatenate([jnp.ones((TM, LANES), F32), jnp.tile(cos64, (1, 2))], axis=0)
    sin_t = jnp.concatenate([jnp.zeros((TM, LANES), F32), jnp.tile(sin64, (1, 2))], axis=0)
    return cos_t, sin_t


def kernel(x_prompt, x_sample, cache_k, cache_v, state_ssm, state_mlstm_c, state_mlstm_n, state_mlstm_m, c, c_ctx, ada_w, ada_b, norm1_g, norm2_g, w_in, da_lambda, da_subln_g, mb_conv_w, mb_conv_b, mb_dt_bias, mb_a_log, mb_d, mb_norm_g, ml_gate_b, ml_norm_g, w_branch, w_mgate, b_mgate, w_out, router_w, ex_w_gate, ex_w_up, ex_w_down, final_g):
    x = (x_prompt.reshape(T_CTX, D_MODEL), x_sample.reshape(T_SMP, D_MODEL))
    cc = jnp.concatenate([c_ctx[None, :], c, jnp.zeros((16 - 1 - DEC_BATCH, D_MODEL), F32)], axis=0)
    mod = _modulation(cc, ada_w, ada_b).reshape(DEPTH, 16, 1, 6 * D_MODEL)
    cos_t, sin_t = _rope_tables()
    cache_k2 = cache_k.reshape(DEC_BATCH, DEPTH, PAST_LEN, DA_WIDTH)
    cache_v2 = cache_v.reshape(DEC_BATCH, DEPTH, PAST_LEN, DA_WIDTH)

    outs = {n: [] for n in ("k", "v", "ssm", "C", "n", "m")}
    fg = final_g[None]
    for l in range(DEPTH):
        w = w_in[l]
        wm = jnp.concatenate([w[:, :2816], w[:, 2832:4368]], axis=1).astype(BF16)
        ws = jnp.concatenate([w[:, 2816:2832], w[:, 4368:4384]], axis=1).astype(BF16)
        res = _projection(x, mod[l], norm1_g[l][None], wm, ws, ws.T, cos_t, sin_t)
        q, k, v, kf, vf, z, xbc, mq, mk, mv, mo, sm, smt = res[:13]
        if l == 0:
            x = res[13]
        outs["k"].append(kf[:T_CTX].reshape(BATCH, SEQ, DA_HEADS, 2, DA_HD))
        outs["v"].append(vf[:T_CTX].reshape(BATCH, SEQ, DA_HEADS, 2 * DA_HD))

        lp = da_lambda[l]
        sg = da_subln_g[l][None]
        cw = mb_conv_w[l]
        cb = mb_conv_b[l][None]
        dtb = mb_dt_bias[l].reshape(16)
        a_neg = -jnp.exp(mb_a_log[l]).reshape(16)
        dexp = jnp.repeat(mb_d[l], MB_HD)[None]
        mng = mb_norm_g[l][None]
        gb = ml_gate_b[l].reshape(16)
        lng = ml_norm_g[l].reshape(1, ML_WIDTH)
        ys = []
        for sample in (False, True):
            nb = DEC_BATCH if sample else BATCH
            y_da = _attention(l, q, k, v, cache_k2, cache_v2, lp, sg, sample)
            if sample:
                h0 = state_ssm[:, l]
                c0 = state_mlstm_c[:, l]
                n0 = state_mlstm_n[:, l]
                m0 = state_mlstm_m[:, l]
            else:
                h0 = jnp.zeros((nb, 2, MB_HEADS, MB_HD, MB_STATE), F32)
                c0 = jnp.zeros((nb, 2, ML_HEADS, ML_DK, ML_DV), F32)
                n0 = jnp.zeros((nb, 2, ML_HEADS, ML_DK), F32)
                m0 = jnp.zeros((nb, 2, ML_HEADS), F32)
            h0t = jnp.transpose(h0, (0, 1, 4, 2, 3)).reshape(nb, 2, MB_STATE, MB_INNER)
            y_mb, hfin = _ssd(z, xbc, sm, smt, cw, cb, dtb, a_neg, dexp, mng, h0t, sample)
            y_ml, cfin, nfin, mfin = _mlstm(
                mq, mk, mv, mo, sm, smt, gb, lng,
                c0.reshape(nb, 2, 2, LANES, ML_DV), n0.reshape(nb, 2, 2, LANES, 1),
                m0.reshape(nb, 1, 8), sample)
            ys.append((y_da, y_mb, y_ml))
            if not sample:
                outs["ssm"].append(jnp.transpose(
                    hfin.reshape(nb, 2, MB_STATE, MB_HEADS, MB_HD), (0, 1, 3, 4, 2)))
                outs["C"].append(cfin.reshape(nb, 2, ML_HEADS, ML_DK, ML_DV))
                outs["n"].append(nfin.reshape(nb, 2, ML_HEADS, ML_DK))
                outs["m"].append(mfin.reshape(nb, 2, ML_HEADS))
        x, h2t, affb = _merge(x, mod[l], norm1_g[l][None], ys[0], ys[1],
                              w_mgate[l].astype(BF16), b_mgate[l][None], w_branch[l].astype(BF16),
                              w_out[l].astype(BF16), norm2_g[l][None], router_w[l].T)
        slotb, slott, a_cnt, jlo, jhi = _route(affb)
        ye = _experts(l, jlo.reshape(-1), jhi.reshape(-1), h2t, slotb, affb,
                      ex_w_gate, ex_w_up, ex_w_down)
        if l + 1 < DEPTH:
            x = _combine(a_cnt.reshape(-1), ye, slott, x, mod[l], None)
        else:
            y_prompt, y_sample = _combine(a_cnt.reshape(-1), ye, slott, x, mod[l], fg)

    y_prompt = y_prompt.reshape(BATCH, SEQ, D_MODEL)
    y_sample = y_sample.reshape(DEC_BATCH, DEC_SEQ, D_MODEL)
    return (y_prompt, y_sample, jnp.stack(outs["k"], axis=1), jnp.stack(outs["v"], axis=1),
            jnp.stack(outs["ssm"], axis=1), jnp.stack(outs["C"], axis=1),
            jnp.stack(outs["n"], axis=1), jnp.stack(outs["m"], axis=1))
```

```python
import functools
import math

import jax
import jax.numpy as jnp
from jax import lax
from jax.experimental import pallas as pl
from jax.experimental.pallas import tpu as pltpu

F32 = jnp.float32
BF16 = jnp.bfloat16

D_MODEL = 1024
BATCH = 16
SEQ = 256
DEPTH = 2
DEC_BATCH = 8
DEC_SEQ = 2048
PAST_LEN = 512
GRID_W = 64
EPS = 1e-6
CHUNK = 128
ROPE_BASE = 10000.0
DA_HEADS = 4
DA_HD = 64
DA_WIDTH = 512
MB_INNER = 512
MB_HD = 64
MB_HEADS = 8
MB_GROUPS = 2
MB_STATE = 64
MB_CONV_DIM = 768
ML_HEADS = 4
ML_DK = 64
ML_DV = 128
ML_WIDTH = 512
N_EXPERTS = 16
EC_FACTOR = 2
EXPERT_FF = 1024

T_CTX = BATCH * SEQ
T_SMP = DEC_BATCH * DEC_SEQ
T_ALL = T_CTX + T_SMP
TM = 256
NT_CTX = T_CTX // TM
NT_ALL = T_ALL // TM
NB_CTX = T_CTX // CHUNK
NB_ALL = T_ALL // CHUNK
CAP_CTX = EC_FACTOR * T_CTX // N_EXPERTS
CAP_SMP = EC_FACTOR * T_SMP // N_EXPERTS
CAP_ALL = CAP_CTX + CAP_SMP
RB = 256
N_RB = CAP_ALL // RB
WIN = 64
GATHER_TILES = 5
GATHER_AHEAD = 3
ROUTE_GROUP = 4
TILES_PER_REQ = DEC_SEQ // TM
TM_MERGE = 512
BRANCH_W = 512
LANES = 128
BF16_ROWS = 16
VMEM_LIMIT = 56 * 1024 * 1024
HI = lax.Precision.HIGHEST
LOG2E = math.log2(math.e)

C_Q, C_K, C_V, C_Z, C_XBC, C_MQ, C_MK, C_MV, C_MO, C_END = (
    0, 512, 1024, 1536, 2048, 2816, 3072, 3328, 3840, 4352)


def _mod_row(i):
    return jnp.where(i < NT_CTX, 0, 1 + (i - NT_CTX) // TILES_PER_REQ)


def _rope_blk(i):
    return jnp.where(i < NT_CTX, 0, 1 + (i - NT_CTX) % TILES_PER_REQ)


def _cparams(n_grid):
    return pltpu.CompilerParams(dimension_semantics=("arbitrary",) * n_grid,
                                vmem_limit_bytes=VMEM_LIMIT)


def _silu(x):
    return x * jax.nn.sigmoid(x)


def _softplus(x):
    u = jnp.exp(-jnp.abs(x))
    w = 1.0 + u
    l1p = jnp.where(w == 1.0, u, jnp.log(w) * (u / (w - 1.0)))
    return jnp.maximum(x, 0.0) + l1p


def _dot(a, b):
    return jnp.dot(a, b, preferred_element_type=F32)


def _dot_nt(a, b):
    return lax.dot_general(a, b, (((1,), (1,)), ((), ())), preferred_element_type=F32)


def _dot_hi(a, b):
    return jnp.dot(a, b, precision=HI, preferred_element_type=F32)


def _split3(a):
    hi = a.astype(BF16)
    r = a - hi.astype(F32)
    mid = r.astype(BF16)
    return hi, mid, (r - mid.astype(F32)).astype(BF16)


def _mask3(mask, axis):
    m = jnp.where(mask, 1.0, 0.0).astype(BF16)
    return jnp.concatenate([m, m, m], axis=axis)


def _mask_dot(m3, a):
    return _dot(m3, jnp.concatenate(_split3(a), axis=0))


def _dot_mask(a, m3):
    return _dot(jnp.concatenate(_split3(a), axis=1), m3)


def _dot_mask_narrow(a, m):
    hi, mid, lo = _split3(a)
    return _dot(hi, m) + _dot(mid, m) + _dot(lo, m)


def _rms(x, g):
    return x * lax.rsqrt(jnp.mean(x * x, axis=-1, keepdims=True) + EPS) * g


def _mod_kernel(c_ref, w_ref, b_ref, o_ref):
    s = _silu(c_ref[...])
    o_ref[...] = _dot(s.astype(BF16), w_ref[...].astype(BF16)) + b_ref[...]


def _modulation(cc, ada_w, ada_b):
    tn = 1536
    return pl.pallas_call(
        _mod_kernel,
        grid=(DEPTH, 6 * D_MODEL // tn),
        in_specs=[pl.BlockSpec((16, D_MODEL), lambda l, j: (0, 0)),
                  pl.BlockSpec((None, D_MODEL, tn), lambda l, j: (l, 0, j)),
                  pl.BlockSpec((None, 1, tn), lambda l, j: (l, 0, j))],
        out_specs=pl.BlockSpec((None, 16, tn), lambda l, j: (l, 0, j)),
        out_shape=jax.ShapeDtypeStruct((DEPTH, 16, 6 * D_MODEL), F32),
        compiler_params=_cparams(2),
        name="adaln_mod",
    )(cc, ada_w, ada_b.reshape(DEPTH, 1, 6 * D_MODEL))


def _rope(t, cos, sin, first_half):
    outs = []
    for c in range(DA_WIDTH // LANES):
        xc = t[:, LANES * c:LANES * (c + 1)]
        partner = jnp.where(first_half, pltpu.roll(xc, LANES - 16, 1), pltpu.roll(xc, 16, 1))
        outs.append(xc * cos + partner * sin)
    return jnp.concatenate(outs, axis=1)


def _proj_body(x, mod_ref, g_ref, wa_ref, wb_ref, ws_ref, wst_ref, cos_ref, sin_ref,
               q_ref, k_ref, v_ref, kf_ref, vf_ref, z_ref, xbc_ref,
               mq_ref, mk_ref, mv_ref, mo_ref, sm_ref, smt_ref):
    mod = mod_ref[...]
    sh1 = mod[:, 0:D_MODEL]
    sc1 = mod[:, D_MODEL:2 * D_MODEL]
    h = _rms(x, g_ref[...]) * (1.0 + sc1) + sh1
    hb = h.astype(BF16)

    def proj(a, b):
        if b <= C_MQ:
            return _dot(hb, wa_ref[:, a:b])
        return _dot(hb, wb_ref[:, a - C_MQ:b - C_MQ])

    lane = lax.broadcasted_iota(jnp.int32, (TM, LANES), 1)
    first_half = (lane % 32) < 16
    cos = cos_ref[...]
    sin = sin_ref[...]
    q = proj(C_Q, C_K)
    k = proj(C_K, C_V)
    v = proj(C_V, C_Z)
    @pl.when(pl.program_id(0) < NT_CTX)
    def _():
        kf_ref[...] = k
        vf_ref[...] = v

    q_ref[...] = (_rope(q, cos, sin, first_half) * (DA_HD ** -0.5 * LOG2E)).astype(BF16)
    k_ref[...] = _rope(k, cos, sin, first_half).astype(BF16)
    v_ref[...] = v.astype(BF16)
    z_ref[...] = proj(C_Z, C_XBC)
    xbc_ref[...] = proj(C_XBC, C_MQ)
    mq_ref[...] = proj(C_MQ, C_MK).astype(BF16)
    mk_ref[...] = (proj(C_MK, C_MV) * (ML_DK ** -0.5)).astype(BF16)
    mv_ref[...] = proj(C_MV, C_MO).astype(BF16)
    mo_ref[...] = proj(C_MO, C_END)
    sm_ref[...] = _dot(hb, ws_ref[...])
    st = _dot_nt(wst_ref[...], hb)
    for j in range(TM // CHUNK):
        smt_ref[j] = st[:, CHUNK * j:CHUNK * (j + 1)]


def _proj_kernel_first(xp_ref, xs_ref, *refs):
    x = jnp.where(pl.program_id(0) < NT_CTX, xp_ref[...], xs_ref[...])
    refs[-1][...] = x
    _proj_body(x, *refs[:-1])


def _proj_kernel_next(x_ref, *refs):
    _proj_body(x_ref[...], *refs)


def _ctx_tile(w):
    return pl.BlockSpec((TM, w), lambda i: (jnp.minimum(i, NT_CTX - 1), 0))


def _smp_tile(w):
    return pl.BlockSpec((TM, w), lambda i: (jnp.maximum(i - NT_CTX, 0), 0))


def _projection(xs, mod_l, g1, wa, wb, ws, wst, cos_t, sin_t):
    tile = lambda w: pl.BlockSpec((TM, w), lambda i: (i, 0))
    full = lambda a: pl.BlockSpec(a.shape, lambda i: (0,) * a.ndim)
    modspec = pl.BlockSpec((None, 1, 6 * D_MODEL), lambda i: (_mod_row(i), 0, 0))
    first = isinstance(xs, tuple)
    if first:
        in_specs = [_ctx_tile(D_MODEL), _smp_tile(D_MODEL)]
        args = list(xs)
        kern = _proj_kernel_first
    else:
        in_specs = [tile(D_MODEL)]
        args = [xs]
        kern = _proj_kernel_next
    in_specs += [modspec, full(g1), full(wa), full(wb), full(ws), full(wst),
                 pl.BlockSpec((TM, LANES), lambda i: (_rope_blk(i), 0)),
                 pl.BlockSpec((TM, LANES), lambda i: (_rope_blk(i), 0))]
    args += [mod_l, g1, wa, wb, ws, wst, cos_t, sin_t]
    ctx_only = _ctx_tile(DA_WIDTH)
    out_specs = [tile(512), tile(512), tile(512), ctx_only, ctx_only, tile(512), tile(768),
                 tile(256), tile(256), tile(512), tile(512), tile(32),
                 pl.BlockSpec((TM // CHUNK, 32, CHUNK), lambda i: (i, 0, 0))]
    sds = jax.ShapeDtypeStruct
    out_shape = [sds((T_ALL, 512), BF16), sds((T_ALL, 512), BF16), sds((T_ALL, 512), BF16),
                 sds((T_CTX, 512), F32), sds((T_CTX, 512), F32),
                 sds((T_ALL, 512), F32), sds((T_ALL, 768), F32),
                 sds((T_ALL, 256), BF16), sds((T_ALL, 256), BF16), sds((T_ALL, 512), BF16),
                 sds((T_ALL, 512), F32), sds((T_ALL, 32), F32),
                 sds((T_ALL // CHUNK, 32, CHUNK), F32)]
    if first:
        out_specs.append(tile(D_MODEL))
        out_shape.append(sds((T_ALL, D_MODEL), F32))
    return pl.pallas_call(
        kern, grid=(NT_ALL,), in_specs=in_specs, out_specs=out_specs, out_shape=out_shape,
        compiler_params=_cparams(1), name="in_proj",
    )(*args)


def _attn_body(lam_init, q_ref, k_ref, v_ref, kc_ref, vc_ref, lp_ref, g_ref, o_ref):
    lp = lp_ref[...]
    s01 = jnp.sum(lp[0:1] * lp[1:2], axis=-1, keepdims=True)
    s23 = jnp.sum(lp[2:3] * lp[3:4], axis=-1, keepdims=True)
    lam = jnp.exp(s01) - jnp.exp(s23) + lam_init
    tq = q_ref.shape[0]
    lane = lax.broadcasted_iota(jnp.int32, (tq, LANES), 1)
    g = g_ref[...]
    def scores(h, m):
        cols = slice(LANES * h, LANES * (h + 1))
        qh = q_ref[:, cols]
        qm = jnp.where((lane < DA_HD) == (m == 0), qh, jnp.zeros_like(qh))
        s = _dot_nt(qm, k_ref[:, cols])
        sc = None if kc_ref is None else _dot_nt(qm, kc_ref[:, cols].astype(BF16))
        return s, sc

    units = [(h, m) for h in range(DA_HEADS) for m in range(2)]
    nxt = scores(*units[0])
    parts = []
    for u, (h, m) in enumerate(units):
        cols = slice(LANES * h, LANES * (h + 1))
        s, sc = nxt
        if u + 1 < len(units):
            nxt = scores(*units[u + 1])
        vh = jnp.concatenate([v_ref[:, cols], jnp.ones((k_ref.shape[0], LANES), BF16)], axis=1)
        mx = jnp.max(s, axis=-1, keepdims=True)
        if sc is not None:
            mx = jnp.maximum(mx, jnp.max(sc, axis=-1, keepdims=True))
        acc = _dot(jnp.exp2(s - mx).astype(BF16), vh)
        if sc is not None:
            vch = jnp.concatenate([vc_ref[:, cols].astype(BF16), jnp.ones((PAST_LEN, LANES), BF16)], axis=1)
            acc = acc + _dot(jnp.exp2(sc - mx).astype(BF16), vch)
        parts.append(acc[:, :LANES] / acc[:, LANES:])
        if m == 1:
            att = parts[-2] - lam * parts[-1]
            o_ref[:, cols] = (_rms(att, g) * (1.0 - lam_init)).astype(BF16)


def _attn_kernel_ctx(lam_init, q_ref, k_ref, v_ref, lp_ref, g_ref, o_ref):
    _attn_body(lam_init, q_ref, k_ref, v_ref, None, None, lp_ref, g_ref, o_ref)


def _attn_kernel_smp(lam_init, q_ref, k_ref, v_ref, kc_ref, vc_ref, lp_ref, g_ref, o_ref):
    _attn_body(lam_init, q_ref, k_ref, v_ref, kc_ref, vc_ref, lp_ref, g_ref, o_ref)


def _attention(l, q, k, v, cache_k, cache_v, lp, g, sample):
    lam_init = 0.8 - 0.6 * math.exp(-0.3 * l)
    full = lambda a: pl.BlockSpec(a.shape, lambda *_: (0,) * a.ndim)
    sds = jax.ShapeDtypeStruct
    if not sample:
        blk = pl.BlockSpec((SEQ, DA_WIDTH), lambda b: (b, 0))
        return pl.pallas_call(
            functools.partial(_attn_kernel_ctx, lam_init), grid=(BATCH,),
            in_specs=[blk, blk, blk, full(lp), full(g)],
            out_specs=blk, out_shape=sds((T_CTX, DA_WIDTH), BF16),
            compiler_params=_cparams(1), name="diff_attn_ctx",
        )(q, k, v, lp, g)
    tq = 256
    nq = DEC_SEQ // tq
    off_q = T_CTX // tq
    off_k = T_CTX // DEC_SEQ
    qblk = pl.BlockSpec((tq, DA_WIDTH), lambda b, i: (off_q + b * nq + i, 0))
    kblk = pl.BlockSpec((DEC_SEQ, DA_WIDTH), lambda b, i: (off_k + b, 0))
    cblk = pl.BlockSpec((None, None, PAST_LEN, DA_WIDTH), lambda b, i: (b, l, 0, 0))
    return pl.pallas_call(
        functools.partial(_attn_kernel_smp, lam_init), grid=(DEC_BATCH, nq),
        in_specs=[qblk, kblk, kblk, cblk, cblk, full(lp), full(g)],
        out_specs=pl.BlockSpec((tq, DA_WIDTH), lambda b, i: (b * nq + i, 0)),
        out_shape=sds((T_SMP, DA_WIDTH), BF16),
        compiler_params=_cparams(2), name="diff_attn_smp",
    )(q, k, v, cache_k, cache_v, lp, g)


def _tri_masks():
    r = lax.broadcasted_iota(jnp.int32, (CHUNK, CHUNK), 0)
    c = lax.broadcasted_iota(jnp.int32, (CHUNK, CHUNK), 1)
    return c <= r, c >= r


def _ssd_kernel(L, z_ref, xbc_ref, sm_ref, smt_ref, cw_ref, cb_ref, dtbr_ref, dtbc_ref,
                ar_ref, ac_ref, dexp_ref, ng_ref, h0_ref,
                y_ref, hfin_ref, xc_s, yacc_s, ht_s):
    nc = L // CHUNK
    low, upp = _tri_masks()
    tri_l3 = (_mask3(low, 1), _mask3(upp, 1))
    tri_r3 = (_mask3(upp, 0), _mask3(low, 0))
    lane512 = lax.broadcasted_iota(jnp.int32, (1, MB_INNER), 1)
    row16 = lax.broadcasted_iota(jnp.int32, (16, MB_INNER), 0)
    lane16 = lax.broadcasted_iota(jnp.int32, (16, MB_INNER), 1)
    expand = tuple(_mask3(row16 == 8 * d + lane16 // MB_HD, 0) for d in range(2))
    lane128 = lax.broadcasted_iota(jnp.int32, (CHUNK, LANES), 1)
    rowblk = lax.broadcasted_iota(jnp.int32, (LANES, MB_INNER), 0) // MB_STATE
    colblk = lax.broadcasted_iota(jnp.int32, (LANES, MB_INNER), 1) // (MB_INNER // MB_GROUPS)
    same_group = rowblk == colblk
    cw = cw_ref[...]
    cbias = cb_ref[...]

    def conv_chunk(c, _):
        base = pl.multiple_of(c * CHUNK, CHUNK)
        x = xbc_ref[pl.ds(base, CHUNK), :]
        prev = xbc_ref[pl.ds(jnp.maximum(base - 1, 0), 1), :]
        nxt = xbc_ref[pl.ds(jnp.minimum(base + CHUNK, L - 1), 1), :]
        prev = jnp.where(c == 0, 0.0, prev)
        nxt = jnp.where(c == nc - 1, 0.0, nxt)
        row = lax.broadcasted_iota(jnp.int32, (CHUNK, 1), 0)
        xp = jnp.where(row == 0, prev, pltpu.roll(x, 1, 0))
        xn = jnp.where(row == CHUNK - 1, nxt, pltpu.roll(x, CHUNK - 1, 0))
        conv = xp * cw[0:1] + x * cw[1:2] + xn * cw[2:3] + cbias
        xc_s[pl.ds(base, CHUNK), :] = _silu(conv)
        return 0

    lax.fori_loop(0, nc, conv_chunk, 0)

    h0 = h0_ref[...]
    for d in range(2):
        h0d = h0[d]
        ht_s[d] = jnp.concatenate(
            [jnp.where(lane512 < MB_INNER // 2, h0d, 0.0),
             jnp.where(lane512 >= MB_INNER // 2, h0d, 0.0)], axis=0)

    def both(i, _):
        jobs = ((0, i), (1, nc - 1 - i))
        rows, xsb, bb, cb, dt_row, cum_col, cum_row = {}, {}, {}, {}, {}, {}, {}
        y_off, cbg = {}, {}
        for d, c in jobs:
            rows[d] = pl.ds(pl.multiple_of(c * CHUNK, CHUNK), CHUNK)
            xs = xc_s[rows[d], 0:MB_INNER]
            bm = xc_s[rows[d], MB_INNER:MB_INNER + LANES]
            cm = xc_s[rows[d], MB_INNER + LANES:MB_CONV_DIM]
            dt_col = _softplus(sm_ref[rows[d], 0:16] + dtbr_ref[...])
            a_col = dt_col * ar_ref[...]
            dt_row[d] = _softplus(smt_ref[c][0:16, :] + dtbc_ref[...])
            a_row = dt_row[d] * ac_ref[...]
            cum_col[d] = _mask_dot(tri_l3[d], a_col)
            cum_row[d] = _dot_mask(a_row, tri_r3[d])
            last = 0 if d else CHUNK - 1
            cum_last = cum_col[d][last:last + 1, :]
            w_exp = _dot_mask(jnp.exp(cum_last - cum_col[d]) * dt_col, expand[d])
            g_exp = _dot_mask(jnp.exp(cum_col[d]), expand[d])
            cd_exp = _dot_mask(jnp.broadcast_to(jnp.exp(cum_last), (8, 16)), expand[d])[0:1]
            xw = (xs * w_exp).astype(BF16)
            xsb[d] = xs.astype(BF16)
            bb[d] = bm.astype(BF16)
            cb[d] = cm.astype(BF16)
            ht = ht_s[d]
            y_off[d] = _dot(cb[d], ht.astype(BF16)) * g_exp
            s_new = _dot(bm.T.astype(BF16), xw)
            ht_s[d] = ht * cd_exp + jnp.where(same_group, s_new, 0.0)
            for g in range(MB_GROUPS):
                cg = jnp.where((lane128 < MB_STATE) == (g == 0), cb[d], jnp.zeros_like(cb[d]))
                cbg[d, g] = _dot_nt(cg, bb[d])
        m16 = {}
        for d, _c in jobs:
            for h in range(MB_HEADS):
                ci = 8 * d + h
                seg = cum_col[d][:, ci:ci + 1] - cum_row[d][ci:ci + 1, :]
                m = jnp.where(upp if d else low, jnp.exp(seg), 0.0) * cbg[d, h // 4] * dt_row[d][ci:ci + 1, :]
                m16[d, h] = m.astype(BF16)
        yd = {}
        for d, _c in jobs:
            for h in range(MB_HEADS):
                k = h // 2
                yd[d, h] = _dot(m16[d, h], xsb[d][:, LANES * k:LANES * (k + 1)])
        for d, _c in jobs:
            pairs = [jnp.where(lane128 < MB_HD, yd[d, 2 * k], yd[d, 2 * k + 1]) for k in range(MB_HEADS // 2)]
            yacc_s[d, rows[d], :] = jnp.concatenate(pairs, axis=1) + y_off[d]
        return 0

    lax.fori_loop(0, nc, both, 0)

    dexp = dexp_ref[...]
    ng = ng_ref[...]

    def fin(c, _):
        rows = pl.ds(pl.multiple_of(c * CHUNK, CHUNK), CHUNK)
        y = yacc_s[0, rows, :] + yacc_s[1, rows, :] + dexp * xc_s[rows, 0:MB_INNER]
        y = y * _silu(z_ref[rows, :])
        y_ref[rows, :] = _rms(y, ng).astype(BF16)
        return 0

    lax.fori_loop(0, nc, fin, 0)
    for d in range(2):
        ht = ht_s[d]
        hfin_ref[d] = ht[0:MB_STATE, :] + ht[MB_STATE:2 * MB_STATE, :]


def _ssd(z, xbc, sm, smt, cw, cb, dtb, a_neg, dexp, ng, h0t, sample):
    nb, L, off = (DEC_BATCH, DEC_SEQ, T_CTX // DEC_SEQ) if sample else (BATCH, SEQ, 0)
    full = lambda a: pl.BlockSpec(a.shape, lambda b: (0,) * a.ndim)
    seq = lambda w: pl.BlockSpec((L, w), lambda b: (off + b, 0))
    dtb_r, dtb_c = dtb.reshape(1, 16), dtb.reshape(16, 1)
    a_r, a_c = a_neg.reshape(1, 16), a_neg.reshape(16, 1)
    sds = jax.ShapeDtypeStruct
    return pl.pallas_call(
        functools.partial(_ssd_kernel, L), grid=(nb,),
        in_specs=[seq(MB_INNER), seq(MB_CONV_DIM), seq(32),
                  pl.BlockSpec((L // CHUNK, 32, CHUNK), lambda b: (off + b, 0, 0)),
                  full(cw), full(cb), full(dtb_r), full(dtb_c), full(a_r), full(a_c),
                  full(dexp), full(ng),
                  pl.BlockSpec((None, 2, MB_STATE, MB_INNER), lambda b: (b, 0, 0, 0))],
        out_specs=[pl.BlockSpec((L, MB_INNER), lambda b: (b, 0)),
                   pl.BlockSpec((None, 2, MB_STATE, MB_INNER), lambda b: (b, 0, 0, 0))],
        out_shape=[sds((nb * L, MB_INNER), BF16), sds((nb, 2, MB_STATE, MB_INNER), F32)],
        scratch_shapes=[pltpu.VMEM((L, MB_CONV_DIM), F32), pltpu.VMEM((2, L, MB_INNER), F32),
                        pltpu.VMEM((2, LANES, MB_INNER), F32)],
        compiler_params=_cparams(1), name="ssd_smp" if sample else "ssd_ctx",
    )(z, xbc, sm, smt, cw, cb, dtb_r, dtb_c, a_r, a_c, dexp, ng, h0t)


def _mlstm_kernel(L, q_ref, k_ref, v_ref, o_ref, sm_ref, smt_ref, gbr_ref, gbc_ref, ng_ref,
                  c0_ref, n0_ref, m0_ref,
                  y_ref, cf_ref, nf_ref, mf_ref, hacc_s, c_s, n_s):
    nc = L // CHUNK
    low, upp = _tri_masks()
    tri_l = (low.astype(F32), upp.astype(F32))
    tri_r = (upp.astype(F32), low.astype(F32))
    lane128 = lax.broadcasted_iota(jnp.int32, (CHUNK, LANES), 1)
    lane8 = lax.broadcasted_iota(jnp.int32, (1, 2 * ML_HEADS), 1)
    neg_inf = -jnp.inf

    ones16 = jnp.ones((CHUNK, LANES), BF16)
    c_s[...] = c0_ref[...]
    n_s[...] = jnp.broadcast_to(n0_ref[...], n_s.shape)
    m0 = m0_ref[...]

    def both(i, m_in):
        jobs = ((0, i), (1, nc - 1 - i))
        heads = [(d, h) for d, _ in jobs for h in range(ML_HEADS)]
        rows, pre_row, b_col, b_row = {}, {}, {}, {}
        for d, c in jobs:
            rows[d] = pl.ds(pl.multiple_of(c * CHUNK, CHUNK), CHUNK)
            pre_col = sm_ref[rows[d], 16:32] + gbr_ref[...]
            pre_row[d] = smt_ref[c][16:32, :] + gbc_ref[...]
            lf_col = -_softplus(-pre_col)
            lf_row = -_softplus(-pre_row[d])
            b_col[d] = _dot_hi(tri_l[d], lf_col)
            b_row[d] = _dot_hi(lf_row, tri_r[d])
        kt, cst, nst, vh, qk, qc, qn = {}, {}, {}, {}, {}, {}, {}
        for d, _ in jobs:
            for pr in range(2):
                qp = q_ref[rows[d], LANES * pr:LANES * (pr + 1)]
                kp = k_ref[rows[d], LANES * pr:LANES * (pr + 1)]
                kt[d, pr] = kp.astype(F32).T
                cst[d, pr] = c_s[d, pr]
                nst[d, pr] = n_s[d, pr]
                cb16 = cst[d, pr].astype(BF16)
                nb16 = nst[d, pr].astype(BF16)
                for hh in range(2):
                    h = 2 * pr + hh
                    qm = jnp.where((lane128 < ML_DK) == (hh == 0), qp, jnp.zeros_like(qp))
                    vh[d, h] = jnp.concatenate([v_ref[rows[d], ML_DV * h:ML_DV * (h + 1)], ones16], axis=1)
                    qk[d, h] = _dot_nt(qm, kp)
                    qc[d, h] = _dot(qm, cb16)
                    qn[d, h] = _dot(qm, nb16)
        m_t, s_intra, s_inter, m_new, s_old, kts = {}, {}, {}, {}, {}, {}
        for d, h in heads:
            m_st = m_in[:, 4 * d + h:4 * d + h + 1]
            bcol = jnp.broadcast_to(b_col[d][:, 8 * d + 4 + h:8 * d + 5 + h], (CHUNK, CHUNK))
            brow = b_row[d][8 * d + 4 + h:8 * d + 5 + h, :]
            li_row = pre_row[d][8 * d + h:8 * d + h + 1, :]
            dm = jnp.where(upp if d else low, bcol - brow + li_row, neg_inf)
            inter = bcol + m_st
            m_t[d, h] = jnp.maximum(inter, jnp.max(dm, axis=-1, keepdims=True))
            s_intra[d, h] = jnp.exp(dm - m_t[d, h]) * qk[d, h]
            s_inter[d, h] = jnp.exp(inter - m_t[d, h])
            last = 0 if d else CHUNK - 1
            b_end = brow[:, last:last + 1]
            w_end = b_end - brow + li_row
            m_new[d, h] = jnp.maximum(b_end + m_st, jnp.max(w_end, axis=-1, keepdims=True))
            s_old[d, h] = jnp.exp(b_end + m_st - m_new[d, h])
            half = slice(ML_DK * (h % 2), ML_DK * (h % 2 + 1))
            kts[d, h] = kt[d, h // 2][half, :] * jnp.exp(w_end - m_new[d, h])
        pv, kv = {}, {}
        for d, h in heads:
            pv[d, h] = _dot(s_intra[d, h].astype(BF16), vh[d, h])
            kv[d, h] = _dot(kts[d, h].astype(BF16), vh[d, h])
        m_out = m_in
        for d, h in heads:
            num = s_inter[d, h] * qc[d, h] + pv[d, h][:, :ML_DV]
            den = s_inter[d, h] * qn[d, h] + pv[d, h][:, ML_DV:]
            hout = num / jnp.maximum(jnp.abs(den), jnp.exp(-m_t[d, h]))
            hacc_s[d, rows[d], ML_DV * h:ML_DV * (h + 1)] = hout
            m_out = jnp.where(lane8 == 4 * d + h, m_new[d, h], m_out)
        for d, _ in jobs:
            for pr in range(2):
                c_new, n_new = [], []
                for hh in range(2):
                    h = 2 * pr + hh
                    half = slice(ML_DK * hh, ML_DK * (hh + 1))
                    c_new.append(s_old[d, h] * cst[d, pr][half, :] + kv[d, h][:, :ML_DV])
                    n_new.append(s_old[d, h] * nst[d, pr][half, :] + kv[d, h][:, ML_DV:])
                c_s[d, pr] = jnp.concatenate(c_new, axis=0)
                n_s[d, pr] = jnp.concatenate(n_new, axis=0)
        return m_out

    m_fin = lax.fori_loop(0, nc, both, m0)

    ng = ng_ref[...]

    def fin(c, _):
        rows = pl.ds(pl.multiple_of(c * CHUNK, CHUNK), CHUNK)
        for h in range(ML_HEADS):
            cols = slice(ML_DV * h, ML_DV * (h + 1))
            y = _rms(hacc_s[0, rows, cols] + hacc_s[1, rows, cols], ng[:, cols]) * jax.nn.sigmoid(o_ref[rows, cols])
            y_ref[rows, cols] = y.astype(BF16)
        return 0

    lax.fori_loop(0, nc, fin, 0)
    cf_ref[...] = c_s[...]
    nf_ref[...] = n_s[:, :, :, 0:1]
    mf_ref[...] = m_fin


def _mlstm(q, k, v, o, sm, smt, gb, ng, c0, n0, m0, sample):
    nb, L, off = (DEC_BATCH, DEC_SEQ, T_CTX // DEC_SEQ) if sample else (BATCH, SEQ, 0)
    full = lambda a: pl.BlockSpec(a.shape, lambda b: (0,) * a.ndim)
    seq = lambda w: pl.BlockSpec((L, w), lambda b: (off + b, 0))
    gb_r, gb_c = gb.reshape(1, 16), gb.reshape(16, 1)
    st_c = pl.BlockSpec((None, 2, 2, LANES, ML_DV), lambda b: (b, 0, 0, 0, 0))
    st_n = pl.BlockSpec((None, 2, 2, LANES, 1), lambda b: (b, 0, 0, 0, 0))
    st_m = pl.BlockSpec((None, 1, 8), lambda b: (b, 0, 0))
    sds = jax.ShapeDtypeStruct
    return pl.pallas_call(
        functools.partial(_mlstm_kernel, L), grid=(nb,),
        in_specs=[seq(256), seq(256), seq(512), seq(512), seq(32),
                  pl.BlockSpec((L // CHUNK, 32, CHUNK), lambda b: (off + b, 0, 0)),
                  full(gb_r), full(gb_c), full(ng), st_c, st_n, st_m],
        out_specs=[pl.BlockSpec((L, ML_WIDTH), lambda b: (b, 0)), st_c, st_n, st_m],
        out_shape=[sds((nb * L, ML_WIDTH), BF16), sds((nb, 2, 2, LANES, ML_DV), F32),
                   sds((nb, 2, 2, LANES, 1), F32), sds((nb, 1, 8), F32)],
        scratch_shapes=[pltpu.VMEM((2, L, ML_WIDTH), F32), pltpu.VMEM((2, 2, LANES, ML_DV), F32),
                        pltpu.VMEM((2, 2, LANES, LANES), F32)],
        compiler_params=_cparams(1), name="mlstm_smp" if sample else "mlstm_ctx",
    )(q, k, v, o, sm, smt, gb_r, gb_c, ng, c0, n0, m0)


def _merge_kernel(x_ref, mod_ref, g1_ref, ydac_ref, ymbc_ref, ymlc_ref, ydas_ref, ymbs_ref, ymls_ref,
                  wg_ref, bg_ref, wb_ref, wo_ref, g2_ref, rwt_ref, xo_ref, h2t_ref, affb_ref):
    x = x_ref[...]
    mod = mod_ref[...]
    sh1, sc1, gt1, sh2, sc2 = (mod[:, j * D_MODEL:(j + 1) * D_MODEL] for j in range(5))
    hb = (_rms(x, g1_ref[...]) * (1.0 + sc1) + sh1).astype(BF16)
    is_ctx = pl.program_id(0) < T_CTX // TM_MERGE
    merged = None
    for n, (yc_ref, ys_ref) in enumerate(((ydac_ref, ydas_ref), (ymbc_ref, ymbs_ref), (ymlc_ref, ymls_ref))):
        cols = slice(n * D_MODEL, (n + 1) * D_MODEL)
        gate = jax.nn.sigmoid(_dot(hb, wg_ref[:, cols]) + bg_ref[:, cols])
        y = jnp.where(is_ctx, yc_ref[...], ys_ref[...])
        term = gate * _dot(y, wb_ref[n])
        merged = term if merged is None else merged + term
    out = _dot(merged.astype(BF16), wo_ref[...])
    xn = x + gt1 * out
    xo_ref[...] = xn
    h2 = _rms(xn, g2_ref[...]) * (1.0 + sc2) + sh2
    h2t_ref[...] = h2.T.astype(BF16)
    logits = lax.dot_general(rwt_ref[...], h2, (((1,), (1,)), ((), ())),
                             precision=HI, preferred_element_type=F32)
    e = jnp.exp(logits - jnp.max(logits, axis=0, keepdims=True))
    aff = e / jnp.sum(e, axis=0, keepdims=True)
    for j in range(TM_MERGE // CHUNK):
        affb_ref[j] = aff[:, CHUNK * j:CHUNK * (j + 1)]


def _merge(x, mod_l, g1, ys_ctx, ys_smp, wg, bg, wb, wo, g2, rwt):
    tm = TM_MERGE
    n_ctx = T_CTX // tm
    per_req = DEC_SEQ // tm
    tile = lambda w: pl.BlockSpec((tm, w), lambda i: (i, 0))
    ctx_tile = pl.BlockSpec((tm, BRANCH_W), lambda i: (jnp.minimum(i, n_ctx - 1), 0))
    smp_tile = pl.BlockSpec((tm, BRANCH_W), lambda i: (jnp.maximum(i - n_ctx, 0), 0))
    full = lambda a: pl.BlockSpec(a.shape, lambda i: (0,) * a.ndim)
    mod_row = lambda i: jnp.where(i < n_ctx, 0, 1 + (i - n_ctx) // per_req)
    sds = jax.ShapeDtypeStruct
    return pl.pallas_call(
        _merge_kernel, grid=(T_ALL // tm,),
        in_specs=[tile(D_MODEL), pl.BlockSpec((None, 1, 6 * D_MODEL), lambda i: (mod_row(i), 0, 0)),
                  full(g1), ctx_tile, ctx_tile, ctx_tile, smp_tile, smp_tile, smp_tile,
                  full(wg), full(bg), full(wb), full(wo), full(g2), full(rwt)],
        out_specs=[tile(D_MODEL), pl.BlockSpec((D_MODEL, tm), lambda i: (0, i)),
                   pl.BlockSpec((tm // CHUNK, N_EXPERTS, CHUNK), lambda i: (i, 0, 0))],
        out_shape=[sds((T_ALL, D_MODEL), F32), sds((D_MODEL, T_ALL), BF16),
                   sds((NB_ALL, N_EXPERTS, CHUNK), F32)],
        compiler_params=_cparams(1), name="merge_out",
    )(x, mod_l, g1, *ys_ctx, *ys_smp, wg, bg, wb, wo, g2, rwt)


def _route_kernel(affb_ref, slotb_ref, slott_ref, a_ref, jlo_ref, jhi_ref, acc_s, run_s):
    r = lax.broadcasted_iota(jnp.int32, (CHUNK, CHUNK), 0)
    c = lax.broadcasted_iota(jnp.int32, (CHUNK, CHUNK), 1)
    upper = jnp.where(r <= c, 1.0, 0.0).astype(BF16)
    eye = jnp.where(r == c, 1.0, 0.0).astype(BF16)
    eye2 = jnp.concatenate([eye, eye], axis=1)
    lane = lax.broadcasted_iota(jnp.int32, (N_EXPERTS, LANES), 1)
    acc_s[...] = jnp.zeros_like(acc_s)
    run_s[...] = jnp.zeros_like(run_s)
    for b0, b1, cap in ((0, NB_CTX, CAP_CTX), (NB_CTX, NB_ALL, CAP_SMP)):
        aff = affb_ref[b0:b1]

        def search(i, thr_bits):
            cand = thr_bits | lax.shift_left(jnp.int32(1), 30 - i)
            cnt = jnp.sum((aff >= pltpu.bitcast(cand, F32)[None]).astype(jnp.int32), axis=0)
            cnt = jnp.sum(cnt, axis=1, keepdims=True)
            return jnp.where(cnt >= cap, cand, thr_bits)

        thr = pltpu.bitcast(lax.fori_loop(0, 31, search, jnp.zeros((N_EXPERTS, 1), jnp.int32)), F32)
        n_gt = jnp.sum(jnp.sum((aff > thr[None]).astype(jnp.int32), axis=0), axis=1, keepdims=True)
        need = (cap - n_gt).astype(F32)

        run_s[1] = jnp.zeros((N_EXPERTS, 1), F32)

        def blocks(gi, _):
            b = pl.multiple_of(b0 + gi * ROUTE_GROUP, ROUTE_GROUP)
            run_sel = run_s[0]
            run_eq = run_s[1]
            xs = [affb_ref[b + i] for i in range(ROUTE_GROUP)]
            eqs = [x == thr for x in xs]
            eq_fs = [jnp.where(eq, 1.0, 0.0) for eq in eqs]
            eq_incls = [_dot(f.astype(BF16), upper) for f in eq_fs]
            sels = []
            for x, eq, f, incl in zip(xs, eqs, eq_fs, eq_incls):
                sels.append((x > thr) | (eq & (run_eq + incl - f < need)))
                run_eq = run_eq + incl[:, CHUNK - 1:CHUNK]
            sel_fs = [jnp.where(sel, 1.0, 0.0) for sel in sels]
            sel_incls = [_dot(f.astype(BF16), upper) for f in sel_fs]
            acc = acc_s[...]
            slots = []
            for i, (sel, f, incl) in enumerate(zip(sels, sel_fs, sel_incls)):
                slots.append(jnp.where(sel, run_sel + incl - f, -1.0))
                if i % 2 == 0:
                    acc = jnp.where(lane == b // 2 + i // 2, run_sel, acc)
                run_sel = run_sel + incl[:, CHUNK - 1:CHUNK]
            acc_s[...] = acc
            run_s[0] = run_sel
            run_s[1] = run_eq
            for i, slot in enumerate(slots):
                slotb_ref[b + i] = slot.astype(jnp.int32)
                hi64 = jnp.floor(slot * (1.0 / 64.0)) * 64.0
                parts = jnp.concatenate([hi64, slot - hi64], axis=1).astype(BF16)
                slott_ref[pl.ds(pl.multiple_of((b + i) * CHUNK, CHUNK), CHUNK), :] = _dot_nt(eye2, parts)
            return 0

        lax.fori_loop(0, (b1 - b0) // ROUTE_GROUP, blocks, 0)
    a_acc = jnp.where(lane == NT_ALL, run_s[0], acc_s[...])
    a_ref[...] = a_acc.astype(jnp.int32)
    a_next = pltpu.roll(a_acc, LANES - 1, 1)
    tile_ok = lane < NT_ALL
    jlo = jnp.zeros((N_EXPERTS, LANES), jnp.int32)
    jhi = jnp.zeros((N_EXPERTS, LANES), jnp.int32)
    for k in range(N_RB):
        lo_k = jnp.sum((tile_ok & (a_next <= float(RB * k))).astype(jnp.int32), axis=1, keepdims=True)
        hi_k = jnp.sum((tile_ok & (a_acc < float(RB * (k + 1)))).astype(jnp.int32), axis=1, keepdims=True) - 1
        jlo = jnp.where(lane == k, lo_k, jlo)
        jhi = jnp.where(lane == k, hi_k, jhi)
    jlo_ref[...] = jlo
    jhi_ref[...] = jhi


def _route(affb):
    sds = jax.ShapeDtypeStruct
    small = sds((N_EXPERTS, LANES), jnp.int32)
    return pl.pallas_call(
        _route_kernel,
        out_shape=[sds((NB_ALL, N_EXPERTS, CHUNK), jnp.int32), sds((T_ALL, N_EXPERTS), F32),
                   small, small, small],
        scratch_shapes=[pltpu.VMEM((N_EXPERTS, LANES), F32), pltpu.VMEM((2, N_EXPERTS, 1), F32)],
        compiler_params=pltpu.CompilerParams(vmem_limit_bytes=VMEM_LIMIT), name="route",
    )(affb)


def _expert_kernel(jlo_ref, jhi_ref, h2t_hbm, slotb_ref, affb_ref, wg_ref, wu_ref, wd_ref, o_ref,
                   wg_s, wu_s, wd_s, chunk_s, sem, acc_s, g_s, cnt_s):
    e = pl.program_id(0)
    k = pl.program_id(1)
    step = e * N_RB + k

    @pl.when(k == 0)
    def _():
        wg_s[...] = wg_ref[...].astype(BF16)
        wu_s[...] = wu_ref[...].astype(BF16)
        wd_s[...] = wd_ref[...].astype(BF16)

    def tile_range(s):
        i = (s // N_RB) * LANES + s % N_RB
        lo = jnp.clip(jlo_ref[i], 0, NT_ALL - 1)
        return lo, jnp.clip(jhi_ref[i] - lo + 1, 1, NT_ALL - lo)

    def chunk_tile(lo, c):
        return jnp.minimum(lo + GATHER_TILES * c, NT_ALL - GATHER_TILES)

    def chunk_copy(j0, buf):
        return pltpu.make_async_copy(
            h2t_hbm.at[:, pl.ds(pl.multiple_of(j0 * TM, TM), GATHER_TILES * TM)],
            chunk_s.at[buf], sem.at[buf])

    def n_chunks(s):
        return (tile_range(s)[1] + GATHER_TILES - 1) // GATHER_TILES

    n_steps = N_EXPERTS * N_RB

    def advance(s, c):
        wrap = (c + 1 >= n_chunks(jnp.minimum(s, n_steps - 1))) | (s >= n_steps)
        return jnp.where(wrap, s + 1, s), jnp.where(wrap, 0, c + 1)

    def start_at(s, c, buf):
        @pl.when(s < n_steps)
        def _():
            chunk_copy(chunk_tile(tile_range(jnp.minimum(s, n_steps - 1))[0], c), buf).start()

    jlo, n = tile_range(step)
    nch = n_chunks(step)

    @pl.when(step == 0)
    def _():
        cnt_s[0] = 0
        pos = (step, 0)
        for i in range(GATHER_AHEAD):
            start_at(pos[0], pos[1], i)
            pos = advance(*pos)

    done = cnt_s[0]
    acc_s[...] = jnp.zeros_like(acc_s)
    g_s[...] = jnp.zeros_like(g_s)
    ntok = GATHER_TILES * TM
    want = lax.broadcasted_iota(jnp.int32, (RB, ntok), 0) + k * RB
    lane_tile = lax.broadcasted_iota(jnp.int32, (1, ntok), 1) // TM

    def body(c, _):
        buf = (done + c) % (GATHER_AHEAD + 1)
        j0 = chunk_tile(jlo, c)
        chunk_copy(j0, buf).wait()
        pos = (step, c)
        for _i in range(GATHER_AHEAD):
            pos = advance(*pos)
        start_at(pos[0], pos[1], (done + c + GATHER_AHEAD) % (GATHER_AHEAD + 1))

        nblk = ntok // CHUNK
        srow = jnp.concatenate([slotb_ref[2 * j0 + i, pl.ds(e, 1), :] for i in range(nblk)], axis=1)
        arow = jnp.concatenate([affb_ref[2 * j0 + i, pl.ds(e, 1), :] for i in range(nblk)], axis=1)
        fresh = j0 + lane_tile >= jlo + GATHER_TILES * c
        hit = (srow == want) & fresh
        onehot = jnp.where(hit, 1.0, 0.0).astype(BF16)
        acc_s[...] += _dot_nt(chunk_s[buf], onehot)
        g_s[...] += jnp.sum(jnp.where(hit, arow, 0.0), axis=1, keepdims=True)
        return 0

    lax.fori_loop(0, nch, body, 0)
    cnt_s[0] = done + nch
    xe = acc_s[...].T.astype(BF16)
    hid = _silu(_dot(xe, wg_s[...])) * _dot(xe, wu_s[...])
    o_ref[...] = (_dot(hid.astype(BF16), wd_s[...]) * g_s[...]).astype(BF16)


def _experts(l, jlo, jhi, h2t, slotb, affb, w_gate, w_up, w_down):
    wsp = pl.BlockSpec((None, None, D_MODEL, EXPERT_FF), lambda e, j, *_: (l, e, 0, 0))
    wsd = pl.BlockSpec((None, None, EXPERT_FF, D_MODEL), lambda e, j, *_: (l, e, 0, 0))
    whole = lambda a: pl.BlockSpec(a.shape, lambda e, j, *_: (0,) * a.ndim)
    grid_spec = pltpu.PrefetchScalarGridSpec(
        num_scalar_prefetch=2, grid=(N_EXPERTS, N_RB),
        in_specs=[pl.BlockSpec(memory_space=pl.ANY), whole(slotb), whole(affb), wsp, wsp, wsd],
        out_specs=pl.BlockSpec((None, RB, D_MODEL), lambda e, j, *_: (e, j, 0)),
        scratch_shapes=[pltpu.VMEM((D_MODEL, EXPERT_FF), BF16), pltpu.VMEM((D_MODEL, EXPERT_FF), BF16),
                        pltpu.VMEM((EXPERT_FF, D_MODEL), BF16),
                        pltpu.VMEM((GATHER_AHEAD + 1, D_MODEL, GATHER_TILES * TM), BF16),
                        pltpu.SemaphoreType.DMA((GATHER_AHEAD + 1,)), pltpu.VMEM((D_MODEL, RB), F32),
                        pltpu.VMEM((RB, 1), F32), pltpu.SMEM((1,), jnp.int32)])
    return pl.pallas_call(
        _expert_kernel, grid_spec=grid_spec,
        out_shape=jax.ShapeDtypeStruct((N_EXPERTS, CAP_ALL, D_MODEL), BF16),
        compiler_params=_cparams(2), name="expert_ffn",
    )(jlo, jhi, h2t, slotb, affb, w_gate, w_up, w_down)


def _combine_body(a_ref, ye_hbm, slott_ref, win_s, sem, xwin_s, xsem, o_ref):
    j = pl.program_id(0)
    nt = pl.num_programs(0)

    def first_row(e, jj):
        a = a_ref[e * LANES + jj]
        return pl.multiple_of(jnp.clip((a // BF16_ROWS) * BF16_ROWS, 0, CAP_ALL - WIN), BF16_ROWS)

    def win_copy(e, row0, buf):
        return pltpu.make_async_copy(ye_hbm.at[e, pl.ds(row0, WIN), :], win_s.at[buf, e], sem.at[buf, e])

    buf = j % 2

    @pl.when(j == 0)
    def _():
        for e in range(N_EXPERTS):
            win_copy(e, first_row(e, 0), 0).start()

    @pl.when(j + 1 < nt)
    def _():
        for e in range(N_EXPERTS):
            win_copy(e, first_row(e, j + 1), 1 - buf).start()

    lane = lax.broadcasted_iota(jnp.int32, (TM, LANES), 1)
    lane_f = lane.astype(F32)
    rows0 = []
    pieces = []
    for e in range(0, N_EXPERTS, LANES // WIN):
        tgt = None
        for i in range(LANES // WIN):
            row0 = first_row(e + i, j)
            win_copy(e + i, row0, buf).wait()
            rows0.append(row0)
            t_i = slott_ref[:, e + i:e + i + 1] - (row0 - WIN * i).astype(F32)
            in_win = (lane >= WIN * i) & (lane < WIN * (i + 1))
            tgt = jnp.where(in_win, t_i, -1.0) if tgt is None else jnp.where(in_win, t_i, tgt)
        pieces.append(jnp.where(tgt == lane_f, 1.0, 0.0).astype(BF16))
    onehot = jnp.concatenate(pieces, axis=1)
    o_ref[...] = _dot(onehot, win_s[buf].reshape(N_EXPERTS * WIN, D_MODEL))

    lane_w = lax.broadcasted_iota(jnp.int32, (TM, WIN), 1).astype(F32)
    for e in range(N_EXPERTS):
        row0 = rows0[e]
        n_more = jnp.maximum((a_ref[e * LANES + j + 1] - row0 + WIN - 1) // WIN - 1, 0)

        def more(i, _):
            lo_slot = row0 + (i + 1) * WIN
            r = pl.multiple_of(jnp.minimum(lo_slot, CAP_ALL - WIN), BF16_ROWS)
            cp = pltpu.make_async_copy(ye_hbm.at[e, pl.ds(r, WIN), :], xwin_s, xsem)
            cp.start()
            cp.wait()
            scol = slott_ref[:, e:e + 1]
            scol = jnp.where(scol >= lo_slot.astype(F32), scol, -1.0)
            oh = jnp.where(scol - r.astype(F32) == lane_w, 1.0, 0.0).astype(BF16)
            o_ref[...] += _dot(oh, xwin_s[...])
            return 0

        lax.fori_loop(0, n_more, more, 0)


def _combine_kernel_mid(a_ref, ye_hbm, slott_ref, x_ref, mod_ref, xo_ref, moe_s, win_s, sem, xwin_s, xsem):
    _combine_body(a_ref, ye_hbm, slott_ref, win_s, sem, xwin_s, xsem, moe_s)
    xo_ref[...] = x_ref[...] + mod_ref[...][:, 5 * D_MODEL:6 * D_MODEL] * moe_s[...]


def _combine_kernel_last(a_ref, ye_hbm, slott_ref, x_ref, mod_ref, fg_ref, yp_ref, ys_ref,
                         moe_s, win_s, sem, xwin_s, xsem):
    _combine_body(a_ref, ye_hbm, slott_ref, win_s, sem, xwin_s, xsem, moe_s)
    y = _rms(x_ref[...] + mod_ref[...][:, 5 * D_MODEL:6 * D_MODEL] * moe_s[...], fg_ref[...])

    @pl.when(pl.program_id(0) < NT_CTX)
    def _():
        yp_ref[...] = y

    @pl.when(pl.program_id(0) >= NT_CTX)
    def _():
        ys_ref[...] = y


def _combine(a, ye, slott, x, mod_l, fg):
    tile = lambda w: pl.BlockSpec((TM, w), lambda i, *_: (i, 0))
    in_specs = [pl.BlockSpec(memory_space=pl.ANY), tile(N_EXPERTS), tile(D_MODEL),
                pl.BlockSpec((None, 1, 6 * D_MODEL), lambda i, *_: (_mod_row(i), 0, 0))]
    args = [a, ye, slott, x, mod_l]
    sds = jax.ShapeDtypeStruct
    if fg is None:
        kern, out_specs, out_shape = _combine_kernel_mid, tile(D_MODEL), sds((T_ALL, D_MODEL), F32)
    else:
        kern = _combine_kernel_last
        in_specs.append(pl.BlockSpec(fg.shape, lambda i, *_: (0, 0)))
        args.append(fg)
        out_specs = [pl.BlockSpec((TM, D_MODEL), lambda i, *_: (jnp.minimum(i, NT_CTX - 1), 0)),
                     pl.BlockSpec((TM, D_MODEL), lambda i, *_: (jnp.maximum(i - NT_CTX, 0), 0))]
        out_shape = [sds((T_CTX, D_MODEL), F32), sds((T_SMP, D_MODEL), F32)]
    grid_spec = pltpu.PrefetchScalarGridSpec(
        num_scalar_prefetch=1, grid=(NT_ALL,), in_specs=in_specs, out_specs=out_specs,
        scratch_shapes=[pltpu.VMEM((TM, D_MODEL), F32),
                        pltpu.VMEM((2, N_EXPERTS, WIN, D_MODEL), BF16),
                        pltpu.SemaphoreType.DMA((2, N_EXPERTS)),
                        pltpu.VMEM((WIN, D_MODEL), BF16), pltpu.SemaphoreType.DMA(())])
    return pl.pallas_call(
        kern, grid_spec=grid_spec, out_shape=out_shape,
        compiler_params=_cparams(1), name="moe_combine",
    )(*args)


def _rope_tables():
    t = jnp.arange(DEC_SEQ)
    pos = jnp.stack([t // GRID_W, t % GRID_W], axis=-1).astype(F32)
    nf = DA_HD // 4
    inv = ROPE_BASE ** (-jnp.arange(nf, dtype=F32) / nf)
    ang = pos[:, :, None] * inv
    cos = jnp.cos(ang)
    sin = jnp.sin(ang)
    cos64 = jnp.stack([cos, cos], axis=2).reshape(DEC_SEQ, DA_HD)
    sin64 = jnp.stack([-sin, sin], axis=2).reshape(DEC_SEQ, DA_HD)
    cos_t = jnp.concatenate([jnp.ones((TM, LANES), F32), jnp.tile(cos64, (1, 2))], axis=0)
    sin_t = jnp.concatenate([jnp.zeros((TM, LANES), F32), jnp.tile(sin64, (1, 2))], axis=0)
    return cos_t, sin_t


def kernel(x_prompt, x_sample, cache_k, cache_v, state_ssm, state_mlstm_c, state_mlstm_n, state_mlstm_m, c, c_ctx, ada_w, ada_b, norm1_g, norm2_g, w_in, da_lambda, da_subln_g, mb_conv_w, mb_conv_b, mb_dt_bias, mb_a_log, mb_d, mb_norm_g, ml_gate_b, ml_norm_g, w_branch, w_mgate, b_mgate, w_out, router_w, ex_w_gate, ex_w_up, ex_w_down, final_g):
    x = (x_prompt.reshape(T_CTX, D_MODEL), x_sample.reshape(T_SMP, D_MODEL))
    cc = jnp.concatenate([c_ctx[None, :], c, jnp.zeros((16 - 1 - DEC_BATCH, D_MODEL), F32)], axis=0)
    mod = _modulation(cc, ada_w, ada_b).reshape(DEPTH, 16, 1, 6 * D_MODEL)
    cos_t, sin_t = _rope_tables()
    cache_k2 = cache_k.reshape(DEC_BATCH, DEPTH, PAST_LEN, DA_WIDTH)
    cache_v2 = cache_v.reshape(DEC_BATCH, DEPTH, PAST_LEN, DA_WIDTH)

    outs = {n: [] for n in ("k", "v", "ssm", "C", "n", "m")}
    fg = final_g[None]
    for l in range(DEPTH):
        w = w_in[l]
        wa = w[:, :2816].astype(BF16)
        wb = w[:, 2832:4368].astype(BF16)
        ws = jnp.concatenate([w[:, 2816:2832], w[:, 4368:4384]], axis=1).astype(BF16)
        res = _projection(x, mod[l], norm1_g[l][None], wa, wb, ws, ws.T, cos_t, sin_t)
        q, k, v, kf, vf, z, xbc, mq, mk, mv, mo, sm, smt = res[:13]
        if l == 0:
            x = res[13]
        outs["k"].append(kf.reshape(BATCH, SEQ, DA_HEADS, 2, DA_HD))
        outs["v"].append(vf.reshape(BATCH, SEQ, DA_HEADS, 2 * DA_HD))

        lp = da_lambda[l]
        sg = da_subln_g[l][None]
        cw = mb_conv_w[l]
        cb = mb_conv_b[l][None]
        dtb = mb_dt_bias[l].reshape(16)
        a_neg = -jnp.exp(mb_a_log[l]).reshape(16)
        dexp = jnp.repeat(mb_d[l], MB_HD)[None]
        mng = mb_norm_g[l][None]
        gb = ml_gate_b[l].reshape(16)
        lng = ml_norm_g[l].reshape(1, ML_WIDTH)
        ys = []
        for sample in (False, True):
            nb = DEC_BATCH if sample else BATCH
            y_da = _attention(l, q, k, v, cache_k2, cache_v2, lp, sg, sample)
            if sample:
                h0 = state_ssm[:, l]
                c0 = state_mlstm_c[:, l]
                n0 = state_mlstm_n[:, l]
                m0 = state_mlstm_m[:, l]
            else:
                h0 = jnp.zeros((nb, 2, MB_HEADS, MB_HD, MB_STATE), F32)
                c0 = jnp.zeros((nb, 2, ML_HEADS, ML_DK, ML_DV), F32)
                n0 = jnp.zeros((nb, 2, ML_HEADS, ML_DK), F32)
                m0 = jnp.zeros((nb, 2, ML_HEADS), F32)
            h0t = jnp.transpose(h0, (0, 1, 4, 2, 3)).reshape(nb, 2, MB_STATE, MB_INNER)
            y_mb, hfin = _ssd(z, xbc, sm, smt, cw, cb, dtb, a_neg, dexp, mng, h0t, sample)
            y_ml, cfin, nfin, mfin = _mlstm(
                mq, mk, mv, mo, sm, smt, gb, lng,
                c0.reshape(nb, 2, 2, LANES, ML_DV), n0.reshape(nb, 2, 2, LANES, 1),
                m0.reshape(nb, 1, 8), sample)
            ys.append((y_da, y_mb, y_ml))
            if not sample:
                outs["ssm"].append(jnp.transpose(
                    hfin.reshape(nb, 2, MB_STATE, MB_HEADS, MB_HD), (0, 1, 3, 4, 2)))
                outs["C"].append(cfin.reshape(nb, 2, ML_HEADS, ML_DK, ML_DV))
                outs["n"].append(nfin.reshape(nb, 2, ML_HEADS, ML_DK))
                outs["m"].append(mfin.reshape(nb, 2, ML_HEADS))
        x, h2t, affb = _merge(x, mod[l], norm1_g[l][None], ys[0], ys[1],
                              w_mgate[l].astype(BF16), b_mgate[l][None], w_branch[l].astype(BF16),
                              w_out[l].astype(BF16), norm2_g[l][None], router_w[l].T)
        slotb, slott, a_cnt, jlo, jhi = _route(affb)
        ye = _experts(l, jlo.reshape(-1), jhi.reshape(-1), h2t, slotb, affb,
                      ex_w_gate, ex_w_up, ex_w_down)
        if l + 1 < DEPTH:
            x = _combine(a_cnt.reshape(-1), ye, slott, x, mod[l], None)
        else:
            y_prompt, y_sample = _combine(a_cnt.reshape(-1), ye, slott, x, mod[l], fg)

    y_prompt = y_prompt.reshape(BATCH, SEQ, D_MODEL)
    y_sample = y_sample.reshape(DEC_BATCH, DEC_SEQ, D_MODEL)
    return (y_prompt, y_sample, jnp.stack(outs["k"], axis=1), jnp.stack(outs["v"], axis=1),
            jnp.stack(outs["ssm"], axis=1), jnp.stack(outs["C"], axis=1),
            jnp.stack(outs["n"], axis=1), jnp.stack(outs["m"], axis=1))
```

```python
import functools
import math

import jax
import jax.numpy as jnp
from jax import lax
from jax.experimental import pallas as pl
from jax.experimental.pallas import tpu as pltpu

F32 = jnp.float32
BF16 = jnp.bfloat16

D_MODEL = 1024
BATCH = 16
SEQ = 256
DEPTH = 2
DEC_BATCH = 8
DEC_SEQ = 2048
PAST_LEN = 512
GRID_W = 64
EPS = 1e-6
CHUNK = 128
ROPE_BASE = 10000.0
DA_HEADS = 4
DA_HD = 64
DA_WIDTH = 512
MB_INNER = 512
MB_HD = 64
MB_HEADS = 8
MB_GROUPS = 2
MB_STATE = 64
MB_CONV_DIM = 768
ML_HEADS = 4
ML_DK = 64
ML_DV = 128
ML_WIDTH = 512
N_EXPERTS = 16
EC_FACTOR = 2
EXPERT_FF = 1024

T_CTX = BATCH * SEQ
T_SMP = DEC_BATCH * DEC_SEQ
T_ALL = T_CTX + T_SMP
TM = 256
NT_CTX = T_CTX // TM
NT_ALL = T_ALL // TM
NB_CTX = T_CTX // CHUNK
NB_ALL = T_ALL // CHUNK
CAP_CTX = EC_FACTOR * T_CTX // N_EXPERTS
CAP_SMP = EC_FACTOR * T_SMP // N_EXPERTS
CAP_ALL = CAP_CTX + CAP_SMP
RB = 256
N_RB = CAP_ALL // RB
WIN = 64
GATHER_TILES = 5
GATHER_AHEAD = 3
ROUTE_GROUP = 4
TILES_PER_REQ = DEC_SEQ // TM
TM_MERGE = 512
BRANCH_W = 512
LANES = 128
BF16_ROWS = 16
VMEM_LIMIT = 56 * 1024 * 1024
HI = lax.Precision.HIGHEST
LOG2E = math.log2(math.e)

C_Q, C_K, C_V, C_Z, C_XBC, C_MQ, C_MK, C_MV, C_MO, C_END = (
    0, 512, 1024, 1536, 2048, 2816, 3072, 3328, 3840, 4352)


def _mod_row(i):
    return jnp.where(i < NT_CTX, 0, 1 + (i - NT_CTX) // TILES_PER_REQ)


def _rope_blk(i):
    return jnp.where(i < NT_CTX, 0, 1 + (i - NT_CTX) % TILES_PER_REQ)


def _cparams(n_grid):
    return pltpu.CompilerParams(dimension_semantics=("arbitrary",) * n_grid,
                                vmem_limit_bytes=VMEM_LIMIT)


def _silu(x):
    return x * jax.nn.sigmoid(x)


def _softplus(x):
    u = jnp.exp(-jnp.abs(x))
    w = 1.0 + u
    l1p = jnp.where(w == 1.0, u, jnp.log(w) * (u / (w - 1.0)))
    return jnp.maximum(x, 0.0) + l1p


def _dot(a, b):
    return jnp.dot(a, b, preferred_element_type=F32)


def _dot_nt(a, b):
    return lax.dot_general(a, b, (((1,), (1,)), ((), ())), preferred_element_type=F32)


def _dot_hi(a, b):
    return jnp.dot(a, b, precision=HI, preferred_element_type=F32)


def _split3(a):
    hi = a.astype(BF16)
    r = a - hi.astype(F32)
    mid = r.astype(BF16)
    return hi, mid, (r - mid.astype(F32)).astype(BF16)


def _mask3(mask, axis):
    m = jnp.where(mask, 1.0, 0.0).astype(BF16)
    return jnp.concatenate([m, m, m], axis=axis)


def _mask_dot(m3, a):
    return _dot(m3, jnp.concatenate(_split3(a), axis=0))


def _dot_mask(a, m3):
    return _dot(jnp.concatenate(_split3(a), axis=1), m3)


def _dot_mask_narrow(a, m):
    hi, mid, lo = _split3(a)
    return _dot(hi, m) + _dot(mid, m) + _dot(lo, m)


def _rms(x, g):
    return x * lax.rsqrt(jnp.mean(x * x, axis=-1, keepdims=True) + EPS) * g


def _mod_kernel(c_ref, w_ref, b_ref, o_ref):
    s = _silu(c_ref[...])
    o_ref[...] = _dot(s.astype(BF16), w_ref[...].astype(BF16)) + b_ref[...]


def _modulation(cc, ada_w, ada_b):
    tn = 1536
    return pl.pallas_call(
        _mod_kernel,
        grid=(DEPTH, 6 * D_MODEL // tn),
        in_specs=[pl.BlockSpec((16, D_MODEL), lambda l, j: (0, 0)),
                  pl.BlockSpec((None, D_MODEL, tn), lambda l, j: (l, 0, j)),
                  pl.BlockSpec((None, 1, tn), lambda l, j: (l, 0, j))],
        out_specs=pl.BlockSpec((None, 16, tn), lambda l, j: (l, 0, j)),
        out_shape=jax.ShapeDtypeStruct((DEPTH, 16, 6 * D_MODEL), F32),
        compiler_params=_cparams(2),
        name="adaln_mod",
    )(cc, ada_w, ada_b.reshape(DEPTH, 1, 6 * D_MODEL))


def _rope(t, cos, sin, first_half):
    outs = []
    for c in range(DA_WIDTH // LANES):
        xc = t[:, LANES * c:LANES * (c + 1)]
        partner = jnp.where(first_half, pltpu.roll(xc, LANES - 16, 1), pltpu.roll(xc, 16, 1))
        outs.append(xc * cos + partner * sin)
    return jnp.concatenate(outs, axis=1)


def _proj_body(x, mod_ref, g_ref, wa_ref, wb_ref, ws_ref, wst_ref, cos_ref, sin_ref,
               q_ref, k_ref, v_ref, kf_ref, vf_ref, z_ref, xbc_ref,
               mq_ref, mk_ref, mv_ref, mo_ref, sm_ref, smt_ref):
    mod = mod_ref[...]
    sh1 = mod[:, 0:D_MODEL]
    sc1 = mod[:, D_MODEL:2 * D_MODEL]
    h = _rms(x, g_ref[...]) * (1.0 + sc1) + sh1
    hb = h.astype(BF16)

    def proj(a, b):
        if b <= C_MQ:
            return _dot(hb, wa_ref[:, a:b])
        return _dot(hb, wb_ref[:, a - C_MQ:b - C_MQ])

    lane = lax.broadcasted_iota(jnp.int32, (TM, LANES), 1)
    first_half = (lane % 32) < 16
    cos = cos_ref[...]
    sin = sin_ref[...]
    q = proj(C_Q, C_K)
    k = proj(C_K, C_V)
    v = proj(C_V, C_Z)
    q_ref[...] = (_rope(q, cos, sin, first_half) * (DA_HD ** -0.5 * LOG2E)).astype(BF16)
    k_ref[...] = _rope(k, cos, sin, first_half).astype(BF16)
    v_ref[...] = v.astype(BF16)
    z_ref[...] = proj(C_Z, C_XBC)
    xbc_ref[...] = proj(C_XBC, C_MQ)
    mq_ref[...] = proj(C_MQ, C_MK).astype(BF16)
    mk_ref[...] = (proj(C_MK, C_MV) * (ML_DK ** -0.5)).astype(BF16)
    mv_ref[...] = proj(C_MV, C_MO).astype(BF16)
    mo_ref[...] = proj(C_MO, C_END)
    sm_ref[...] = _dot(hb, ws_ref[...])
    st = _dot_nt(wst_ref[...], hb)
    for j in range(TM // CHUNK):
        smt_ref[j] = st[:, CHUNK * j:CHUNK * (j + 1)]

    @pl.when(pl.program_id(0) < NT_CTX)
    def _():
        kf_ref[...] = k
        vf_ref[...] = v


def _proj_kernel_first(xp_ref, xs_ref, *refs):
    x = jnp.where(pl.program_id(0) < NT_CTX, xp_ref[...], xs_ref[...])
    refs[-1][...] = x
    _proj_body(x, *refs[:-1])


def _proj_kernel_next(x_ref, *refs):
    _proj_body(x_ref[...], *refs)


def _ctx_tile(w):
    return pl.BlockSpec((TM, w), lambda i: (jnp.minimum(i, NT_CTX - 1), 0))


def _smp_tile(w):
    return pl.BlockSpec((TM, w), lambda i: (jnp.maximum(i - NT_CTX, 0), 0))


def _projection(xs, mod_l, g1, wa, wb, ws, wst, cos_t, sin_t):
    tile = lambda w: pl.BlockSpec((TM, w), lambda i: (i, 0))
    full = lambda a: pl.BlockSpec(a.shape, lambda i: (0,) * a.ndim)
    modspec = pl.BlockSpec((None, 1, 6 * D_MODEL), lambda i: (_mod_row(i), 0, 0))
    first = isinstance(xs, tuple)
    if first:
        in_specs = [_ctx_tile(D_MODEL), _smp_tile(D_MODEL)]
        args = list(xs)
        kern = _proj_kernel_first
    else:
        in_specs = [tile(D_MODEL)]
        args = [xs]
        kern = _proj_kernel_next
    in_specs += [modspec, full(g1), full(wa), full(wb), full(ws), full(wst),
                 pl.BlockSpec((TM, LANES), lambda i: (_rope_blk(i), 0)),
                 pl.BlockSpec((TM, LANES), lambda i: (_rope_blk(i), 0))]
    args += [mod_l, g1, wa, wb, ws, wst, cos_t, sin_t]
    ctx_only = _ctx_tile(DA_WIDTH)
    out_specs = [tile(512), tile(512), tile(512), ctx_only, ctx_only, tile(512), tile(768),
                 tile(256), tile(256), tile(512), tile(512), tile(32),
                 pl.BlockSpec((TM // CHUNK, 32, CHUNK), lambda i: (i, 0, 0))]
    sds = jax.ShapeDtypeStruct
    out_shape = [sds((T_ALL, 512), BF16), sds((T_ALL, 512), BF16), sds((T_ALL, 512), BF16),
                 sds((T_CTX, 512), F32), sds((T_CTX, 512), F32),
                 sds((T_ALL, 512), F32), sds((T_ALL, 768), F32),
                 sds((T_ALL, 256), BF16), sds((T_ALL, 256), BF16), sds((T_ALL, 512), BF16),
                 sds((T_ALL, 512), F32), sds((T_ALL, 32), F32),
                 sds((T_ALL // CHUNK, 32, CHUNK), F32)]
    if first:
        out_specs.append(tile(D_MODEL))
        out_shape.append(sds((T_ALL, D_MODEL), F32))
    return pl.pallas_call(
        kern, grid=(NT_ALL,), in_specs=in_specs, out_specs=out_specs, out_shape=out_shape,
        compiler_params=_cparams(1), name="in_proj",
    )(*args)


def _attn_body(lam_init, q_ref, k_ref, v_ref, kc_ref, vc_ref, lp_ref, g_ref, o_ref):
    lp = lp_ref[...]
    s01 = jnp.sum(lp[0:1] * lp[1:2], axis=-1, keepdims=True)
    s23 = jnp.sum(lp[2:3] * lp[3:4], axis=-1, keepdims=True)
    lam = jnp.exp(s01) - jnp.exp(s23) + lam_init
    tq = q_ref.shape[0]
    lane = lax.broadcasted_iota(jnp.int32, (tq, LANES), 1)
    g = g_ref[...]
    def scores(h, m):
        cols = slice(LANES * h, LANES * (h + 1))
        qh = q_ref[:, cols]
        qm = jnp.where((lane < DA_HD) == (m == 0), qh, jnp.zeros_like(qh))
        s = _dot_nt(qm, k_ref[:, cols])
        sc = None if kc_ref is None else _dot_nt(qm, kc_ref[:, cols].astype(BF16))
        return s, sc

    units = [(h, m) for h in range(DA_HEADS) for m in range(2)]
    nxt = scores(*units[0])
    parts = []
    for u, (h, m) in enumerate(units):
        cols = slice(LANES * h, LANES * (h + 1))
        s, sc = nxt
        if u + 1 < len(units):
            nxt = scores(*units[u + 1])
        vh = jnp.concatenate([v_ref[:, cols], jnp.ones((k_ref.shape[0], LANES), BF16)], axis=1)
        mx = jnp.max(s, axis=-1, keepdims=True)
        if sc is not None:
            mx = jnp.maximum(mx, jnp.max(sc, axis=-1, keepdims=True))
        acc = _dot(jnp.exp2(s - mx).astype(BF16), vh)
        if sc is not None:
            vch = jnp.concatenate([vc_ref[:, cols].astype(BF16), jnp.ones((PAST_LEN, LANES), BF16)], axis=1)
            acc = acc + _dot(jnp.exp2(sc - mx).astype(BF16), vch)
        parts.append(acc[:, :LANES] / acc[:, LANES:])
        if m == 1:
            att = parts[-2] - lam * parts[-1]
            o_ref[:, cols] = (_rms(att, g) * (1.0 - lam_init)).astype(BF16)


def _attn_kernel_ctx(lam_init, q_ref, k_ref, v_ref, lp_ref, g_ref, o_ref):
    _attn_body(lam_init, q_ref, k_ref, v_ref, None, None, lp_ref, g_ref, o_ref)


def _attn_kernel_smp(lam_init, q_ref, k_ref, v_ref, kc_ref, vc_ref, lp_ref, g_ref, o_ref):
    _attn_body(lam_init, q_ref, k_ref, v_ref, kc_ref, vc_ref, lp_ref, g_ref, o_ref)


def _attention(l, q, k, v, cache_k, cache_v, lp, g, sample):
    lam_init = 0.8 - 0.6 * math.exp(-0.3 * l)
    full = lambda a: pl.BlockSpec(a.shape, lambda *_: (0,) * a.ndim)
    sds = jax.ShapeDtypeStruct
    if not sample:
        blk = pl.BlockSpec((SEQ, DA_WIDTH), lambda b: (b, 0))
        return pl.pallas_call(
            functools.partial(_attn_kernel_ctx, lam_init), grid=(BATCH,),
            in_specs=[blk, blk, blk, full(lp), full(g)],
            out_specs=blk, out_shape=sds((T_CTX, DA_WIDTH), BF16),
            compiler_params=_cparams(1), name="diff_attn_ctx",
        )(q, k, v, lp, g)
    tq = 256
    nq = DEC_SEQ // tq
    off_q = T_CTX // tq
    off_k = T_CTX // DEC_SEQ
    qblk = pl.BlockSpec((tq, DA_WIDTH), lambda b, i: (off_q + b * nq + i, 0))
    kblk = pl.BlockSpec((DEC_SEQ, DA_WIDTH), lambda b, i: (off_k + b, 0))
    cblk = pl.BlockSpec((None, None, PAST_LEN, DA_WIDTH), lambda b, i: (b, l, 0, 0))
    return pl.pallas_call(
        functools.partial(_attn_kernel_smp, lam_init), grid=(DEC_BATCH, nq),
        in_specs=[qblk, kblk, kblk, cblk, cblk, full(lp), full(g)],
        out_specs=pl.BlockSpec((tq, DA_WIDTH), lambda b, i: (b * nq + i, 0)),
        out_shape=sds((T_SMP, DA_WIDTH), BF16),
        compiler_params=_cparams(2), name="diff_attn_smp",
    )(q, k, v, cache_k, cache_v, lp, g)


def _tri_masks():
    r = lax.broadcasted_iota(jnp.int32, (CHUNK, CHUNK), 0)
    c = lax.broadcasted_iota(jnp.int32, (CHUNK, CHUNK), 1)
    return c <= r, c >= r


def _ssd_kernel(L, z_ref, xbc_ref, sm_ref, smt_ref, cw_ref, cb_ref, dtbr_ref, dtbc_ref,
                ar_ref, ac_ref, dexp_ref, ng_ref, h0_ref,
                y_ref, hfin_ref, xc_s, yacc_s, ht_s):
    nc = L // CHUNK
    low, upp = _tri_masks()
    tri_l3 = (_mask3(low, 1), _mask3(upp, 1))
    tri_r3 = (_mask3(upp, 0), _mask3(low, 0))
    lane512 = lax.broadcasted_iota(jnp.int32, (1, MB_INNER), 1)
    row16 = lax.broadcasted_iota(jnp.int32, (16, MB_INNER), 0)
    lane16 = lax.broadcasted_iota(jnp.int32, (16, MB_INNER), 1)
    expand = tuple(_mask3(row16 == 8 * d + lane16 // MB_HD, 0) for d in range(2))
    lane128 = lax.broadcasted_iota(jnp.int32, (CHUNK, LANES), 1)
    rowblk = lax.broadcasted_iota(jnp.int32, (LANES, MB_INNER), 0) // MB_STATE
    colblk = lax.broadcasted_iota(jnp.int32, (LANES, MB_INNER), 1) // (MB_INNER // MB_GROUPS)
    same_group = rowblk == colblk
    cw = cw_ref[...]
    cbias = cb_ref[...]

    def conv_chunk(c, _):
        base = pl.multiple_of(c * CHUNK, CHUNK)
        x = xbc_ref[pl.ds(base, CHUNK), :]
        prev = xbc_ref[pl.ds(jnp.maximum(base - 1, 0), 1), :]
        nxt = xbc_ref[pl.ds(jnp.minimum(base + CHUNK, L - 1), 1), :]
        prev = jnp.where(c == 0, 0.0, prev)
        nxt = jnp.where(c == nc - 1, 0.0, nxt)
        row = lax.broadcasted_iota(jnp.int32, (CHUNK, 1), 0)
        xp = jnp.where(row == 0, prev, pltpu.roll(x, 1, 0))
        xn = jnp.where(row == CHUNK - 1, nxt, pltpu.roll(x, CHUNK - 1, 0))
        conv = xp * cw[0:1] + x * cw[1:2] + xn * cw[2:3] + cbias
        xc_s[pl.ds(base, CHUNK), :] = _silu(conv)
        return 0

    lax.fori_loop(0, nc, conv_chunk, 0)

    h0 = h0_ref[...]
    for d in range(2):
        h0d = h0[d]
        ht_s[d] = jnp.concatenate(
            [jnp.where(lane512 < MB_INNER // 2, h0d, 0.0),
             jnp.where(lane512 >= MB_INNER // 2, h0d, 0.0)], axis=0)

    def both(i, _):
        jobs = ((0, i), (1, nc - 1 - i))
        rows, xsb, bb, cb, dt_row, cum_col, cum_row = {}, {}, {}, {}, {}, {}, {}
        y_off, cbg = {}, {}
        for d, c in jobs:
            rows[d] = pl.ds(pl.multiple_of(c * CHUNK, CHUNK), CHUNK)
            xs = xc_s[rows[d], 0:MB_INNER]
            bm = xc_s[rows[d], MB_INNER:MB_INNER + LANES]
            cm = xc_s[rows[d], MB_INNER + LANES:MB_CONV_DIM]
            dt_col = _softplus(sm_ref[rows[d], 0:16] + dtbr_ref[...])
            a_col = dt_col * ar_ref[...]
            dt_row[d] = _softplus(smt_ref[c][0:16, :] + dtbc_ref[...])
            a_row = dt_row[d] * ac_ref[...]
            cum_col[d] = _mask_dot(tri_l3[d], a_col)
            cum_row[d] = _dot_mask(a_row, tri_r3[d])
            last = 0 if d else CHUNK - 1
            cum_last = cum_col[d][last:last + 1, :]
            w_exp = _dot_mask(jnp.exp(cum_last - cum_col[d]) * dt_col, expand[d])
            g_exp = _dot_mask(jnp.exp(cum_col[d]), expand[d])
            cd_exp = _dot_mask(jnp.broadcast_to(jnp.exp(cum_last), (8, 16)), expand[d])[0:1]
            xw = (xs * w_exp).astype(BF16)
            xsb[d] = xs.astype(BF16)
            bb[d] = bm.astype(BF16)
            cb[d] = cm.astype(BF16)
            ht = ht_s[d]
            y_off[d] = _dot(cb[d], ht.astype(BF16)) * g_exp
            s_new = _dot(bm.T.astype(BF16), xw)
            ht_s[d] = ht * cd_exp + jnp.where(same_group, s_new, 0.0)
            for g in range(MB_GROUPS):
                cg = jnp.where((lane128 < MB_STATE) == (g == 0), cb[d], jnp.zeros_like(cb[d]))
                cbg[d, g] = _dot_nt(cg, bb[d])
        m16 = {}
        for d, _c in jobs:
            for h in range(MB_HEADS):
                ci = 8 * d + h
                seg = cum_col[d][:, ci:ci + 1] - cum_row[d][ci:ci + 1, :]
                m = jnp.where(upp if d else low, jnp.exp(seg), 0.0) * cbg[d, h // 4] * dt_row[d][ci:ci + 1, :]
                m16[d, h] = m.astype(BF16)
        yd = {}
        for d, _c in jobs:
            for h in range(MB_HEADS):
                k = h // 2
                yd[d, h] = _dot(m16[d, h], xsb[d][:, LANES * k:LANES * (k + 1)])
        for d, _c in jobs:
            pairs = [jnp.where(lane128 < MB_HD, yd[d, 2 * k], yd[d, 2 * k + 1]) for k in range(MB_HEADS // 2)]
            yacc_s[d, rows[d], :] = jnp.concatenate(pairs, axis=1) + y_off[d]
        return 0

    lax.fori_loop(0, nc, both, 0)

    dexp = dexp_ref[...]
    ng = ng_ref[...]

    def fin(c, _):
        rows = pl.ds(pl.multiple_of(c * CHUNK, CHUNK), CHUNK)
        y = yacc_s[0, rows, :] + yacc_s[1, rows, :] + dexp * xc_s[rows, 0:MB_INNER]
        y = y * _silu(z_ref[rows, :])
        y_ref[rows, :] = _rms(y, ng).astype(BF16)
        return 0

    lax.fori_loop(0, nc, fin, 0)
    for d in range(2):
        ht = ht_s[d]
        hfin_ref[d] = ht[0:MB_STATE, :] + ht[MB_STATE:2 * MB_STATE, :]


def _ssd(z, xbc, sm, smt, cw, cb, dtb, a_neg, dexp, ng, h0t, sample):
    nb, L, off = (DEC_BATCH, DEC_SEQ, T_CTX // DEC_SEQ) if sample else (BATCH, SEQ, 0)
    full = lambda a: pl.BlockSpec(a.shape, lambda b: (0,) * a.ndim)
    seq = lambda w: pl.BlockSpec((L, w), lambda b: (off + b, 0))
    dtb_r, dtb_c = dtb.reshape(1, 16), dtb.reshape(16, 1)
    a_r, a_c = a_neg.reshape(1, 16), a_neg.reshape(16, 1)
    sds = jax.ShapeDtypeStruct
    return pl.pallas_call(
        functools.partial(_ssd_kernel, L), grid=(nb,),
        in_specs=[seq(MB_INNER), seq(MB_CONV_DIM), seq(32),
                  pl.BlockSpec((L // CHUNK, 32, CHUNK), lambda b: (off + b, 0, 0)),
                  full(cw), full(cb), full(dtb_r), full(dtb_c), full(a_r), full(a_c),
                  full(dexp), full(ng),
                  pl.BlockSpec((None, 2, MB_STATE, MB_INNER), lambda b: (b, 0, 0, 0))],
        out_specs=[pl.BlockSpec((L, MB_INNER), lambda b: (b, 0)),
                   pl.BlockSpec((None, 2, MB_STATE, MB_INNER), lambda b: (b, 0, 0, 0))],
        out_shape=[sds((nb * L, MB_INNER), BF16), sds((nb, 2, MB_STATE, MB_INNER), F32)],
        scratch_shapes=[pltpu.VMEM((L, MB_CONV_DIM), F32), pltpu.VMEM((2, L, MB_INNER), F32),
                        pltpu.VMEM((2, LANES, MB_INNER), F32)],
        compiler_params=_cparams(1), name="ssd_smp" if sample else "ssd_ctx",
    )(z, xbc, sm, smt, cw, cb, dtb_r, dtb_c, a_r, a_c, dexp, ng, h0t)


def _mlstm_kernel(L, q_ref, k_ref, v_ref, o_ref, sm_ref, smt_ref, gbr_ref, gbc_ref, ng_ref,
                  c0_ref, n0_ref, m0_ref,
                  y_ref, cf_ref, nf_ref, mf_ref, hacc_s, c_s, n_s):
    nc = L // CHUNK
    low, upp = _tri_masks()
    tri_l = (low.astype(F32), upp.astype(F32))
    tri_r = (upp.astype(F32), low.astype(F32))
    lane128 = lax.broadcasted_iota(jnp.int32, (CHUNK, LANES), 1)
    lane8 = lax.broadcasted_iota(jnp.int32, (1, 2 * ML_HEADS), 1)
    neg_inf = -jnp.inf

    ones16 = jnp.ones((CHUNK, LANES), BF16)
    c_s[...] = c0_ref[...]
    n_s[...] = jnp.broadcast_to(n0_ref[...], n_s.shape)
    m0 = m0_ref[...]

    def both(i, m_in):
        jobs = ((0, i), (1, nc - 1 - i))
        heads = [(d, h) for d, _ in jobs for h in range(ML_HEADS)]
        rows, pre_row, b_col, b_row = {}, {}, {}, {}
        for d, c in jobs:
            rows[d] = pl.ds(pl.multiple_of(c * CHUNK, CHUNK), CHUNK)
            pre_col = sm_ref[rows[d], 16:32] + gbr_ref[...]
            pre_row[d] = smt_ref[c][16:32, :] + gbc_ref[...]
            lf_col = -_softplus(-pre_col)
            lf_row = -_softplus(-pre_row[d])
            b_col[d] = _dot_hi(tri_l[d], lf_col)
            b_row[d] = _dot_hi(lf_row, tri_r[d])
        kt, cst, nst, vh, qk, qc, qn = {}, {}, {}, {}, {}, {}, {}
        for d, _ in jobs:
            for pr in range(2):
                qp = q_ref[rows[d], LANES * pr:LANES * (pr + 1)]
                kp = k_ref[rows[d], LANES * pr:LANES * (pr + 1)]
                kt[d, pr] = kp.astype(F32).T
                cst[d, pr] = c_s[d, pr]
                nst[d, pr] = n_s[d, pr]
                cb16 = cst[d, pr].astype(BF16)
                nb16 = nst[d, pr].astype(BF16)
                for hh in range(2):
                    h = 2 * pr + hh
                    qm = jnp.where((lane128 < ML_DK) == (hh == 0), qp, jnp.zeros_like(qp))
                    vh[d, h] = jnp.concatenate([v_ref[rows[d], ML_DV * h:ML_DV * (h + 1)], ones16], axis=1)
                    qk[d, h] = _dot_nt(qm, kp)
                    qc[d, h] = _dot(qm, cb16)
                    qn[d, h] = _dot(qm, nb16)
        m_t, s_intra, s_inter, m_new, s_old, kts = {}, {}, {}, {}, {}, {}
        for d, h in heads:
            m_st = m_in[:, 4 * d + h:4 * d + h + 1]
            bcol = jnp.broadcast_to(b_col[d][:, 8 * d + 4 + h:8 * d + 5 + h], (CHUNK, CHUNK))
            brow = b_row[d][8 * d + 4 + h:8 * d + 5 + h, :]
            li_row = pre_row[d][8 * d + h:8 * d + h + 1, :]
            dm = jnp.where(upp if d else low, bcol - brow + li_row, neg_inf)
            inter = bcol + m_st
            m_t[d, h] = jnp.maximum(inter, jnp.max(dm, axis=-1, keepdims=True))
            s_intra[d, h] = jnp.exp(dm - m_t[d, h]) * qk[d, h]
            s_inter[d, h] = jnp.exp(inter - m_t[d, h])
            last = 0 if d else CHUNK - 1
            b_end = brow[:, last:last + 1]
            w_end = b_end - brow + li_row
            m_new[d, h] = jnp.maximum(b_end + m_st, jnp.max(w_end, axis=-1, keepdims=True))
            s_old[d, h] = jnp.exp(b_end + m_st - m_new[d, h])
            half = slice(ML_DK * (h % 2), ML_DK * (h % 2 + 1))
            kts[d, h] = kt[d, h // 2][half, :] * jnp.exp(w_end - m_new[d, h])
        pv, kv = {}, {}
        for d, h in heads:
            pv[d, h] = _dot(s_intra[d, h].astype(BF16), vh[d, h])
            kv[d, h] = _dot(kts[d, h].astype(BF16), vh[d, h])
        m_out = m_in
        for d, h in heads:
            num = s_inter[d, h] * qc[d, h] + pv[d, h][:, :ML_DV]
            den = s_inter[d, h] * qn[d, h] + pv[d, h][:, ML_DV:]
            hout = num / jnp.maximum(jnp.abs(den), jnp.exp(-m_t[d, h]))
            hacc_s[d, rows[d], ML_DV * h:ML_DV * (h + 1)] = hout
            m_out = jnp.where(lane8 == 4 * d + h, m_new[d, h], m_out)
        for d, _ in jobs:
            for pr in range(2):
                c_new, n_new = [], []
                for hh in range(2):
                    h = 2 * pr + hh
                    half = slice(ML_DK * hh, ML_DK * (hh + 1))
                    c_new.append(s_old[d, h] * cst[d, pr][half, :] + kv[d, h][:, :ML_DV])
                    n_new.append(s_old[d, h] * nst[d, pr][half, :] + kv[d, h][:, ML_DV:])
                c_s[d, pr] = jnp.concatenate(c_new, axis=0)
                n_s[d, pr] = jnp.concatenate(n_new, axis=0)
        return m_out

    m_fin = lax.fori_loop(0, nc, both, m0)

    ng = ng_ref[...]

    def fin(c, _):
        rows = pl.ds(pl.multiple_of(c * CHUNK, CHUNK), CHUNK)
        for h in range(ML_HEADS):
            cols = slice(ML_DV * h, ML_DV * (h + 1))
            y = _rms(hacc_s[0, rows, cols] + hacc_s[1, rows, cols], ng[:, cols]) * jax.nn.sigmoid(o_ref[rows, cols])
            y_ref[rows, cols] = y.astype(BF16)
        return 0

    lax.fori_loop(0, nc, fin, 0)
    cf_ref[...] = c_s[...]
    nf_ref[...] = n_s[:, :, :, 0:1]
    mf_ref[...] = m_fin


def _mlstm(q, k, v, o, sm, smt, gb, ng, c0, n0, m0, sample):
    nb, L, off = (DEC_BATCH, DEC_SEQ, T_CTX // DEC_SEQ) if sample else (BATCH, SEQ, 0)
    full = lambda a: pl.BlockSpec(a.shape, lambda b: (0,) * a.ndim)
    seq = lambda w: pl.BlockSpec((L, w), lambda b: (off + b, 0))
    gb_r, gb_c = gb.reshape(1, 16), gb.reshape(16, 1)
    st_c = pl.BlockSpec((None, 2, 2, LANES, ML_DV), lambda b: (b, 0, 0, 0, 0))
    st_n = pl.BlockSpec((None, 2, 2, LANES, 1), lambda b: (b, 0, 0, 0, 0))
    st_m = pl.BlockSpec((None, 1, 8), lambda b: (b, 0, 0))
    sds = jax.ShapeDtypeStruct
    return pl.pallas_call(
        functools.partial(_mlstm_kernel, L), grid=(nb,),
        in_specs=[seq(256), seq(256), seq(512), seq(512), seq(32),
                  pl.BlockSpec((L // CHUNK, 32, CHUNK), lambda b: (off + b, 0, 0)),
                  full(gb_r), full(gb_c), full(ng), st_c, st_n, st_m],
        out_specs=[pl.BlockSpec((L, ML_WIDTH), lambda b: (b, 0)), st_c, st_n, st_m],
        out_shape=[sds((nb * L, ML_WIDTH), BF16), sds((nb, 2, 2, LANES, ML_DV), F32),
                   sds((nb, 2, 2, LANES, 1), F32), sds((nb, 1, 8), F32)],
        scratch_shapes=[pltpu.VMEM((2, L, ML_WIDTH), F32), pltpu.VMEM((2, 2, LANES, ML_DV), F32),
                        pltpu.VMEM((2, 2, LANES, LANES), F32)],
        compiler_params=_cparams(1), name="mlstm_smp" if sample else "mlstm_ctx",
    )(q, k, v, o, sm, smt, gb_r, gb_c, ng, c0, n0, m0)


def _merge_kernel(x_ref, mod_ref, g1_ref, ydac_ref, ymbc_ref, ymlc_ref, ydas_ref, ymbs_ref, ymls_ref,
                  wg_ref, bg_ref, wb_ref, wo_ref, g2_ref, rwt_ref, xo_ref, h2t_ref, affb_ref):
    x = x_ref[...]
    mod = mod_ref[...]
    sh1, sc1, gt1, sh2, sc2 = (mod[:, j * D_MODEL:(j + 1) * D_MODEL] for j in range(5))
    hb = (_rms(x, g1_ref[...]) * (1.0 + sc1) + sh1).astype(BF16)
    is_ctx = pl.program_id(0) < T_CTX // TM_MERGE
    merged = None
    for n, (yc_ref, ys_ref) in enumerate(((ydac_ref, ydas_ref), (ymbc_ref, ymbs_ref), (ymlc_ref, ymls_ref))):
        cols = slice(n * D_MODEL, (n + 1) * D_MODEL)
        gate = jax.nn.sigmoid(_dot(hb, wg_ref[:, cols]) + bg_ref[:, cols])
        y = jnp.where(is_ctx, yc_ref[...], ys_ref[...])
        term = gate * _dot(y, wb_ref[n])
        merged = term if merged is None else merged + term
    out = _dot(merged.astype(BF16), wo_ref[...])
    xn = x + gt1 * out
    xo_ref[...] = xn
    h2 = _rms(xn, g2_ref[...]) * (1.0 + sc2) + sh2
    h2t_ref[...] = h2.T.astype(BF16)
    logits = lax.dot_general(rwt_ref[...], h2, (((1,), (1,)), ((), ())),
                             precision=HI, preferred_element_type=F32)
    e = jnp.exp(logits - jnp.max(logits, axis=0, keepdims=True))
    aff = e / jnp.sum(e, axis=0, keepdims=True)
    for j in range(TM_MERGE // CHUNK):
        affb_ref[j] = aff[:, CHUNK * j:CHUNK * (j + 1)]


def _merge(x, mod_l, g1, ys_ctx, ys_smp, wg, bg, wb, wo, g2, rwt):
    tm = TM_MERGE
    n_ctx = T_CTX // tm
    per_req = DEC_SEQ // tm
    tile = lambda w: pl.BlockSpec((tm, w), lambda i: (i, 0))
    ctx_tile = pl.BlockSpec((tm, BRANCH_W), lambda i: (jnp.minimum(i, n_ctx - 1), 0))
    smp_tile = pl.BlockSpec((tm, BRANCH_W), lambda i: (jnp.maximum(i - n_ctx, 0), 0))
    full = lambda a: pl.BlockSpec(a.shape, lambda i: (0,) * a.ndim)
    mod_row = lambda i: jnp.where(i < n_ctx, 0, 1 + (i - n_ctx) // per_req)
    sds = jax.ShapeDtypeStruct
    return pl.pallas_call(
        _merge_kernel, grid=(T_ALL // tm,),
        in_specs=[tile(D_MODEL), pl.BlockSpec((None, 1, 6 * D_MODEL), lambda i: (mod_row(i), 0, 0)),
                  full(g1), ctx_tile, ctx_tile, ctx_tile, smp_tile, smp_tile, smp_tile,
                  full(wg), full(bg), full(wb), full(wo), full(g2), full(rwt)],
        out_specs=[tile(D_MODEL), pl.BlockSpec((D_MODEL, tm), lambda i: (0, i)),
                   pl.BlockSpec((tm // CHUNK, N_EXPERTS, CHUNK), lambda i: (i, 0, 0))],
        out_shape=[sds((T_ALL, D_MODEL), F32), sds((D_MODEL, T_ALL), BF16),
                   sds((NB_ALL, N_EXPERTS, CHUNK), F32)],
        compiler_params=_cparams(1), name="merge_out",
    )(x, mod_l, g1, *ys_ctx, *ys_smp, wg, bg, wb, wo, g2, rwt)


def _route_kernel(affb_ref, slotb_ref, slott_ref, a_ref, jlo_ref, jhi_ref, acc_s, run_s):
    r = lax.broadcasted_iota(jnp.int32, (CHUNK, CHUNK), 0)
    c = lax.broadcasted_iota(jnp.int32, (CHUNK, CHUNK), 1)
    upper = jnp.where(r <= c, 1.0, 0.0).astype(BF16)
    eye = jnp.where(r == c, 1.0, 0.0).astype(BF16)
    eye2 = jnp.concatenate([eye, eye], axis=1)
    lane = lax.broadcasted_iota(jnp.int32, (N_EXPERTS, LANES), 1)
    acc_s[...] = jnp.zeros_like(acc_s)
    run_s[...] = jnp.zeros_like(run_s)
    for b0, b1, cap in ((0, NB_CTX, CAP_CTX), (NB_CTX, NB_ALL, CAP_SMP)):
        aff = affb_ref[b0:b1]

        def search(i, thr_bits):
            cand = thr_bits | lax.shift_left(jnp.int32(1), 30 - i)
            cnt = jnp.sum((aff >= pltpu.bitcast(cand, F32)[None]).astype(jnp.int32), axis=0)
            cnt = jnp.sum(cnt, axis=1, keepdims=True)
            return jnp.where(cnt >= cap, cand, thr_bits)

        thr = pltpu.bitcast(lax.fori_loop(0, 31, search, jnp.zeros((N_EXPERTS, 1), jnp.int32)), F32)
        n_gt = jnp.sum(jnp.sum((aff > thr[None]).astype(jnp.int32), axis=0), axis=1, keepdims=True)
        need = (cap - n_gt).astype(F32)

        run_s[1] = jnp.zeros((N_EXPERTS, 1), F32)

        def blocks(gi, _):
            b = pl.multiple_of(b0 + gi * ROUTE_GROUP, ROUTE_GROUP)
            run_sel = run_s[0]
            run_eq = run_s[1]
            xs = [affb_ref[b + i] for i in range(ROUTE_GROUP)]
            eqs = [x == thr for x in xs]
            eq_fs = [jnp.where(eq, 1.0, 0.0) for eq in eqs]
            eq_incls = [_dot(f.astype(BF16), upper) for f in eq_fs]
            sels = []
            for x, eq, f, incl in zip(xs, eqs, eq_fs, eq_incls):
                sels.append((x > thr) | (eq & (run_eq + incl - f < need)))
                run_eq = run_eq + incl[:, CHUNK - 1:CHUNK]
            sel_fs = [jnp.where(sel, 1.0, 0.0) for sel in sels]
            sel_incls = [_dot(f.astype(BF16), upper) for f in sel_fs]
            acc = acc_s[...]
            slots = []
            for i, (sel, f, incl) in enumerate(zip(sels, sel_fs, sel_incls)):
                slots.append(jnp.where(sel, run_sel + incl - f, -1.0))
                if i % 2 == 0:
                    acc = jnp.where(lane == b // 2 + i // 2, run_sel, acc)
                run_sel = run_sel + incl[:, CHUNK - 1:CHUNK]
            acc_s[...] = acc
            run_s[0] = run_sel
            run_s[1] = run_eq
            for i, slot in enumerate(slots):
                slotb_ref[b + i] = slot.astype(jnp.int32)
                hi64 = jnp.floor(slot * (1.0 / 64.0)) * 64.0
                parts = jnp.concatenate([hi64, slot - hi64], axis=1).astype(BF16)
                slott_ref[pl.ds(pl.multiple_of((b + i) * CHUNK, CHUNK), CHUNK), :] = _dot_nt(eye2, parts)
            return 0

        lax.fori_loop(0, (b1 - b0) // ROUTE_GROUP, blocks, 0)
    a_acc = jnp.where(lane == NT_ALL, run_s[0], acc_s[...])
    a_ref[...] = a_acc.astype(jnp.int32)
    a_next = pltpu.roll(a_acc, LANES - 1, 1)
    tile_ok = lane < NT_ALL
    jlo = jnp.zeros((N_EXPERTS, LANES), jnp.int32)
    jhi = jnp.zeros((N_EXPERTS, LANES), jnp.int32)
    for k in range(N_RB):
        lo_k = jnp.sum((tile_ok & (a_next <= float(RB * k))).astype(jnp.int32), axis=1, keepdims=True)
        hi_k = jnp.sum((tile_ok & (a_acc < float(RB * (k + 1)))).astype(jnp.int32), axis=1, keepdims=True) - 1
        jlo = jnp.where(lane == k, lo_k, jlo)
        jhi = jnp.where(lane == k, hi_k, jhi)
    jlo_ref[...] = jlo
    jhi_ref[...] = jhi


def _route(affb):
    sds = jax.ShapeDtypeStruct
    small = sds((N_EXPERTS, LANES), jnp.int32)
    return pl.pallas_call(
        _route_kernel,
        out_shape=[sds((NB_ALL, N_EXPERTS, CHUNK), jnp.int32), sds((T_ALL, N_EXPERTS), F32),
                   small, small, small],
        scratch_shapes=[pltpu.VMEM((N_EXPERTS, LANES), F32), pltpu.VMEM((2, N_EXPERTS, 1), F32)],
        compiler_params=pltpu.CompilerParams(vmem_limit_bytes=VMEM_LIMIT), name="route",
    )(affb)


def _expert_kernel(jlo_ref, jhi_ref, h2t_hbm, slotb_ref, affb_ref, wg_ref, wu_ref, wd_ref, o_ref,
                   wg_s, wu_s, wd_s, chunk_s, sem, acc_s, g_s, cnt_s):
    e = pl.program_id(0)
    k = pl.program_id(1)
    step = e * N_RB + k

    @pl.when(k == 0)
    def _():
        wg_s[...] = wg_ref[...].astype(BF16)
        wu_s[...] = wu_ref[...].astype(BF16)
        wd_s[...] = wd_ref[...].astype(BF16)

    def tile_range(s):
        i = (s // N_RB) * LANES + s % N_RB
        lo = jnp.clip(jlo_ref[i], 0, NT_ALL - 1)
        return lo, jnp.clip(jhi_ref[i] - lo + 1, 1, NT_ALL - lo)

    def chunk_tile(lo, c):
        return jnp.minimum(lo + GATHER_TILES * c, NT_ALL - GATHER_TILES)

    def chunk_copy(j0, buf):
        return pltpu.make_async_copy(
            h2t_hbm.at[:, pl.ds(pl.multiple_of(j0 * TM, TM), GATHER_TILES * TM)],
            chunk_s.at[buf], sem.at[buf])

    def n_chunks(s):
        return (tile_range(s)[1] + GATHER_TILES - 1) // GATHER_TILES

    n_steps = N_EXPERTS * N_RB

    def advance(s, c):
        wrap = (c + 1 >= n_chunks(jnp.minimum(s, n_steps - 1))) | (s >= n_steps)
        return jnp.where(wrap, s + 1, s), jnp.where(wrap, 0, c + 1)

    def start_at(s, c, buf):
        @pl.when(s < n_steps)
        def _():
            chunk_copy(chunk_tile(tile_range(jnp.minimum(s, n_steps - 1))[0], c), buf).start()

    jlo, n = tile_range(step)
    nch = n_chunks(step)

    @pl.when(step == 0)
    def _():
        cnt_s[0] = 0
        pos = (step, 0)
        for i in range(GATHER_AHEAD):
            start_at(pos[0], pos[1], i)
            pos = advance(*pos)

    done = cnt_s[0]
    acc_s[...] = jnp.zeros_like(acc_s)
    g_s[...] = jnp.zeros_like(g_s)
    ntok = GATHER_TILES * TM
    want = lax.broadcasted_iota(jnp.int32, (RB, ntok), 0) + k * RB
    lane_tile = lax.broadcasted_iota(jnp.int32, (1, ntok), 1) // TM

    def body(c, _):
        buf = (done + c) % (GATHER_AHEAD + 1)
        j0 = chunk_tile(jlo, c)
        chunk_copy(j0, buf).wait()
        pos = (step, c)
        for _i in range(GATHER_AHEAD):
            pos = advance(*pos)
        start_at(pos[0], pos[1], (done + c + GATHER_AHEAD) % (GATHER_AHEAD + 1))

        nblk = ntok // CHUNK
        srow = jnp.concatenate([slotb_ref[2 * j0 + i, pl.ds(e, 1), :] for i in range(nblk)], axis=1)
        arow = jnp.concatenate([affb_ref[2 * j0 + i, pl.ds(e, 1), :] for i in range(nblk)], axis=1)
        fresh = j0 + lane_tile >= jlo + GATHER_TILES * c
        hit = (srow == want) & fresh
        onehot = jnp.where(hit, 1.0, 0.0).astype(BF16)
        acc_s[...] += _dot_nt(chunk_s[buf], onehot)
        g_s[...] += jnp.sum(jnp.where(hit, arow, 0.0), axis=1, keepdims=True)
        return 0

    lax.fori_loop(0, nch, body, 0)
    cnt_s[0] = done + nch
    xe = acc_s[...].T.astype(BF16)
    hid = _silu(_dot(xe, wg_s[...])) * _dot(xe, wu_s[...])
    o_ref[...] = (_dot(hid.astype(BF16), wd_s[...]) * g_s[...]).astype(BF16)


def _experts(l, jlo, jhi, h2t, slotb, affb, w_gate, w_up, w_down):
    wsp = pl.BlockSpec((None, None, D_MODEL, EXPERT_FF), lambda e, j, *_: (l, e, 0, 0))
    wsd = pl.BlockSpec((None, None, EXPERT_FF, D_MODEL), lambda e, j, *_: (l, e, 0, 0))
    whole = lambda a: pl.BlockSpec(a.shape, lambda e, j, *_: (0,) * a.ndim)
    grid_spec = pltpu.PrefetchScalarGridSpec(
        num_scalar_prefetch=2, grid=(N_EXPERTS, N_RB),
        in_specs=[pl.BlockSpec(memory_space=pl.ANY), whole(slotb), whole(affb), wsp, wsp, wsd],
        out_specs=pl.BlockSpec((None, RB, D_MODEL), lambda e, j, *_: (e, j, 0)),
        scratch_shapes=[pltpu.VMEM((D_MODEL, EXPERT_FF), BF16), pltpu.VMEM((D_MODEL, EXPERT_FF), BF16),
                        pltpu.VMEM((EXPERT_FF, D_MODEL), BF16),
                        pltpu.VMEM((GATHER_AHEAD + 1, D_MODEL, GATHER_TILES * TM), BF16),
                        pltpu.SemaphoreType.DMA((GATHER_AHEAD + 1,)), pltpu.VMEM((D_MODEL, RB), F32),
                        pltpu.VMEM((RB, 1), F32), pltpu.SMEM((1,), jnp.int32)])
    return pl.pallas_call(
        _expert_kernel, grid_spec=grid_spec,
        out_shape=jax.ShapeDtypeStruct((N_EXPERTS, CAP_ALL, D_MODEL), BF16),
        compiler_params=_cparams(2), name="expert_ffn",
    )(jlo, jhi, h2t, slotb, affb, w_gate, w_up, w_down)


def _combine_body(a_ref, ye_hbm, slott_ref, win_s, sem, xwin_s, xsem, o_ref):
    j = pl.program_id(0)
    nt = pl.num_programs(0)

    def first_row(e, jj):
        a = a_ref[e * LANES + jj]
        return pl.multiple_of(jnp.clip((a // BF16_ROWS) * BF16_ROWS, 0, CAP_ALL - WIN), BF16_ROWS)

    def win_copy(e, row0, buf):
        return pltpu.make_async_copy(ye_hbm.at[e, pl.ds(row0, WIN), :], win_s.at[buf, e], sem.at[buf, e])

    buf = j % 2

    @pl.when(j == 0)
    def _():
        for e in range(N_EXPERTS):
            win_copy(e, first_row(e, 0), 0).start()

    @pl.when(j + 1 < nt)
    def _():
        for e in range(N_EXPERTS):
            win_copy(e, first_row(e, j + 1), 1 - buf).start()

    lane = lax.broadcasted_iota(jnp.int32, (TM, LANES), 1)
    lane_f = lane.astype(F32)
    rows0 = []
    pieces = []
    for e in range(0, N_EXPERTS, LANES // WIN):
        tgt = None
        for i in range(LANES // WIN):
            row0 = first_row(e + i, j)
            win_copy(e + i, row0, buf).wait()
            rows0.append(row0)
            t_i = slott_ref[:, e + i:e + i + 1] - (row0 - WIN * i).astype(F32)
            in_win = (lane >= WIN * i) & (lane < WIN * (i + 1))
            tgt = jnp.where(in_win, t_i, -1.0) if tgt is None else jnp.where(in_win, t_i, tgt)
        pieces.append(jnp.where(tgt == lane_f, 1.0, 0.0).astype(BF16))
    onehot = jnp.concatenate(pieces, axis=1)
    o_ref[...] = _dot(onehot, win_s[buf].reshape(N_EXPERTS * WIN, D_MODEL))

    lane_w = lax.broadcasted_iota(jnp.int32, (TM, WIN), 1).astype(F32)
    for e in range(N_EXPERTS):
        row0 = rows0[e]
        n_more = jnp.maximum((a_ref[e * LANES + j + 1] - row0 + WIN - 1) // WIN - 1, 0)

        def more(i, _):
            lo_slot = row0 + (i + 1) * WIN
            r = pl.multiple_of(jnp.minimum(lo_slot, CAP_ALL - WIN), BF16_ROWS)
            cp = pltpu.make_async_copy(ye_hbm.at[e, pl.ds(r, WIN), :], xwin_s, xsem)
            cp.start()
            cp.wait()
            scol = slott_ref[:, e:e + 1]
            scol = jnp.where(scol >= lo_slot.astype(F32), scol, -1.0)
            oh = jnp.where(scol - r.astype(F32) == lane_w, 1.0, 0.0).astype(BF16)
            o_ref[...] += _dot(oh, xwin_s[...])
            return 0

        lax.fori_loop(0, n_more, more, 0)


def _combine_kernel_mid(a_ref, ye_hbm, slott_ref, x_ref, mod_ref, xo_ref, moe_s, win_s, sem, xwin_s, xsem):
    _combine_body(a_ref, ye_hbm, slott_ref, win_s, sem, xwin_s, xsem, moe_s)
    xo_ref[...] = x_ref[...] + mod_ref[...][:, 5 * D_MODEL:6 * D_MODEL] * moe_s[...]


def _combine_kernel_last(a_ref, ye_hbm, slott_ref, x_ref, mod_ref, fg_ref, yp_ref, ys_ref,
                         moe_s, win_s, sem, xwin_s, xsem):
    _combine_body(a_ref, ye_hbm, slott_ref, win_s, sem, xwin_s, xsem, moe_s)
    y = _rms(x_ref[...] + mod_ref[...][:, 5 * D_MODEL:6 * D_MODEL] * moe_s[...], fg_ref[...])

    @pl.when(pl.program_id(0) < NT_CTX)
    def _():
        yp_ref[...] = y

    @pl.when(pl.program_id(0) >= NT_CTX)
    def _():
        ys_ref[...] = y


def _combine(a, ye, slott, x, mod_l, fg):
    tile = lambda w: pl.BlockSpec((TM, w), lambda i, *_: (i, 0))
    in_specs = [pl.BlockSpec(memory_space=pl.ANY), tile(N_EXPERTS), tile(D_MODEL),
                pl.BlockSpec((None, 1, 6 * D_MODEL), lambda i, *_: (_mod_row(i), 0, 0))]
    args = [a, ye, slott, x, mod_l]
    sds = jax.ShapeDtypeStruct
    if fg is None:
        kern, out_specs, out_shape = _combine_kernel_mid, tile(D_MODEL), sds((T_ALL, D_MODEL), F32)
    else:
        kern = _combine_kernel_last
        in_specs.append(pl.BlockSpec(fg.shape, lambda i, *_: (0, 0)))
        args.append(fg)
        out_specs = [pl.BlockSpec((TM, D_MODEL), lambda i, *_: (jnp.minimum(i, NT_CTX - 1), 0)),
                     pl.BlockSpec((TM, D_MODEL), lambda i, *_: (jnp.maximum(i - NT_CTX, 0), 0))]
        out_shape = [sds((T_CTX, D_MODEL), F32), sds((T_SMP, D_MODEL), F32)]
    grid_spec = pltpu.PrefetchScalarGridSpec(
        num_scalar_prefetch=1, grid=(NT_ALL,), in_specs=in_specs, out_specs=out_specs,
        scratch_shapes=[pltpu.VMEM((TM, D_MODEL), F32),
                        pltpu.VMEM((2, N_EXPERTS, WIN, D_MODEL), BF16),
                        pltpu.SemaphoreType.DMA((2, N_EXPERTS)),
                        pltpu.VMEM((WIN, D_MODEL), BF16), pltpu.SemaphoreType.DMA(())])
    return pl.pallas_call(
        kern, grid_spec=grid_spec, out_shape=out_shape,
        compiler_params=_cparams(1), name="moe_combine",
    )(*args)


def _rope_tables():
    t = jnp.arange(DEC_SEQ)
    pos = jnp.stack([t // GRID_W, t % GRID_W], axis=-1).astype(F32)
    nf = DA_HD // 4
    inv = ROPE_BASE ** (-jnp.arange(nf, dtype=F32) / nf)
    ang = pos[:, :, None] * inv
    cos = jnp.cos(ang)
    sin = jnp.sin(ang)
    cos64 = jnp.stack([cos, cos], axis=2).reshape(DEC_SEQ, DA_HD)
    sin64 = jnp.stack([-sin, sin], axis=2).reshape(DEC_SEQ, DA_HD)
    cos_t = jnp.concatenate([jnp.ones((TM, LANES), F32), jnp.tile(cos64, (1, 2))], axis=0)
    sin_t = jnp.concatenate([jnp.zeros((TM, LANES), F32), jnp.tile(sin64, (1, 2))], axis=0)
    return cos_t, sin_t


def kernel(x_prompt, x_sample, cache_k, cache_v, state_ssm, state_mlstm_c, state_mlstm_n, state_mlstm_m, c, c_ctx, ada_w, ada_b, norm1_g, norm2_g, w_in, da_lambda, da_subln_g, mb_conv_w, mb_conv_b, mb_dt_bias, mb_a_log, mb_d, mb_norm_g, ml_gate_b, ml_norm_g, w_branch, w_mgate, b_mgate, w_out, router_w, ex_w_gate, ex_w_up, ex_w_down, final_g):
    x = (x_prompt.reshape(T_CTX, D_MODEL), x_sample.reshape(T_SMP, D_MODEL))
    cc = jnp.concatenate([c_ctx[None, :], c, jnp.zeros((16 - 1 - DEC_BATCH, D_MODEL), F32)], axis=0)
    mod = _modulation(cc, ada_w, ada_b).reshape(DEPTH, 16, 1, 6 * D_MODEL)
    cos_t, sin_t = _rope_tables()
    cache_k2 = cache_k.reshape(DEC_BATCH, DEPTH, PAST_LEN, DA_WIDTH)
    cache_v2 = cache_v.reshape(DEC_BATCH, DEPTH, PAST_LEN, DA_WIDTH)

    outs = {n: [] for n in ("k", "v", "ssm", "C", "n", "m")}
    fg = final_g[None]
    for l in range(DEPTH):
        w = w_in[l]
        wa = w[:, :2816].astype(BF16)
        wb = w[:, 2832:4368].astype(BF16)
        ws = jnp.concatenate([w[:, 2816:2832], w[:, 4368:4384]], axis=1).astype(BF16)
        res = _projection(x, mod[l], norm1_g[l][None], wa, wb, ws, ws.T, cos_t, sin_t)
        q, k, v, kf, vf, z, xbc, mq, mk, mv, mo, sm, smt = res[:13]
        if l == 0:
            x = res[13]
        outs["k"].append(kf.reshape(BATCH, SEQ, DA_HEADS, 2, DA_HD))
        outs["v"].append(vf.reshape(BATCH, SEQ, DA_HEADS, 2 * DA_HD))

        lp = da_lambda[l]
        sg = da_subln_g[l][None]
        cw = mb_conv_w[l]
        cb = mb_conv_b[l][None]
        dtb = mb_dt_bias[l].reshape(16)
        a_neg = -jnp.exp(mb_a_log[l]).reshape(16)
        dexp = jnp.repeat(mb_d[l], MB_HD)[None]
        mng = mb_norm_g[l][None]
        gb = ml_gate_b[l].reshape(16)
        lng = ml_norm_g[l].reshape(1, ML_WIDTH)
        ys = []
        for sample in (False, True):
            nb = DEC_BATCH if sample else BATCH
            y_da = _attention(l, q, k, v, cache_k2, cache_v2, lp, sg, sample)
            if sample:
                h0 = state_ssm[:, l]
                c0 = state_mlstm_c[:, l]
                n0 = state_mlstm_n[:, l]
                m0 = state_mlstm_m[:, l]
            else:
                h0 = jnp.zeros((nb, 2, MB_HEADS, MB_HD, MB_STATE), F32)
                c0 = jnp.zeros((nb, 2, ML_HEADS, ML_DK, ML_DV), F32)
                n0 = jnp.zeros((nb, 2, ML_HEADS, ML_DK), F32)
                m0 = jnp.zeros((nb, 2, ML_HEADS), F32)
            h0t = jnp.transpose(h0, (0, 1, 4, 2, 3)).reshape(nb, 2, MB_STATE, MB_INNER)
            y_mb, hfin = _ssd(z, xbc, sm, smt, cw, cb, dtb, a_neg, dexp, mng, h0t, sample)
            y_ml, cfin, nfin, mfin = _mlstm(
                mq, mk, mv, mo, sm, smt, gb, lng,
                c0.reshape(nb, 2, 2, LANES, ML_DV), n0.reshape(nb, 2, 2, LANES, 1),
                m0.reshape(nb, 1, 8), sample)
            ys.append((y_da, y_mb, y_ml))
            if not sample:
                outs["ssm"].append(jnp.transpose(
                    hfin.reshape(nb, 2, MB_STATE, MB_HEADS, MB_HD), (0, 1, 3, 4, 2)))
                outs["C"].append(cfin.reshape(nb, 2, ML_HEADS, ML_DK, ML_DV))
                outs["n"].append(nfin.reshape(nb, 2, ML_HEADS, ML_DK))
                outs["m"].append(mfin.reshape(nb, 2, ML_HEADS))
        x, h2t, affb = _merge(x, mod[l], norm1_g[l][None], ys[0], ys[1],
                              w_mgate[l].astype(BF16), b_mgate[l][None], w_branch[l].astype(BF16),
                              w_out[l].astype(BF16), norm2_g[l][None], router_w[l].T)
        slotb, slott, a_cnt, jlo, jhi = _route(affb)
        ye = _experts(l, jlo.reshape(-1), jhi.reshape(-1), h2t, slotb, affb,
                      ex_w_gate, ex_w_up, ex_w_down)
        if l + 1 < DEPTH:
            x = _combine(a_cnt.reshape(-1), ye, slott, x, mod[l], None)
        else:
            y_prompt, y_sample = _combine(a_cnt.reshape(-1), ye, slott, x, mod[l], fg)

    y_prompt = y_prompt.reshape(BATCH, SEQ, D_MODEL)
    y_sample = y_sample.reshape(DEC_BATCH, DEC_SEQ, D_MODEL)
    return (y_prompt, y_sample, jnp.stack(outs["k"], axis=1), jnp.stack(outs["v"], axis=1),
            jnp.stack(outs["ssm"], axis=1), jnp.stack(outs["C"], axis=1),
            jnp.stack(outs["n"], axis=1), jnp.stack(outs["m"], axis=1))
```

```python
import functools
import math

import jax
import jax.numpy as jnp
from jax import lax
from jax.experimental import pallas as pl
from jax.experimental.pallas import tpu as pltpu

F32 = jnp.float32
BF16 = jnp.bfloat16

D_MODEL = 1024
BATCH = 16
SEQ = 256
DEPTH = 2
DEC_BATCH = 8
DEC_SEQ = 2048
PAST_LEN = 512
GRID_W = 64
EPS = 1e-6
CHUNK = 128
ROPE_BASE = 10000.0
DA_HEADS = 4
DA_HD = 64
DA_WIDTH = 512
MB_INNER = 512
MB_HD = 64
MB_HEADS = 8
MB_GROUPS = 2
MB_STATE = 64
MB_CONV_DIM = 768
ML_HEADS = 4
ML_DK = 64
ML_DV = 128
ML_WIDTH = 512
N_EXPERTS = 16
EC_FACTOR = 2
EXPERT_FF = 1024

T_CTX = BATCH * SEQ
T_SMP = DEC_BATCH * DEC_SEQ
T_ALL = T_CTX + T_SMP
TM = 256
NT_CTX = T_CTX // TM
NT_ALL = T_ALL // TM
NB_CTX = T_CTX // CHUNK
NB_ALL = T_ALL // CHUNK
CAP_CTX = EC_FACTOR * T_CTX // N_EXPERTS
CAP_SMP = EC_FACTOR * T_SMP // N_EXPERTS
CAP_ALL = CAP_CTX + CAP_SMP
RB = 256
N_RB = CAP_ALL // RB
WIN = 64
GATHER_TILES = 5
GATHER_AHEAD = 4
ROUTE_GROUP = 4
TILES_PER_REQ = DEC_SEQ // TM
TM_MERGE = 512
BRANCH_W = 512
LANES = 128
BF16_ROWS = 16
VMEM_LIMIT = 56 * 1024 * 1024
HI = lax.Precision.HIGHEST
LOG2E = math.log2(math.e)

C_Q, C_K, C_V, C_Z, C_XBC, C_MQ, C_MK, C_MV, C_MO, C_END = (
    0, 512, 1024, 1536, 2048, 2816, 3072, 3328, 3840, 4352)


def _mod_row(i):
    return jnp.where(i < NT_CTX, 0, 1 + (i - NT_CTX) // TILES_PER_REQ)


def _rope_blk(i):
    return jnp.where(i < NT_CTX, 0, 1 + (i - NT_CTX) % TILES_PER_REQ)


def _cparams(n_grid):
    return pltpu.CompilerParams(dimension_semantics=("arbitrary",) * n_grid,
                                vmem_limit_bytes=VMEM_LIMIT)


def _silu(x):
    return x * jax.nn.sigmoid(x)


def _softplus(x):
    u = jnp.exp(-jnp.abs(x))
    w = 1.0 + u
    l1p = jnp.where(w == 1.0, u, jnp.log(w) * (u / (w - 1.0)))
    return jnp.maximum(x, 0.0) + l1p


def _dot(a, b):
    return jnp.dot(a, b, preferred_element_type=F32)


def _dot_nt(a, b):
    return lax.dot_general(a, b, (((1,), (1,)), ((), ())), preferred_element_type=F32)


def _dot_hi(a, b):
    return jnp.dot(a, b, precision=HI, preferred_element_type=F32)


def _split3(a):
    hi = a.astype(BF16)
    r = a - hi.astype(F32)
    mid = r.astype(BF16)
    return hi, mid, (r - mid.astype(F32)).astype(BF16)


def _mask3(mask, axis):
    m = jnp.where(mask, 1.0, 0.0).astype(BF16)
    return jnp.concatenate([m, m, m], axis=axis)


def _mask_dot(m3, a):
    return _dot(m3, jnp.concatenate(_split3(a), axis=0))


def _dot_mask(a, m3):
    return _dot(jnp.concatenate(_split3(a), axis=1), m3)


def _dot_mask_narrow(a, m):
    hi, mid, lo = _split3(a)
    return _dot(hi, m) + _dot(mid, m) + _dot(lo, m)


def _rms(x, g):
    return x * lax.rsqrt(jnp.mean(x * x, axis=-1, keepdims=True) + EPS) * g


def _mod_kernel(c_ref, w_ref, b_ref, o_ref):
    s = _silu(c_ref[...])
    o_ref[...] = _dot(s.astype(BF16), w_ref[...].astype(BF16)) + b_ref[...]


def _modulation(cc, ada_w, ada_b):
    tn = 1536
    return pl.pallas_call(
        _mod_kernel,
        grid=(DEPTH, 6 * D_MODEL // tn),
        in_specs=[pl.BlockSpec((16, D_MODEL), lambda l, j: (0, 0)),
                  pl.BlockSpec((None, D_MODEL, tn), lambda l, j: (l, 0, j)),
                  pl.BlockSpec((None, 1, tn), lambda l, j: (l, 0, j))],
        out_specs=pl.BlockSpec((None, 16, tn), lambda l, j: (l, 0, j)),
        out_shape=jax.ShapeDtypeStruct((DEPTH, 16, 6 * D_MODEL), F32),
        compiler_params=_cparams(2),
        name="adaln_mod",
    )(cc, ada_w, ada_b.reshape(DEPTH, 1, 6 * D_MODEL))


def _rope(t, cos, sin, first_half):
    outs = []
    for c in range(DA_WIDTH // LANES):
        xc = t[:, LANES * c:LANES * (c + 1)]
        partner = jnp.where(first_half, pltpu.roll(xc, LANES - 16, 1), pltpu.roll(xc, 16, 1))
        outs.append(xc * cos + partner * sin)
    return jnp.concatenate(outs, axis=1)


def _proj_body(x, mod_ref, g_ref, wa_ref, wb_ref, ws_ref, wst_ref, cos_ref, sin_ref,
               q_ref, k_ref, v_ref, kf_ref, vf_ref, z_ref, xbc_ref,
               mq_ref, mk_ref, mv_ref, mo_ref, sm_ref, smt_ref):
    mod = mod_ref[...]
    sh1 = mod[:, 0:D_MODEL]
    sc1 = mod[:, D_MODEL:2 * D_MODEL]
    h = _rms(x, g_ref[...]) * (1.0 + sc1) + sh1
    hb = h.astype(BF16)

    def proj(a, b):
        if b <= C_MQ:
            return _dot(hb, wa_ref[:, a:b])
        return _dot(hb, wb_ref[:, a - C_MQ:b - C_MQ])

    lane = lax.broadcasted_iota(jnp.int32, (TM, LANES), 1)
    first_half = (lane % 32) < 16
    cos = cos_ref[...]
    sin = sin_ref[...]
    q = proj(C_Q, C_K)
    k = proj(C_K, C_V)
    v = proj(C_V, C_Z)
    q_ref[...] = (_rope(q, cos, sin, first_half) * (DA_HD ** -0.5 * LOG2E)).astype(BF16)
    k_ref[...] = _rope(k, cos, sin, first_half).astype(BF16)
    v_ref[...] = v.astype(BF16)
    z_ref[...] = proj(C_Z, C_XBC)
    xbc_ref[...] = proj(C_XBC, C_MQ)
    mq_ref[...] = proj(C_MQ, C_MK).astype(BF16)
    mk_ref[...] = (proj(C_MK, C_MV) * (ML_DK ** -0.5)).astype(BF16)
    mv_ref[...] = proj(C_MV, C_MO).astype(BF16)
    mo_ref[...] = proj(C_MO, C_END)
    sm_ref[...] = _dot(hb, ws_ref[...])
    st = _dot_nt(wst_ref[...], hb)
    for j in range(TM // CHUNK):
        smt_ref[j] = st[:, CHUNK * j:CHUNK * (j + 1)]

    @pl.when(pl.program_id(0) < NT_CTX)
    def _():
        kf_ref[...] = k
        vf_ref[...] = v


def _proj_kernel_first(xp_ref, xs_ref, *refs):
    x = jnp.where(pl.program_id(0) < NT_CTX, xp_ref[...], xs_ref[...])
    refs[-1][...] = x
    _proj_body(x, *refs[:-1])


def _proj_kernel_next(x_ref, *refs):
    _proj_body(x_ref[...], *refs)


def _ctx_tile(w):
    return pl.BlockSpec((TM, w), lambda i: (jnp.minimum(i, NT_CTX - 1), 0))


def _smp_tile(w):
    return pl.BlockSpec((TM, w), lambda i: (jnp.maximum(i - NT_CTX, 0), 0))


def _projection(xs, mod_l, g1, wa, wb, ws, wst, cos_t, sin_t):
    tile = lambda w: pl.BlockSpec((TM, w), lambda i: (i, 0))
    full = lambda a: pl.BlockSpec(a.shape, lambda i: (0,) * a.ndim)
    modspec = pl.BlockSpec((None, 1, 6 * D_MODEL), lambda i: (_mod_row(i), 0, 0))
    first = isinstance(xs, tuple)
    if first:
        in_specs = [_ctx_tile(D_MODEL), _smp_tile(D_MODEL)]
        args = list(xs)
        kern = _proj_kernel_first
    else:
        in_specs = [tile(D_MODEL)]
        args = [xs]
        kern = _proj_kernel_next
    in_specs += [modspec, full(g1), full(wa), full(wb), full(ws), full(wst),
                 pl.BlockSpec((TM, LANES), lambda i: (_rope_blk(i), 0)),
                 pl.BlockSpec((TM, LANES), lambda i: (_rope_blk(i), 0))]
    args += [mod_l, g1, wa, wb, ws, wst, cos_t, sin_t]
    ctx_only = _ctx_tile(DA_WIDTH)
    out_specs = [tile(512), tile(512), tile(512), ctx_only, ctx_only, tile(512), tile(768),
                 tile(256), tile(256), tile(512), tile(512), tile(32),
                 pl.BlockSpec((TM // CHUNK, 32, CHUNK), lambda i: (i, 0, 0))]
    sds = jax.ShapeDtypeStruct
    out_shape = [sds((T_ALL, 512), BF16), sds((T_ALL, 512), BF16), sds((T_ALL, 512), BF16),
                 sds((T_CTX, 512), F32), sds((T_CTX, 512), F32),
                 sds((T_ALL, 512), F32), sds((T_ALL, 768), F32),
                 sds((T_ALL, 256), BF16), sds((T_ALL, 256), BF16), sds((T_ALL, 512), BF16),
                 sds((T_ALL, 512), F32), sds((T_ALL, 32), F32),
                 sds((T_ALL // CHUNK, 32, CHUNK), F32)]
    if first:
        out_specs.append(tile(D_MODEL))
        out_shape.append(sds((T_ALL, D_MODEL), F32))
    return pl.pallas_call(
        kern, grid=(NT_ALL,), in_specs=in_specs, out_specs=out_specs, out_shape=out_shape,
        compiler_params=_cparams(1), name="in_proj",
    )(*args)


def _attn_body(lam_init, q_ref, k_ref, v_ref, kc_ref, vc_ref, lp_ref, g_ref, o_ref):
    lp = lp_ref[...]
    s01 = jnp.sum(lp[0:1] * lp[1:2], axis=-1, keepdims=True)
    s23 = jnp.sum(lp[2:3] * lp[3:4], axis=-1, keepdims=True)
    lam = jnp.exp(s01) - jnp.exp(s23) + lam_init
    tq = q_ref.shape[0]
    lane = lax.broadcasted_iota(jnp.int32, (tq, LANES), 1)
    g = g_ref[...]
    def scores(h, m):
        cols = slice(LANES * h, LANES * (h + 1))
        qh = q_ref[:, cols]
        qm = jnp.where((lane < DA_HD) == (m == 0), qh, jnp.zeros_like(qh))
        s = _dot_nt(qm, k_ref[:, cols])
        sc = None if kc_ref is None else _dot_nt(qm, kc_ref[:, cols].astype(BF16))
        return s, sc

    units = [(h, m) for h in range(DA_HEADS) for m in range(2)]
    nxt = scores(*units[0])
    parts = []
    for u, (h, m) in enumerate(units):
        cols = slice(LANES * h, LANES * (h + 1))
        s, sc = nxt
        if u + 1 < len(units):
            nxt = scores(*units[u + 1])
        vh = jnp.concatenate([v_ref[:, cols], jnp.ones((k_ref.shape[0], LANES), BF16)], axis=1)
        mx = jnp.max(s, axis=-1, keepdims=True)
        if sc is not None:
            mx = jnp.maximum(mx, jnp.max(sc, axis=-1, keepdims=True))
        acc = _dot(jnp.exp2(s - mx).astype(BF16), vh)
        if sc is not None:
            vch = jnp.concatenate([vc_ref[:, cols].astype(BF16), jnp.ones((PAST_LEN, LANES), BF16)], axis=1)
            acc = acc + _dot(jnp.exp2(sc - mx).astype(BF16), vch)
        parts.append(acc[:, :LANES] / acc[:, LANES:])
        if m == 1:
            att = parts[-2] - lam * parts[-1]
            o_ref[:, cols] = (_rms(att, g) * (1.0 - lam_init)).astype(BF16)


def _attn_kernel_ctx(lam_init, q_ref, k_ref, v_ref, lp_ref, g_ref, o_ref):
    _attn_body(lam_init, q_ref, k_ref, v_ref, None, None, lp_ref, g_ref, o_ref)


def _attn_kernel_smp(lam_init, q_ref, k_ref, v_ref, kc_ref, vc_ref, lp_ref, g_ref, o_ref):
    _attn_body(lam_init, q_ref, k_ref, v_ref, kc_ref, vc_ref, lp_ref, g_ref, o_ref)


def _attention(l, q, k, v, cache_k, cache_v, lp, g, sample):
    lam_init = 0.8 - 0.6 * math.exp(-0.3 * l)
    full = lambda a: pl.BlockSpec(a.shape, lambda *_: (0,) * a.ndim)
    sds = jax.ShapeDtypeStruct
    if not sample:
        blk = pl.BlockSpec((SEQ, DA_WIDTH), lambda b: (b, 0))
        return pl.pallas_call(
            functools.partial(_attn_kernel_ctx, lam_init), grid=(BATCH,),
            in_specs=[blk, blk, blk, full(lp), full(g)],
            out_specs=blk, out_shape=sds((T_CTX, DA_WIDTH), BF16),
            compiler_params=_cparams(1), name="diff_attn_ctx",
        )(q, k, v, lp, g)
    tq = 256
    nq = DEC_SEQ // tq
    off_q = T_CTX // tq
    off_k = T_CTX // DEC_SEQ
    qblk = pl.BlockSpec((tq, DA_WIDTH), lambda b, i: (off_q + b * nq + i, 0))
    kblk = pl.BlockSpec((DEC_SEQ, DA_WIDTH), lambda b, i: (off_k + b, 0))
    cblk = pl.BlockSpec((None, None, PAST_LEN, DA_WIDTH), lambda b, i: (b, l, 0, 0))
    return pl.pallas_call(
        functools.partial(_attn_kernel_smp, lam_init), grid=(DEC_BATCH, nq),
        in_specs=[qblk, kblk, kblk, cblk, cblk, full(lp), full(g)],
        out_specs=pl.BlockSpec((tq, DA_WIDTH), lambda b, i: (b * nq + i, 0)),
        out_shape=sds((T_SMP, DA_WIDTH), BF16),
        compiler_params=_cparams(2), name="diff_attn_smp",
    )(q, k, v, cache_k, cache_v, lp, g)


def _tri_masks():
    r = lax.broadcasted_iota(jnp.int32, (CHUNK, CHUNK), 0)
    c = lax.broadcasted_iota(jnp.int32, (CHUNK, CHUNK), 1)
    return c <= r, c >= r


def _ssd_kernel(L, z_ref, xbc_ref, sm_ref, smt_ref, cw_ref, cb_ref, dtbr_ref, dtbc_ref,
                ar_ref, ac_ref, dexp_ref, ng_ref, h0_ref,
                y_ref, hfin_ref, xc_s, yacc_s, ht_s):
    nc = L // CHUNK
    low, upp = _tri_masks()
    tri_l3 = (_mask3(low, 1), _mask3(upp, 1))
    tri_r3 = (_mask3(upp, 0), _mask3(low, 0))
    lane512 = lax.broadcasted_iota(jnp.int32, (1, MB_INNER), 1)
    row16 = lax.broadcasted_iota(jnp.int32, (16, MB_INNER), 0)
    lane16 = lax.broadcasted_iota(jnp.int32, (16, MB_INNER), 1)
    expand = tuple(_mask3(row16 == 8 * d + lane16 // MB_HD, 0) for d in range(2))
    lane128 = lax.broadcasted_iota(jnp.int32, (CHUNK, LANES), 1)
    rowblk = lax.broadcasted_iota(jnp.int32, (LANES, MB_INNER), 0) // MB_STATE
    colblk = lax.broadcasted_iota(jnp.int32, (LANES, MB_INNER), 1) // (MB_INNER // MB_GROUPS)
    same_group = rowblk == colblk
    cw = cw_ref[...]
    cbias = cb_ref[...]

    def conv_chunk(c, _):
        base = pl.multiple_of(c * CHUNK, CHUNK)
        x = xbc_ref[pl.ds(base, CHUNK), :]
        prev = xbc_ref[pl.ds(jnp.maximum(base - 1, 0), 1), :]
        nxt = xbc_ref[pl.ds(jnp.minimum(base + CHUNK, L - 1), 1), :]
        prev = jnp.where(c == 0, 0.0, prev)
        nxt = jnp.where(c == nc - 1, 0.0, nxt)
        row = lax.broadcasted_iota(jnp.int32, (CHUNK, 1), 0)
        xp = jnp.where(row == 0, prev, pltpu.roll(x, 1, 0))
        xn = jnp.where(row == CHUNK - 1, nxt, pltpu.roll(x, CHUNK - 1, 0))
        conv = xp * cw[0:1] + x * cw[1:2] + xn * cw[2:3] + cbias
        xc_s[pl.ds(base, CHUNK), :] = _silu(conv)
        return 0

    lax.fori_loop(0, nc, conv_chunk, 0)

    h0 = h0_ref[...]
    for d in range(2):
        h0d = h0[d]
        ht_s[d] = jnp.concatenate(
            [jnp.where(lane512 < MB_INNER // 2, h0d, 0.0),
             jnp.where(lane512 >= MB_INNER // 2, h0d, 0.0)], axis=0)

    def both(i, _):
        jobs = ((0, i), (1, nc - 1 - i))
        rows, xsb, bb, cb, dt_row, cum_col, cum_row = {}, {}, {}, {}, {}, {}, {}
        y_off, cbg = {}, {}
        for d, c in jobs:
            rows[d] = pl.ds(pl.multiple_of(c * CHUNK, CHUNK), CHUNK)
            xs = xc_s[rows[d], 0:MB_INNER]
            bm = xc_s[rows[d], MB_INNER:MB_INNER + LANES]
            cm = xc_s[rows[d], MB_INNER + LANES:MB_CONV_DIM]
            dt_col = _softplus(sm_ref[rows[d], 0:16] + dtbr_ref[...])
            a_col = dt_col * ar_ref[...]
            dt_row[d] = _softplus(smt_ref[c][0:16, :] + dtbc_ref[...])
            a_row = dt_row[d] * ac_ref[...]
            cum_col[d] = _mask_dot(tri_l3[d], a_col)
            cum_row[d] = _dot_mask(a_row, tri_r3[d])
            last = 0 if d else CHUNK - 1
            cum_last = cum_col[d][last:last + 1, :]
            w_exp = _dot_mask(jnp.exp(cum_last - cum_col[d]) * dt_col, expand[d])
            g_exp = _dot_mask(jnp.exp(cum_col[d]), expand[d])
            cd_exp = _dot_mask(jnp.broadcast_to(jnp.exp(cum_last), (8, 16)), expand[d])[0:1]
            xw = (xs * w_exp).astype(BF16)
            xsb[d] = xs.astype(BF16)
            bb[d] = bm.astype(BF16)
            cb[d] = cm.astype(BF16)
            ht = ht_s[d]
            y_off[d] = _dot(cb[d], ht.astype(BF16)) * g_exp
            s_new = _dot(bm.T.astype(BF16), xw)
            ht_s[d] = ht * cd_exp + jnp.where(same_group, s_new, 0.0)
            for g in range(MB_GROUPS):
                cg = jnp.where((lane128 < MB_STATE) == (g == 0), cb[d], jnp.zeros_like(cb[d]))
                cbg[d, g] = _dot_nt(cg, bb[d])
        m16 = {}
        for d, _c in jobs:
            for h in range(MB_HEADS):
                ci = 8 * d + h
                seg = cum_col[d][:, ci:ci + 1] - cum_row[d][ci:ci + 1, :]
                m = jnp.where(upp if d else low, jnp.exp(seg), 0.0) * cbg[d, h // 4] * dt_row[d][ci:ci + 1, :]
                m16[d, h] = m.astype(BF16)
        yd = {}
        for d, _c in jobs:
            for h in range(MB_HEADS):
                k = h // 2
                yd[d, h] = _dot(m16[d, h], xsb[d][:, LANES * k:LANES * (k + 1)])
        for d, _c in jobs:
            pairs = [jnp.where(lane128 < MB_HD, yd[d, 2 * k], yd[d, 2 * k + 1]) for k in range(MB_HEADS // 2)]
            yacc_s[d, rows[d], :] = jnp.concatenate(pairs, axis=1) + y_off[d]
        return 0

    lax.fori_loop(0, nc, both, 0)

    dexp = dexp_ref[...]
    ng = ng_ref[...]

    def fin(c, _):
        rows = pl.ds(pl.multiple_of(c * CHUNK, CHUNK), CHUNK)
        y = yacc_s[0, rows, :] + yacc_s[1, rows, :] + dexp * xc_s[rows, 0:MB_INNER]
        y = y * _silu(z_ref[rows, :])
        y_ref[rows, :] = _rms(y, ng).astype(BF16)
        return 0

    lax.fori_loop(0, nc, fin, 0)
    for d in range(2):
        ht = ht_s[d]
        hfin_ref[d] = ht[0:MB_STATE, :] + ht[MB_STATE:2 * MB_STATE, :]


def _ssd(z, xbc, sm, smt, cw, cb, dtb, a_neg, dexp, ng, h0t, sample):
    nb, L, off = (DEC_BATCH, DEC_SEQ, T_CTX // DEC_SEQ) if sample else (BATCH, SEQ, 0)
    full = lambda a: pl.BlockSpec(a.shape, lambda b: (0,) * a.ndim)
    seq = lambda w: pl.BlockSpec((L, w), lambda b: (off + b, 0))
    dtb_r, dtb_c = dtb.reshape(1, 16), dtb.reshape(16, 1)
    a_r, a_c = a_neg.reshape(1, 16), a_neg.reshape(16, 1)
    sds = jax.ShapeDtypeStruct
    return pl.pallas_call(
        functools.partial(_ssd_kernel, L), grid=(nb,),
        in_specs=[seq(MB_INNER), seq(MB_CONV_DIM), seq(32),
                  pl.BlockSpec((L // CHUNK, 32, CHUNK), lambda b: (off + b, 0, 0)),
                  full(cw), full(cb), full(dtb_r), full(dtb_c), full(a_r), full(a_c),
                  full(dexp), full(ng),
                  pl.BlockSpec((None, 2, MB_STATE, MB_INNER), lambda b: (b, 0, 0, 0))],
        out_specs=[pl.BlockSpec((L, MB_INNER), lambda b: (b, 0)),
                   pl.BlockSpec((None, 2, MB_STATE, MB_INNER), lambda b: (b, 0, 0, 0))],
        out_shape=[sds((nb * L, MB_INNER), BF16), sds((nb, 2, MB_STATE, MB_INNER), F32)],
        scratch_shapes=[pltpu.VMEM((L, MB_CONV_DIM), F32), pltpu.VMEM((2, L, MB_INNER), F32),
                        pltpu.VMEM((2, LANES, MB_INNER), F32)],
        compiler_params=_cparams(1), name="ssd_smp" if sample else "ssd_ctx",
    )(z, xbc, sm, smt, cw, cb, dtb_r, dtb_c, a_r, a_c, dexp, ng, h0t)


def _mlstm_kernel(L, q_ref, k_ref, v_ref, o_ref, sm_ref, smt_ref, gbr_ref, gbc_ref, ng_ref,
                  c0_ref, n0_ref, m0_ref,
                  y_ref, cf_ref, nf_ref, mf_ref, hacc_s, c_s, n_s):
    nc = L // CHUNK
    low, upp = _tri_masks()
    tri_l = (low.astype(F32), upp.astype(F32))
    tri_r = (upp.astype(F32), low.astype(F32))
    lane128 = lax.broadcasted_iota(jnp.int32, (CHUNK, LANES), 1)
    lane8 = lax.broadcasted_iota(jnp.int32, (1, 2 * ML_HEADS), 1)
    neg_inf = -jnp.inf

    ones16 = jnp.ones((CHUNK, LANES), BF16)
    c_s[...] = c0_ref[...]
    n_s[...] = jnp.broadcast_to(n0_ref[...], n_s.shape)
    m0 = m0_ref[...]

    def both(i, m_in):
        jobs = ((0, i), (1, nc - 1 - i))
        heads = [(d, h) for d, _ in jobs for h in range(ML_HEADS)]
        rows, pre_row, b_col, b_row = {}, {}, {}, {}
        for d, c in jobs:
            rows[d] = pl.ds(pl.multiple_of(c * CHUNK, CHUNK), CHUNK)
            pre_col = sm_ref[rows[d], 16:32] + gbr_ref[...]
            pre_row[d] = smt_ref[c][16:32, :] + gbc_ref[...]
            lf_col = -_softplus(-pre_col)
            lf_row = -_softplus(-pre_row[d])
            b_col[d] = _dot_hi(tri_l[d], lf_col)
            b_row[d] = _dot_hi(lf_row, tri_r[d])
        kt, cst, nst, vh, qk, qc, qn = {}, {}, {}, {}, {}, {}, {}
        for d, _ in jobs:
            for pr in range(2):
                qp = q_ref[rows[d], LANES * pr:LANES * (pr + 1)]
                kp = k_ref[rows[d], LANES * pr:LANES * (pr + 1)]
                kt[d, pr] = kp.astype(F32).T
                cst[d, pr] = c_s[d, pr]
                nst[d, pr] = n_s[d, pr]
                cb16 = cst[d, pr].astype(BF16)
                nb16 = nst[d, pr].astype(BF16)
                for hh in range(2):
                    h = 2 * pr + hh
                    qm = jnp.where((lane128 < ML_DK) == (hh == 0), qp, jnp.zeros_like(qp))
                    vh[d, h] = jnp.concatenate([v_ref[rows[d], ML_DV * h:ML_DV * (h + 1)], ones16], axis=1)
                    qk[d, h] = _dot_nt(qm, kp)
                    qc[d, h] = _dot(qm, cb16)
                    qn[d, h] = _dot(qm, nb16)
        m_t, s_intra, s_inter, m_new, s_old, kts = {}, {}, {}, {}, {}, {}
        for d, h in heads:
            m_st = m_in[:, 4 * d + h:4 * d + h + 1]
            bcol = jnp.broadcast_to(b_col[d][:, 8 * d + 4 + h:8 * d + 5 + h], (CHUNK, CHUNK))
            brow = b_row[d][8 * d + 4 + h:8 * d + 5 + h, :]
            li_row = pre_row[d][8 * d + h:8 * d + h + 1, :]
            dm = jnp.where(upp if d else low, bcol - brow + li_row, neg_inf)
            inter = bcol + m_st
            m_t[d, h] = jnp.maximum(inter, jnp.max(dm, axis=-1, keepdims=True))
            s_intra[d, h] = jnp.exp(dm - m_t[d, h]) * qk[d, h]
            s_inter[d, h] = jnp.exp(inter - m_t[d, h])
            last = 0 if d else CHUNK - 1
            b_end = brow[:, last:last + 1]
            w_end = b_end - brow + li_row
            m_new[d, h] = jnp.maximum(b_end + m_st, jnp.max(w_end, axis=-1, keepdims=True))
            s_old[d, h] = jnp.exp(b_end + m_st - m_new[d, h])
            half = slice(ML_DK * (h % 2), ML_DK * (h % 2 + 1))
            kts[d, h] = kt[d, h // 2][half, :] * jnp.exp(w_end - m_new[d, h])
        pv, kv = {}, {}
        for d, h in heads:
            pv[d, h] = _dot(s_intra[d, h].astype(BF16), vh[d, h])
            kv[d, h] = _dot(kts[d, h].astype(BF16), vh[d, h])
        m_out = m_in
        for d, h in heads:
            num = s_inter[d, h] * qc[d, h] + pv[d, h][:, :ML_DV]
            den = s_inter[d, h] * qn[d, h] + pv[d, h][:, ML_DV:]
            hout = num / jnp.maximum(jnp.abs(den), jnp.exp(-m_t[d, h]))
            hacc_s[d, rows[d], ML_DV * h:ML_DV * (h + 1)] = hout
            m_out = jnp.where(lane8 == 4 * d + h, m_new[d, h], m_out)
        for d, _ in jobs:
            for pr in range(2):
                c_new, n_new = [], []
                for hh in range(2):
                    h = 2 * pr + hh
                    half = slice(ML_DK * hh, ML_DK * (hh + 1))
                    c_new.append(s_old[d, h] * cst[d, pr][half, :] + kv[d, h][:, :ML_DV])
                    n_new.append(s_old[d, h] * nst[d, pr][half, :] + kv[d, h][:, ML_DV:])
                c_s[d, pr] = jnp.concatenate(c_new, axis=0)
                n_s[d, pr] = jnp.concatenate(n_new, axis=0)
        return m_out

    m_fin = lax.fori_loop(0, nc, both, m0)

    ng = ng_ref[...]

    def fin(c, _):
        rows = pl.ds(pl.multiple_of(c * CHUNK, CHUNK), CHUNK)
        for h in range(ML_HEADS):
            cols = slice(ML_DV * h, ML_DV * (h + 1))
            y = _rms(hacc_s[0, rows, cols] + hacc_s[1, rows, cols], ng[:, cols]) * jax.nn.sigmoid(o_ref[rows, cols])
            y_ref[rows, cols] = y.astype(BF16)
        return 0

    lax.fori_loop(0, nc, fin, 0)
    cf_ref[...] = c_s[...]
    nf_ref[...] = n_s[:, :, :, 0:1]
    mf_ref[...] = m_fin


def _mlstm(q, k, v, o, sm, smt, gb, ng, c0, n0, m0, sample):
    nb, L, off = (DEC_BATCH, DEC_SEQ, T_CTX // DEC_SEQ) if sample else (BATCH, SEQ, 0)
    full = lambda a: pl.BlockSpec(a.shape, lambda b: (0,) * a.ndim)
    seq = lambda w: pl.BlockSpec((L, w), lambda b: (off + b, 0))
    gb_r, gb_c = gb.reshape(1, 16), gb.reshape(16, 1)
    st_c = pl.BlockSpec((None, 2, 2, LANES, ML_DV), lambda b: (b, 0, 0, 0, 0))
    st_n = pl.BlockSpec((None, 2, 2, LANES, 1), lambda b: (b, 0, 0, 0, 0))
    st_m = pl.BlockSpec((None, 1, 8), lambda b: (b, 0, 0))
    sds = jax.ShapeDtypeStruct
    return pl.pallas_call(
        functools.partial(_mlstm_kernel, L), grid=(nb,),
        in_specs=[seq(256), seq(256), seq(512), seq(512), seq(32),
                  pl.BlockSpec((L // CHUNK, 32, CHUNK), lambda b: (off + b, 0, 0)),
                  full(gb_r), full(gb_c), full(ng), st_c, st_n, st_m],
        out_specs=[pl.BlockSpec((L, ML_WIDTH), lambda b: (b, 0)), st_c, st_n, st_m],
        out_shape=[sds((nb * L, ML_WIDTH), BF16), sds((nb, 2, 2, LANES, ML_DV), F32),
                   sds((nb, 2, 2, LANES, 1), F32), sds((nb, 1, 8), F32)],
        scratch_shapes=[pltpu.VMEM((2, L, ML_WIDTH), F32), pltpu.VMEM((2, 2, LANES, ML_DV), F32),
                        pltpu.VMEM((2, 2, LANES, LANES), F32)],
        compiler_params=_cparams(1), name="mlstm_smp" if sample else "mlstm_ctx",
    )(q, k, v, o, sm, smt, gb_r, gb_c, ng, c0, n0, m0)


def _merge_kernel(x_ref, mod_ref, g1_ref, ydac_ref, ymbc_ref, ymlc_ref, ydas_ref, ymbs_ref, ymls_ref,
                  wg_ref, bg_ref, wb_ref, wo_ref, g2_ref, rwt_ref, xo_ref, h2t_ref, affb_ref):
    x = x_ref[...]
    mod = mod_ref[...]
    sh1, sc1, gt1, sh2, sc2 = (mod[:, j * D_MODEL:(j + 1) * D_MODEL] for j in range(5))
    hb = (_rms(x, g1_ref[...]) * (1.0 + sc1) + sh1).astype(BF16)
    is_ctx = pl.program_id(0) < T_CTX // TM_MERGE
    merged = None
    for n, (yc_ref, ys_ref) in enumerate(((ydac_ref, ydas_ref), (ymbc_ref, ymbs_ref), (ymlc_ref, ymls_ref))):
        cols = slice(n * D_MODEL, (n + 1) * D_MODEL)
        gate = jax.nn.sigmoid(_dot(hb, wg_ref[:, cols]) + bg_ref[:, cols])
        y = jnp.where(is_ctx, yc_ref[...], ys_ref[...])
        term = gate * _dot(y, wb_ref[n])
        merged = term if merged is None else merged + term
    out = _dot(merged.astype(BF16), wo_ref[...])
    xn = x + gt1 * out
    xo_ref[...] = xn
    h2 = _rms(xn, g2_ref[...]) * (1.0 + sc2) + sh2
    h2t_ref[...] = h2.T.astype(BF16)
    logits = lax.dot_general(rwt_ref[...], h2, (((1,), (1,)), ((), ())),
                             precision=HI, preferred_element_type=F32)
    e = jnp.exp(logits - jnp.max(logits, axis=0, keepdims=True))
    aff = e / jnp.sum(e, axis=0, keepdims=True)
    for j in range(TM_MERGE // CHUNK):
        affb_ref[j] = aff[:, CHUNK * j:CHUNK * (j + 1)]


def _merge(x, mod_l, g1, ys_ctx, ys_smp, wg, bg, wb, wo, g2, rwt):
    tm = TM_MERGE
    n_ctx = T_CTX // tm
    per_req = DEC_SEQ // tm
    tile = lambda w: pl.BlockSpec((tm, w), lambda i: (i, 0))
    ctx_tile = pl.BlockSpec((tm, BRANCH_W), lambda i: (jnp.minimum(i, n_ctx - 1), 0))
    smp_tile = pl.BlockSpec((tm, BRANCH_W), lambda i: (jnp.maximum(i - n_ctx, 0), 0))
    full = lambda a: pl.BlockSpec(a.shape, lambda i: (0,) * a.ndim)
    mod_row = lambda i: jnp.where(i < n_ctx, 0, 1 + (i - n_ctx) // per_req)
    sds = jax.ShapeDtypeStruct
    return pl.pallas_call(
        _merge_kernel, grid=(T_ALL // tm,),
        in_specs=[tile(D_MODEL), pl.BlockSpec((None, 1, 6 * D_MODEL), lambda i: (mod_row(i), 0, 0)),
                  full(g1), ctx_tile, ctx_tile, ctx_tile, smp_tile, smp_tile, smp_tile,
                  full(wg), full(bg), full(wb), full(wo), full(g2), full(rwt)],
        out_specs=[tile(D_MODEL), pl.BlockSpec((D_MODEL, tm), lambda i: (0, i)),
                   pl.BlockSpec((tm // CHUNK, N_EXPERTS, CHUNK), lambda i: (i, 0, 0))],
        out_shape=[sds((T_ALL, D_MODEL), F32), sds((D_MODEL, T_ALL), BF16),
                   sds((NB_ALL, N_EXPERTS, CHUNK), F32)],
        compiler_params=_cparams(1), name="merge_out",
    )(x, mod_l, g1, *ys_ctx, *ys_smp, wg, bg, wb, wo, g2, rwt)


def _route_kernel(affb_ref, slotb_ref, slott_ref, a_ref, jlo_ref, jhi_ref, acc_s, run_s):
    r = lax.broadcasted_iota(jnp.int32, (CHUNK, CHUNK), 0)
    c = lax.broadcasted_iota(jnp.int32, (CHUNK, CHUNK), 1)
    upper = jnp.where(r <= c, 1.0, 0.0).astype(BF16)
    eye = jnp.where(r == c, 1.0, 0.0).astype(BF16)
    eye2 = jnp.concatenate([eye, eye], axis=1)
    lane = lax.broadcasted_iota(jnp.int32, (N_EXPERTS, LANES), 1)
    acc_s[...] = jnp.zeros_like(acc_s)
    run_s[...] = jnp.zeros_like(run_s)
    for b0, b1, cap in ((0, NB_CTX, CAP_CTX), (NB_CTX, NB_ALL, CAP_SMP)):
        aff = affb_ref[b0:b1]

        def search(i, thr_bits):
            cand = thr_bits | lax.shift_left(jnp.int32(1), 30 - i)
            cnt = jnp.sum((aff >= pltpu.bitcast(cand, F32)[None]).astype(jnp.int32), axis=0)
            cnt = jnp.sum(cnt, axis=1, keepdims=True)
            return jnp.where(cnt >= cap, cand, thr_bits)

        thr = pltpu.bitcast(lax.fori_loop(0, 31, search, jnp.zeros((N_EXPERTS, 1), jnp.int32)), F32)
        n_gt = jnp.sum(jnp.sum((aff > thr[None]).astype(jnp.int32), axis=0), axis=1, keepdims=True)
        need = (cap - n_gt).astype(F32)

        run_s[1] = jnp.zeros((N_EXPERTS, 1), F32)

        def blocks(gi, _):
            b = pl.multiple_of(b0 + gi * ROUTE_GROUP, ROUTE_GROUP)
            run_sel = run_s[0]
            run_eq = run_s[1]
            xs = [affb_ref[b + i] for i in range(ROUTE_GROUP)]
            eqs = [x == thr for x in xs]
            eq_fs = [jnp.where(eq, 1.0, 0.0) for eq in eqs]
            eq_incls = [_dot(f.astype(BF16), upper) for f in eq_fs]
            sels = []
            for x, eq, f, incl in zip(xs, eqs, eq_fs, eq_incls):
                sels.append((x > thr) | (eq & (run_eq + incl - f < need)))
                run_eq = run_eq + incl[:, CHUNK - 1:CHUNK]
            sel_fs = [jnp.where(sel, 1.0, 0.0) for sel in sels]
            sel_incls = [_dot(f.astype(BF16), upper) for f in sel_fs]
            acc = acc_s[...]
            slots = []
            for i, (sel, f, incl) in enumerate(zip(sels, sel_fs, sel_incls)):
                slots.append(jnp.where(sel, run_sel + incl - f, -1.0))
                if i % 2 == 0:
                    acc = jnp.where(lane == b // 2 + i // 2, run_sel, acc)
                run_sel = run_sel + incl[:, CHUNK - 1:CHUNK]
            acc_s[...] = acc
            run_s[0] = run_sel
            run_s[1] = run_eq
            for i, slot in enumerate(slots):
                slotb_ref[b + i] = slot.astype(jnp.int32)
                hi64 = jnp.floor(slot * (1.0 / 64.0)) * 64.0
                parts = jnp.concatenate([hi64, slot - hi64], axis=1).astype(BF16)
                slott_ref[pl.ds(pl.multiple_of((b + i) * CHUNK, CHUNK), CHUNK), :] = _dot_nt(eye2, parts)
            return 0

        lax.fori_loop(0, (b1 - b0) // ROUTE_GROUP, blocks, 0)
    a_acc = jnp.where(lane == NT_ALL, run_s[0], acc_s[...])
    a_ref[...] = a_acc.astype(jnp.int32)
    a_next = pltpu.roll(a_acc, LANES - 1, 1)
    tile_ok = lane < NT_ALL
    jlo = jnp.zeros((N_EXPERTS, LANES), jnp.int32)
    jhi = jnp.zeros((N_EXPERTS, LANES), jnp.int32)
    for k in range(N_RB):
        lo_k = jnp.sum((tile_ok & (a_next <= float(RB * k))).astype(jnp.int32), axis=1, keepdims=True)
        hi_k = jnp.sum((tile_ok & (a_acc < float(RB * (k + 1)))).astype(jnp.int32), axis=1, keepdims=True) - 1
        jlo = jnp.where(lane == k, lo_k, jlo)
        jhi = jnp.where(lane == k, hi_k, jhi)
    jlo_ref[...] = jlo
    jhi_ref[...] = jhi


def _route(affb):
    sds = jax.ShapeDtypeStruct
    small = sds((N_EXPERTS, LANES), jnp.int32)
    return pl.pallas_call(
        _route_kernel,
        out_shape=[sds((NB_ALL, N_EXPERTS, CHUNK), jnp.int32), sds((T_ALL, N_EXPERTS), F32),
                   small, small, small],
        scratch_shapes=[pltpu.VMEM((N_EXPERTS, LANES), F32), pltpu.VMEM((2, N_EXPERTS, 1), F32)],
        compiler_params=pltpu.CompilerParams(vmem_limit_bytes=VMEM_LIMIT), name="route",
    )(affb)


def _expert_kernel(jlo_ref, jhi_ref, h2t_hbm, slotb_ref, affb_ref, wg_ref, wu_ref, wd_ref, o_ref,
                   wg_s, wu_s, wd_s, chunk_s, sem, acc_s, g_s, cnt_s):
    e = pl.program_id(0)
    k = pl.program_id(1)
    step = e * N_RB + k

    @pl.when(k == 0)
    def _():
        wg_s[...] = wg_ref[...].astype(BF16)
        wu_s[...] = wu_ref[...].astype(BF16)
        wd_s[...] = wd_ref[...].astype(BF16)

    def tile_range(s):
        i = (s // N_RB) * LANES + s % N_RB
        lo = jnp.clip(jlo_ref[i], 0, NT_ALL - 1)
        return lo, jnp.clip(jhi_ref[i] - lo + 1, 1, NT_ALL - lo)

    def chunk_tile(lo, c):
        return jnp.minimum(lo + GATHER_TILES * c, NT_ALL - GATHER_TILES)

    def chunk_copy(j0, buf):
        return pltpu.make_async_copy(
            h2t_hbm.at[:, pl.ds(pl.multiple_of(j0 * TM, TM), GATHER_TILES * TM)],
            chunk_s.at[buf], sem.at[buf])

    def n_chunks(s):
        return (tile_range(s)[1] + GATHER_TILES - 1) // GATHER_TILES

    n_steps = N_EXPERTS * N_RB

    def advance(s, c):
        wrap = (c + 1 >= n_chunks(jnp.minimum(s, n_steps - 1))) | (s >= n_steps)
        return jnp.where(wrap, s + 1, s), jnp.where(wrap, 0, c + 1)

    def start_at(s, c, buf):
        @pl.when(s < n_steps)
        def _():
            chunk_copy(chunk_tile(tile_range(jnp.minimum(s, n_steps - 1))[0], c), buf).start()

    jlo, n = tile_range(step)
    nch = n_chunks(step)

    @pl.when(step == 0)
    def _():
        cnt_s[0] = 0
        pos = (step, 0)
        for i in range(GATHER_AHEAD):
            start_at(pos[0], pos[1], i)
            pos = advance(*pos)

    done = cnt_s[0]
    acc_s[...] = jnp.zeros_like(acc_s)
    g_s[...] = jnp.zeros_like(g_s)
    ntok = GATHER_TILES * TM
    want = lax.broadcasted_iota(jnp.int32, (RB, ntok), 0) + k * RB
    lane_tile = lax.broadcasted_iota(jnp.int32, (1, ntok), 1) // TM

    def body(c, _):
        buf = (done + c) % (GATHER_AHEAD + 1)
        j0 = chunk_tile(jlo, c)
        chunk_copy(j0, buf).wait()
        pos = (step, c)
        for _i in range(GATHER_AHEAD):
            pos = advance(*pos)
        start_at(pos[0], pos[1], (done + c + GATHER_AHEAD) % (GATHER_AHEAD + 1))

        nblk = ntok // CHUNK
        srow = jnp.concatenate([slotb_ref[2 * j0 + i, pl.ds(e, 1), :] for i in range(nblk)], axis=1)
        arow = jnp.concatenate([affb_ref[2 * j0 + i, pl.ds(e, 1), :] for i in range(nblk)], axis=1)
        fresh = j0 + lane_tile >= jlo + GATHER_TILES * c
        hit = (srow == want) & fresh
        onehot = jnp.where(hit, 1.0, 0.0).astype(BF16)
        acc_s[...] += _dot_nt(chunk_s[buf], onehot)
        g_s[...] += jnp.sum(jnp.where(hit, arow, 0.0), axis=1, keepdims=True)
        return 0

    lax.fori_loop(0, nch, body, 0)
    cnt_s[0] = done + nch
    xe = acc_s[...].T.astype(BF16)
    hid = _silu(_dot(xe, wg_s[...])) * _dot(xe, wu_s[...])
    o_ref[...] = (_dot(hid.astype(BF16), wd_s[...]) * g_s[...]).astype(BF16)


def _experts(l, jlo, jhi, h2t, slotb, affb, w_gate, w_up, w_down):
    wsp = pl.BlockSpec((None, None, D_MODEL, EXPERT_FF), lambda e, j, *_: (l, e, 0, 0))
    wsd = pl.BlockSpec((None, None, EXPERT_FF, D_MODEL), lambda e, j, *_: (l, e, 0, 0))
    whole = lambda a: pl.BlockSpec(a.shape, lambda e, j, *_: (0,) * a.ndim)
    grid_spec = pltpu.PrefetchScalarGridSpec(
        num_scalar_prefetch=2, grid=(N_EXPERTS, N_RB),
        in_specs=[pl.BlockSpec(memory_space=pl.ANY), whole(slotb), whole(affb), wsp, wsp, wsd],
        out_specs=pl.BlockSpec((None, RB, D_MODEL), lambda e, j, *_: (e, j, 0)),
        scratch_shapes=[pltpu.VMEM((D_MODEL, EXPERT_FF), BF16), pltpu.VMEM((D_MODEL, EXPERT_FF), BF16),
                        pltpu.VMEM((EXPERT_FF, D_MODEL), BF16),
                        pltpu.VMEM((GATHER_AHEAD + 1, D_MODEL, GATHER_TILES * TM), BF16),
                        pltpu.SemaphoreType.DMA((GATHER_AHEAD + 1,)), pltpu.VMEM((D_MODEL, RB), F32),
                        pltpu.VMEM((RB, 1), F32), pltpu.SMEM((1,), jnp.int32)])
    return pl.pallas_call(
        _expert_kernel, grid_spec=grid_spec,
        out_shape=jax.ShapeDtypeStruct((N_EXPERTS, CAP_ALL, D_MODEL), BF16),
        compiler_params=_cparams(2), name="expert_ffn",
    )(jlo, jhi, h2t, slotb, affb, w_gate, w_up, w_down)


def _combine_body(a_ref, ye_hbm, slott_ref, win_s, sem, xwin_s, xsem, o_ref):
    j = pl.program_id(0)
    nt = pl.num_programs(0)

    def first_row(e, jj):
        a = a_ref[e * LANES + jj]
        return pl.multiple_of(jnp.clip((a // BF16_ROWS) * BF16_ROWS, 0, CAP_ALL - WIN), BF16_ROWS)

    def win_copy(e, row0, buf):
        return pltpu.make_async_copy(ye_hbm.at[e, pl.ds(row0, WIN), :], win_s.at[buf, e], sem.at[buf, e])

    buf = j % 2

    @pl.when(j == 0)
    def _():
        for e in range(N_EXPERTS):
            win_copy(e, first_row(e, 0), 0).start()

    @pl.when(j + 1 < nt)
    def _():
        for e in range(N_EXPERTS):
            win_copy(e, first_row(e, j + 1), 1 - buf).start()

    lane = lax.broadcasted_iota(jnp.int32, (TM, LANES), 1)
    lane_f = lane.astype(F32)
    rows0 = []
    pieces = []
    for e in range(0, N_EXPERTS, LANES // WIN):
        tgt = None
        for i in range(LANES // WIN):
            row0 = first_row(e + i, j)
            win_copy(e + i, row0, buf).wait()
            rows0.append(row0)
            t_i = slott_ref[:, e + i:e + i + 1] - (row0 - WIN * i).astype(F32)
            in_win = (lane >= WIN * i) & (lane < WIN * (i + 1))
            tgt = jnp.where(in_win, t_i, -1.0) if tgt is None else jnp.where(in_win, t_i, tgt)
        pieces.append(jnp.where(tgt == lane_f, 1.0, 0.0).astype(BF16))
    onehot = jnp.concatenate(pieces, axis=1)
    o_ref[...] = _dot(onehot, win_s[buf].reshape(N_EXPERTS * WIN, D_MODEL))

    lane_w = lax.broadcasted_iota(jnp.int32, (TM, WIN), 1).astype(F32)
    for e in range(N_EXPERTS):
        row0 = rows0[e]
        n_more = jnp.maximum((a_ref[e * LANES + j + 1] - row0 + WIN - 1) // WIN - 1, 0)

        def more(i, _):
            lo_slot = row0 + (i + 1) * WIN
            r = pl.multiple_of(jnp.minimum(lo_slot, CAP_ALL - WIN), BF16_ROWS)
            cp = pltpu.make_async_copy(ye_hbm.at[e, pl.ds(r, WIN), :], xwin_s, xsem)
            cp.start()
            cp.wait()
            scol = slott_ref[:, e:e + 1]
            scol = jnp.where(scol >= lo_slot.astype(F32), scol, -1.0)
            oh = jnp.where(scol - r.astype(F32) == lane_w, 1.0, 0.0).astype(BF16)
            o_ref[...] += _dot(oh, xwin_s[...])
            return 0

        lax.fori_loop(0, n_more, more, 0)


def _combine_kernel_mid(a_ref, ye_hbm, slott_ref, x_ref, mod_ref, xo_ref, moe_s, win_s, sem, xwin_s, xsem):
    _combine_body(a_ref, ye_hbm, slott_ref, win_s, sem, xwin_s, xsem, moe_s)
    xo_ref[...] = x_ref[...] + mod_ref[...][:, 5 * D_MODEL:6 * D_MODEL] * moe_s[...]


def _combine_kernel_last(a_ref, ye_hbm, slott_ref, x_ref, mod_ref, fg_ref, yp_ref, ys_ref,
                         moe_s, win_s, sem, xwin_s, xsem):
    _combine_body(a_ref, ye_hbm, slott_ref, win_s, sem, xwin_s, xsem, moe_s)
    y = _rms(x_ref[...] + mod_ref[...][:, 5 * D_MODEL:6 * D_MODEL] * moe_s[...], fg_ref[...])

    @pl.when(pl.program_id(0) < NT_CTX)
    def _():
        yp_ref[...] = y

    @pl.when(pl.program_id(0) >= NT_CTX)
    def _():
        ys_ref[...] = y


def _combine(a, ye, slott, x, mod_l, fg):
    tile = lambda w: pl.BlockSpec((TM, w), lambda i, *_: (i, 0))
    in_specs = [pl.BlockSpec(memory_space=pl.ANY), tile(N_EXPERTS), tile(D_MODEL),
                pl.BlockSpec((None, 1, 6 * D_MODEL), lambda i, *_: (_mod_row(i), 0, 0))]
    args = [a, ye, slott, x, mod_l]
    sds = jax.ShapeDtypeStruct
    if fg is None:
        kern, out_specs, out_shape = _combine_kernel_mid, tile(D_MODEL), sds((T_ALL, D_MODEL), F32)
    else:
        kern = _combine_kernel_last
        in_specs.append(pl.BlockSpec(fg.shape, lambda i, *_: (0, 0)))
        args.append(fg)
        out_specs = [pl.BlockSpec((TM, D_MODEL), lambda i, *_: (jnp.minimum(i, NT_CTX - 1), 0)),
                     pl.BlockSpec((TM, D_MODEL), lambda i, *_: (jnp.maximum(i - NT_CTX, 0), 0))]
        out_shape = [sds((T_CTX, D_MODEL), F32), sds((T_SMP, D_MODEL), F32)]
    grid_spec = pltpu.PrefetchScalarGridSpec(
        num_scalar_prefetch=1, grid=(NT_ALL,), in_specs=in_specs, out_specs=out_specs,
        scratch_shapes=[pltpu.VMEM((TM, D_MODEL), F32),
                        pltpu.VMEM((2, N_EXPERTS, WIN, D_MODEL), BF16),
                        pltpu.SemaphoreType.DMA((2, N_EXPERTS)),
                        pltpu.VMEM((WIN, D_MODEL), BF16), pltpu.SemaphoreType.DMA(())])
    return pl.pallas_call(
        kern, grid_spec=grid_spec, out_shape=out_shape,
        compiler_params=_cparams(1), name="moe_combine",
    )(*args)


def _rope_tables():
    t = jnp.arange(DEC_SEQ)
    pos = jnp.stack([t // GRID_W, t % GRID_W], axis=-1).astype(F32)
    nf = DA_HD // 4
    inv = ROPE_BASE ** (-jnp.arange(nf, dtype=F32) / nf)
    ang = pos[:, :, None] * inv
    cos = jnp.cos(ang)
    sin = jnp.sin(ang)
    cos64 = jnp.stack([cos, cos], axis=2).reshape(DEC_SEQ, DA_HD)
    sin64 = jnp.stack([-sin, sin], axis=2).reshape(DEC_SEQ, DA_HD)
    cos_t = jnp.concatenate([jnp.ones((TM, LANES), F32), jnp.tile(cos64, (1, 2))], axis=0)
    sin_t = jnp.concatenate([jnp.zeros((TM, LANES), F32), jnp.tile(sin64, (1, 2))], axis=0)
    return cos_t, sin_t


def kernel(x_prompt, x_sample, cache_k, cache_v, state_ssm, state_mlstm_c, state_mlstm_n, state_mlstm_m, c, c_ctx, ada_w, ada_b, norm1_g, norm2_g, w_in, da_lambda, da_subln_g, mb_conv_w, mb_conv_b, mb_dt_bias, mb_a_log, mb_d, mb_norm_g, ml_gate_b, ml_norm_g, w_branch, w_mgate, b_mgate, w_out, router_w, ex_w_gate, ex_w_up, ex_w_down, final_g):
    x = (x_prompt.reshape(T_CTX, D_MODEL), x_sample.reshape(T_SMP, D_MODEL))
    cc = jnp.concatenate([c_ctx[None, :], c, jnp.zeros((16 - 1 - DEC_BATCH, D_MODEL), F32)], axis=0)
    mod = _modulation(cc, ada_w, ada_b).reshape(DEPTH, 16, 1, 6 * D_MODEL)
    cos_t, sin_t = _rope_tables()
    cache_k2 = cache_k.reshape(DEC_BATCH, DEPTH, PAST_LEN, DA_WIDTH)
    cache_v2 = cache_v.reshape(DEC_BATCH, DEPTH, PAST_LEN, DA_WIDTH)

    outs = {n: [] for n in ("k", "v", "ssm", "C", "n", "m")}
    fg = final_g[None]
    for l in range(DEPTH):
        w = w_in[l]
        wa = w[:, :2816].astype(BF16)
        wb = w[:, 2832:4368].astype(BF16)
        ws = jnp.concatenate([w[:, 2816:2832], w[:, 4368:4384]], axis=1).astype(BF16)
        res = _projection(x, mod[l], norm1_g[l][None], wa, wb, ws, ws.T, cos_t, sin_t)
        q, k, v, kf, vf, z, xbc, mq, mk, mv, mo, sm, smt = res[:13]
        if l == 0:
            x = res[13]
        outs["k"].append(kf.reshape(BATCH, SEQ, DA_HEADS, 2, DA_HD))
        outs["v"].append(vf.reshape(BATCH, SEQ, DA_HEADS, 2 * DA_HD))

        lp = da_lambda[l]
        sg = da_subln_g[l][None]
        cw = mb_conv_w[l]
        cb = mb_conv_b[l][None]
        dtb = mb_dt_bias[l].reshape(16)
        a_neg = -jnp.exp(mb_a_log[l]).reshape(16)
        dexp = jnp.repeat(mb_d[l], MB_HD)[None]
        mng = mb_norm_g[l][None]
        gb = ml_gate_b[l].reshape(16)
        lng = ml_norm_g[l].reshape(1, ML_WIDTH)
        ys = []
        for sample in (False, True):
            nb = DEC_BATCH if sample else BATCH
            y_da = _attention(l, q, k, v, cache_k2, cache_v2, lp, sg, sample)
            if sample:
                h0 = state_ssm[:, l]
                c0 = state_mlstm_c[:, l]
                n0 = state_mlstm_n[:, l]
                m0 = state_mlstm_m[:, l]
            else:
                h0 = jnp.zeros((nb, 2, MB_HEADS, MB_HD, MB_STATE), F32)
                c0 = jnp.zeros((nb, 2, ML_HEADS, ML_DK, ML_DV), F32)
                n0 = jnp.zeros((nb, 2, ML_HEADS, ML_DK), F32)
                m0 = jnp.zeros((nb, 2, ML_HEADS), F32)
            h0t = jnp.transpose(h0, (0, 1, 4, 2, 3)).reshape(nb, 2, MB_STATE, MB_INNER)
            y_mb, hfin = _ssd(z, xbc, sm, smt, cw, cb, dtb, a_neg, dexp, mng, h0t, sample)
            y_ml, cfin, nfin, mfin = _mlstm(
                mq, mk, mv, mo, sm, smt, gb, lng,
                c0.reshape(nb, 2, 2, LANES, ML_DV), n0.reshape(nb, 2, 2, LANES, 1),
                m0.reshape(nb, 1, 8), sample)
            ys.append((y_da, y_mb, y_ml))
            if not sample:
                outs["ssm"].append(jnp.transpose(
                    hfin.reshape(nb, 2, MB_STATE, MB_HEADS, MB_HD), (0, 1, 3, 4, 2)))
                outs["C"].append(cfin.reshape(nb, 2, ML_HEADS, ML_DK, ML_DV))
                outs["n"].append(nfin.reshape(nb, 2, ML_HEADS, ML_DK))
                outs["m"].append(mfin.reshape(nb, 2, ML_HEADS))
        x, h2t, affb = _merge(x, mod[l], norm1_g[l][None], ys[0], ys[1],
                              w_mgate[l].astype(BF16), b_mgate[l][None], w_branch[l].astype(BF16),
                              w_out[l].astype(BF16), norm2_g[l][None], router_w[l].T)
        slotb, slott, a_cnt, jlo, jhi = _route(affb)
        ye = _experts(l, jlo.reshape(-1), jhi.reshape(-1), h2t, slotb, affb,
                      ex_w_gate, ex_w_up, ex_w_down)
        if l + 1 < DEPTH:
            x = _combine(a_cnt.reshape(-1), ye, slott, x, mod[l], None)
        else:
            y_prompt, y_sample = _combine(a_cnt.reshape(-1), ye, slott, x, mod[l], fg)

    y_prompt = y_prompt.reshape(BATCH, SEQ, D_MODEL)
    y_sample = y_sample.reshape(DEC_BATCH, DEC_SEQ, D_MODEL)
    return (y_prompt, y_sample, jnp.stack(outs["k"], axis=1), jnp.stack(outs["v"], axis=1),
            jnp.stack(outs["ssm"], axis=1), jnp.stack(outs["C"], axis=1),
            jnp.stack(outs["n"], axis=1), jnp.stack(outs["m"], axis=1))
```
